```python
import math
import jax, jax.numpy as jnp
from jax import lax
import numpy as np

D_MODEL = 1024
BATCH = 32
SEQ = 256
DEPTH = 2
DEC_BATCH = 8
DEC_SEQ = 4096
PAST_LEN = 256

GRID_W = 64
NA_HEADS = 16
NA_HEAD_DIM = D_MODEL // NA_HEADS
NA_ROWS = 8
NA_COLS = 16
DIFF_HEADS = 8
DIFF_HEAD_DIM = D_MODEL // (2 * DIFF_HEADS)
N_EXPERTS = 64
TOP_K = 8
EXPERT_DIM = 256
SHARED_DIM = 256
ROUTED_SCALE = 2.5
EXPERT_BLOCK = 128
Q_BLOCK = 128
ROPE_BASE = 10000.0
EPS = 1e-6
N_MOD = 6
N_NA_LAYERS = (DEPTH + 1) // 2
N_DIFF_LAYERS = DEPTH // 2

kernel_name = 'hybrid_natten_diffattn_moe_diffusion_step'


def rms_norm(x, g):
    xf = x.astype(jnp.float32)
    y = xf * lax.rsqrt(jnp.mean(xf * xf, axis=-1, keepdims=True) + EPS)
    return (y * g.astype(jnp.float32)).astype(x.dtype)


def ada_modulation(cond, w, b):
    m = jax.nn.silu(cond) @ w + b
    return jnp.split(m, N_MOD, axis=-1)


def to_heads(t, n_heads):
    b, s, _ = t.shape
    return t.reshape(b, s, n_heads, -1).transpose(0, 2, 1, 3)


def from_heads(t):
    b, h, s, d = t.shape
    return t.transpose(0, 2, 1, 3).reshape(b, s, h * d)


def diff_split(t):
    b, s, _ = t.shape
    t = t.reshape(b, s, DIFF_HEADS, 2, DIFF_HEAD_DIM).transpose(3, 0, 2, 1, 4)
    return t[0], t[1]


def query_blocks(q):
    b, h, s, d = q.shape
    return q.reshape(b, h, s // Q_BLOCK, Q_BLOCK, d).transpose(2, 0, 1, 3, 4)


def merge_blocks(o):
    nb, b, h, qb, d = o.shape
    return o.transpose(1, 2, 0, 3, 4).reshape(b, h, nb * qb, d)


def dense_attention(q, k, v):
    scale = q.shape[-1] ** -0.5

    def block(qb):
        s = jnp.einsum('bhqd,bhkd->bhqk', qb, k).astype(jnp.float32) * scale
        p = jax.nn.softmax(s, axis=-1).astype(v.dtype)
        return jnp.einsum('bhqk,bhkd->bhqd', p, v)

    return merge_blocks(lax.map(block, query_blocks(q)))


def differential_attention(q1, q2, k1, k2, v, lam):
    scale = q1.shape[-1] ** -0.5

    def block(qq):
        qb1, qb2 = qq
        s1 = jnp.einsum('bhqd,bhkd->bhqk', qb1, k1).astype(jnp.float32) * scale
        s2 = jnp.einsum('bhqd,bhkd->bhqk', qb2, k2).astype(jnp.float32) * scale
        p = (jax.nn.softmax(s1, axis=-1) - lam * jax.nn.softmax(s2, axis=-1)).astype(v.dtype)
        return jnp.einsum('bhqk,bhkd->bhqd', p, v)

    return merge_blocks(lax.map(block, (query_blocks(q1), query_blocks(q2))))


def lambda_init_for(layer):
    return 0.8 - 0.6 * math.exp(-0.3 * layer)


def diff_lambda_value(lp, lam_init):
    lpf = lp.astype(jnp.float32)
    return jnp.exp(jnp.sum(lpf[0] * lpf[1])) - jnp.exp(jnp.sum(lpf[2] * lpf[3])) + lam_init


def axial_rope(n_tokens, dim):
    n_freq = dim // 4
    inv = ROPE_BASE ** (-jnp.arange(n_freq, dtype=jnp.float32) / n_freq)
    pos = jnp.arange(n_tokens)
    row = (pos // GRID_W).astype(jnp.float32)
    col = (pos % GRID_W).astype(jnp.float32)
    ang = jnp.concatenate([row[:, None] * inv, col[:, None] * inv], axis=-1)
    return jnp.cos(ang), jnp.sin(ang)


def apply_rope(x, cos, sin):
    half = x.shape[-1] // 2
    x1, x2 = x[..., :half], x[..., half:]
    cos = cos.astype(x.dtype)
    sin = sin.astype(x.dtype)
    return jnp.concatenate([x1 * cos - x2 * sin, x1 * sin + x2 * cos], axis=-1)


def neighbourhood_attention(q, k, v, ck, cv, rpb):
    b, h, t, d = q.shape
    rows = t // GRID_W
    kr = min(NA_ROWS, rows)
    n_win = kr * NA_COLS
    scale = d ** -0.5
    r = jnp.arange(rows)
    col = jnp.arange(GRID_W)
    win_r = jnp.clip(r - kr // 2, 0, rows - kr)[:, None] + jnp.arange(kr)
    win_c = jnp.clip(col - NA_COLS // 2, 0, GRID_W - NA_COLS)[:, None] + jnp.arange(NA_COLS)
    key_idx = (win_r[:, None, :, None] * GRID_W + win_c[None, :, None, :]).reshape(rows, GRID_W, n_win)
    off_r = win_r - r[:, None] + (NA_ROWS - 1)
    off_c = win_c - col[:, None] + (NA_COLS - 1)
    bias = rpb.astype(jnp.float32)[:, off_r[:, None, :, None], off_c[None, :, None, :]]
    bias = bias.reshape(h, rows, GRID_W, n_win).transpose(1, 0, 2, 3)
    q_rows = q.reshape(b, h, rows, GRID_W, d).transpose(2, 0, 1, 3, 4)

    def row_block(args):
        q_r, idx_r, bias_r = args
        kg = k[:, :, idx_r]
        vg = v[:, :, idx_r]
        s_loc = jnp.einsum('bhqd,bhqnd->bhqn', q_r, kg).astype(jnp.float32) * scale + bias_r
        s_ctx = jnp.einsum('bhqd,bhld->bhql', q_r, ck).astype(jnp.float32) * scale
        p = jax.nn.softmax(jnp.concatenate([s_loc, s_ctx], axis=-1), axis=-1).astype(v.dtype)
        return (jnp.einsum('bhqn,bhqnd->bhqd', p[..., :n_win], vg)
                + jnp.einsum('bhql,bhld->bhqd', p[..., n_win:], cv))

    o = lax.map(row_block, (q_rows, key_idx, bias))
    return o.transpose(1, 2, 0, 3, 4).reshape(b, h, t, d)


def swiglu(x, wg, wu, wd):
    return (jax.nn.silu(x @ wg) * (x @ wu)) @ wd


def moe_ffn(h, router_w, router_b, w_gate, w_up, w_down, s_gate, s_up, s_down):
    shape = h.shape
    x = h.reshape(-1, shape[-1])
    n = x.shape[0]
    scores = jax.nn.sigmoid((x @ router_w).astype(jnp.float32))
    _, sel = lax.top_k(scores + router_b.astype(jnp.float32), TOP_K)
    gates = jnp.take_along_axis(scores, sel, axis=-1)
    gates = gates / jnp.sum(gates, axis=-1, keepdims=True) * ROUTED_SCALE
    nk = n * TOP_K
    e_flat = sel.reshape(nk)
    order = jnp.argsort(e_flat)
    e_sorted = e_flat[order]
    tok_sorted = (jnp.arange(nk) // TOP_K)[order].astype(jnp.int32)
    g_sorted = gates.reshape(nk)[order]
    counts = jnp.bincount(e_flat, length=N_EXPERTS)
    starts = jnp.cumsum(counts) - counts
    padded = (counts + EXPERT_BLOCK - 1) // EXPERT_BLOCK * EXPERT_BLOCK
    pad_end = jnp.cumsum(padded)
    pad_start = pad_end - padded
    dest = pad_start[e_sorted] + jnp.arange(nk) - starts[e_sorted]
    n_blocks = -(-nk // EXPERT_BLOCK) + N_EXPERTS
    cap = n_blocks * EXPERT_BLOCK
    buf_tok = jnp.zeros((cap,), jnp.int32).at[dest].set(tok_sorted)
    buf_g = jnp.zeros((cap,), x.dtype).at[dest].set(g_sorted.astype(x.dtype))
    block_expert = jnp.minimum(
        jnp.searchsorted(pad_end, jnp.arange(n_blocks) * EXPERT_BLOCK, side='right'), N_EXPERTS - 1)
    xb = x[buf_tok].reshape(n_blocks, EXPERT_BLOCK, -1)

    def expert_block(args):
        xx, e = args
        return swiglu(xx, w_gate[e], w_up[e], w_down[e])

    yb = lax.map(expert_block, (xb, block_expert)).reshape(cap, -1)
    routed = jnp.zeros_like(x).at[buf_tok].add(yb * buf_g[:, None])
    return (routed + swiglu(x, s_gate, s_up, s_down)).reshape(shape)


def _normal(k, shape, scale):
    return jax.random.normal(k, shape, jnp.float32) * scale


def setup_inputs(seed: int = 0) -> dict:
    key = jax.random.key(seed)
    ks = jax.random.split(key, 24)
    D = D_MODEL
    sd = D ** -0.5
    return {
        'x_prompt': _normal(ks[0], (BATCH, SEQ, D), 1.0),
        'x_sample': _normal(ks[1], (DEC_BATCH, DEC_SEQ, D), 1.0),
        'cache_k': _normal(ks[2], (DEC_BATCH, DEPTH, PAST_LEN, D), 1.0),
        'cache_v': _normal(ks[3], (DEC_BATCH, DEPTH, PAST_LEN, D), 1.0),
        'c': _normal(ks[4], (DEC_BATCH, D), 1.0),
        'c_ctx': _normal(ks[5], (D,), 1.0),
        'ada_w': _normal(ks[6], (DEPTH, D, N_MOD * D), 0.5 * sd),
        'ada_b': _normal(ks[7], (DEPTH, N_MOD * D), 0.02),
        'norm1_g': 1.0 + _normal(ks[8], (DEPTH, D), 0.02),
        'norm2_g': 1.0 + _normal(ks[9], (DEPTH, D), 0.02),
        'w_qkv': _normal(ks[10], (DEPTH, D, 3 * D), sd),
        'w_o': _normal(ks[11], (DEPTH, D, D), sd),
        'na_rpb': _normal(ks[12], (N_NA_LAYERS, NA_HEADS, 2 * NA_ROWS - 1, 2 * NA_COLS - 1), 0.1),
        'diff_lambda': _normal(ks[13], (N_DIFF_LAYERS, 4, DIFF_HEAD_DIM), 0.1),
        'diff_subln_g': 1.0 + _normal(ks[14], (N_DIFF_LAYERS, 2 * DIFF_HEAD_DIM), 0.02),
        'router_w': _normal(ks[15], (DEPTH, D, N_EXPERTS), sd),
        'router_b': _normal(ks[16], (DEPTH, N_EXPERTS), 0.01),
        'exp_w_gate': _normal(ks[17], (DEPTH, N_EXPERTS, D, EXPERT_DIM), sd),
        'exp_w_up': _normal(ks[18], (DEPTH, N_EXPERTS, D, EXPERT_DIM), sd),
        'exp_w_down': _normal(ks[19], (DEPTH, N_EXPERTS, EXPERT_DIM, D), EXPERT_DIM ** -0.5),
        'shared_w_gate': _normal(ks[20], (DEPTH, D, SHARED_DIM), sd),
        'shared_w_up': _normal(ks[21], (DEPTH, D, SHARED_DIM), sd),
        'shared_w_down': _normal(ks[22], (DEPTH, SHARED_DIM, D), SHARED_DIM ** -0.5),
        'final_g': 1.0 + _normal(ks[23], (D,), 0.02),
    }


def reference(x_prompt, x_sample, cache_k, cache_v, c, c_ctx, ada_w, ada_b,
              norm1_g, norm2_g, w_qkv, w_o, na_rpb, diff_lambda, diff_subln_g,
              router_w, router_b, exp_w_gate, exp_w_up, exp_w_down,
              shared_w_gate, shared_w_up, shared_w_down, final_g):
    xc = x_prompt
    new_k = []
    new_v = []
    for i in range(DEPTH):
        sh1, sc1, g1, sh2, sc2, g2 = ada_modulation(c_ctx, ada_w[i], ada_b[i])
        h = rms_norm(xc, norm1_g[i]) * (1 + sc1) + sh1
        q, k, v = jnp.split(h @ w_qkv[i], 3, axis=-1)
        new_k.append(k)
        new_v.append(v)
        if i % 2 == 0:
            o = dense_attention(to_heads(q, NA_HEADS), to_heads(k, NA_HEADS), to_heads(v, NA_HEADS))
        else:
            j = i // 2
            lam_init = lambda_init_for(i)
            q1, q2 = diff_split(q)
            k1, k2 = diff_split(k)
            o = differential_attention(q1, q2, k1, k2, to_heads(v, DIFF_HEADS),
                                       diff_lambda_value(diff_lambda[j], lam_init))
            o = rms_norm(o, diff_subln_g[j]) * (1.0 - lam_init)
        xc = xc + g1 * (from_heads(o) @ w_o[i])
        h = rms_norm(xc, norm2_g[i]) * (1 + sc2) + sh2
        xc = xc + g2 * moe_ffn(h, router_w[i], router_b[i], exp_w_gate[i], exp_w_up[i], exp_w_down[i],
                               shared_w_gate[i], shared_w_up[i], shared_w_down[i])
    y_prompt = rms_norm(xc, final_g)
    new_cache_k = jnp.stack(new_k, axis=1)
    new_cache_v = jnp.stack(new_v, axis=1)

    n_lat = x_sample.shape[1]
    cos, sin = axial_rope(n_lat, DIFF_HEAD_DIM)
    xs = x_sample
    for i in range(DEPTH):
        sh1, sc1, g1, sh2, sc2, g2 = [m[:, None, :] for m in ada_modulation(c, ada_w[i], ada_b[i])]
        h = rms_norm(xs, norm1_g[i]) * (1 + sc1) + sh1
        q, k, v = jnp.split(h @ w_qkv[i], 3, axis=-1)
        ck = cache_k[:, i]
        cv = cache_v[:, i]
        if i % 2 == 0:
            o = neighbourhood_attention(to_heads(q, NA_HEADS), to_heads(k, NA_HEADS), to_heads(v, NA_HEADS),
                                        to_heads(ck, NA_HEADS), to_heads(cv, NA_HEADS), na_rpb[i // 2])
        else:
            j = i // 2
            lam_init = lambda_init_for(i)
            q1, q2 = diff_split(q)
            k1, k2 = diff_split(k)
            q1, q2, k1, k2 = [apply_rope(t, cos, sin) for t in (q1, q2, k1, k2)]
            ck1, ck2 = diff_split(ck)
            o = differential_attention(
                q1, q2,
                jnp.concatenate([k1, ck1], axis=2),
                jnp.concatenate([k2, ck2], axis=2),
                jnp.concatenate([to_heads(v, DIFF_HEADS), to_heads(cv, DIFF_HEADS)], axis=2),
                diff_lambda_value(diff_lambda[j], lam_init))
            o = rms_norm(o, diff_subln_g[j]) * (1.0 - lam_init)
        xs = xs + g1 * (from_heads(o) @ w_o[i])
        h = rms_norm(xs, norm2_g[i]) * (1 + sc2) + sh2
        xs = xs + g2 * moe_ffn(h, router_w[i], router_b[i], exp_w_gate[i], exp_w_up[i], exp_w_down[i],
                               shared_w_gate[i], shared_w_up[i], shared_w_down[i])
    y_sample = rms_norm(xs, final_g)
    return (y_prompt, y_sample, new_cache_k, new_cache_v)
```

```python
import functools
import math

import jax
import jax.numpy as jnp
from jax import lax
from jax.experimental import pallas as pl
from jax.experimental.pallas import tpu as pltpu

GRID_W = 64
NA_ROWS = 8
NA_COLS = 16
TOP_K = 8
ROUTED_SCALE = 2.5
ROPE_BASE = 10000.0
EPS = 1e-6
N_MOD = 6
HEAD_DIM = 64
LANES = 128
MASKED = -1e30
VMEM_LIMIT = 56 * 1024 * 1024

F32 = jnp.float32
BF16 = jnp.bfloat16


def _cparams(*sem):
    return pltpu.CompilerParams(dimension_semantics=sem, vmem_limit_bytes=VMEM_LIMIT)


def _dot(a, b):
    return jnp.dot(a, b, preferred_element_type=F32)


def _dot_nt(a, b):
    return lax.dot_general(a, b, (((1,), (1,)), ((), ())), preferred_element_type=F32)


def _sigmoid(x):
    return 1.0 / (1.0 + jnp.exp(-x))


def _rms(x, g):
    return x * lax.rsqrt(jnp.mean(x * x, axis=-1, keepdims=True) + EPS) * g


def _pick_tile(n, pref):
    t = pref
    while n % t:
        t //= 2
    return t


def _mod_kernel(c_ref, w_ref, b_ref, o_ref):
    c = c_ref[...]
    o_ref[...] = _dot(c * _sigmoid(c), w_ref[...]) + b_ref[...]


def _modulation(cond, ada_w, ada_b):
    depth, d, n6 = ada_w.shape
    rows = cond.shape[0]
    tn = _pick_tile(n6, 512)
    return pl.pallas_call(
        _mod_kernel,
        grid=(depth, n6 // tn),
        in_specs=[
            pl.BlockSpec((rows, d), lambda l, j: (0, 0)),
            pl.BlockSpec((None, d, tn), lambda l, j: (l, 0, j)),
            pl.BlockSpec((None, 1, tn), lambda l, j: (l, 0, j)),
        ],
        out_specs=pl.BlockSpec((None, rows, tn), lambda l, j: (l, 0, j)),
        out_shape=jax.ShapeDtypeStruct((depth, rows, n6), F32),
        compiler_params=_cparams("arbitrary", "arbitrary"),
        name="modulation",
    )(cond, ada_w, ada_b.reshape(depth, 1, n6))


def _qkv_kernel(x_ref, mod_ref, g_ref, w_ref, *rest, d, scale, rope, n_lat_tiles):
    if rope:
        rope_ref, q_ref, k_ref, v_ref, kf_ref, vf_ref = rest
    else:
        q_ref, k_ref, v_ref, kf_ref, vf_ref = rest
    i = pl.program_id(0)
    x = x_ref[...]
    h = _rms(x, g_ref[...]) * (1.0 + mod_ref[1:2, :]) + mod_ref[0:1, :]
    hb = h.astype(BF16)

    def rot(t):
        if not rope:
            return t
        cols = []
        for j in range(d // LANES):
            tb = t[:, j * LANES:(j + 1) * LANES]
            cols.append(tb * rope_ref[0]
                        + pltpu.roll(tb, LANES - HEAD_DIM // 2, axis=1) * rope_ref[1]
                        + pltpu.roll(tb, HEAD_DIM // 2, axis=1) * rope_ref[2])
        return jnp.concatenate(cols, axis=1)

    q = _dot(hb, w_ref[:, 0:d])
    q_ref[...] = (rot(q) * scale).astype(BF16)
    k = _dot(hb, w_ref[:, d:2 * d])
    k_ref[...] = rot(k).astype(BF16)
    v = _dot(hb, w_ref[:, 2 * d:3 * d])
    v_ref[...] = v.astype(BF16)

    @pl.when(i >= n_lat_tiles)
    def _():
        kf_ref[...] = k
        vf_ref[...] = v


def _qkv(x, mod, g, w, rope_tab, *, layer, n_lat, lat_seq, n_lat_batch, tm, scale):
    n, d = x.shape
    n_ctx = n - n_lat
    n_lat_tiles = n_lat // tm
    rope = rope_tab is not None

    def mod_idx(i):
        return (layer, jnp.minimum(i * tm // lat_seq, n_lat_batch), 0, 0)

    def ctx_idx(i):
        return (jnp.maximum(i - n_lat_tiles, 0), 0)

    in_specs = [
        pl.BlockSpec((tm, d), lambda i: (i, 0)),
        pl.BlockSpec((None, None, N_MOD, d), mod_idx),
        pl.BlockSpec((None, 1, d), lambda i: (layer, 0, 0)),
        pl.BlockSpec((None, d, 3 * d), lambda i: (layer, 0, 0)),
    ]
    args = [x, mod, g, w]
    if rope:
        seq_tiles = lat_seq // tm
        in_specs.append(pl.BlockSpec(
            (3, tm, LANES),
            lambda i: (0, jnp.where(i < n_lat_tiles, i % seq_tiles, seq_tiles), 0)))
        args.append(rope_tab)
    tok = pl.BlockSpec((tm, d), lambda i: (i, 0))
    return pl.pallas_call(
        functools.partial(_qkv_kernel, d=d, scale=scale, rope=rope, n_lat_tiles=n_lat_tiles),
        grid=(n // tm,),
        in_specs=in_specs,
        out_specs=[tok, tok, tok, pl.BlockSpec((tm, d), ctx_idx), pl.BlockSpec((tm, d), ctx_idx)],
        out_shape=[jax.ShapeDtypeStruct((n, d), BF16)] * 3
        + [jax.ShapeDtypeStruct((n_ctx, d), F32)] * 2,
        compiler_params=_cparams("arbitrary"),
        name=f"qkv{layer}",
    )(*args)


def _rope_tables(lat_seq, tm):
    n_freq = HEAD_DIM // 4
    inv = ROPE_BASE ** (-jnp.arange(n_freq, dtype=F32) / n_freq)
    pos = jnp.arange(lat_seq)
    row = (pos // GRID_W).astype(F32)
    col = (pos % GRID_W).astype(F32)
    ang = jnp.concatenate([row[:, None] * inv, col[:, None] * inv], axis=-1)
    reps = LANES // (HEAD_DIM // 2)
    cos = jnp.tile(jnp.cos(ang), (1, reps))
    sin = jnp.tile(jnp.sin(ang), (1, reps))
    first_half = (jnp.arange(LANES) % HEAD_DIM) < HEAD_DIM // 2
    s_next = jnp.where(first_half, -sin, 0.0)
    s_prev = jnp.where(first_half, 0.0, sin)
    ident = jnp.stack([jnp.ones((tm, LANES), F32), jnp.zeros((tm, LANES), F32),
                       jnp.zeros((tm, LANES), F32)])
    return jnp.concatenate([jnp.stack([cos, s_next, s_prev]), ident], axis=1)


def _lane_lo():
    return lax.broadcasted_iota(jnp.int32, (1, LANES), 1) < HEAD_DIM


def _ctx_dense_kernel(q_ref, k_ref, v_ref, o_ref, *, d):
    lo = _lane_lo()
    for j in range(d // LANES):
        blk = slice(j * LANES, (j + 1) * LANES)
        q2, k2, v2 = q_ref[:, blk], k_ref[:, blk], v_ref[:, blk]
        outs = []
        for qm in (jnp.where(lo, q2, 0), jnp.where(lo, 0, q2)):
            s = _dot_nt(qm, k2)
            p = jnp.exp(s - jnp.max(s, axis=-1, keepdims=True))
            outs.append(_dot(p.astype(BF16), v2) / jnp.sum(p, axis=-1, keepdims=True))
        o_ref[:, blk] = jnp.where(lo, outs[0], outs[1]).astype(BF16)


def _ctx_dense(q, k, v, *, n_lat, seq):
    n, d = q.shape
    off = n_lat // seq
    spec = pl.BlockSpec((seq, d), lambda b: (off + b, 0))
    return pl.pallas_call(
        functools.partial(_ctx_dense_kernel, d=d),
        grid=((n - n_lat) // seq,),
        in_specs=[spec, spec, spec],
        out_specs=pl.BlockSpec((seq, d), lambda b: (b, 0)),
        out_shape=jax.ShapeDtypeStruct((n - n_lat, d), BF16),
        compiler_params=_cparams("arbitrary"),
        name="ctx_dense_attn",
    )(q, k, v)


def _subln(o, g_ref, out_scale):
    return _rms(o, g_ref[...]) * out_scale


def _ctx_diff_kernel(lam_ref, q_ref, k_ref, v_ref, g_ref, o_ref, *, d, out_scale):
    lo = _lane_lo()
    lam = lam_ref[0]
    for j in range(d // LANES):
        blk = slice(j * LANES, (j + 1) * LANES)
        q2, k2, v2 = q_ref[:, blk], k_ref[:, blk], v_ref[:, blk]
        ps = []
        for qm in (jnp.where(lo, q2, 0), jnp.where(lo, 0, q2)):
            s = _dot_nt(qm, k2)
            p = jnp.exp(s - jnp.max(s, axis=-1, keepdims=True))
            ps.append(p / jnp.sum(p, axis=-1, keepdims=True))
        o = _dot((ps[0] - lam * ps[1]).astype(BF16), v2)
        o_ref[:, blk] = _subln(o, g_ref, out_scale).astype(BF16)


def _ctx_diff(lam, q, k, v, subg, *, n_lat, seq, out_scale):
    n, d = q.shape
    off = n_lat // seq
    spec = pl.BlockSpec((seq, d), lambda b: (off + b, 0))
    return pl.pallas_call(
        functools.partial(_ctx_diff_kernel, d=d, out_scale=out_scale),
        grid=((n - n_lat) // seq,),
        in_specs=[pl.BlockSpec(memory_space=pltpu.SMEM), spec, spec, spec,
                  pl.BlockSpec((1, LANES), lambda b: (0, 0))],
        out_specs=pl.BlockSpec((seq, d), lambda b: (b, 0)),
        out_shape=jax.ShapeDtypeStruct((n - n_lat, d), BF16),
        compiler_params=_cparams("arbitrary"),
        name="ctx_diff_attn",
    )(lam, q, k, v, subg)


def _na_kernel(q_ref, k_ref, v_ref, ck_ref, cv_ref, bias_ref, o_ref, *, d, rows):
    lo = _lane_lo()
    r = pl.program_id(1)
    win = NA_ROWS * GRID_W
    start = pl.multiple_of(jnp.clip(r - NA_ROWS // 2, 0, rows - NA_ROWS) * GRID_W, GRID_W)
    for j in range(d // LANES):
        blk = slice(j * LANES, (j + 1) * LANES)
        q2 = q_ref[:, blk]
        kw = k_ref[pl.ds(start, win), blk]
        vw = v_ref[pl.ds(start, win), blk]
        ck2, cv2 = ck_ref[:, blk], cv_ref[:, blk]
        outs = []
        for hh, qm in enumerate((jnp.where(lo, q2, 0), jnp.where(lo, 0, q2))):
            s_loc = _dot_nt(qm, kw) + bias_ref[2 * j + hh]
            s_ctx = _dot_nt(qm, ck2)
            m = jnp.maximum(jnp.max(s_loc, axis=-1, keepdims=True),
                            jnp.max(s_ctx, axis=-1, keepdims=True))
            p_loc = jnp.exp(s_loc - m)
            p_ctx = jnp.exp(s_ctx - m)
            l = jnp.sum(p_loc, axis=-1, keepdims=True) + jnp.sum(p_ctx, axis=-1, keepdims=True)
            outs.append((_dot(p_loc.astype(BF16), vw) + _dot(p_ctx.astype(BF16), cv2)) / l)
        o_ref[:, blk] = jnp.where(lo, outs[0], outs[1]).astype(BF16)


def _na_bias_table(rpb, rows):
    h = rpb.shape[0]
    delta = jnp.arange(NA_ROWS)
    kr = jnp.arange(NA_ROWS)
    c = jnp.arange(GRID_W)
    kc = jnp.arange(GRID_W)
    off_r = kr[None, :] - delta[:, None] + (NA_ROWS - 1)
    c0 = jnp.clip(c - NA_COLS // 2, 0, GRID_W - NA_COLS)
    off_c = kc[None, :] - c[:, None] + (NA_COLS - 1)
    inside = (kc[None, :] >= c0[:, None]) & (kc[None, :] < c0[:, None] + NA_COLS)
    off_r = jnp.clip(off_r, 0, 2 * NA_ROWS - 2)
    off_c = jnp.clip(off_c, 0, 2 * NA_COLS - 2)
    t = rpb.astype(F32)[:, off_r[:, None, :, None], off_c[None, :, None, :]]
    t = jnp.where(inside[None, None, :, None, :], t, MASKED)
    return t.transpose(1, 0, 2, 3, 4).reshape(NA_ROWS, h, GRID_W, NA_ROWS * GRID_W)


def _na_attention(q, k, v, ck, cv, bias, *, n_lat_batch, lat_seq):
    n, d = q.shape
    rows = lat_seq // GRID_W
    past = ck.shape[1]
    n_heads = bias.shape[1]

    def delta_idx(b, r):
        return (r - jnp.clip(r - NA_ROWS // 2, 0, rows - NA_ROWS), 0, 0, 0)

    kv_spec = pl.BlockSpec((lat_seq, d), lambda b, r: (b, 0))
    c_spec = pl.BlockSpec((None, past, d), lambda b, r: (b, 0, 0))
    return pl.pallas_call(
        functools.partial(_na_kernel, d=d, rows=rows),
        grid=(n_lat_batch, rows),
        in_specs=[pl.BlockSpec((GRID_W, d), lambda b, r: (b * rows + r, 0)), kv_spec, kv_spec,
                  c_spec, c_spec,
                  pl.BlockSpec((None, n_heads, GRID_W, NA_ROWS * GRID_W), delta_idx)],
        out_specs=pl.BlockSpec((GRID_W, d), lambda b, r: (b * rows + r, 0)),
        out_shape=jax.ShapeDtypeStruct((n_lat_batch * lat_seq, d), BF16),
        compiler_params=_cparams("arbitrary", "arbitrary"),
        name="na_attn",
    )(q, k, v, ck, cv, bias)


def _lat_diff_kernel(lam_ref, q_ref, k_ref, v_ref, ck_ref, cv_ref, g_ref, o_ref, *, tk, out_scale):
    lo = _lane_lo()
    lam = lam_ref[0]
    q2 = q_ref[...]
    tq = q2.shape[0]
    qa, qb = jnp.where(lo, q2, 0), jnp.where(lo, 0, q2)

    def update(state, qm, kt, vt):
        m, l, acc = state
        s = _dot_nt(qm, kt)
        m_new = jnp.maximum(m, jnp.max(s, axis=-1, keepdims=True))
        alpha = jnp.exp(m - m_new)
        p = jnp.exp(s - m_new)
        l = alpha * l + jnp.sum(p, axis=-1, keepdims=True)
        acc = alpha * acc + _dot(p.astype(BF16), vt)
        return m_new, l, acc

    def body(i, carry):
        off = pl.multiple_of(i * tk, tk)
        kt = k_ref[pl.ds(off, tk), :]
        vt = v_ref[pl.ds(off, tk), :]
        return update(carry[0], qa, kt, vt), update(carry[1], qb, kt, vt)

    init = (jnp.full((tq, 1), -jnp.inf, F32), jnp.zeros((tq, 1), F32), jnp.zeros((tq, LANES), F32))
    s1, s2 = lax.fori_loop(0, k_ref.shape[0] // tk, body, (init, init))
    s1 = update(s1, qa, ck_ref[...], cv_ref[...])
    s2 = update(s2, qb, ck_ref[...], cv_ref[...])
    o = s1[2] / s1[1] - lam * (s2[2] / s2[1])
    o_ref[...] = _subln(o, g_ref, out_scale).astype(BF16)


def _lat_diff(lam, q, k, v, ck, cv, subg, *, n_lat_batch, lat_seq, out_scale):
    n, d = q.shape
    past = ck.shape[1]
    tq = _pick_tile(lat_seq, 512)
    tk = _pick_tile(lat_seq, 512)
    qt = lat_seq // tq
    kv_spec = pl.BlockSpec((lat_seq, LANES), lambda b, h, i: (b, h))
    c_spec = pl.BlockSpec((None, past, LANES), lambda b, h, i: (b, 0, h))
    q_spec = pl.BlockSpec((tq, LANES), lambda b, h, i: (b * qt + i, h))
    return pl.pallas_call(
        functools.partial(_lat_diff_kernel, tk=tk, out_scale=out_scale),
        grid=(n_lat_batch, d // LANES, qt),
        in_specs=[pl.BlockSpec(memory_space=pltpu.SMEM), q_spec, kv_spec, kv_spec, c_spec, c_spec,
                  pl.BlockSpec((1, LANES), lambda b, h, i: (0, 0))],
        out_specs=q_spec,
        out_shape=jax.ShapeDtypeStruct((n_lat_batch * lat_seq, d), BF16),
        compiler_params=_cparams("arbitrary", "arbitrary", "arbitrary"),
        name="lat_diff_attn",
    )(lam, q, k, v, ck, cv, subg)


def _post_attn_kernel(o_ref, x_ref, mod_ref, wo_ref, g_ref, rwh_ref, rwl_ref, rb_ref, tri_ref,
                      xmid_ref, h_ref, sel_ref, gate_ref, rank_ref, cnt_ref, carry_ref):
    i = pl.program_id(0)

    @pl.when(i == 0)
    def _():
        carry_ref[...] = jnp.zeros_like(carry_ref)

    x1 = x_ref[...] + mod_ref[2:3, :] * _dot(o_ref[...], wo_ref[...])
    xmid_ref[...] = x1
    h = _rms(x1, g_ref[...]) * (1.0 + mod_ref[4:5, :]) + mod_ref[3:4, :]
    hb = h.astype(BF16)
    h_ref[...] = hb
    h_lo = (h - hb.astype(F32)).astype(BF16)
    logits = _dot_nt(rwh_ref[...], hb) + (_dot_nt(rwl_ref[...], hb) + _dot_nt(rwh_ref[...], h_lo))
    scores = _sigmoid(logits)
    n_exp, tm = scores.shape
    biased = scores + rb_ref[...]
    row = lax.broadcasted_iota(jnp.int32, (n_exp, tm), 0).astype(F32)
    total = jnp.zeros((n_exp, tm), F32)
    sels, gates = [], []
    for _ in range(TOP_K):
        m = jnp.max(biased, axis=0, keepdims=True)
        idx = jnp.min(jnp.where(biased == m, row, float(n_exp)), axis=0, keepdims=True)
        hit = row == idx
        gates.append(jnp.sum(jnp.where(hit, scores, 0.0), axis=0, keepdims=True))
        biased = jnp.where(hit, -jnp.inf, biased)
        total = total + jnp.where(hit, 1.0, 0.0)
        sels.append(idx)
    denom = gates[0]
    for g in gates[1:]:
        denom = denom + g
    before = _dot(total.astype(BF16), tri_ref[...]) + carry_ref[:, 0:1]
    for k in range(TOP_K):
        sel_ref[k:k + 1, :] = sels[k].astype(jnp.int32)
        gate_ref[k:k + 1, :] = gates[k] / denom * ROUTED_SCALE
        rank_ref[k:k + 1, :] = jnp.sum(jnp.where(row == sels[k], before, 0.0), axis=0,
                                       keepdims=True).astype(jnp.int32)
    carry_ref[...] = carry_ref[...] + jnp.sum(total, axis=1, keepdims=True)
    cnt_ref[...] = carry_ref[...].astype(jnp.int32)


def _post_attn(o, x, mod, wo, g, rwh, rwl, rb, *, layer, lat_seq, n_lat_batch, tm):
    n, d = x.shape
    n_exp = rwh.shape[0]
    tri = (jnp.arange(tm)[:, None] < jnp.arange(tm)[None, :]).astype(BF16)

    def mod_idx(i):
        return (layer, jnp.minimum(i * tm // lat_seq, n_lat_batch), 0, 0)

    tok = pl.BlockSpec((tm, d), lambda i: (i, 0))
    kt = pl.BlockSpec((TOP_K, tm), lambda i: (0, i))
    const2 = lambda i: (0, 0)
    return pl.pallas_call(
        _post_attn_kernel,
        grid=(n // tm,),
        in_specs=[tok, tok, pl.BlockSpec((None, None, N_MOD, d), mod_idx),
                  pl.BlockSpec((d, d), const2), pl.BlockSpec((1, d), const2),
                  pl.BlockSpec((n_exp, d), const2), pl.BlockSpec((n_exp, d), const2),
                  pl.BlockSpec((n_exp, 1), const2), pl.BlockSpec((tm, tm), const2)],
        out_specs=[tok, tok, kt, kt, kt, pl.BlockSpec((n_exp, LANES), const2)],
        out_shape=[jax.ShapeDtypeStruct((n, d), F32), jax.ShapeDtypeStruct((n, d), BF16),
                   jax.ShapeDtypeStruct((TOP_K, n), jnp.int32),
                   jax.ShapeDtypeStruct((TOP_K, n), F32),
                   jax.ShapeDtypeStruct((TOP_K, n), jnp.int32),
                   jax.ShapeDtypeStruct((n_exp, LANES), jnp.int32)],
        scratch_shapes=[pltpu.VMEM((n_exp, LANES), F32)],
        compiler_params=_cparams("arbitrary"),
        name=f"post_attn{layer}",
    )(o, x, mod, wo, g, rwh, rwl, rb, tri)


def _expert_kernel(te_ref, nu_ref, x_ref, wgu_ref, wd_ref, y_ref, *, f):
    @pl.when(pl.program_id(0) < nu_ref[0])
    def _():
        gu = _dot(x_ref[...], wgu_ref[...])
        gate, up = gu[:, :f], gu[:, f:]
        a = gate * _sigmoid(gate) * up
        y_ref[...] = _dot(a.astype(BF16), wd_ref[...]).astype(BF16)


def _experts(tile_expert, n_used, xb, wgu, wd, *, tmb):
    cap, d = xb.shape
    f = wd.shape[1]

    def row_idx(i, te, nu):
        return (jnp.minimum(i, nu[0] - 1), 0)

    return pl.pallas_call(
        functools.partial(_expert_kernel, f=f),
        grid_spec=pltpu.PrefetchScalarGridSpec(
            num_scalar_prefetch=2,
            grid=(cap // tmb,),
            in_specs=[pl.BlockSpec((tmb, d), row_idx),
                      pl.BlockSpec((None, d, 2 * f), lambda i, te, nu: (te[i], 0, 0)),
                      pl.BlockSpec((None, f, d), lambda i, te, nu: (te[i], 0, 0))],
            out_specs=pl.BlockSpec((tmb, d), row_idx)),
        out_shape=jax.ShapeDtypeStruct((cap, d), BF16),
        compiler_params=_cparams("arbitrary"),
        name="experts",
    )(tile_expert, n_used, xb, wgu, wd)


def _combine_kernel(x_ref, h_ref, yg_ref, gt_ref, mod_ref, wgu_ref, wd_ref, fg_ref, o_ref, *, f, final):
    gu = _dot(h_ref[...], wgu_ref[...])
    gate, up = gu[:, :f], gu[:, f:]
    acc = _dot((gate * _sigmoid(gate) * up).astype(BF16), wd_ref[...])
    for k in range(TOP_K):
        acc = acc + gt_ref[:, k:k + 1] * yg_ref[k].astype(F32)
    x2 = x_ref[...] + mod_ref[5:6, :] * acc
    o_ref[...] = _rms(x2, fg_ref[...]) if final else x2


def _combine(x, h, yg, gates_t, mod, wgu, wd, final_g, *, layer, lat_seq, n_lat_batch, tm, final):
    n, d = x.shape
    f = wd.shape[0]

    def mod_idx(i):
        return (layer, jnp.minimum(i * tm // lat_seq, n_lat_batch), 0, 0)

    tok = pl.BlockSpec((tm, d), lambda i: (i, 0))
    const2 = lambda i: (0, 0)
    return pl.pallas_call(
        functools.partial(_combine_kernel, f=f, final=final),
        grid=(n // tm,),
        in_specs=[tok, tok, pl.BlockSpec((TOP_K, tm, d), lambda i: (0, i, 0)),
                  pl.BlockSpec((tm, TOP_K), lambda i: (i, 0)),
                  pl.BlockSpec((None, None, N_MOD, d), mod_idx),
                  pl.BlockSpec((d, 2 * f), const2), pl.BlockSpec((f, d), const2),
                  pl.BlockSpec((1, d), const2)],
        out_specs=tok,
        out_shape=jax.ShapeDtypeStruct((n, d), F32),
        compiler_params=_cparams("arbitrary"),
        name=f"combine{layer}",
    )(x, h, yg, gates_t, mod, wgu, wd, final_g)


def _split_bf16(w):
    hi = w.astype(BF16)
    return hi, (w - hi.astype(F32)).astype(BF16)


def kernel(x_prompt, x_sample, cache_k, cache_v, c, c_ctx, ada_w, ada_b, norm1_g, norm2_g, w_qkv, w_o, na_rpb, diff_lambda, diff_subln_g, router_w, router_b, exp_w_gate, exp_w_up, exp_w_down, shared_w_gate, shared_w_up, shared_w_down, final_g):
    batch, seq, d = x_prompt.shape
    n_lat_batch, lat_seq, _ = x_sample.shape
    depth = w_qkv.shape[0]
    n_exp = router_w.shape[-1]
    n_lat = n_lat_batch * lat_seq
    n_ctx = batch * seq
    n = n_lat + n_ctx
    assert d % LANES == 0 and na_rpb.shape[1] * HEAD_DIM == d
    assert diff_lambda.shape[-1] == HEAD_DIM and lat_seq % GRID_W == 0
    assert lat_seq // GRID_W >= NA_ROWS and n_lat % seq == 0
    tm = _pick_tile(math.gcd(lat_seq, n_ctx), 512)
    tmb = 512
    scale = HEAD_DIM ** -0.5

    x = jnp.concatenate([x_sample.reshape(n_lat, d), x_prompt.reshape(n_ctx, d)], axis=0)
    mod_rows = -(-(n_lat_batch + 1) // 8) * 8
    cond = jnp.zeros((mod_rows, d), F32).at[:n_lat_batch].set(c).at[n_lat_batch].set(c_ctx)
    mod = _modulation(cond, ada_w, ada_b).reshape(depth, mod_rows, N_MOD, d)

    w_qkv_b = w_qkv.astype(BF16)
    w_o_b = w_o.astype(BF16)
    rope_tab = _rope_tables(lat_seq, tm)
    new_k, new_v = [], []
    for i in range(depth):
        is_diff = i % 2 == 1
        j = i // 2
        q, k, v, kf, vf = _qkv(x, mod, norm1_g.reshape(depth, 1, d), w_qkv_b,
                               rope_tab if is_diff else None, layer=i, n_lat=n_lat,
                               lat_seq=lat_seq, n_lat_batch=n_lat_batch, tm=tm, scale=scale)
        new_k.append(kf.reshape(batch, seq, d))
        new_v.append(vf.reshape(batch, seq, d))
        ck = cache_k[:, i].astype(BF16)
        cv = cache_v[:, i].astype(BF16)
        if not is_diff:
            o_lat = _na_attention(q, k, v, ck, cv, _na_bias_table(na_rpb[j], lat_seq // GRID_W),
                                  n_lat_batch=n_lat_batch, lat_seq=lat_seq)
            o_ctx = _ctx_dense(q, k, v, n_lat=n_lat, seq=seq)
        else:
            lam_init = 0.8 - 0.6 * math.exp(-0.3 * i)
            lp = diff_lambda[j].astype(F32)
            lam = (jnp.exp(jnp.sum(lp[0] * lp[1])) - jnp.exp(jnp.sum(lp[2] * lp[3]))
                   + lam_init).reshape(1)
            subg = diff_subln_g[j].reshape(1, LANES).astype(F32)
            o_lat = _lat_diff(lam, q, k, v, ck, cv, subg, n_lat_batch=n_lat_batch,
                              lat_seq=lat_seq, out_scale=1.0 - lam_init)
            o_ctx = _ctx_diff(lam, q, k, v, subg, n_lat=n_lat, seq=seq, out_scale=1.0 - lam_init)
        o = jnp.concatenate([o_lat, o_ctx], axis=0)

        rwh, rwl = _split_bf16(router_w[i].T)
        xmid, h2, sel, gates, rank, counts = _post_attn(
            o, x, mod, w_o_b[i], norm2_g[i].reshape(1, d), rwh, rwl,
            router_b[i].reshape(n_exp, 1).astype(F32), layer=i, lat_seq=lat_seq,
            n_lat_batch=n_lat_batch, tm=tm)

        counts = counts[:, 0]
        padded = (counts + tmb - 1) // tmb * tmb
        pad_end = jnp.cumsum(padded)
        pad_start = pad_end - padded
        dest = pad_start[sel] + rank
        n_tiles = -(-(n * TOP_K) // tmb) + n_exp
        cap = n_tiles * tmb
        tile_expert = jnp.minimum(
            jnp.searchsorted(pad_end, jnp.arange(n_tiles) * tmb, side='right'), n_exp - 1
        ).astype(jnp.int32)
        n_used = (pad_end[-1] // tmb).astype(jnp.int32).reshape(1)
        tok_ids = jnp.broadcast_to(jnp.arange(n, dtype=jnp.int32), (TOP_K, n))
        buf_tok = jnp.zeros((cap,), jnp.int32).at[dest.reshape(-1)].set(tok_ids.reshape(-1))
        xb = h2[buf_tok]
        wgu = jnp.concatenate([exp_w_gate[i], exp_w_up[i]], axis=-1).astype(BF16)
        yb = _experts(tile_expert, n_used, xb, wgu, exp_w_down[i].astype(BF16), tmb=tmb)
        yg = yb[dest]
        swgu = jnp.concatenate([shared_w_gate[i], shared_w_up[i]], axis=-1).astype(BF16)
        x = _combine(xmid, h2, yg, gates.T, mod, swgu, shared_w_down[i].astype(BF16),
                     final_g.reshape(1, d), layer=i, lat_seq=lat_seq, n_lat_batch=n_lat_batch,
                     tm=tm, final=i == depth - 1)

    y_sample = x[:n_lat].reshape(n_lat_batch, lat_seq, d)
    y_prompt = x[n_lat:].reshape(batch, seq, d)
    return (y_prompt, y_sample, jnp.stack(new_k, axis=1), jnp.stack(new_v, axis=1))
```

```python
import functools
import math

import jax
import jax.numpy as jnp
from jax import lax
from jax.experimental import pallas as pl
from jax.experimental.pallas import tpu as pltpu
from jax.experimental.pallas import tpu_sc as plsc

GRID_W = 64
NA_ROWS = 8
NA_COLS = 16
TOP_K = 8
ROUTED_SCALE = 2.5
ROPE_BASE = 10000.0
EPS = 1e-6
N_MOD = 6
HEAD_DIM = 64
LANES = 128
MASKED = -1e30
VMEM_LIMIT = 56 * 1024 * 1024

F32 = jnp.float32
BF16 = jnp.bfloat16


def _cparams(*sem):
    return pltpu.CompilerParams(dimension_semantics=sem, vmem_limit_bytes=VMEM_LIMIT)


def _dot(a, b):
    return jnp.dot(a, b, preferred_element_type=F32)


def _dot_nt(a, b):
    return lax.dot_general(a, b, (((1,), (1,)), ((), ())), preferred_element_type=F32)


def _sigmoid(x):
    return 1.0 / (1.0 + jnp.exp(-x))


def _rms(x, g):
    return x * lax.rsqrt(jnp.mean(x * x, axis=-1, keepdims=True) + EPS) * g


def _pack_rows(xb):
    half = xb.shape[1] // 2
    u = lax.bitcast_convert_type(xb.astype(F32), jnp.uint32)
    packed = (u[:, :half] >> 16) | (u[:, half:] & jnp.uint32(0xFFFF0000))
    return lax.bitcast_convert_type(packed, jnp.int32)


def _unpack_rows(p):
    u = lax.bitcast_convert_type(p, jnp.uint32)
    return (lax.bitcast_convert_type(u << 16, F32),
            lax.bitcast_convert_type(u & jnp.uint32(0xFFFF0000), F32))


def _pick_tile(n, pref):
    t = pref
    while n % t:
        t //= 2
    return t


def _mod_kernel(c_ref, w_ref, b_ref, o_ref):
    c = c_ref[...]
    o_ref[...] = _dot(c * _sigmoid(c), w_ref[...]) + b_ref[...]


def _modulation(cond, ada_w, ada_b):
    depth, d, n6 = ada_w.shape
    rows = cond.shape[0]
    tn = _pick_tile(n6, 512)
    return pl.pallas_call(
        _mod_kernel,
        grid=(depth, n6 // tn),
        in_specs=[
            pl.BlockSpec((rows, d), lambda l, j: (0, 0)),
            pl.BlockSpec((None, d, tn), lambda l, j: (l, 0, j)),
            pl.BlockSpec((None, 1, tn), lambda l, j: (l, 0, j)),
        ],
        out_specs=pl.BlockSpec((None, rows, tn), lambda l, j: (l, 0, j)),
        out_shape=jax.ShapeDtypeStruct((depth, rows, n6), F32),
        compiler_params=_cparams("arbitrary", "arbitrary"),
        name="modulation",
    )(cond, ada_w, ada_b.reshape(depth, 1, n6))


def _qkv_kernel(x_ref, mod_ref, g_ref, w_ref, *rest, d, scale, rope, n_lat_tiles):
    if rope:
        rope_ref, q_ref, k_ref, v_ref, kf_ref, vf_ref = rest
    else:
        q_ref, k_ref, v_ref, kf_ref, vf_ref = rest
    i = pl.program_id(0)
    x = x_ref[...]
    h = _rms(x, g_ref[...]) * (1.0 + mod_ref[1:2, :]) + mod_ref[0:1, :]
    hb = h.astype(BF16)

    def rot(t):
        if not rope:
            return t
        cols = []
        for j in range(d // LANES):
            tb = t[:, j * LANES:(j + 1) * LANES]
            cols.append(tb * rope_ref[0]
                        + pltpu.roll(tb, LANES - HEAD_DIM // 2, axis=1) * rope_ref[1]
                        + pltpu.roll(tb, HEAD_DIM // 2, axis=1) * rope_ref[2])
        return jnp.concatenate(cols, axis=1)

    q = _dot(hb, w_ref[:, 0:d])
    q_ref[...] = (rot(q) * scale).astype(BF16)
    k = _dot(hb, w_ref[:, d:2 * d])
    k_ref[...] = rot(k).astype(BF16)
    v = _dot(hb, w_ref[:, 2 * d:3 * d])
    v_ref[...] = v.astype(BF16)

    @pl.when(i >= n_lat_tiles)
    def _():
        kf_ref[...] = k
        vf_ref[...] = v


def _qkv(x, mod, g, w, rope_tab, *, layer, n_lat, lat_seq, n_lat_batch, tm, scale):
    n, d = x.shape
    n_ctx = n - n_lat
    n_lat_tiles = n_lat // tm
    rope = rope_tab is not None

    def mod_idx(i):
        return (layer, jnp.minimum(i * tm // lat_seq, n_lat_batch), 0, 0)

    def ctx_idx(i):
        return (jnp.maximum(i - n_lat_tiles, 0), 0)

    in_specs = [
        pl.BlockSpec((tm, d), lambda i: (i, 0)),
        pl.BlockSpec((None, None, N_MOD, d), mod_idx),
        pl.BlockSpec((None, 1, d), lambda i: (layer, 0, 0)),
        pl.BlockSpec((None, d, 3 * d), lambda i: (layer, 0, 0)),
    ]
    args = [x, mod, g, w]
    if rope:
        seq_tiles = lat_seq // tm
        in_specs.append(pl.BlockSpec(
            (3, tm, LANES),
            lambda i: (0, jnp.where(i < n_lat_tiles, i % seq_tiles, seq_tiles), 0)))
        args.append(rope_tab)
    tok = pl.BlockSpec((tm, d), lambda i: (i, 0))
    return pl.pallas_call(
        functools.partial(_qkv_kernel, d=d, scale=scale, rope=rope, n_lat_tiles=n_lat_tiles),
        grid=(n // tm,),
        in_specs=in_specs,
        out_specs=[tok, tok, tok, pl.BlockSpec((tm, d), ctx_idx), pl.BlockSpec((tm, d), ctx_idx)],
        out_shape=[jax.ShapeDtypeStruct((n, d), BF16)] * 3
        + [jax.ShapeDtypeStruct((n_ctx, d), F32)] * 2,
        compiler_params=_cparams("arbitrary"),
        name=f"qkv{layer}",
    )(*args)


def _rope_tables(lat_seq, tm):
    n_freq = HEAD_DIM // 4
    inv = ROPE_BASE ** (-jnp.arange(n_freq, dtype=F32) / n_freq)
    pos = jnp.arange(lat_seq)
    row = (pos // GRID_W).astype(F32)
    col = (pos % GRID_W).astype(F32)
    ang = jnp.concatenate([row[:, None] * inv, col[:, None] * inv], axis=-1)
    reps = LANES // (HEAD_DIM // 2)
    cos = jnp.tile(jnp.cos(ang), (1, reps))
    sin = jnp.tile(jnp.sin(ang), (1, reps))
    first_half = (jnp.arange(LANES) % HEAD_DIM) < HEAD_DIM // 2
    s_next = jnp.where(first_half, -sin, 0.0)
    s_prev = jnp.where(first_half, 0.0, sin)
    ident = jnp.stack([jnp.ones((tm, LANES), F32), jnp.zeros((tm, LANES), F32),
                       jnp.zeros((tm, LANES), F32)])
    return jnp.concatenate([jnp.stack([cos, s_next, s_prev]), ident], axis=1)


def _lane_lo():
    return lax.broadcasted_iota(jnp.int32, (1, LANES), 1) < HEAD_DIM


def _ctx_dense_kernel(q_ref, k_ref, v_ref, o_ref, *, d):
    lo = _lane_lo()
    for j in range(d // LANES):
        blk = slice(j * LANES, (j + 1) * LANES)
        q2, k2, v2 = q_ref[:, blk], k_ref[:, blk], v_ref[:, blk]
        outs = []
        for qm in (jnp.where(lo, q2, 0), jnp.where(lo, 0, q2)):
            s = _dot_nt(qm, k2)
            p = jnp.exp(s - jnp.max(s, axis=-1, keepdims=True))
            outs.append(_dot(p.astype(BF16), v2) / jnp.sum(p, axis=-1, keepdims=True))
        o_ref[:, blk] = jnp.where(lo, outs[0], outs[1]).astype(BF16)


def _ctx_dense(q, k, v, *, n_lat, seq):
    n, d = q.shape
    off = n_lat // seq
    spec = pl.BlockSpec((seq, d), lambda b: (off + b, 0))
    return pl.pallas_call(
        functools.partial(_ctx_dense_kernel, d=d),
        grid=((n - n_lat) // seq,),
        in_specs=[spec, spec, spec],
        out_specs=pl.BlockSpec((seq, d), lambda b: (b, 0)),
        out_shape=jax.ShapeDtypeStruct((n - n_lat, d), BF16),
        compiler_params=_cparams("arbitrary"),
        name="ctx_dense_attn",
    )(q, k, v)


def _subln(o, g_ref, out_scale):
    return _rms(o, g_ref[...]) * out_scale


def _ctx_diff_kernel(lam_ref, q_ref, k_ref, v_ref, g_ref, o_ref, *, d, out_scale):
    lo = _lane_lo()
    lam = lam_ref[0]
    for j in range(d // LANES):
        blk = slice(j * LANES, (j + 1) * LANES)
        q2, k2, v2 = q_ref[:, blk], k_ref[:, blk], v_ref[:, blk]
        ps = []
        for qm in (jnp.where(lo, q2, 0), jnp.where(lo, 0, q2)):
            s = _dot_nt(qm, k2)
            p = jnp.exp(s - jnp.max(s, axis=-1, keepdims=True))
            ps.append(p / jnp.sum(p, axis=-1, keepdims=True))
        o = _dot((ps[0] - lam * ps[1]).astype(BF16), v2)
        o_ref[:, blk] = _subln(o, g_ref, out_scale).astype(BF16)


def _ctx_diff(lam, q, k, v, subg, *, n_lat, seq, out_scale):
    n, d = q.shape
    off = n_lat // seq
    spec = pl.BlockSpec((seq, d), lambda b: (off + b, 0))
    return pl.pallas_call(
        functools.partial(_ctx_diff_kernel, d=d, out_scale=out_scale),
        grid=((n - n_lat) // seq,),
        in_specs=[pl.BlockSpec(memory_space=pltpu.SMEM), spec, spec, spec,
                  pl.BlockSpec((1, LANES), lambda b: (0, 0))],
        out_specs=pl.BlockSpec((seq, d), lambda b: (b, 0)),
        out_shape=jax.ShapeDtypeStruct((n - n_lat, d), BF16),
        compiler_params=_cparams("arbitrary"),
        name="ctx_diff_attn",
    )(lam, q, k, v, subg)


def _na_kernel(q_ref, k_ref, v_ref, ck_ref, cv_ref, bias_ref, o_ref, *, d, rows):
    lo = _lane_lo()
    r = pl.program_id(1)
    win = NA_ROWS * GRID_W
    start = pl.multiple_of(jnp.clip(r - NA_ROWS // 2, 0, rows - NA_ROWS) * GRID_W, GRID_W)
    for j in range(d // LANES):
        blk = slice(j * LANES, (j + 1) * LANES)
        q2 = q_ref[:, blk]
        kw = k_ref[pl.ds(start, win), blk]
        vw = v_ref[pl.ds(start, win), blk]
        ck2, cv2 = ck_ref[:, blk], cv_ref[:, blk]
        outs = []
        for hh, qm in enumerate((jnp.where(lo, q2, 0), jnp.where(lo, 0, q2))):
            s_loc = _dot_nt(qm, kw) + bias_ref[2 * j + hh]
            s_ctx = _dot_nt(qm, ck2)
            m = jnp.maximum(jnp.max(s_loc, axis=-1, keepdims=True),
                            jnp.max(s_ctx, axis=-1, keepdims=True))
            p_loc = jnp.exp(s_loc - m)
            p_ctx = jnp.exp(s_ctx - m)
            l = jnp.sum(p_loc, axis=-1, keepdims=True) + jnp.sum(p_ctx, axis=-1, keepdims=True)
            outs.append((_dot(p_loc.astype(BF16), vw) + _dot(p_ctx.astype(BF16), cv2)) / l)
        o_ref[:, blk] = jnp.where(lo, outs[0], outs[1]).astype(BF16)


def _na_bias_table(rpb, rows):
    h = rpb.shape[0]
    delta = jnp.arange(NA_ROWS)
    kr = jnp.arange(NA_ROWS)
    c = jnp.arange(GRID_W)
    kc = jnp.arange(GRID_W)
    off_r = kr[None, :] - delta[:, None] + (NA_ROWS - 1)
    c0 = jnp.clip(c - NA_COLS // 2, 0, GRID_W - NA_COLS)
    off_c = kc[None, :] - c[:, None] + (NA_COLS - 1)
    inside = (kc[None, :] >= c0[:, None]) & (kc[None, :] < c0[:, None] + NA_COLS)
    off_r = jnp.clip(off_r, 0, 2 * NA_ROWS - 2)
    off_c = jnp.clip(off_c, 0, 2 * NA_COLS - 2)
    t = rpb.astype(F32)[:, off_r[:, None, :, None], off_c[None, :, None, :]]
    t = jnp.where(inside[None, None, :, None, :], t, MASKED)
    return t.transpose(1, 0, 2, 3, 4).reshape(NA_ROWS, h, GRID_W, NA_ROWS * GRID_W)


def _na_attention(q, k, v, ck, cv, bias, *, n_lat_batch, lat_seq):
    n, d = q.shape
    rows = lat_seq // GRID_W
    past = ck.shape[1]
    n_heads = bias.shape[1]

    def delta_idx(b, r):
        return (r - jnp.clip(r - NA_ROWS // 2, 0, rows - NA_ROWS), 0, 0, 0)

    kv_spec = pl.BlockSpec((lat_seq, d), lambda b, r: (b, 0))
    c_spec = pl.BlockSpec((None, past, d), lambda b, r: (b, 0, 0))
    return pl.pallas_call(
        functools.partial(_na_kernel, d=d, rows=rows),
        grid=(n_lat_batch, rows),
        in_specs=[pl.BlockSpec((GRID_W, d), lambda b, r: (b * rows + r, 0)), kv_spec, kv_spec,
                  c_spec, c_spec,
                  pl.BlockSpec((None, n_heads, GRID_W, NA_ROWS * GRID_W), delta_idx)],
        out_specs=pl.BlockSpec((GRID_W, d), lambda b, r: (b * rows + r, 0)),
        out_shape=jax.ShapeDtypeStruct((n_lat_batch * lat_seq, d), BF16),
        compiler_params=_cparams("arbitrary", "arbitrary"),
        name="na_attn",
    )(q, k, v, ck, cv, bias)


def _lat_diff_kernel(lam_ref, q_ref, k_ref, v_ref, ck_ref, cv_ref, g_ref, o_ref, *, tk, out_scale):
    lo = _lane_lo()
    lam = lam_ref[0]
    q2 = q_ref[...]
    tq = q2.shape[0]
    qa, qb = jnp.where(lo, q2, 0), jnp.where(lo, 0, q2)

    def update(state, qm, kt, vt):
        m, l, acc = state
        s = _dot_nt(qm, kt)
        m_new = jnp.maximum(m, jnp.max(s, axis=-1, keepdims=True))
        alpha = jnp.exp(m - m_new)
        p = jnp.exp(s - m_new)
        l = alpha * l + jnp.sum(p, axis=-1, keepdims=True)
        acc = alpha * acc + _dot(p.astype(BF16), vt)
        return m_new, l, acc

    def body(i, carry):
        off = pl.multiple_of(i * tk, tk)
        kt = k_ref[pl.ds(off, tk), :]
        vt = v_ref[pl.ds(off, tk), :]
        return update(carry[0], qa, kt, vt), update(carry[1], qb, kt, vt)

    init = (jnp.full((tq, 1), -jnp.inf, F32), jnp.zeros((tq, 1), F32), jnp.zeros((tq, LANES), F32))
    s1, s2 = lax.fori_loop(0, k_ref.shape[0] // tk, body, (init, init))
    s1 = update(s1, qa, ck_ref[...], cv_ref[...])
    s2 = update(s2, qb, ck_ref[...], cv_ref[...])
    o = s1[2] / s1[1] - lam * (s2[2] / s2[1])
    o_ref[...] = _subln(o, g_ref, out_scale).astype(BF16)


def _lat_diff(lam, q, k, v, ck, cv, subg, *, n_lat_batch, lat_seq, out_scale):
    n, d = q.shape
    past = ck.shape[1]
    tq = _pick_tile(lat_seq, 512)
    tk = _pick_tile(lat_seq, 512)
    qt = lat_seq // tq
    kv_spec = pl.BlockSpec((lat_seq, LANES), lambda b, h, i: (b, h))
    c_spec = pl.BlockSpec((None, past, LANES), lambda b, h, i: (b, 0, h))
    q_spec = pl.BlockSpec((tq, LANES), lambda b, h, i: (b * qt + i, h))
    return pl.pallas_call(
        functools.partial(_lat_diff_kernel, tk=tk, out_scale=out_scale),
        grid=(n_lat_batch, d // LANES, qt),
        in_specs=[pl.BlockSpec(memory_space=pltpu.SMEM), q_spec, kv_spec, kv_spec, c_spec, c_spec,
                  pl.BlockSpec((1, LANES), lambda b, h, i: (0, 0))],
        out_specs=q_spec,
        out_shape=jax.ShapeDtypeStruct((n_lat_batch * lat_seq, d), BF16),
        compiler_params=_cparams("arbitrary", "arbitrary", "arbitrary"),
        name="lat_diff_attn",
    )(lam, q, k, v, ck, cv, subg)


def _post_attn_kernel(o_ref, x_ref, mod_ref, wo_ref, g_ref, rwh_ref, rwl_ref, rb_ref, tri_ref,
                      xmid_ref, h_ref, sel_ref, gate_ref, rank_ref, cnt_ref, carry_ref):
    i = pl.program_id(0)

    @pl.when(i == 0)
    def _():
        carry_ref[...] = jnp.zeros_like(carry_ref)

    x1 = x_ref[...] + mod_ref[2:3, :] * _dot(o_ref[...], wo_ref[...])
    xmid_ref[...] = x1
    h = _rms(x1, g_ref[...]) * (1.0 + mod_ref[4:5, :]) + mod_ref[3:4, :]
    hb = h.astype(BF16)
    h_ref[...] = _pack_rows(hb)
    h_lo = (h - hb.astype(F32)).astype(BF16)
    logits = _dot_nt(rwh_ref[...], hb) + (_dot_nt(rwl_ref[...], hb) + _dot_nt(rwh_ref[...], h_lo))
    scores = _sigmoid(logits)
    n_exp, tm = scores.shape
    biased = scores + rb_ref[...]
    row = lax.broadcasted_iota(jnp.int32, (n_exp, tm), 0).astype(F32)
    total = jnp.zeros((n_exp, tm), F32)
    sels, gates = [], []
    for _ in range(TOP_K):
        m = jnp.max(biased, axis=0, keepdims=True)
        idx = jnp.min(jnp.where(biased == m, row, float(n_exp)), axis=0, keepdims=True)
        hit = row == idx
        gates.append(jnp.sum(jnp.where(hit, scores, 0.0), axis=0, keepdims=True))
        biased = jnp.where(hit, -jnp.inf, biased)
        total = total + jnp.where(hit, 1.0, 0.0)
        sels.append(idx)
    denom = gates[0]
    for g in gates[1:]:
        denom = denom + g
    before = _dot(total.astype(BF16), tri_ref[...]) + carry_ref[:, 0:1]
    for k in range(TOP_K):
        sel_ref[k:k + 1, :] = sels[k].astype(jnp.int32)
        gate_ref[k:k + 1, :] = gates[k] / denom * ROUTED_SCALE
        rank_ref[k:k + 1, :] = jnp.sum(jnp.where(row == sels[k], before, 0.0), axis=0,
                                       keepdims=True).astype(jnp.int32)
    carry_ref[...] = carry_ref[...] + jnp.sum(total, axis=1, keepdims=True)
    cnt_ref[...] = carry_ref[...].astype(jnp.int32)


def _post_attn(o, x, mod, wo, g, rwh, rwl, rb, *, layer, lat_seq, n_lat_batch, tm):
    n, d = x.shape
    n_exp = rwh.shape[0]
    tri = (jnp.arange(tm)[:, None] < jnp.arange(tm)[None, :]).astype(BF16)

    def mod_idx(i):
        return (layer, jnp.minimum(i * tm // lat_seq, n_lat_batch), 0, 0)

    tok = pl.BlockSpec((tm, d), lambda i: (i, 0))
    tok_p = pl.BlockSpec((tm, d // 2), lambda i: (i, 0))
    kt = pl.BlockSpec((TOP_K, tm), lambda i: (0, i))
    const2 = lambda i: (0, 0)
    return pl.pallas_call(
        _post_attn_kernel,
        grid=(n // tm,),
        in_specs=[tok, tok, pl.BlockSpec((None, None, N_MOD, d), mod_idx),
                  pl.BlockSpec((d, d), const2), pl.BlockSpec((1, d), const2),
                  pl.BlockSpec((n_exp, d), const2), pl.BlockSpec((n_exp, d), const2),
                  pl.BlockSpec((n_exp, 1), const2), pl.BlockSpec((tm, tm), const2)],
        out_specs=[tok, tok_p, kt, kt, kt, pl.BlockSpec((n_exp, LANES), const2)],
        out_shape=[jax.ShapeDtypeStruct((n, d), F32), jax.ShapeDtypeStruct((n, d // 2), jnp.int32),
                   jax.ShapeDtypeStruct((TOP_K, n), jnp.int32),
                   jax.ShapeDtypeStruct((TOP_K, n), F32),
                   jax.ShapeDtypeStruct((TOP_K, n), jnp.int32),
                   jax.ShapeDtypeStruct((n_exp, LANES), jnp.int32)],
        scratch_shapes=[pltpu.VMEM((n_exp, LANES), F32)],
        compiler_params=_cparams("arbitrary"),
        name=f"post_attn{layer}",
    )(o, x, mod, wo, g, rwh, rwl, rb, tri)


def _expert_kernel(te_ref, nu_ref, x_ref, wgu_ref, wd_ref, y_ref, *, f):
    @pl.when(pl.program_id(0) < nu_ref[0])
    def _():
        x = jnp.concatenate(_unpack_rows(x_ref[...]), axis=1).astype(BF16)
        gu = _dot(x, wgu_ref[...])
        gate, up = gu[:, :f], gu[:, f:]
        a = gate * _sigmoid(gate) * up
        y_ref[...] = _pack_rows(_dot(a.astype(BF16), wd_ref[...]).astype(BF16))


def _experts(tile_expert, n_used, xb, wgu, wd, *, tmb):
    cap, dp = xb.shape
    f, d = wd.shape[1:]

    def row_idx(i, te, nu):
        return (jnp.minimum(i, nu[0] - 1), 0)

    return pl.pallas_call(
        functools.partial(_expert_kernel, f=f),
        grid_spec=pltpu.PrefetchScalarGridSpec(
            num_scalar_prefetch=2,
            grid=(cap // tmb,),
            in_specs=[pl.BlockSpec((tmb, dp), row_idx),
                      pl.BlockSpec((None, d, 2 * f), lambda i, te, nu: (te[i], 0, 0)),
                      pl.BlockSpec((None, f, d), lambda i, te, nu: (te[i], 0, 0))],
            out_specs=pl.BlockSpec((tmb, dp), row_idx)),
        out_shape=jax.ShapeDtypeStruct((cap, dp), jnp.int32),
        compiler_params=_cparams("arbitrary"),
        name="experts",
    )(tile_expert, n_used, xb, wgu, wd)


def _combine_kernel(x_ref, h_ref, yg_ref, gt_ref, mod_ref, wgu_ref, wd_ref, fg_ref, o_ref, *, f, final):
    h = jnp.concatenate(_unpack_rows(h_ref[...]), axis=1).astype(BF16)
    gu = _dot(h, wgu_ref[...])
    gate, up = gu[:, :f], gu[:, f:]
    shared = _dot((gate * _sigmoid(gate) * up).astype(BF16), wd_ref[...])
    lo = hi = None
    for k in range(TOP_K):
        y_lo, y_hi = _unpack_rows(yg_ref[k])
        g = gt_ref[:, k:k + 1]
        lo = g * y_lo if lo is None else lo + g * y_lo
        hi = g * y_hi if hi is None else hi + g * y_hi
    acc = shared + jnp.concatenate([lo, hi], axis=1)
    x2 = x_ref[...] + mod_ref[5:6, :] * acc
    o_ref[...] = _rms(x2, fg_ref[...]) if final else x2


def _combine(x, h, yg, gates_t, mod, wgu, wd, final_g, *, layer, lat_seq, n_lat_batch, tm, final):
    n, d = x.shape
    f = wd.shape[0]

    def mod_idx(i):
        return (layer, jnp.minimum(i * tm // lat_seq, n_lat_batch), 0, 0)

    tok = pl.BlockSpec((tm, d), lambda i: (i, 0))
    tok_p = pl.BlockSpec((tm, d // 2), lambda i: (i, 0))
    const2 = lambda i: (0, 0)
    return pl.pallas_call(
        functools.partial(_combine_kernel, f=f, final=final),
        grid=(n // tm,),
        in_specs=[tok, tok_p, pl.BlockSpec((TOP_K, tm, d // 2), lambda i: (0, i, 0)),
                  pl.BlockSpec((tm, TOP_K), lambda i: (i, 0)),
                  pl.BlockSpec((None, None, N_MOD, d), mod_idx),
                  pl.BlockSpec((d, 2 * f), const2), pl.BlockSpec((f, d), const2),
                  pl.BlockSpec((1, d), const2)],
        out_specs=tok,
        out_shape=jax.ShapeDtypeStruct((n, d), F32),
        compiler_params=_cparams("arbitrary"),
        name=f"combine{layer}",
    )(x, h, yg, gates_t, mod, wgu, wd, final_g)


SC_CHUNK = 128


def _sc_workers():
    info = plsc.get_sparse_core_info()
    return info.num_cores, info.num_subcores


def _sc_dispatch(rows, dest, cap):
    n, w = rows.shape
    nc, ns = _sc_workers()
    per_w = n // (nc * ns)
    assert per_w * nc * ns == n and per_w % SC_CHUNK == 0
    mesh = plsc.VectorSubcoreMesh(core_axis_name="c", subcore_axis_name="s")

    @functools.partial(
        pl.kernel, mesh=mesh, out_type=jax.ShapeDtypeStruct((cap, w), rows.dtype),
        scratch_types=[pltpu.VMEM((TOP_K, SC_CHUNK), jnp.int32),
                       pltpu.VMEM((SC_CHUNK, w), rows.dtype)],
        name="sc_dispatch")
    def run(rows_hbm, dest_hbm, out_hbm, idx_v, rows_v):
        wid = lax.axis_index("s") * nc + lax.axis_index("c")

        @pl.loop(0, per_w // SC_CHUNK)
        def _(ci):
            base = pl.multiple_of(wid * per_w + ci * SC_CHUNK, SC_CHUNK)
            pltpu.sync_copy(dest_hbm.at[wid * (per_w // SC_CHUNK) + ci], idx_v)
            pltpu.sync_copy(rows_hbm.at[pl.ds(base, SC_CHUNK)], rows_v)
            for k in range(TOP_K):
                pltpu.sync_copy(rows_v, out_hbm.at[idx_v.at[k]])

    return run(rows, dest)


def _sc_collect(table, dest):
    n_chunks, k_top, _ = dest.shape
    n = n_chunks * SC_CHUNK
    w = table.shape[1]
    nc, ns = _sc_workers()
    per_w = n // (nc * ns)
    assert per_w * nc * ns == n and per_w % SC_CHUNK == 0
    mesh = plsc.VectorSubcoreMesh(core_axis_name="c", subcore_axis_name="s")

    @functools.partial(
        pl.kernel, mesh=mesh, out_type=jax.ShapeDtypeStruct((k_top, n, w), table.dtype),
        scratch_types=[pltpu.VMEM((TOP_K, SC_CHUNK), jnp.int32),
                       pltpu.VMEM((SC_CHUNK, w), table.dtype)],
        name="sc_collect")
    def run(table_hbm, dest_hbm, out_hbm, idx_v, rows_v):
        wid = lax.axis_index("s") * nc + lax.axis_index("c")

        @pl.loop(0, per_w // SC_CHUNK)
        def _(ci):
            base = pl.multiple_of(wid * per_w + ci * SC_CHUNK, SC_CHUNK)
            pltpu.sync_copy(dest_hbm.at[wid * (per_w // SC_CHUNK) + ci], idx_v)
            for k in range(TOP_K):
                pltpu.sync_copy(table_hbm.at[idx_v.at[k]], rows_v)
                pltpu.sync_copy(rows_v, out_hbm.at[k, pl.ds(base, SC_CHUNK)])

    return run(table, dest)


def _split_bf16(w):
    hi = w.astype(BF16)
    return hi, (w - hi.astype(F32)).astype(BF16)


def kernel(x_prompt, x_sample, cache_k, cache_v, c, c_ctx, ada_w, ada_b, norm1_g, norm2_g, w_qkv, w_o, na_rpb, diff_lambda, diff_subln_g, router_w, router_b, exp_w_gate, exp_w_up, exp_w_down, shared_w_gate, shared_w_up, shared_w_down, final_g):
    batch, seq, d = x_prompt.shape
    n_lat_batch, lat_seq, _ = x_sample.shape
    depth = w_qkv.shape[0]
    n_exp = router_w.shape[-1]
    n_lat = n_lat_batch * lat_seq
    n_ctx = batch * seq
    n = n_lat + n_ctx
    assert d % LANES == 0 and na_rpb.shape[1] * HEAD_DIM == d
    assert diff_lambda.shape[-1] == HEAD_DIM and lat_seq % GRID_W == 0
    assert lat_seq // GRID_W >= NA_ROWS and n_lat % seq == 0
    tm = _pick_tile(math.gcd(lat_seq, n_ctx), 512)
    tmb = 512
    scale = HEAD_DIM ** -0.5

    x = jnp.concatenate([x_sample.reshape(n_lat, d), x_prompt.reshape(n_ctx, d)], axis=0)
    mod_rows = -(-(n_lat_batch + 1) // 8) * 8
    cond = jnp.zeros((mod_rows, d), F32).at[:n_lat_batch].set(c).at[n_lat_batch].set(c_ctx)
    mod = _modulation(cond, ada_w, ada_b).reshape(depth, mod_rows, N_MOD, d)

    w_qkv_b = w_qkv.astype(BF16)
    w_o_b = w_o.astype(BF16)
    rope_tab = _rope_tables(lat_seq, tm)
    new_k, new_v = [], []
    for i in range(depth):
        is_diff = i % 2 == 1
        j = i // 2
        q, k, v, kf, vf = _qkv(x, mod, norm1_g.reshape(depth, 1, d), w_qkv_b,
                               rope_tab if is_diff else None, layer=i, n_lat=n_lat,
                               lat_seq=lat_seq, n_lat_batch=n_lat_batch, tm=tm, scale=scale)
        new_k.append(kf.reshape(batch, seq, d))
        new_v.append(vf.reshape(batch, seq, d))
        ck = cache_k[:, i].astype(BF16)
        cv = cache_v[:, i].astype(BF16)
        if not is_diff:
            o_lat = _na_attention(q, k, v, ck, cv, _na_bias_table(na_rpb[j], lat_seq // GRID_W),
                                  n_lat_batch=n_lat_batch, lat_seq=lat_seq)
            o_ctx = _ctx_dense(q, k, v, n_lat=n_lat, seq=seq)
        else:
            lam_init = 0.8 - 0.6 * math.exp(-0.3 * i)
            lp = diff_lambda[j].astype(F32)
            lam = (jnp.exp(jnp.sum(lp[0] * lp[1])) - jnp.exp(jnp.sum(lp[2] * lp[3]))
                   + lam_init).reshape(1)
            subg = diff_subln_g[j].reshape(1, LANES).astype(F32)
            o_lat = _lat_diff(lam, q, k, v, ck, cv, subg, n_lat_batch=n_lat_batch,
                              lat_seq=lat_seq, out_scale=1.0 - lam_init)
            o_ctx = _ctx_diff(lam, q, k, v, subg, n_lat=n_lat, seq=seq, out_scale=1.0 - lam_init)
        o = jnp.concatenate([o_lat, o_ctx], axis=0)

        rwh, rwl = _split_bf16(router_w[i].T)
        xmid, h2, sel, gates, rank, counts = _post_attn(
            o, x, mod, w_o_b[i], norm2_g[i].reshape(1, d), rwh, rwl,
            router_b[i].reshape(n_exp, 1).astype(F32), layer=i, lat_seq=lat_seq,
            n_lat_batch=n_lat_batch, tm=tm)

        counts = counts[:, 0]
        padded = (counts + tmb - 1) // tmb * tmb
        pad_end = jnp.cumsum(padded)
        pad_start = pad_end - padded
        dest = pad_start[sel] + rank
        n_tiles = -(-(n * TOP_K) // tmb) + n_exp
        cap = n_tiles * tmb
        tile_expert = jnp.minimum(
            jnp.searchsorted(pad_end, jnp.arange(n_tiles) * tmb, side='right'), n_exp - 1
        ).astype(jnp.int32)
        n_used = (pad_end[-1] // tmb).astype(jnp.int32).reshape(1)
        dest = dest.reshape(TOP_K, n // SC_CHUNK, SC_CHUNK).transpose(1, 0, 2)
        xb = _sc_dispatch(h2, dest, cap)
        wgu = jnp.concatenate([exp_w_gate[i], exp_w_up[i]], axis=-1).astype(BF16)
        yb = _experts(tile_expert, n_used, xb, wgu, exp_w_down[i].astype(BF16), tmb=tmb)
        yg = _sc_collect(yb, dest)
        swgu = jnp.concatenate([shared_w_gate[i], shared_w_up[i]], axis=-1).astype(BF16)
        x = _combine(xmid, h2, yg, gates.T, mod, swgu, shared_w_down[i].astype(BF16),
                     final_g.reshape(1, d), layer=i, lat_seq=lat_seq, n_lat_batch=n_lat_batch,
                     tm=tm, final=i == depth - 1)

    y_sample = x[:n_lat].reshape(n_lat_batch, lat_seq, d)
    y_prompt = x[n_lat:].reshape(batch, seq, d)
    return (y_prompt, y_sample, jnp.stack(new_k, axis=1), jnp.stack(new_v, axis=1))
```

```python
import functools
import math

import jax
import jax.numpy as jnp
from jax import lax
from jax.experimental import pallas as pl
from jax.experimental.pallas import tpu as pltpu
from jax.experimental.pallas import tpu_sc as plsc

GRID_W = 64
NA_ROWS = 8
NA_COLS = 16
TOP_K = 8
ROUTED_SCALE = 2.5
ROPE_BASE = 10000.0
EPS = 1e-6
N_MOD = 6
HEAD_DIM = 64
LANES = 128
LOG2E = math.log2(math.e)
MASKED = -1e30
VMEM_LIMIT = 56 * 1024 * 1024

F32 = jnp.float32
BF16 = jnp.bfloat16


def _cparams(*sem):
    return pltpu.CompilerParams(dimension_semantics=sem, vmem_limit_bytes=VMEM_LIMIT)


def _dot(a, b):
    return jnp.dot(a, b, preferred_element_type=F32)


def _dot_nt(a, b):
    return lax.dot_general(a, b, (((1,), (1,)), ((), ())), preferred_element_type=F32)


def _sigmoid(x):
    return 1.0 / (1.0 + jnp.exp(-x))


def _rms(x, g):
    return x * lax.rsqrt(jnp.mean(x * x, axis=-1, keepdims=True) + EPS) * g


def _pack_rows(xb):
    half = xb.shape[1] // 2
    u = lax.bitcast_convert_type(xb.astype(F32), jnp.uint32)
    packed = (u[:, :half] >> 16) | (u[:, half:] & jnp.uint32(0xFFFF0000))
    return lax.bitcast_convert_type(packed, jnp.int32)


def _unpack_rows(p):
    u = lax.bitcast_convert_type(p, jnp.uint32)
    return (lax.bitcast_convert_type(u << 16, F32),
            lax.bitcast_convert_type(u & jnp.uint32(0xFFFF0000), F32))


def _pick_tile(n, pref):
    t = pref
    while n % t:
        t //= 2
    return t


def _mod_kernel(c_ref, w_ref, b_ref, o_ref):
    c = c_ref[...]
    o_ref[...] = _dot(c * _sigmoid(c), w_ref[...]) + b_ref[...]


def _modulation(cond, ada_w, ada_b):
    depth, d, n6 = ada_w.shape
    rows = cond.shape[0]
    tn = _pick_tile(n6, 512)
    return pl.pallas_call(
        _mod_kernel,
        grid=(depth, n6 // tn),
        in_specs=[
            pl.BlockSpec((rows, d), lambda l, j: (0, 0)),
            pl.BlockSpec((None, d, tn), lambda l, j: (l, 0, j)),
            pl.BlockSpec((None, 1, tn), lambda l, j: (l, 0, j)),
        ],
        out_specs=pl.BlockSpec((None, rows, tn), lambda l, j: (l, 0, j)),
        out_shape=jax.ShapeDtypeStruct((depth, rows, n6), F32),
        compiler_params=_cparams("arbitrary", "arbitrary"),
        name="modulation",
    )(cond, ada_w, ada_b.reshape(depth, 1, n6))


def _qkv_kernel(x_ref, mod_ref, g_ref, w_ref, *rest, d, scale, rope, n_lat_tiles):
    if rope:
        rope_ref, q_ref, k_ref, v_ref, kf_ref, vf_ref = rest
    else:
        q_ref, k_ref, v_ref, kf_ref, vf_ref = rest
    i = pl.program_id(0)
    x = x_ref[...]
    h = _rms(x, g_ref[...]) * (1.0 + mod_ref[1:2, :]) + mod_ref[0:1, :]
    hb = h.astype(BF16)

    def rot(t):
        if not rope:
            return t
        cols = []
        for j in range(d // LANES):
            tb = t[:, j * LANES:(j + 1) * LANES]
            cols.append(tb * rope_ref[0]
                        + pltpu.roll(tb, LANES - HEAD_DIM // 2, axis=1) * rope_ref[1]
                        + pltpu.roll(tb, HEAD_DIM // 2, axis=1) * rope_ref[2])
        return jnp.concatenate(cols, axis=1)

    q = _dot(hb, w_ref[:, 0:d])
    q_ref[...] = (rot(q) * scale).astype(BF16)
    k = _dot(hb, w_ref[:, d:2 * d])
    k_ref[...] = rot(k).astype(BF16)
    v = _dot(hb, w_ref[:, 2 * d:3 * d])
    v_ref[...] = v.astype(BF16)

    @pl.when(i >= n_lat_tiles)
    def _():
        kf_ref[...] = k
        vf_ref[...] = v


def _qkv(x, mod, g, w, rope_tab, *, layer, n_lat, lat_seq, n_lat_batch, tm, scale):
    n, d = x.shape
    n_ctx = n - n_lat
    n_lat_tiles = n_lat // tm
    rope = rope_tab is not None

    def mod_idx(i):
        return (layer, jnp.minimum(i * tm // lat_seq, n_lat_batch), 0, 0)

    def ctx_idx(i):
        return (jnp.maximum(i - n_lat_tiles, 0), 0)

    in_specs = [
        pl.BlockSpec((tm, d), lambda i: (i, 0)),
        pl.BlockSpec((None, None, N_MOD, d), mod_idx),
        pl.BlockSpec((None, 1, d), lambda i: (layer, 0, 0)),
        pl.BlockSpec((None, d, 3 * d), lambda i: (layer, 0, 0)),
    ]
    args = [x, mod, g, w]
    if rope:
        seq_tiles = lat_seq // tm
        in_specs.append(pl.BlockSpec(
            (3, tm, LANES),
            lambda i: (0, jnp.where(i < n_lat_tiles, i % seq_tiles, seq_tiles), 0)))
        args.append(rope_tab)
    tok = pl.BlockSpec((tm, d), lambda i: (i, 0))
    return pl.pallas_call(
        functools.partial(_qkv_kernel, d=d, scale=scale, rope=rope, n_lat_tiles=n_lat_tiles),
        grid=(n // tm,),
        in_specs=in_specs,
        out_specs=[tok, tok, tok, pl.BlockSpec((tm, d), ctx_idx), pl.BlockSpec((tm, d), ctx_idx)],
        out_shape=[jax.ShapeDtypeStruct((n, d), BF16)] * 3
        + [jax.ShapeDtypeStruct((n_ctx, d), F32)] * 2,
        compiler_params=_cparams("arbitrary"),
        name=f"qkv{layer}",
    )(*args)


def _rope_tables(lat_seq, tm):
    n_freq = HEAD_DIM // 4
    inv = ROPE_BASE ** (-jnp.arange(n_freq, dtype=F32) / n_freq)
    pos = jnp.arange(lat_seq)
    row = (pos // GRID_W).astype(F32)
    col = (pos % GRID_W).astype(F32)
    ang = jnp.concatenate([row[:, None] * inv, col[:, None] * inv], axis=-1)
    reps = LANES // (HEAD_DIM // 2)
    cos = jnp.tile(jnp.cos(ang), (1, reps))
    sin = jnp.tile(jnp.sin(ang), (1, reps))
    first_half = (jnp.arange(LANES) % HEAD_DIM) < HEAD_DIM // 2
    s_next = jnp.where(first_half, -sin, 0.0)
    s_prev = jnp.where(first_half, 0.0, sin)
    ident = jnp.stack([jnp.ones((tm, LANES), F32), jnp.zeros((tm, LANES), F32),
                       jnp.zeros((tm, LANES), F32)])
    return jnp.concatenate([jnp.stack([cos, s_next, s_prev]), ident], axis=1)


def _lane_lo():
    return lax.broadcasted_iota(jnp.int32, (1, LANES), 1) < HEAD_DIM


def _ctx_dense_kernel(q_ref, k_ref, v_ref, o_ref, *, d):
    lo = _lane_lo()
    cols = []
    for j in range(d // LANES):
        blk = slice(j * LANES, (j + 1) * LANES)
        q2, k2, v2 = q_ref[:, blk], k_ref[:, blk], v_ref[:, blk]
        outs = []
        for qm in (jnp.where(lo, q2, 0), jnp.where(lo, 0, q2)):
            s = _dot_nt(qm, k2)
            p = jnp.exp2(s - jnp.max(s, axis=-1, keepdims=True))
            outs.append(_dot(p.astype(BF16), v2) / jnp.sum(p, axis=-1, keepdims=True))
        cols.append(jnp.where(lo, outs[0], outs[1]).astype(BF16))
    o_ref[...] = jnp.concatenate(cols, axis=1)


def _ctx_dense(q, k, v, *, n_lat, seq):
    n, d = q.shape
    off = n_lat // seq
    spec = pl.BlockSpec((seq, d), lambda b: (off + b, 0))
    return pl.pallas_call(
        functools.partial(_ctx_dense_kernel, d=d),
        grid=((n - n_lat) // seq,),
        in_specs=[spec, spec, spec],
        out_specs=pl.BlockSpec((seq, d), lambda b: (b, 0)),
        out_shape=jax.ShapeDtypeStruct((n - n_lat, d), BF16),
        compiler_params=_cparams("arbitrary"),
        name="ctx_dense_attn",
    )(q, k, v)


def _subln(o, g_ref, out_scale):
    return _rms(o, g_ref[...]) * out_scale


def _ctx_diff_kernel(lam_ref, q_ref, k_ref, v_ref, g_ref, o_ref, *, d, out_scale):
    lo = _lane_lo()
    lam = lam_ref[0]
    cols = []
    for j in range(d // LANES):
        blk = slice(j * LANES, (j + 1) * LANES)
        q2, k2, v2 = q_ref[:, blk], k_ref[:, blk], v_ref[:, blk]
        ps = []
        for qm in (jnp.where(lo, q2, 0), jnp.where(lo, 0, q2)):
            s = _dot_nt(qm, k2)
            p = jnp.exp2(s - jnp.max(s, axis=-1, keepdims=True))
            ps.append(p / jnp.sum(p, axis=-1, keepdims=True))
        o = _dot((ps[0] - lam * ps[1]).astype(BF16), v2)
        cols.append(_subln(o, g_ref, out_scale).astype(BF16))
    o_ref[...] = jnp.concatenate(cols, axis=1)


def _ctx_diff(lam, q, k, v, subg, *, n_lat, seq, out_scale):
    n, d = q.shape
    off = n_lat // seq
    spec = pl.BlockSpec((seq, d), lambda b: (off + b, 0))
    return pl.pallas_call(
        functools.partial(_ctx_diff_kernel, d=d, out_scale=out_scale),
        grid=((n - n_lat) // seq,),
        in_specs=[pl.BlockSpec(memory_space=pltpu.SMEM), spec, spec, spec,
                  pl.BlockSpec((1, LANES), lambda b: (0, 0))],
        out_specs=pl.BlockSpec((seq, d), lambda b: (b, 0)),
        out_shape=jax.ShapeDtypeStruct((n - n_lat, d), BF16),
        compiler_params=_cparams("arbitrary"),
        name="ctx_diff_attn",
    )(lam, q, k, v, subg)


def _na_kernel(q_ref, k_ref, v_ref, ck_ref, cv_ref, bias_ref, o_ref, *, d, rows):
    lo = _lane_lo()
    r = pl.program_id(1)
    win = NA_ROWS * GRID_W
    start = pl.multiple_of(jnp.clip(r - NA_ROWS // 2, 0, rows - NA_ROWS) * GRID_W, GRID_W)
    n_heads = 2 * (d // LANES)

    def blk(h):
        return slice(h // 2 * LANES, (h // 2 + 1) * LANES)

    scores = []
    for h in range(n_heads):
        q2 = q_ref[:, blk(h)]
        qm = jnp.where(lo, q2, 0) if h % 2 == 0 else jnp.where(lo, 0, q2)
        scores.append((_dot_nt(qm, k_ref[pl.ds(start, win), blk(h)]) + bias_ref[h],
                       _dot_nt(qm, ck_ref[:, blk(h)])))
    probs = []
    for s_loc, s_ctx in scores:
        m = jnp.maximum(jnp.max(s_loc, axis=-1, keepdims=True),
                        jnp.max(s_ctx, axis=-1, keepdims=True))
        p_loc = jnp.exp2(s_loc - m)
        p_ctx = jnp.exp2(s_ctx - m)
        l = jnp.sum(p_loc, axis=-1, keepdims=True) + jnp.sum(p_ctx, axis=-1, keepdims=True)
        probs.append((p_loc.astype(BF16), p_ctx.astype(BF16), l))
    outs = []
    for h, (p_loc, p_ctx, l) in enumerate(probs):
        outs.append((_dot(p_loc, v_ref[pl.ds(start, win), blk(h)])
                     + _dot(p_ctx, cv_ref[:, blk(h)])) / l)
    o_ref[...] = jnp.concatenate(
        [jnp.where(lo, outs[h], outs[h + 1]).astype(BF16) for h in range(0, n_heads, 2)], axis=1)


def _na_bias_table(rpb, rows):
    h = rpb.shape[0]
    delta = jnp.arange(NA_ROWS)
    kr = jnp.arange(NA_ROWS)
    c = jnp.arange(GRID_W)
    kc = jnp.arange(GRID_W)
    off_r = kr[None, :] - delta[:, None] + (NA_ROWS - 1)
    c0 = jnp.clip(c - NA_COLS // 2, 0, GRID_W - NA_COLS)
    off_c = kc[None, :] - c[:, None] + (NA_COLS - 1)
    inside = (kc[None, :] >= c0[:, None]) & (kc[None, :] < c0[:, None] + NA_COLS)
    oh_r = jax.nn.one_hot(off_r, 2 * NA_ROWS - 1, dtype=F32)
    oh_c = jax.nn.one_hot(off_c, 2 * NA_COLS - 1, dtype=F32)
    t = jnp.einsum('hrs,dkr,cqs->dhckq', rpb.astype(F32), oh_r, oh_c,
                   precision=lax.Precision.HIGHEST)
    t = jnp.where(inside[None, None, :, None, :], t * LOG2E, MASKED)
    return t.reshape(NA_ROWS, h, GRID_W, NA_ROWS * GRID_W)


def _na_attention(q, k, v, ck, cv, bias, *, n_lat_batch, lat_seq):
    n, d = q.shape
    rows = lat_seq // GRID_W
    past = ck.shape[1]
    n_heads = bias.shape[1]

    def delta_idx(b, r):
        return (r - jnp.clip(r - NA_ROWS // 2, 0, rows - NA_ROWS), 0, 0, 0)

    kv_spec = pl.BlockSpec((lat_seq, d), lambda b, r: (b, 0))
    c_spec = pl.BlockSpec((None, past, d), lambda b, r: (b, 0, 0))
    return pl.pallas_call(
        functools.partial(_na_kernel, d=d, rows=rows),
        grid=(n_lat_batch, rows),
        in_specs=[pl.BlockSpec((GRID_W, d), lambda b, r: (b * rows + r, 0)), kv_spec, kv_spec,
                  c_spec, c_spec,
                  pl.BlockSpec((None, n_heads, GRID_W, NA_ROWS * GRID_W), delta_idx)],
        out_specs=pl.BlockSpec((GRID_W, d), lambda b, r: (b * rows + r, 0)),
        out_shape=jax.ShapeDtypeStruct((n_lat_batch * lat_seq, d), BF16),
        compiler_params=_cparams("arbitrary", "arbitrary"),
        name="na_attn",
    )(q, k, v, ck, cv, bias)


def _lat_diff_kernel(lam_ref, q_ref, k_ref, v_ref, ck_ref, cv_ref, g_ref, o_ref, *, tk, out_scale):
    lo = _lane_lo()
    lam = lam_ref[0]
    q2 = q_ref[...]
    tq = q2.shape[0]
    qa, qb = jnp.where(lo, q2, 0), jnp.where(lo, 0, q2)

    def softmax_step(state, s):
        m, l, acc = state
        m_new = jnp.maximum(m, jnp.max(s, axis=-1, keepdims=True))
        alpha = jnp.exp2(m - m_new)
        p = jnp.exp2(s - m_new)
        return m_new, alpha * l + jnp.sum(p, axis=-1, keepdims=True), alpha * acc, p.astype(BF16)

    def update(carry, kt, vt):
        sa, sb = _dot_nt(qa, kt), _dot_nt(qb, kt)
        ma, la, acc_a, pa = softmax_step(carry[0], sa)
        mb, lb, acc_b, pb = softmax_step(carry[1], sb)
        return (ma, la, acc_a + _dot(pa, vt)), (mb, lb, acc_b + _dot(pb, vt))

    def body(i, carry):
        off = pl.multiple_of(i * tk, tk)
        return update(carry, k_ref[pl.ds(off, tk), :], v_ref[pl.ds(off, tk), :])

    init = (jnp.full((tq, 1), -jnp.inf, F32), jnp.zeros((tq, 1), F32), jnp.zeros((tq, LANES), F32))
    carry = lax.fori_loop(0, k_ref.shape[0] // tk, body, (init, init), unroll=2)
    s1, s2 = update(carry, ck_ref[...], cv_ref[...])
    o = s1[2] / s1[1] - lam * (s2[2] / s2[1])
    o_ref[...] = _subln(o, g_ref, out_scale).astype(BF16)


def _lat_diff(lam, q, k, v, ck, cv, subg, *, n_lat_batch, lat_seq, out_scale):
    n, d = q.shape
    past = ck.shape[1]
    tq = _pick_tile(lat_seq, 512)
    tk = _pick_tile(lat_seq, 512)
    qt = lat_seq // tq
    kv_spec = pl.BlockSpec((lat_seq, LANES), lambda b, h, i: (b, h))
    c_spec = pl.BlockSpec((None, past, LANES), lambda b, h, i: (b, 0, h))
    q_spec = pl.BlockSpec((tq, LANES), lambda b, h, i: (b * qt + i, h))
    return pl.pallas_call(
        functools.partial(_lat_diff_kernel, tk=tk, out_scale=out_scale),
        grid=(n_lat_batch, d // LANES, qt),
        in_specs=[pl.BlockSpec(memory_space=pltpu.SMEM), q_spec, kv_spec, kv_spec, c_spec, c_spec,
                  pl.BlockSpec((1, LANES), lambda b, h, i: (0, 0))],
        out_specs=q_spec,
        out_shape=jax.ShapeDtypeStruct((n_lat_batch * lat_seq, d), BF16),
        compiler_params=_cparams("arbitrary", "arbitrary", "arbitrary"),
        name="lat_diff_attn",
    )(lam, q, k, v, ck, cv, subg)


def _post_attn_kernel(o_ref, x_ref, mod_ref, wo_ref, g_ref, rwh_ref, rwl_ref, rb_ref, tri_ref,
                      xmid_ref, h_ref, sel_ref, gate_ref, rank_ref, cnt_ref, carry_ref):
    i = pl.program_id(0)

    @pl.when(i == 0)
    def _():
        carry_ref[...] = jnp.zeros_like(carry_ref)

    x1 = x_ref[...] + mod_ref[2:3, :] * _dot(o_ref[...], wo_ref[...])
    xmid_ref[...] = x1
    h = _rms(x1, g_ref[...]) * (1.0 + mod_ref[4:5, :]) + mod_ref[3:4, :]
    hb = h.astype(BF16)
    h_ref[...] = _pack_rows(hb)
    h_lo = (h - hb.astype(F32)).astype(BF16)
    logits = _dot_nt(rwh_ref[...], hb) + (_dot_nt(rwl_ref[...], hb) + _dot_nt(rwh_ref[...], h_lo))
    scores = _sigmoid(logits)
    n_exp, tm = scores.shape
    biased = scores + rb_ref[...]
    row = lax.broadcasted_iota(jnp.int32, (n_exp, tm), 0).astype(F32)
    total = jnp.zeros((n_exp, tm), F32)
    sels, gates = [], []
    for _ in range(TOP_K):
        m = jnp.max(biased, axis=0, keepdims=True)
        idx = jnp.min(jnp.where(biased == m, row, float(n_exp)), axis=0, keepdims=True)
        hit = row == idx
        gates.append(jnp.sum(jnp.where(hit, scores, 0.0), axis=0, keepdims=True))
        biased = jnp.where(hit, -jnp.inf, biased)
        total = total + jnp.where(hit, 1.0, 0.0)
        sels.append(idx)
    denom = gates[0]
    for g in gates[1:]:
        denom = denom + g
    before = _dot(total.astype(BF16), tri_ref[...]) + carry_ref[:, 0:1]
    for k in range(TOP_K):
        sel_ref[k:k + 1, :] = sels[k].astype(jnp.int32)
        gate_ref[k:k + 1, :] = gates[k] / denom * ROUTED_SCALE
        rank_ref[k:k + 1, :] = jnp.sum(jnp.where(row == sels[k], before, 0.0), axis=0,
                                       keepdims=True).astype(jnp.int32)
    carry_ref[...] = carry_ref[...] + jnp.sum(total, axis=1, keepdims=True)
    cnt_ref[...] = carry_ref[...].astype(jnp.int32)


def _post_attn(o, x, mod, wo, g, rwh, rwl, rb, *, layer, lat_seq, n_lat_batch, tm):
    n, d = x.shape
    n_exp = rwh.shape[0]
    tri = (jnp.arange(tm)[:, None] < jnp.arange(tm)[None, :]).astype(BF16)

    def mod_idx(i):
        return (layer, jnp.minimum(i * tm // lat_seq, n_lat_batch), 0, 0)

    tok = pl.BlockSpec((tm, d), lambda i: (i, 0))
    tok_p = pl.BlockSpec((tm, d // 2), lambda i: (i, 0))
    kt = pl.BlockSpec((TOP_K, tm), lambda i: (0, i))
    const2 = lambda i: (0, 0)
    return pl.pallas_call(
        _post_attn_kernel,
        grid=(n // tm,),
        in_specs=[tok, tok, pl.BlockSpec((None, None, N_MOD, d), mod_idx),
                  pl.BlockSpec((d, d), const2), pl.BlockSpec((1, d), const2),
                  pl.BlockSpec((n_exp, d), const2), pl.BlockSpec((n_exp, d), const2),
                  pl.BlockSpec((n_exp, 1), const2), pl.BlockSpec((tm, tm), const2)],
        out_specs=[tok, tok_p, kt, kt, kt, pl.BlockSpec((n_exp, LANES), const2)],
        out_shape=[jax.ShapeDtypeStruct((n, d), F32), jax.ShapeDtypeStruct((n, d // 2), jnp.int32),
                   jax.ShapeDtypeStruct((TOP_K, n), jnp.int32),
                   jax.ShapeDtypeStruct((TOP_K, n), F32),
                   jax.ShapeDtypeStruct((TOP_K, n), jnp.int32),
                   jax.ShapeDtypeStruct((n_exp, LANES), jnp.int32)],
        scratch_shapes=[pltpu.VMEM((n_exp, LANES), F32)],
        compiler_params=_cparams("arbitrary"),
        name=f"post_attn{layer}",
    )(o, x, mod, wo, g, rwh, rwl, rb, tri)


def _plan_kernel(cnt_ref, sel_ref, rank_ref, dest_ref, te_ref, nu_ref, *, tmb):
    n_exp = cnt_ref.shape[0]
    sel = sel_ref[...]
    tile_start = lax.broadcasted_iota(jnp.int32, te_ref.shape, 1) * tmb
    dest = rank_ref[...]
    te = jnp.zeros(te_ref.shape, jnp.int32)
    start = jnp.zeros((1, 1), jnp.int32)
    for e in range(n_exp):
        dest = dest + jnp.where(sel == e, start, 0)
        start = start + ((cnt_ref[e:e + 1, 0:1] + (tmb - 1)) & -tmb)
        te = te + jnp.where(start <= tile_start, 1, 0)
    te_ref[...] = jnp.minimum(te, n_exp - 1)
    nu_ref[...] = jnp.broadcast_to(start, nu_ref.shape)
    for c in range(dest_ref.shape[0]):
        dest_ref[c] = dest[:, c * SC_CHUNK:(c + 1) * SC_CHUNK]


def _plan(counts, sel, rank, *, tmb, n_tiles):
    k_top, n = sel.shape
    assert tmb & (tmb - 1) == 0
    tn = _pick_tile(n, 16 * SC_CHUNK)
    te_w = -(-n_tiles // LANES) * LANES
    tok = pl.BlockSpec((k_top, tn), lambda i: (0, i))
    const2 = lambda i: (0, 0)
    dest, te, nu = pl.pallas_call(
        functools.partial(_plan_kernel, tmb=tmb),
        grid=(n // tn,),
        in_specs=[pl.BlockSpec(counts.shape, const2), tok, tok],
        out_specs=[pl.BlockSpec((tn // SC_CHUNK, k_top, SC_CHUNK), lambda i: (i, 0, 0)),
                   pl.BlockSpec((1, te_w), const2), pl.BlockSpec((1, LANES), const2)],
        out_shape=[jax.ShapeDtypeStruct((n // SC_CHUNK, k_top, SC_CHUNK), jnp.int32),
                   jax.ShapeDtypeStruct((1, te_w), jnp.int32),
                   jax.ShapeDtypeStruct((1, LANES), jnp.int32)],
        compiler_params=_cparams("arbitrary"),
        name="plan",
    )(counts, sel, rank)
    return dest, te[0, :n_tiles], nu[0, :1] // tmb


def _expert_kernel(te_ref, nu_ref, x_ref, wgu_ref, wd_ref, y_ref, *, f):
    @pl.when(pl.program_id(0) < nu_ref[0])
    def _():
        x = jnp.concatenate(_unpack_rows(x_ref[...]), axis=1).astype(BF16)
        gu = _dot(x, wgu_ref[...])
        gate, up = gu[:, :f], gu[:, f:]
        a = gate * _sigmoid(gate) * up
        y_ref[...] = _pack_rows(_dot(a.astype(BF16), wd_ref[...]).astype(BF16))


def _experts(tile_expert, n_used, xb, wgu, wd, *, tmb):
    cap, dp = xb.shape
    f, d = wd.shape[1:]

    def row_idx(i, te, nu):
        return (jnp.minimum(i, nu[0] - 1), 0)

    return pl.pallas_call(
        functools.partial(_expert_kernel, f=f),
        grid_spec=pltpu.PrefetchScalarGridSpec(
            num_scalar_prefetch=2,
            grid=(cap // tmb,),
            in_specs=[pl.BlockSpec((tmb, dp), row_idx),
                      pl.BlockSpec((None, d, 2 * f), lambda i, te, nu: (te[i], 0, 0)),
                      pl.BlockSpec((None, f, d), lambda i, te, nu: (te[i], 0, 0))],
            out_specs=pl.BlockSpec((tmb, dp), row_idx)),
        out_shape=jax.ShapeDtypeStruct((cap, dp), jnp.int32),
        compiler_params=_cparams("arbitrary"),
        name="experts",
    )(tile_expert, n_used, xb, wgu, wd)


def _combine_kernel(x_ref, h_ref, yg_ref, gt_ref, mod_ref, wgu_ref, wd_ref, fg_ref, o_ref, *, f, final):
    h = jnp.concatenate(_unpack_rows(h_ref[...]), axis=1).astype(BF16)
    gu = _dot(h, wgu_ref[...])
    gate, up = gu[:, :f], gu[:, f:]
    shared = _dot((gate * _sigmoid(gate) * up).astype(BF16), wd_ref[...])
    lo = hi = None
    for k in range(TOP_K):
        y_lo, y_hi = _unpack_rows(yg_ref[k])
        g = gt_ref[:, k:k + 1]
        lo = g * y_lo if lo is None else lo + g * y_lo
        hi = g * y_hi if hi is None else hi + g * y_hi
    acc = shared + jnp.concatenate([lo, hi], axis=1)
    x2 = x_ref[...] + mod_ref[5:6, :] * acc
    o_ref[...] = _rms(x2, fg_ref[...]) if final else x2


def _combine(x, h, yg, gates_t, mod, wgu, wd, final_g, *, layer, lat_seq, n_lat_batch, tm, final):
    n, d = x.shape
    f = wd.shape[0]

    def mod_idx(i):
        return (layer, jnp.minimum(i * tm // lat_seq, n_lat_batch), 0, 0)

    tok = pl.BlockSpec((tm, d), lambda i: (i, 0))
    tok_p = pl.BlockSpec((tm, d // 2), lambda i: (i, 0))
    const2 = lambda i: (0, 0)
    return pl.pallas_call(
        functools.partial(_combine_kernel, f=f, final=final),
        grid=(n // tm,),
        in_specs=[tok, tok_p, pl.BlockSpec((TOP_K, tm, d // 2), lambda i: (0, i, 0)),
                  pl.BlockSpec((tm, TOP_K), lambda i: (i, 0)),
                  pl.BlockSpec((None, None, N_MOD, d), mod_idx),
                  pl.BlockSpec((d, 2 * f), const2), pl.BlockSpec((f, d), const2),
                  pl.BlockSpec((1, d), const2)],
        out_specs=tok,
        out_shape=jax.ShapeDtypeStruct((n, d), F32),
        compiler_params=_cparams("arbitrary"),
        name=f"combine{layer}",
    )(x, h, yg, gates_t, mod, wgu, wd, final_g)


SC_CHUNK = 128


def _sc_workers():
    info = plsc.get_sparse_core_info()
    return info.num_cores, info.num_subcores


def _sc_dispatch(rows, dest, cap):
    n, w = rows.shape
    nc, ns = _sc_workers()
    per_w = n // (nc * ns)
    assert per_w * nc * ns == n and per_w % SC_CHUNK == 0
    mesh = plsc.VectorSubcoreMesh(core_axis_name="c", subcore_axis_name="s")

    @functools.partial(
        pl.kernel, mesh=mesh, out_type=jax.ShapeDtypeStruct((cap, w), rows.dtype),
        scratch_types=[pltpu.VMEM((TOP_K, SC_CHUNK), jnp.int32),
                       pltpu.VMEM((SC_CHUNK, w), rows.dtype)],
        name="sc_dispatch")
    def run(rows_hbm, dest_hbm, out_hbm, idx_v, rows_v):
        wid = lax.axis_index("s") * nc + lax.axis_index("c")

        @pl.loop(0, per_w // SC_CHUNK)
        def _(ci):
            base = pl.multiple_of(wid * per_w + ci * SC_CHUNK, SC_CHUNK)
            pltpu.sync_copy(dest_hbm.at[wid * (per_w // SC_CHUNK) + ci], idx_v)
            pltpu.sync_copy(rows_hbm.at[pl.ds(base, SC_CHUNK)], rows_v)
            for k in range(TOP_K):
                pltpu.sync_copy(rows_v, out_hbm.at[idx_v.at[k]])

    return run(rows, dest)


def _sc_collect(table, dest):
    n_chunks, k_top, _ = dest.shape
    n = n_chunks * SC_CHUNK
    w = table.shape[1]
    nc, ns = _sc_workers()
    per_w = n // (nc * ns)
    assert per_w * nc * ns == n and per_w % SC_CHUNK == 0
    mesh = plsc.VectorSubcoreMesh(core_axis_name="c", subcore_axis_name="s")

    @functools.partial(
        pl.kernel, mesh=mesh, out_type=jax.ShapeDtypeStruct((k_top, n, w), table.dtype),
        scratch_types=[pltpu.VMEM((TOP_K, SC_CHUNK), jnp.int32),
                       pltpu.VMEM((SC_CHUNK, w), table.dtype)],
        name="sc_collect")
    def run(table_hbm, dest_hbm, out_hbm, idx_v, rows_v):
        wid = lax.axis_index("s") * nc + lax.axis_index("c")

        @pl.loop(0, per_w // SC_CHUNK)
        def _(ci):
            base = pl.multiple_of(wid * per_w + ci * SC_CHUNK, SC_CHUNK)
            pltpu.sync_copy(dest_hbm.at[wid * (per_w // SC_CHUNK) + ci], idx_v)
            for k in range(TOP_K):
                pltpu.sync_copy(table_hbm.at[idx_v.at[k]], rows_v)
                pltpu.sync_copy(rows_v, out_hbm.at[k, pl.ds(base, SC_CHUNK)])

    return run(table, dest)


def _split_bf16(w):
    hi = w.astype(BF16)
    return hi, (w - hi.astype(F32)).astype(BF16)


def kernel(x_prompt, x_sample, cache_k, cache_v, c, c_ctx, ada_w, ada_b, norm1_g, norm2_g, w_qkv, w_o, na_rpb, diff_lambda, diff_subln_g, router_w, router_b, exp_w_gate, exp_w_up, exp_w_down, shared_w_gate, shared_w_up, shared_w_down, final_g):
    batch, seq, d = x_prompt.shape
    n_lat_batch, lat_seq, _ = x_sample.shape
    depth = w_qkv.shape[0]
    n_exp = router_w.shape[-1]
    n_lat = n_lat_batch * lat_seq
    n_ctx = batch * seq
    n = n_lat + n_ctx
    assert d % LANES == 0 and na_rpb.shape[1] * HEAD_DIM == d
    assert diff_lambda.shape[-1] == HEAD_DIM and lat_seq % GRID_W == 0
    assert lat_seq // GRID_W >= NA_ROWS and n_lat % seq == 0
    tm = _pick_tile(math.gcd(lat_seq, n_ctx), 512)
    tmb = 512
    scale = HEAD_DIM ** -0.5 * LOG2E

    x = jnp.concatenate([x_sample.reshape(n_lat, d), x_prompt.reshape(n_ctx, d)], axis=0)
    mod_rows = -(-(n_lat_batch + 1) // 8) * 8
    cond = jnp.zeros((mod_rows, d), F32).at[:n_lat_batch].set(c).at[n_lat_batch].set(c_ctx)
    mod = _modulation(cond, ada_w, ada_b).reshape(depth, mod_rows, N_MOD, d)

    w_qkv_b = w_qkv.astype(BF16)
    w_o_b = w_o.astype(BF16)
    rope_tab = _rope_tables(lat_seq, tm)
    new_k, new_v = [], []
    for i in range(depth):
        is_diff = i % 2 == 1
        j = i // 2
        q, k, v, kf, vf = _qkv(x, mod, norm1_g.reshape(depth, 1, d), w_qkv_b,
                               rope_tab if is_diff else None, layer=i, n_lat=n_lat,
                               lat_seq=lat_seq, n_lat_batch=n_lat_batch, tm=tm, scale=scale)
        new_k.append(kf.reshape(batch, seq, d))
        new_v.append(vf.reshape(batch, seq, d))
        ck = cache_k[:, i].astype(BF16)
        cv = cache_v[:, i].astype(BF16)
        if not is_diff:
            o_lat = _na_attention(q, k, v, ck, cv, _na_bias_table(na_rpb[j], lat_seq // GRID_W),
                                  n_lat_batch=n_lat_batch, lat_seq=lat_seq)
            o_ctx = _ctx_dense(q, k, v, n_lat=n_lat, seq=seq)
        else:
            lam_init = 0.8 - 0.6 * math.exp(-0.3 * i)
            lp = diff_lambda[j].astype(F32)
            lam = (jnp.exp(jnp.sum(lp[0] * lp[1])) - jnp.exp(jnp.sum(lp[2] * lp[3]))
                   + lam_init).reshape(1)
            subg = diff_subln_g[j].reshape(1, LANES).astype(F32)
            o_lat = _lat_diff(lam, q, k, v, ck, cv, subg, n_lat_batch=n_lat_batch,
                              lat_seq=lat_seq, out_scale=1.0 - lam_init)
            o_ctx = _ctx_diff(lam, q, k, v, subg, n_lat=n_lat, seq=seq, out_scale=1.0 - lam_init)
        o = jnp.concatenate([o_lat, o_ctx], axis=0)

        rwh, rwl = _split_bf16(router_w[i].T)
        xmid, h2, sel, gates, rank, counts = _post_attn(
            o, x, mod, w_o_b[i], norm2_g[i].reshape(1, d), rwh, rwl,
            router_b[i].reshape(n_exp, 1).astype(F32), layer=i, lat_seq=lat_seq,
            n_lat_batch=n_lat_batch, tm=tm)

        n_tiles = -(-(n * TOP_K) // tmb) + n_exp
        cap = n_tiles * tmb
        dest, tile_expert, n_used = _plan(counts, sel, rank, tmb=tmb, n_tiles=n_tiles)
        xb = _sc_dispatch(h2, dest, cap)
        wgu = jnp.concatenate([exp_w_gate[i], exp_w_up[i]], axis=-1).astype(BF16)
        yb = _experts(tile_expert, n_used, xb, wgu, exp_w_down[i].astype(BF16), tmb=tmb)
        yg = _sc_collect(yb, dest)
        swgu = jnp.concatenate([shared_w_gate[i], shared_w_up[i]], axis=-1).astype(BF16)
        x = _combine(xmid, h2, yg, gates.T, mod, swgu, shared_w_down[i].astype(BF16),
                     final_g.reshape(1, d), layer=i, lat_seq=lat_seq, n_lat_batch=n_lat_batch,
                     tm=tm, final=i == depth - 1)

    y_sample = x[:n_lat].reshape(n_lat_batch, lat_seq, d)
    y_prompt = x[n_lat:].reshape(batch, seq, d)
    return (y_prompt, y_sample, jnp.stack(new_k, axis=1), jnp.stack(new_v, axis=1))
```

```python
import functools
import math

import jax
import jax.numpy as jnp
from jax import lax
from jax.experimental import pallas as pl
from jax.experimental.pallas import tpu as pltpu
from jax.experimental.pallas import tpu_sc as plsc

GRID_W = 64
NA_ROWS = 8
NA_COLS = 16
TOP_K = 8
ROUTED_SCALE = 2.5
ROPE_BASE = 10000.0
EPS = 1e-6
N_MOD = 6
HEAD_DIM = 64
LANES = 128
LOG2E = math.log2(math.e)
MASKED = -1e30
VMEM_LIMIT = 56 * 1024 * 1024

F32 = jnp.float32
BF16 = jnp.bfloat16


def _cparams(*sem):
    return pltpu.CompilerParams(dimension_semantics=sem, vmem_limit_bytes=VMEM_LIMIT)


def _dot(a, b):
    return jnp.dot(a, b, preferred_element_type=F32)


def _dot_nt(a, b):
    return lax.dot_general(a, b, (((1,), (1,)), ((), ())), preferred_element_type=F32)


def _sigmoid(x):
    return 1.0 / (1.0 + jnp.exp(-x))


def _rms(x, g):
    return x * lax.rsqrt(jnp.mean(x * x, axis=-1, keepdims=True) + EPS) * g


def _pack_rows(xb):
    half = xb.shape[1] // 2
    u = lax.bitcast_convert_type(xb.astype(F32), jnp.uint32)
    packed = (u[:, :half] >> 16) | (u[:, half:] & jnp.uint32(0xFFFF0000))
    return lax.bitcast_convert_type(packed, jnp.int32)


def _unpack_rows(p):
    u = lax.bitcast_convert_type(p, jnp.uint32)
    return (lax.bitcast_convert_type(u << 16, F32),
            lax.bitcast_convert_type(u & jnp.uint32(0xFFFF0000), F32))


def _pick_tile(n, pref):
    t = pref
    while n % t:
        t //= 2
    return t


def _mod_kernel(c_ref, w_ref, b_ref, o_ref):
    c = c_ref[...]
    o_ref[...] = _dot(c * _sigmoid(c), w_ref[...]) + b_ref[...]


def _modulation(cond, ada_w, ada_b):
    depth, d, n6 = ada_w.shape
    rows = cond.shape[0]
    tn = _pick_tile(n6, 512)
    return pl.pallas_call(
        _mod_kernel,
        grid=(depth, n6 // tn),
        in_specs=[
            pl.BlockSpec((rows, d), lambda l, j: (0, 0)),
            pl.BlockSpec((None, d, tn), lambda l, j: (l, 0, j)),
            pl.BlockSpec((None, 1, tn), lambda l, j: (l, 0, j)),
        ],
        out_specs=pl.BlockSpec((None, rows, tn), lambda l, j: (l, 0, j)),
        out_shape=jax.ShapeDtypeStruct((depth, rows, n6), F32),
        compiler_params=_cparams("arbitrary", "arbitrary"),
        name="modulation",
    )(cond, ada_w, ada_b.reshape(depth, 1, n6))


def _read_tokens(x_refs, i, n_lat_tiles):
    if len(x_refs) == 1:
        return x_refs[0][...]
    return jnp.where(i < n_lat_tiles, x_refs[0][...], x_refs[1][...])


def _token_specs(xs, tm, n_lat_tiles):
    d = xs[0].shape[1]
    if len(xs) == 1:
        return [pl.BlockSpec((tm, d), lambda i: (i, 0))]
    return [pl.BlockSpec((tm, d), lambda i: (jnp.minimum(i, n_lat_tiles - 1), 0)),
            pl.BlockSpec((tm, d), lambda i: (jnp.maximum(i - n_lat_tiles, 0), 0))]


def _qkv_kernel(*refs, d, scale, rope, n_lat_tiles, n_x, n_alias):
    x_refs = refs[:n_x]
    mod_ref, g_ref, w_ref = refs[n_x:n_x + 3]
    rest = refs[n_x + 3:]
    if rope:
        rope_ref, rest = rest[0], rest[1:]
    q_ref, k_ref, v_ref, kf_ref, vf_ref = rest[n_alias:]
    i = pl.program_id(0)
    x = _read_tokens(x_refs, i, n_lat_tiles)
    h = _rms(x, g_ref[...]) * (1.0 + mod_ref[1:2, :]) + mod_ref[0:1, :]
    hb = h.astype(BF16)

    def rot(t):
        if not rope:
            return t
        cols = []
        for j in range(d // LANES):
            tb = t[:, j * LANES:(j + 1) * LANES]
            cols.append(tb * rope_ref[0]
                        + pltpu.roll(tb, LANES - HEAD_DIM // 2, axis=1) * rope_ref[1]
                        + pltpu.roll(tb, HEAD_DIM // 2, axis=1) * rope_ref[2])
        return jnp.concatenate(cols, axis=1)

    q = _dot(hb, w_ref[:, 0:d])
    q_ref[...] = (rot(q) * scale).astype(BF16)
    k = _dot(hb, w_ref[:, d:2 * d])
    k_ref[...] = rot(k).astype(BF16)
    v = _dot(hb, w_ref[:, 2 * d:3 * d])
    v_ref[...] = v.astype(BF16)

    @pl.when(i >= n_lat_tiles)
    def _():
        kf_ref[...] = k.reshape(kf_ref.shape)
        vf_ref[...] = v.reshape(vf_ref.shape)


def _qkv(xs, mod, g, w, rope_tab, caches, *, layer, n_lat, batch, seq, lat_seq, n_lat_batch, tm,
         scale):
    d = xs[0].shape[1]
    depth = w.shape[0]
    n = n_lat + batch * seq
    n_lat_tiles = n_lat // tm
    rope = rope_tab is not None
    assert tm % seq == 0

    def mod_idx(i):
        return (layer, jnp.minimum(i * tm // lat_seq, n_lat_batch), 0, 0)

    in_specs = _token_specs(xs, tm, n_lat_tiles) + [
        pl.BlockSpec((None, None, N_MOD, d), mod_idx),
        pl.BlockSpec((None, 1, d), lambda i: (layer, 0, 0)),
        pl.BlockSpec((None, d, 3 * d), lambda i: (layer, 0, 0)),
    ]
    args = list(xs) + [mod, g, w]
    if rope:
        seq_tiles = lat_seq // tm
        in_specs.append(pl.BlockSpec(
            (3, tm, LANES),
            lambda i: (0, jnp.where(i < n_lat_tiles, i % seq_tiles, seq_tiles), 0)))
        args.append(rope_tab)
    aliases = {}
    if caches is not None:
        aliases = {len(args): 3, len(args) + 1: 4}
        in_specs += [pl.BlockSpec(memory_space=pl.ANY)] * 2
        args += list(caches)
    tok = pl.BlockSpec((tm, d), lambda i: (i, 0))
    cache_spec = pl.BlockSpec((tm // seq, None, seq, d),
                              lambda i: (jnp.maximum(i - n_lat_tiles, 0), layer, 0, 0))
    return pl.pallas_call(
        functools.partial(_qkv_kernel, d=d, scale=scale, rope=rope, n_lat_tiles=n_lat_tiles,
                          n_x=len(xs), n_alias=len(aliases)),
        grid=(n // tm,),
        in_specs=in_specs,
        out_specs=[tok, tok, tok, cache_spec, cache_spec],
        out_shape=[jax.ShapeDtypeStruct((n, d), BF16)] * 3
        + [jax.ShapeDtypeStruct((batch, depth, seq, d), F32)] * 2,
        input_output_aliases=aliases,
        compiler_params=_cparams("arbitrary"),
        name=f"qkv{layer}",
    )(*args)


def _rope_tables(lat_seq, tm):
    n_freq = HEAD_DIM // 4
    inv = ROPE_BASE ** (-jnp.arange(n_freq, dtype=F32) / n_freq)
    pos = jnp.arange(lat_seq)
    row = (pos // GRID_W).astype(F32)
    col = (pos % GRID_W).astype(F32)
    ang = jnp.concatenate([row[:, None] * inv, col[:, None] * inv], axis=-1)
    reps = LANES // (HEAD_DIM // 2)
    cos = jnp.tile(jnp.cos(ang), (1, reps))
    sin = jnp.tile(jnp.sin(ang), (1, reps))
    first_half = (jnp.arange(LANES) % HEAD_DIM) < HEAD_DIM // 2
    s_next = jnp.where(first_half, -sin, 0.0)
    s_prev = jnp.where(first_half, 0.0, sin)
    ident = jnp.stack([jnp.ones((tm, LANES), F32), jnp.zeros((tm, LANES), F32),
                       jnp.zeros((tm, LANES), F32)])
    return jnp.concatenate([jnp.stack([cos, s_next, s_prev]), ident], axis=1)


def _lane_lo():
    return lax.broadcasted_iota(jnp.int32, (1, LANES), 1) < HEAD_DIM


def _ctx_dense_kernel(q_ref, k_ref, v_ref, o_all_ref, o_ref, *, d):
    del o_all_ref
    lo = _lane_lo()
    cols = []
    for j in range(d // LANES):
        blk = slice(j * LANES, (j + 1) * LANES)
        q2, k2, v2 = q_ref[:, blk], k_ref[:, blk], v_ref[:, blk]
        outs = []
        for qm in (jnp.where(lo, q2, 0), jnp.where(lo, 0, q2)):
            s = _dot_nt(qm, k2)
            p = jnp.exp2(s - jnp.max(s, axis=-1, keepdims=True))
            outs.append(_dot(p.astype(BF16), v2) / jnp.sum(p, axis=-1, keepdims=True))
        cols.append(jnp.where(lo, outs[0], outs[1]).astype(BF16))
    o_ref[...] = jnp.concatenate(cols, axis=1)


def _ctx_dense(q, k, v, o_all, *, n_lat, seq):
    n, d = q.shape
    off = n_lat // seq
    spec = pl.BlockSpec((seq, d), lambda b: (off + b, 0))
    return pl.pallas_call(
        functools.partial(_ctx_dense_kernel, d=d),
        grid=((n - n_lat) // seq,),
        in_specs=[spec, spec, spec, pl.BlockSpec(memory_space=pl.ANY)],
        out_specs=spec,
        out_shape=jax.ShapeDtypeStruct((n, d), BF16),
        input_output_aliases={3: 0},
        compiler_params=_cparams("arbitrary"),
        name="ctx_dense_attn",
    )(q, k, v, o_all)


def _subln(o, g_ref, out_scale):
    return _rms(o, g_ref[...]) * out_scale


def _ctx_diff_kernel(lam_ref, q_ref, k_ref, v_ref, g_ref, o_all_ref, o_ref, *, d, out_scale):
    del o_all_ref
    lo = _lane_lo()
    lam = lam_ref[0]
    cols = []
    for j in range(d // LANES):
        blk = slice(j * LANES, (j + 1) * LANES)
        q2, k2, v2 = q_ref[:, blk], k_ref[:, blk], v_ref[:, blk]
        ps = []
        for qm in (jnp.where(lo, q2, 0), jnp.where(lo, 0, q2)):
            s = _dot_nt(qm, k2)
            p = jnp.exp2(s - jnp.max(s, axis=-1, keepdims=True))
            ps.append(p / jnp.sum(p, axis=-1, keepdims=True))
        o = _dot((ps[0] - lam * ps[1]).astype(BF16), v2)
        cols.append(_subln(o, g_ref, out_scale).astype(BF16))
    o_ref[...] = jnp.concatenate(cols, axis=1)


def _ctx_diff(lam, q, k, v, subg, o_all, *, n_lat, seq, out_scale):
    n, d = q.shape
    off = n_lat // seq
    spec = pl.BlockSpec((seq, d), lambda b: (off + b, 0))
    return pl.pallas_call(
        functools.partial(_ctx_diff_kernel, d=d, out_scale=out_scale),
        grid=((n - n_lat) // seq,),
        in_specs=[pl.BlockSpec(memory_space=pltpu.SMEM), spec, spec, spec,
                  pl.BlockSpec((1, LANES), lambda b: (0, 0)), pl.BlockSpec(memory_space=pl.ANY)],
        out_specs=spec,
        out_shape=jax.ShapeDtypeStruct((n, d), BF16),
        input_output_aliases={5: 0},
        compiler_params=_cparams("arbitrary"),
        name="ctx_diff_attn",
    )(lam, q, k, v, subg, o_all)


def _na_kernel(q_ref, k_ref, v_ref, ck_ref, cv_ref, bias_ref, o_ref, *, d, rows):
    lo = _lane_lo()
    r = pl.program_id(1)
    win = NA_ROWS * GRID_W
    start = pl.multiple_of(jnp.clip(r - NA_ROWS // 2, 0, rows - NA_ROWS) * GRID_W, GRID_W)
    n_heads = 2 * (d // LANES)

    def blk(h):
        return slice(h // 2 * LANES, (h // 2 + 1) * LANES)

    scores = []
    for h in range(n_heads):
        q2 = q_ref[:, blk(h)]
        qm = jnp.where(lo, q2, 0) if h % 2 == 0 else jnp.where(lo, 0, q2)
        scores.append((_dot_nt(qm, k_ref[pl.ds(start, win), blk(h)]) + bias_ref[h],
                       _dot_nt(qm, ck_ref[:, blk(h)].astype(BF16))))
    probs = []
    for s_loc, s_ctx in scores:
        m = jnp.maximum(jnp.max(s_loc, axis=-1, keepdims=True),
                        jnp.max(s_ctx, axis=-1, keepdims=True))
        p_loc = jnp.exp2(s_loc - m)
        p_ctx = jnp.exp2(s_ctx - m)
        l = jnp.sum(p_loc, axis=-1, keepdims=True) + jnp.sum(p_ctx, axis=-1, keepdims=True)
        probs.append((p_loc.astype(BF16), p_ctx.astype(BF16), l))
    outs = []
    for h, (p_loc, p_ctx, l) in enumerate(probs):
        outs.append((_dot(p_loc, v_ref[pl.ds(start, win), blk(h)])
                     + _dot(p_ctx, cv_ref[:, blk(h)].astype(BF16))) / l)
    o_ref[...] = jnp.concatenate(
        [jnp.where(lo, outs[h], outs[h + 1]).astype(BF16) for h in range(0, n_heads, 2)], axis=1)


def _na_bias_table(rpb, rows):
    h = rpb.shape[0]
    delta = jnp.arange(NA_ROWS)
    kr = jnp.arange(NA_ROWS)
    c = jnp.arange(GRID_W)
    kc = jnp.arange(GRID_W)
    off_r = kr[None, :] - delta[:, None] + (NA_ROWS - 1)
    c0 = jnp.clip(c - NA_COLS // 2, 0, GRID_W - NA_COLS)
    off_c = kc[None, :] - c[:, None] + (NA_COLS - 1)
    inside = (kc[None, :] >= c0[:, None]) & (kc[None, :] < c0[:, None] + NA_COLS)
    oh_r = jax.nn.one_hot(off_r, 2 * NA_ROWS - 1, dtype=F32)
    oh_c = jax.nn.one_hot(off_c, 2 * NA_COLS - 1, dtype=F32)
    t = jnp.einsum('hrs,dkr,cqs->dhckq', rpb.astype(F32), oh_r, oh_c,
                   precision=lax.Precision.HIGHEST)
    t = jnp.where(inside[None, None, :, None, :], t * LOG2E, MASKED)
    return t.reshape(NA_ROWS, h, GRID_W, NA_ROWS * GRID_W)


def _na_attention(q, k, v, ck, cv, bias, *, layer, n_lat_batch, lat_seq):
    n, d = q.shape
    rows = lat_seq // GRID_W
    past = ck.shape[2]
    n_heads = bias.shape[1]

    def delta_idx(b, r):
        return (r - jnp.clip(r - NA_ROWS // 2, 0, rows - NA_ROWS), 0, 0, 0)

    kv_spec = pl.BlockSpec((lat_seq, d), lambda b, r: (b, 0))
    c_spec = pl.BlockSpec((None, None, past, d), lambda b, r: (b, layer, 0, 0))
    return pl.pallas_call(
        functools.partial(_na_kernel, d=d, rows=rows),
        grid=(n_lat_batch, rows),
        in_specs=[pl.BlockSpec((GRID_W, d), lambda b, r: (b * rows + r, 0)), kv_spec, kv_spec,
                  c_spec, c_spec,
                  pl.BlockSpec((None, n_heads, GRID_W, NA_ROWS * GRID_W), delta_idx)],
        out_specs=pl.BlockSpec((GRID_W, d), lambda b, r: (b * rows + r, 0)),
        out_shape=jax.ShapeDtypeStruct((n, d), BF16),
        compiler_params=_cparams("arbitrary", "arbitrary"),
        name="na_attn",
    )(q, k, v, ck, cv, bias)


def _lat_diff_kernel(lam_ref, q_ref, k_ref, v_ref, ck_ref, cv_ref, g_ref, o_ref, *, tk, out_scale):
    lo = _lane_lo()
    lam = lam_ref[0]
    q2 = q_ref[...]
    tq = q2.shape[0]
    qa, qb = jnp.where(lo, q2, 0), jnp.where(lo, 0, q2)

    def softmax_step(state, s):
        m, l, acc = state
        m_new = jnp.maximum(m, jnp.max(s, axis=-1, keepdims=True))
        alpha = jnp.exp2(m - m_new)
        p = jnp.exp2(s - m_new)
        return m_new, alpha * l + jnp.sum(p, axis=-1, keepdims=True), alpha * acc, p.astype(BF16)

    def update(carry, kt, vt):
        sa, sb = _dot_nt(qa, kt), _dot_nt(qb, kt)
        ma, la, acc_a, pa = softmax_step(carry[0], sa)
        mb, lb, acc_b, pb = softmax_step(carry[1], sb)
        return (ma, la, acc_a + _dot(pa, vt)), (mb, lb, acc_b + _dot(pb, vt))

    def body(i, carry):
        off = pl.multiple_of(i * tk, tk)
        return update(carry, k_ref[pl.ds(off, tk), :], v_ref[pl.ds(off, tk), :])

    init = (jnp.full((tq, 1), -jnp.inf, F32), jnp.zeros((tq, 1), F32), jnp.zeros((tq, LANES), F32))
    carry = lax.fori_loop(0, k_ref.shape[0] // tk, body, (init, init), unroll=2)
    s1, s2 = update(carry, ck_ref[...].astype(BF16), cv_ref[...].astype(BF16))
    o = s1[2] / s1[1] - lam * (s2[2] / s2[1])
    o_ref[...] = _subln(o, g_ref, out_scale).astype(BF16)


def _lat_diff(lam, q, k, v, ck, cv, subg, *, layer, n_lat_batch, lat_seq, out_scale):
    n, d = q.shape
    past = ck.shape[2]
    tq = _pick_tile(lat_seq, 512)
    tk = _pick_tile(lat_seq, 512)
    qt = lat_seq // tq
    kv_spec = pl.BlockSpec((lat_seq, LANES), lambda b, h, i: (b, h))
    c_spec = pl.BlockSpec((None, None, past, LANES), lambda b, h, i: (b, layer, 0, h))
    q_spec = pl.BlockSpec((tq, LANES), lambda b, h, i: (b * qt + i, h))
    return pl.pallas_call(
        functools.partial(_lat_diff_kernel, tk=tk, out_scale=out_scale),
        grid=(n_lat_batch, d // LANES, qt),
        in_specs=[pl.BlockSpec(memory_space=pltpu.SMEM), q_spec, kv_spec, kv_spec, c_spec, c_spec,
                  pl.BlockSpec((1, LANES), lambda b, h, i: (0, 0))],
        out_specs=q_spec,
        out_shape=jax.ShapeDtypeStruct((n, d), BF16),
        compiler_params=_cparams("arbitrary", "arbitrary", "arbitrary"),
        name="lat_diff_attn",
    )(lam, q, k, v, ck, cv, subg)


def _post_attn_kernel(o_ref, *refs, n_x, n_lat_tiles):
    x_refs = refs[:n_x]
    (mod_ref, wo_ref, g_ref, rwh_ref, rwl_ref, rb_ref, tri_ref,
     xmid_ref, h_ref, sel_ref, gate_ref, rank_ref, cnt_ref, carry_ref) = refs[n_x:]
    i = pl.program_id(0)

    @pl.when(i == 0)
    def _():
        carry_ref[...] = jnp.zeros_like(carry_ref)

    x1 = _read_tokens(x_refs, i, n_lat_tiles) + mod_ref[2:3, :] * _dot(o_ref[...], wo_ref[...])
    xmid_ref[...] = x1
    h = _rms(x1, g_ref[...]) * (1.0 + mod_ref[4:5, :]) + mod_ref[3:4, :]
    hb = h.astype(BF16)
    h_ref[...] = _pack_rows(hb)
    h_lo = (h - hb.astype(F32)).astype(BF16)
    logits = _dot_nt(rwh_ref[...], hb) + (_dot_nt(rwl_ref[...], hb) + _dot_nt(rwh_ref[...], h_lo))
    scores = _sigmoid(logits)
    n_exp, tm = scores.shape
    biased = scores + rb_ref[...]
    row = lax.broadcasted_iota(jnp.int32, (n_exp, tm), 0).astype(F32)
    total = jnp.zeros((n_exp, tm), F32)
    sels, gates = [], []
    for _ in range(TOP_K):
        m = jnp.max(biased, axis=0, keepdims=True)
        idx = jnp.min(jnp.where(biased == m, row, float(n_exp)), axis=0, keepdims=True)
        hit = row == idx
        gates.append(jnp.sum(jnp.where(hit, scores, 0.0), axis=0, keepdims=True))
        biased = jnp.where(hit, -jnp.inf, biased)
        total = total + jnp.where(hit, 1.0, 0.0)
        sels.append(idx)
    denom = gates[0]
    for g in gates[1:]:
        denom = denom + g
    before = _dot(total.astype(BF16), tri_ref[...]) + carry_ref[:, 0:1]
    for k in range(TOP_K):
        sel_ref[k:k + 1, :] = sels[k].astype(jnp.int32)
        gate_ref[k:k + 1, :] = gates[k] / denom * ROUTED_SCALE
        rank_ref[k:k + 1, :] = jnp.sum(jnp.where(row == sels[k], before, 0.0), axis=0,
                                       keepdims=True).astype(jnp.int32)
    carry_ref[...] = carry_ref[...] + jnp.sum(total, axis=1, keepdims=True)
    cnt_ref[...] = carry_ref[...].astype(jnp.int32)


def _post_attn(o, xs, mod, wo, g, rwh, rwl, rb, *, layer, n_lat, lat_seq, n_lat_batch, tm):
    n, d = o.shape
    n_exp = rwh.shape[0]
    n_lat_tiles = n_lat // tm
    tri = (jnp.arange(tm)[:, None] < jnp.arange(tm)[None, :]).astype(BF16)

    def mod_idx(i):
        return (layer, jnp.minimum(i * tm // lat_seq, n_lat_batch), 0, 0)

    tok = pl.BlockSpec((tm, d), lambda i: (i, 0))
    tok_p = pl.BlockSpec((tm, d // 2), lambda i: (i, 0))
    kt = pl.BlockSpec((TOP_K, tm), lambda i: (0, i))
    const2 = lambda i: (0, 0)
    return pl.pallas_call(
        functools.partial(_post_attn_kernel, n_x=len(xs), n_lat_tiles=n_lat_tiles),
        grid=(n // tm,),
        in_specs=[tok] + _token_specs(xs, tm, n_lat_tiles) + [
                  pl.BlockSpec((None, None, N_MOD, d), mod_idx),
                  pl.BlockSpec((d, d), const2), pl.BlockSpec((1, d), const2),
                  pl.BlockSpec((n_exp, d), const2), pl.BlockSpec((n_exp, d), const2),
                  pl.BlockSpec((n_exp, 1), const2), pl.BlockSpec((tm, tm), const2)],
        out_specs=[tok, tok_p, kt, kt, kt, pl.BlockSpec((n_exp, LANES), const2)],
        out_shape=[jax.ShapeDtypeStruct((n, d), F32), jax.ShapeDtypeStruct((n, d // 2), jnp.int32),
                   jax.ShapeDtypeStruct((TOP_K, n), jnp.int32),
                   jax.ShapeDtypeStruct((TOP_K, n), F32),
                   jax.ShapeDtypeStruct((TOP_K, n), jnp.int32),
                   jax.ShapeDtypeStruct((n_exp, LANES), jnp.int32)],
        scratch_shapes=[pltpu.VMEM((n_exp, LANES), F32)],
        compiler_params=_cparams("arbitrary"),
        name=f"post_attn{layer}",
    )(o, *xs, mod, wo, g, rwh, rwl, rb, tri)


def _plan_kernel(cnt_ref, sel_ref, rank_ref, dest_ref, te_ref, nu_ref, *, tmb):
    n_exp = cnt_ref.shape[0]
    sel = sel_ref[...]
    tile_start = lax.broadcasted_iota(jnp.int32, te_ref.shape, 1) * tmb
    dest = rank_ref[...]
    te = jnp.zeros(te_ref.shape, jnp.int32)
    start = jnp.zeros((1, 1), jnp.int32)
    for e in range(n_exp):
        dest = dest + jnp.where(sel == e, start, 0)
        start = start + ((cnt_ref[e:e + 1, 0:1] + (tmb - 1)) & -tmb)
        te = te + jnp.where(start <= tile_start, 1, 0)
    te_ref[...] = jnp.minimum(te, n_exp - 1)
    nu_ref[...] = jnp.broadcast_to(start, nu_ref.shape)
    for c in range(dest_ref.shape[0]):
        dest_ref[c] = dest[:, c * SC_CHUNK:(c + 1) * SC_CHUNK]


def _plan(counts, sel, rank, *, tmb, n_tiles):
    k_top, n = sel.shape
    assert tmb & (tmb - 1) == 0
    tn = _pick_tile(n, 16 * SC_CHUNK)
    te_w = -(-n_tiles // LANES) * LANES
    tok = pl.BlockSpec((k_top, tn), lambda i: (0, i))
    const2 = lambda i: (0, 0)
    dest, te, nu = pl.pallas_call(
        functools.partial(_plan_kernel, tmb=tmb),
        grid=(n // tn,),
        in_specs=[pl.BlockSpec(counts.shape, const2), tok, tok],
        out_specs=[pl.BlockSpec((tn // SC_CHUNK, k_top, SC_CHUNK), lambda i: (i, 0, 0)),
                   pl.BlockSpec((1, te_w), const2), pl.BlockSpec((1, LANES), const2)],
        out_shape=[jax.ShapeDtypeStruct((n // SC_CHUNK, k_top, SC_CHUNK), jnp.int32),
                   jax.ShapeDtypeStruct((1, te_w), jnp.int32),
                   jax.ShapeDtypeStruct((1, LANES), jnp.int32)],
        compiler_params=_cparams("arbitrary"),
        name="plan",
    )(counts, sel, rank)
    return dest, te[0, :n_tiles], nu[0, :1] // tmb


def _expert_kernel(te_ref, nu_ref, x_ref, wg_ref, wu_ref, wd_ref, y_ref, wg_b, wu_b, wd_b):
    i = pl.program_id(0)

    @pl.when(i < nu_ref[0])
    def _():
        @pl.when((i == 0) | (te_ref[i] != te_ref[jnp.maximum(i - 1, 0)]))
        def _():
            wg_b[...] = wg_ref[...].astype(BF16)
            wu_b[...] = wu_ref[...].astype(BF16)
            wd_b[...] = wd_ref[...].astype(BF16)

        x = jnp.concatenate(_unpack_rows(x_ref[...]), axis=1).astype(BF16)
        gate = _dot(x, wg_b[...])
        a = gate * _sigmoid(gate) * _dot(x, wu_b[...])
        y_ref[...] = _pack_rows(_dot(a.astype(BF16), wd_b[...]).astype(BF16))


def _experts(tile_expert, n_used, xb, wg, wu, wd, *, layer, tmb):
    cap, dp = xb.shape
    f, d = wd.shape[2:]

    def row_idx(i, te, nu):
        return (jnp.minimum(i, nu[0] - 1), 0)

    def w_idx(i, te, nu):
        return (layer, te[i], 0, 0)

    return pl.pallas_call(
        _expert_kernel,
        grid_spec=pltpu.PrefetchScalarGridSpec(
            num_scalar_prefetch=2,
            grid=(cap // tmb,),
            in_specs=[pl.BlockSpec((tmb, dp), row_idx),
                      pl.BlockSpec((None, None, d, f), w_idx),
                      pl.BlockSpec((None, None, d, f), w_idx),
                      pl.BlockSpec((None, None, f, d), w_idx)],
            out_specs=pl.BlockSpec((tmb, dp), row_idx),
            scratch_shapes=[pltpu.VMEM((d, f), BF16), pltpu.VMEM((d, f), BF16),
                            pltpu.VMEM((f, d), BF16)]),
        out_shape=jax.ShapeDtypeStruct((cap, dp), jnp.int32),
        compiler_params=_cparams("arbitrary"),
        name="experts",
    )(tile_expert, n_used, xb, wg, wu, wd)


def _combine_kernel(x_ref, h_ref, yg_ref, gt_ref, mod_ref, wgu_ref, wd_ref, fg_ref, *o_refs, f,
                    n_lat_tiles):
    h = jnp.concatenate(_unpack_rows(h_ref[...]), axis=1).astype(BF16)
    gu = _dot(h, wgu_ref[...])
    gate, up = gu[:, :f], gu[:, f:]
    shared = _dot((gate * _sigmoid(gate) * up).astype(BF16), wd_ref[...])
    lo = hi = None
    for k in range(TOP_K):
        y_lo, y_hi = _unpack_rows(yg_ref[k])
        g = gt_ref[:, k:k + 1]
        lo = g * y_lo if lo is None else lo + g * y_lo
        hi = g * y_hi if hi is None else hi + g * y_hi
    acc = shared + jnp.concatenate([lo, hi], axis=1)
    x2 = x_ref[...] + mod_ref[5:6, :] * acc
    if len(o_refs) == 1:
        o_refs[0][...] = x2
    else:
        y = _rms(x2, fg_ref[...])
        i = pl.program_id(0)

        @pl.when(i < n_lat_tiles)
        def _():
            o_refs[0][...] = y

        @pl.when(i >= n_lat_tiles)
        def _():
            o_refs[1][...] = y


def _combine(x, h, yg, gates_t, mod, wgu, wd, final_g, *, layer, n_lat, lat_seq, n_lat_batch, tm,
             final):
    n, d = x.shape
    f = wd.shape[0]
    n_lat_tiles = n_lat // tm

    def mod_idx(i):
        return (layer, jnp.minimum(i * tm // lat_seq, n_lat_batch), 0, 0)

    tok = pl.BlockSpec((tm, d), lambda i: (i, 0))
    tok_p = pl.BlockSpec((tm, d // 2), lambda i: (i, 0))
    const2 = lambda i: (0, 0)
    if final:
        out_specs = [pl.BlockSpec((tm, d), lambda i: (jnp.minimum(i, n_lat_tiles - 1), 0)),
                     pl.BlockSpec((tm, d), lambda i: (jnp.maximum(i - n_lat_tiles, 0), 0))]
        out_shape = [jax.ShapeDtypeStruct((n_lat, d), F32), jax.ShapeDtypeStruct((n - n_lat, d), F32)]
    else:
        out_specs, out_shape = [tok], [jax.ShapeDtypeStruct((n, d), F32)]
    return pl.pallas_call(
        functools.partial(_combine_kernel, f=f, n_lat_tiles=n_lat_tiles),
        grid=(n // tm,),
        in_specs=[tok, tok_p, pl.BlockSpec((TOP_K, tm, d // 2), lambda i: (0, i, 0)),
                  pl.BlockSpec((tm, TOP_K), lambda i: (i, 0)),
                  pl.BlockSpec((None, None, N_MOD, d), mod_idx),
                  pl.BlockSpec((d, 2 * f), const2), pl.BlockSpec((f, d), const2),
                  pl.BlockSpec((1, d), const2)],
        out_specs=out_specs,
        out_shape=out_shape,
        compiler_params=_cparams("arbitrary"),
        name=f"combine{layer}",
    )(x, h, yg, gates_t, mod, wgu, wd, final_g)


SC_CHUNK = 128


def _sc_workers():
    info = plsc.get_sparse_core_info()
    return info.num_cores, info.num_subcores


def _sc_dispatch(rows, dest, cap):
    n, w = rows.shape
    nc, ns = _sc_workers()
    per_w = n // (nc * ns)
    assert per_w * nc * ns == n and per_w % SC_CHUNK == 0
    mesh = plsc.VectorSubcoreMesh(core_axis_name="c", subcore_axis_name="s")

    @functools.partial(
        pl.kernel, mesh=mesh, out_type=jax.ShapeDtypeStruct((cap, w), rows.dtype),
        scratch_types=[pltpu.VMEM((TOP_K, SC_CHUNK), jnp.int32),
                       pltpu.VMEM((SC_CHUNK, w), rows.dtype)],
        name="sc_dispatch")
    def run(rows_hbm, dest_hbm, out_hbm, idx_v, rows_v):
        wid = lax.axis_index("s") * nc + lax.axis_index("c")

        @pl.loop(0, per_w // SC_CHUNK)
        def _(ci):
            base = pl.multiple_of(wid * per_w + ci * SC_CHUNK, SC_CHUNK)
            pltpu.sync_copy(dest_hbm.at[wid * (per_w // SC_CHUNK) + ci], idx_v)
            pltpu.sync_copy(rows_hbm.at[pl.ds(base, SC_CHUNK)], rows_v)
            for k in range(TOP_K):
                pltpu.sync_copy(rows_v, out_hbm.at[idx_v.at[k]])

    return run(rows, dest)


def _sc_collect(table, dest):
    n_chunks, k_top, _ = dest.shape
    n = n_chunks * SC_CHUNK
    w = table.shape[1]
    nc, ns = _sc_workers()
    per_w = n // (nc * ns)
    assert per_w * nc * ns == n and per_w % SC_CHUNK == 0
    mesh = plsc.VectorSubcoreMesh(core_axis_name="c", subcore_axis_name="s")

    @functools.partial(
        pl.kernel, mesh=mesh, out_type=jax.ShapeDtypeStruct((k_top, n, w), table.dtype),
        scratch_types=[pltpu.VMEM((TOP_K, SC_CHUNK), jnp.int32),
                       pltpu.VMEM((SC_CHUNK, w), table.dtype)],
        name="sc_collect")
    def run(table_hbm, dest_hbm, out_hbm, idx_v, rows_v):
        wid = lax.axis_index("s") * nc + lax.axis_index("c")

        @pl.loop(0, per_w // SC_CHUNK)
        def _(ci):
            base = pl.multiple_of(wid * per_w + ci * SC_CHUNK, SC_CHUNK)
            pltpu.sync_copy(dest_hbm.at[wid * (per_w // SC_CHUNK) + ci], idx_v)
            for k in range(TOP_K):
                pltpu.sync_copy(table_hbm.at[idx_v.at[k]], rows_v)
                pltpu.sync_copy(rows_v, out_hbm.at[k, pl.ds(base, SC_CHUNK)])

    return run(table, dest)


def _split_bf16(w):
    hi = w.astype(BF16)
    return hi, (w - hi.astype(F32)).astype(BF16)


def kernel(x_prompt, x_sample, cache_k, cache_v, c, c_ctx, ada_w, ada_b, norm1_g, norm2_g, w_qkv, w_o, na_rpb, diff_lambda, diff_subln_g, router_w, router_b, exp_w_gate, exp_w_up, exp_w_down, shared_w_gate, shared_w_up, shared_w_down, final_g):
    batch, seq, d = x_prompt.shape
    n_lat_batch, lat_seq, _ = x_sample.shape
    depth = w_qkv.shape[0]
    n_exp = router_w.shape[-1]
    n_lat = n_lat_batch * lat_seq
    n_ctx = batch * seq
    n = n_lat + n_ctx
    assert d % LANES == 0 and na_rpb.shape[1] * HEAD_DIM == d
    assert diff_lambda.shape[-1] == HEAD_DIM and lat_seq % GRID_W == 0
    assert lat_seq // GRID_W >= NA_ROWS and n_lat % seq == 0
    tm = _pick_tile(math.gcd(lat_seq, n_ctx), 512)
    tmb = 512
    scale = HEAD_DIM ** -0.5 * LOG2E

    xs = [x_sample.reshape(n_lat, d), x_prompt.reshape(n_ctx, d)]
    mod_rows = -(-(n_lat_batch + 1) // 8) * 8
    cond = jnp.zeros((mod_rows, d), F32).at[:n_lat_batch].set(c).at[n_lat_batch].set(c_ctx)
    mod = _modulation(cond, ada_w, ada_b).reshape(depth, mod_rows, N_MOD, d)

    w_qkv_b = w_qkv.astype(BF16)
    w_o_b = w_o.astype(BF16)
    rope_tab = _rope_tables(lat_seq, tm)
    caches = None
    for i in range(depth):
        is_diff = i % 2 == 1
        j = i // 2
        q, k, v, *caches = _qkv(xs, mod, norm1_g.reshape(depth, 1, d), w_qkv_b,
                                rope_tab if is_diff else None, caches, layer=i, n_lat=n_lat,
                                batch=batch, seq=seq, lat_seq=lat_seq, n_lat_batch=n_lat_batch,
                                tm=tm, scale=scale)
        if not is_diff:
            o = _na_attention(q, k, v, cache_k, cache_v,
                              _na_bias_table(na_rpb[j], lat_seq // GRID_W), layer=i,
                              n_lat_batch=n_lat_batch, lat_seq=lat_seq)
            o = _ctx_dense(q, k, v, o, n_lat=n_lat, seq=seq)
        else:
            lam_init = 0.8 - 0.6 * math.exp(-0.3 * i)
            lp = diff_lambda[j].astype(F32)
            lam = (jnp.exp(jnp.sum(lp[0] * lp[1])) - jnp.exp(jnp.sum(lp[2] * lp[3]))
                   + lam_init).reshape(1)
            subg = diff_subln_g[j].reshape(1, LANES).astype(F32)
            o = _lat_diff(lam, q, k, v, cache_k, cache_v, subg, layer=i,
                          n_lat_batch=n_lat_batch, lat_seq=lat_seq, out_scale=1.0 - lam_init)
            o = _ctx_diff(lam, q, k, v, subg, o, n_lat=n_lat, seq=seq, out_scale=1.0 - lam_init)

        rwh, rwl = _split_bf16(router_w[i].T)
        xmid, h2, sel, gates, rank, counts = _post_attn(
            o, xs, mod, w_o_b[i], norm2_g[i].reshape(1, d), rwh, rwl,
            router_b[i].reshape(n_exp, 1).astype(F32), layer=i, n_lat=n_lat, lat_seq=lat_seq,
            n_lat_batch=n_lat_batch, tm=tm)

        n_tiles = -(-(n * TOP_K) // tmb) + n_exp
        cap = n_tiles * tmb
        dest, tile_expert, n_used = _plan(counts, sel, rank, tmb=tmb, n_tiles=n_tiles)
        xb = _sc_dispatch(h2, dest, cap)
        yb = _experts(tile_expert, n_used, xb, exp_w_gate, exp_w_up, exp_w_down, layer=i, tmb=tmb)
        yg = _sc_collect(yb, dest)
        swgu = jnp.concatenate([shared_w_gate[i], shared_w_up[i]], axis=-1).astype(BF16)
        xs = _combine(xmid, h2, yg, gates.T, mod, swgu, shared_w_down[i].astype(BF16),
                      final_g.reshape(1, d), layer=i, n_lat=n_lat, lat_seq=lat_seq,
                      n_lat_batch=n_lat_batch, tm=tm, final=i == depth - 1)

    y_sample = xs[0].reshape(n_lat_batch, lat_seq, d)
    y_prompt = xs[1].reshape(batch, seq, d)
    return (y_prompt, y_sample, caches[0], caches[1])
```

```python
import functools
import math

import jax
import jax.numpy as jnp
from jax import lax
from jax.experimental import pallas as pl
from jax.experimental.pallas import tpu as pltpu
from jax.experimental.pallas import tpu_sc as plsc

GRID_W = 64
NA_ROWS = 8
NA_COLS = 16
TOP_K = 8
ROUTED_SCALE = 2.5
ROPE_BASE = 10000.0
EPS = 1e-6
N_MOD = 6
HEAD_DIM = 64
LANES = 128
LOG2E = math.log2(math.e)
MASKED = -1e30
QUERY_GROUPS = 2
VMEM_LIMIT = 56 * 1024 * 1024

F32 = jnp.float32
BF16 = jnp.bfloat16


def _cparams(*sem):
    return pltpu.CompilerParams(dimension_semantics=sem, vmem_limit_bytes=VMEM_LIMIT)


def _dot(a, b):
    return jnp.dot(a, b, preferred_element_type=F32)


def _dot_nt(a, b):
    return lax.dot_general(a, b, (((1,), (1,)), ((), ())), preferred_element_type=F32)


def _sigmoid(x):
    return 1.0 / (1.0 + jnp.exp(-x))


def _rms(x, g):
    return x * lax.rsqrt(jnp.mean(x * x, axis=-1, keepdims=True) + EPS) * g


def _pack_rows(xb):
    half = xb.shape[1] // 2
    u = lax.bitcast_convert_type(xb.astype(F32), jnp.uint32)
    packed = (u[:, :half] >> 16) | (u[:, half:] & jnp.uint32(0xFFFF0000))
    return lax.bitcast_convert_type(packed, jnp.int32)


def _unpack_rows(p):
    u = lax.bitcast_convert_type(p, jnp.uint32)
    return (lax.bitcast_convert_type(u << 16, F32),
            lax.bitcast_convert_type(u & jnp.uint32(0xFFFF0000), F32))


def _pick_tile(n, pref):
    t = pref
    while n % t:
        t //= 2
    return t


def _mod_kernel(c_ref, w_ref, b_ref, o_ref):
    c = c_ref[...]
    o_ref[...] = _dot(c * _sigmoid(c), w_ref[...]) + b_ref[...]


def _modulation(cond, ada_w, ada_b):
    depth, d, n6 = ada_w.shape
    rows = cond.shape[0]
    tn = _pick_tile(n6, 512)
    return pl.pallas_call(
        _mod_kernel,
        grid=(depth, n6 // tn),
        in_specs=[
            pl.BlockSpec((rows, d), lambda l, j: (0, 0)),
            pl.BlockSpec((None, d, tn), lambda l, j: (l, 0, j)),
            pl.BlockSpec((None, 1, tn), lambda l, j: (l, 0, j)),
        ],
        out_specs=pl.BlockSpec((None, rows, tn), lambda l, j: (l, 0, j)),
        out_shape=jax.ShapeDtypeStruct((depth, rows, n6), F32),
        compiler_params=_cparams("arbitrary", "arbitrary"),
        name="modulation",
    )(cond, ada_w, ada_b.reshape(depth, 1, n6))


def _read_tokens(x_refs, i, n_lat_tiles):
    if len(x_refs) == 1:
        return x_refs[0][...]
    return jnp.where(i < n_lat_tiles, x_refs[0][...], x_refs[1][...])


def _token_specs(xs, tm, n_lat_tiles):
    d = xs[0].shape[1]
    if len(xs) == 1:
        return [pl.BlockSpec((tm, d), lambda i: (i, 0))]
    return [pl.BlockSpec((tm, d), lambda i: (jnp.minimum(i, n_lat_tiles - 1), 0)),
            pl.BlockSpec((tm, d), lambda i: (jnp.maximum(i - n_lat_tiles, 0), 0))]


def _qkv_kernel(*refs, d, scale, rope, n_lat_tiles, n_x, n_alias):
    x_refs = refs[:n_x]
    mod_ref, g_ref, w_ref = refs[n_x:n_x + 3]
    rest = refs[n_x + 3:]
    if rope:
        rope_ref, wvt_ref, rest = rest[0], rest[1], rest[2:]
    rest = rest[n_alias:]
    q_ref, k_ref, v_ref, kf_ref, vf_ref = rest[:5]
    i = pl.program_id(0)
    x = _read_tokens(x_refs, i, n_lat_tiles)
    h = _rms(x, g_ref[...]) * (1.0 + mod_ref[1:2, :]) + mod_ref[0:1, :]
    hb = h.astype(BF16)

    def rot(t):
        if not rope:
            return t
        cols = []
        for j in range(d // LANES):
            tb = t[:, j * LANES:(j + 1) * LANES]
            cols.append(tb * rope_ref[0]
                        + pltpu.roll(tb, LANES - HEAD_DIM // 2, axis=1) * rope_ref[1]
                        + pltpu.roll(tb, HEAD_DIM // 2, axis=1) * rope_ref[2])
        return jnp.concatenate(cols, axis=1)

    q = _dot(hb, w_ref[:, 0:d])
    q_ref[...] = (rot(q) * scale).astype(BF16)
    k = _dot(hb, w_ref[:, d:2 * d])
    k_ref[...] = rot(k).astype(BF16)
    if rope:
        @pl.when(i < n_lat_tiles)
        def _():
            rest[5][...] = _dot_nt(wvt_ref[...], hb).astype(BF16)

    def context_values():
        v = _dot(hb, w_ref[:, 2 * d:3 * d])
        v_ref[...] = v.astype(BF16)
        return v

    if not rope:
        v = context_values()

    @pl.when(i >= n_lat_tiles)
    def _():
        kf_ref[...] = k.reshape(kf_ref.shape)
        vf_ref[...] = (context_values() if rope else v).reshape(vf_ref.shape)


def _qkv(xs, mod, g, w, rope_tab, caches, *, layer, n_lat, batch, seq, lat_seq, n_lat_batch, tm,
         scale):
    d = xs[0].shape[1]
    depth = w.shape[0]
    n = n_lat + batch * seq
    n_lat_tiles = n_lat // tm
    rope = rope_tab is not None
    assert tm % seq == 0

    def mod_idx(i):
        return (layer, jnp.minimum(i * tm // lat_seq, n_lat_batch), 0, 0)

    in_specs = _token_specs(xs, tm, n_lat_tiles) + [
        pl.BlockSpec((None, None, N_MOD, d), mod_idx),
        pl.BlockSpec((None, 1, d), lambda i: (layer, 0, 0)),
        pl.BlockSpec((None, d, 3 * d), lambda i: (layer, 0, 0)),
    ]
    args = list(xs) + [mod, g, w]
    if rope:
        seq_tiles = lat_seq // tm
        in_specs.append(pl.BlockSpec(
            (3, tm, LANES),
            lambda i: (0, jnp.where(i < n_lat_tiles, i % seq_tiles, seq_tiles), 0)))
        in_specs.append(pl.BlockSpec((d, d), lambda i: (0, 0)))
        args += [rope_tab, w[layer, :, 2 * d:].T]
    aliases = {}
    if caches is not None:
        aliases = {len(args): 3, len(args) + 1: 4}
        in_specs += [pl.BlockSpec(memory_space=pl.ANY)] * 2
        args += list(caches)
    tok = pl.BlockSpec((tm, d), lambda i: (i, 0))
    cache_spec = pl.BlockSpec((tm // seq, None, seq, d),
                              lambda i: (jnp.maximum(i - n_lat_tiles, 0), layer, 0, 0))
    out_specs = [tok, tok, tok, cache_spec, cache_spec]
    out_shape = ([jax.ShapeDtypeStruct((n, d), BF16)] * 3
                 + [jax.ShapeDtypeStruct((batch, depth, seq, d), F32)] * 2)
    if rope:
        out_specs.append(pl.BlockSpec((None, d, tm),
                                      lambda i: (jnp.minimum(i, n_lat_tiles - 1), 0, 0)))
        out_shape.append(jax.ShapeDtypeStruct((n_lat_tiles, d, tm), BF16))
    return pl.pallas_call(
        functools.partial(_qkv_kernel, d=d, scale=scale, rope=rope, n_lat_tiles=n_lat_tiles,
                          n_x=len(xs), n_alias=len(aliases)),
        grid=(n // tm,),
        in_specs=in_specs,
        out_specs=out_specs,
        out_shape=out_shape,
        input_output_aliases=aliases,
        compiler_params=_cparams("arbitrary"),
        name=f"qkv{layer}",
    )(*args)


def _rope_tables(lat_seq, tm):
    n_freq = HEAD_DIM // 4
    inv = ROPE_BASE ** (-jnp.arange(n_freq, dtype=F32) / n_freq)
    pos = jnp.arange(lat_seq)
    row = (pos // GRID_W).astype(F32)
    col = (pos % GRID_W).astype(F32)
    ang = jnp.concatenate([row[:, None] * inv, col[:, None] * inv], axis=-1)
    reps = LANES // (HEAD_DIM // 2)
    cos = jnp.tile(jnp.cos(ang), (1, reps))
    sin = jnp.tile(jnp.sin(ang), (1, reps))
    first_half = (jnp.arange(LANES) % HEAD_DIM) < HEAD_DIM // 2
    s_next = jnp.where(first_half, -sin, 0.0)
    s_prev = jnp.where(first_half, 0.0, sin)
    ident = jnp.stack([jnp.ones((tm, LANES), F32), jnp.zeros((tm, LANES), F32),
                       jnp.zeros((tm, LANES), F32)])
    return jnp.concatenate([jnp.stack([cos, s_next, s_prev]), ident], axis=1)


def _lane_lo():
    return lax.broadcasted_iota(jnp.int32, (1, LANES), 1) < HEAD_DIM


def _ctx_dense_kernel(q_ref, k_ref, v_ref, o_all_ref, o_ref, *, d):
    del o_all_ref
    lo = _lane_lo()
    cols = []
    for j in range(d // LANES):
        blk = slice(j * LANES, (j + 1) * LANES)
        q2, k2, v2 = q_ref[:, blk], k_ref[:, blk], v_ref[:, blk]
        outs = []
        for qm in (jnp.where(lo, q2, 0), jnp.where(lo, 0, q2)):
            s = _dot_nt(qm, k2)
            p = jnp.exp2(s - jnp.max(s, axis=-1, keepdims=True))
            outs.append(_dot(p.astype(BF16), v2) / jnp.sum(p, axis=-1, keepdims=True))
        cols.append(jnp.where(lo, outs[0], outs[1]).astype(BF16))
    o_ref[...] = jnp.concatenate(cols, axis=1)


def _ctx_dense(q, k, v, o_all, *, n_lat, seq):
    n, d = q.shape
    off = n_lat // seq
    spec = pl.BlockSpec((seq, d), lambda b: (off + b, 0))
    return pl.pallas_call(
        functools.partial(_ctx_dense_kernel, d=d),
        grid=((n - n_lat) // seq,),
        in_specs=[spec, spec, spec, pl.BlockSpec(memory_space=pl.ANY)],
        out_specs=spec,
        out_shape=jax.ShapeDtypeStruct((n, d), BF16),
        input_output_aliases={3: 0},
        compiler_params=_cparams("arbitrary"),
        name="ctx_dense_attn",
    )(q, k, v, o_all)


def _subln(o, g_ref, out_scale):
    return _rms(o, g_ref[...]) * out_scale


def _ctx_diff_kernel(lam_ref, q_ref, k_ref, v_ref, g_ref, o_all_ref, o_ref, *, d, out_scale):
    del o_all_ref
    lo = _lane_lo()
    lam = lam_ref[0]
    cols = []
    for j in range(d // LANES):
        blk = slice(j * LANES, (j + 1) * LANES)
        q2, k2, v2 = q_ref[:, blk], k_ref[:, blk], v_ref[:, blk]
        ps = []
        for qm in (jnp.where(lo, q2, 0), jnp.where(lo, 0, q2)):
            s = _dot_nt(qm, k2)
            p = jnp.exp2(s - jnp.max(s, axis=-1, keepdims=True))
            ps.append(p / jnp.sum(p, axis=-1, keepdims=True))
        o = _dot((ps[0] - lam * ps[1]).astype(BF16), v2)
        cols.append(_subln(o, g_ref, out_scale).astype(BF16))
    o_ref[...] = jnp.concatenate(cols, axis=1)


def _ctx_diff(lam, q, k, v, subg, o_all, *, n_lat, seq, out_scale):
    n, d = q.shape
    off = n_lat // seq
    spec = pl.BlockSpec((seq, d), lambda b: (off + b, 0))
    return pl.pallas_call(
        functools.partial(_ctx_diff_kernel, d=d, out_scale=out_scale),
        grid=((n - n_lat) // seq,),
        in_specs=[pl.BlockSpec(memory_space=pltpu.SMEM), spec, spec, spec,
                  pl.BlockSpec((1, LANES), lambda b: (0, 0)), pl.BlockSpec(memory_space=pl.ANY)],
        out_specs=spec,
        out_shape=jax.ShapeDtypeStruct((n, d), BF16),
        input_output_aliases={5: 0},
        compiler_params=_cparams("arbitrary"),
        name="ctx_diff_attn",
    )(lam, q, k, v, subg, o_all)


def _na_kernel(q_ref, k_ref, v_ref, ck_ref, cv_ref, bias_ref, o_ref, *, d, rows):
    lo = _lane_lo()
    r = pl.program_id(1)
    win = NA_ROWS * GRID_W
    start = pl.multiple_of(jnp.clip(r - NA_ROWS // 2, 0, rows - NA_ROWS) * GRID_W, GRID_W)
    n_heads = 2 * (d // LANES)

    def blk(h):
        return slice(h // 2 * LANES, (h // 2 + 1) * LANES)

    scores = []
    for h in range(n_heads):
        q2 = q_ref[:, blk(h)]
        qm = jnp.where(lo, q2, 0) if h % 2 == 0 else jnp.where(lo, 0, q2)
        scores.append((_dot_nt(qm, k_ref[pl.ds(start, win), blk(h)]) + bias_ref[h],
                       _dot_nt(qm, ck_ref[:, blk(h)].astype(BF16))))
    probs = []
    for s_loc, s_ctx in scores:
        m = jnp.maximum(jnp.max(s_loc, axis=-1, keepdims=True),
                        jnp.max(s_ctx, axis=-1, keepdims=True))
        p_loc = jnp.exp2(s_loc - m)
        p_ctx = jnp.exp2(s_ctx - m)
        l = jnp.sum(p_loc, axis=-1, keepdims=True) + jnp.sum(p_ctx, axis=-1, keepdims=True)
        probs.append((p_loc.astype(BF16), p_ctx.astype(BF16), l))
    outs = []
    for h, (p_loc, p_ctx, l) in enumerate(probs):
        outs.append((_dot(p_loc, v_ref[pl.ds(start, win), blk(h)])
                     + _dot(p_ctx, cv_ref[:, blk(h)].astype(BF16))) / l)
    o_ref[...] = jnp.concatenate(
        [jnp.where(lo, outs[h], outs[h + 1]).astype(BF16) for h in range(0, n_heads, 2)], axis=1)


def _na_bias_table(rpb, rows):
    h = rpb.shape[0]
    delta = jnp.arange(NA_ROWS)
    kr = jnp.arange(NA_ROWS)
    c = jnp.arange(GRID_W)
    kc = jnp.arange(GRID_W)
    off_r = kr[None, :] - delta[:, None] + (NA_ROWS - 1)
    c0 = jnp.clip(c - NA_COLS // 2, 0, GRID_W - NA_COLS)
    off_c = kc[None, :] - c[:, None] + (NA_COLS - 1)
    inside = (kc[None, :] >= c0[:, None]) & (kc[None, :] < c0[:, None] + NA_COLS)
    oh_r = jax.nn.one_hot(off_r, 2 * NA_ROWS - 1, dtype=F32)
    oh_c = jax.nn.one_hot(off_c, 2 * NA_COLS - 1, dtype=F32)
    t = jnp.einsum('hrs,dkr,cqs->dhckq', rpb.astype(F32), oh_r, oh_c,
                   precision=lax.Precision.HIGHEST)
    t = jnp.where(inside[None, None, :, None, :], t * LOG2E, MASKED)
    return t.reshape(NA_ROWS, h, GRID_W, NA_ROWS * GRID_W)


def _na_attention(q, k, v, ck, cv, bias, *, layer, n_lat_batch, lat_seq):
    n, d = q.shape
    rows = lat_seq // GRID_W
    past = ck.shape[2]
    n_heads = bias.shape[1]

    def delta_idx(b, r):
        return (r - jnp.clip(r - NA_ROWS // 2, 0, rows - NA_ROWS), 0, 0, 0)

    kv_spec = pl.BlockSpec((lat_seq, d), lambda b, r: (b, 0))
    c_spec = pl.BlockSpec((None, None, past, d), lambda b, r: (b, layer, 0, 0))
    return pl.pallas_call(
        functools.partial(_na_kernel, d=d, rows=rows),
        grid=(n_lat_batch, rows),
        in_specs=[pl.BlockSpec((GRID_W, d), lambda b, r: (b * rows + r, 0)), kv_spec, kv_spec,
                  c_spec, c_spec,
                  pl.BlockSpec((None, n_heads, GRID_W, NA_ROWS * GRID_W), delta_idx)],
        out_specs=pl.BlockSpec((GRID_W, d), lambda b, r: (b * rows + r, 0)),
        out_shape=jax.ShapeDtypeStruct((n, d), BF16),
        compiler_params=_cparams("arbitrary", "arbitrary"),
        name="na_attn",
    )(q, k, v, ck, cv, bias)


def _lat_diff_kernel(lam_ref, q_ref, k_ref, vt_ref, ck_ref, cv_ref, g_ref, o_ref, *, out_scale):
    lo = _lane_lo()
    lam = lam_ref[0]
    q2 = q_ref[...]
    tq = q2.shape[0]
    tg = tq // QUERY_GROUPS
    qs = []
    for g in range(QUERY_GROUPS):
        qg = q2[g * tg:(g + 1) * tg, :]
        qs += [jnp.where(lo, qg, 0), jnp.where(lo, 0, qg)]

    def softmax_step(state, s):
        m, l, acc = state
        m_new = jnp.maximum(m, jnp.max(s, axis=0, keepdims=True))
        alpha = jnp.exp2(m - m_new)
        p = jnp.exp2(s - m_new)
        return m_new, alpha * l + jnp.sum(p, axis=0, keepdims=True), alpha * acc, p.astype(BF16)

    def scores(kt):
        return [_dot_nt(kt, qm) for qm in qs]

    def absorb(states, ss, vt):
        out = []
        for g in range(0, len(qs), 2):
            stepped = [softmax_step(states[g + j], ss[g + j]) for j in range(2)]
            out += [(m, l, acc + _dot(vt, p)) for m, l, acc, p in stepped]
        return out

    tile = vt_ref.shape[2]
    n_tiles = vt_ref.shape[0]

    def keys(t):
        return k_ref[t * tile:(t + 1) * tile, :] if t < n_tiles else ck_ref[...].astype(BF16)

    def values_t(t):
        return vt_ref[t] if t < n_tiles else cv_ref[...].T.astype(BF16)

    init = (jnp.full((1, tg), -jnp.inf, F32), jnp.zeros((1, tg), F32), jnp.zeros((LANES, tg), F32))
    states = [init] * len(qs)
    ss = scores(keys(0))
    for t in range(n_tiles + 1):
        ss_next = scores(keys(t + 1)) if t < n_tiles else None
        states = absorb(states, ss, values_t(t))
        ss = ss_next
    o = jnp.concatenate([states[g][2] / states[g][1] - lam * (states[g + 1][2] / states[g + 1][1])
                         for g in range(0, len(qs), 2)], axis=1).T
    o_ref[...] = _subln(o, g_ref, out_scale).astype(BF16)


def _lat_diff(lam, q, k, vt, ck, cv, subg, *, layer, n_lat_batch, lat_seq, out_scale):
    n, d = q.shape
    past = ck.shape[2]
    tk = vt.shape[2]
    tq = _pick_tile(lat_seq, 512)
    qt = lat_seq // tq
    kt = lat_seq // tk
    c_spec = pl.BlockSpec((None, None, past, LANES), lambda b, h, i: (b, layer, 0, h))
    q_spec = pl.BlockSpec((tq, LANES), lambda b, h, i: (b * qt + i, h))
    return pl.pallas_call(
        functools.partial(_lat_diff_kernel, out_scale=out_scale),
        grid=(n_lat_batch, d // LANES, qt),
        in_specs=[pl.BlockSpec(memory_space=pltpu.SMEM), q_spec,
                  pl.BlockSpec((lat_seq, LANES), lambda b, h, i: (b, h)),
                  pl.BlockSpec((kt, LANES, tk), lambda b, h, i: (b, h, 0)),
                  c_spec, c_spec, pl.BlockSpec((1, LANES), lambda b, h, i: (0, 0))],
        out_specs=q_spec,
        out_shape=jax.ShapeDtypeStruct((n, d), BF16),
        compiler_params=_cparams("arbitrary", "arbitrary", "arbitrary"),
        name="lat_diff_attn",
    )(lam, q, k, vt, ck, cv, subg)


def _post_attn_kernel(o_ref, *refs, n_x, n_lat_tiles):
    x_refs = refs[:n_x]
    (mod_ref, wo_ref, g_ref, rwh_ref, rwl_ref, rb_ref, tri_ref,
     xmid_ref, h_ref, sel_ref, gate_ref, rank_ref, cnt_ref, carry_ref) = refs[n_x:]
    i = pl.program_id(0)

    @pl.when(i == 0)
    def _():
        carry_ref[...] = jnp.zeros_like(carry_ref)

    x1 = _read_tokens(x_refs, i, n_lat_tiles) + mod_ref[2:3, :] * _dot(o_ref[...], wo_ref[...])
    xmid_ref[...] = x1
    h = _rms(x1, g_ref[...]) * (1.0 + mod_ref[4:5, :]) + mod_ref[3:4, :]
    hb = h.astype(BF16)
    h_ref[...] = _pack_rows(hb)
    h_lo = (h - hb.astype(F32)).astype(BF16)
    logits = _dot_nt(rwh_ref[...], hb) + (_dot_nt(rwl_ref[...], hb) + _dot_nt(rwh_ref[...], h_lo))
    scores = _sigmoid(logits)
    n_exp, tm = scores.shape
    biased = scores + rb_ref[...]
    row = lax.broadcasted_iota(jnp.int32, (n_exp, tm), 0).astype(F32)
    total = jnp.zeros((n_exp, tm), F32)
    sels, gates = [], []
    for _ in range(TOP_K):
        m = jnp.max(biased, axis=0, keepdims=True)
        idx = jnp.min(jnp.where(biased == m, row, float(n_exp)), axis=0, keepdims=True)
        hit = row == idx
        gates.append(jnp.sum(jnp.where(hit, scores, 0.0), axis=0, keepdims=True))
        biased = jnp.where(hit, -jnp.inf, biased)
        total = total + jnp.where(hit, 1.0, 0.0)
        sels.append(idx)
    denom = gates[0]
    for g in gates[1:]:
        denom = denom + g
    before = _dot(total.astype(BF16), tri_ref[...]) + carry_ref[:, 0:1]
    for k in range(TOP_K):
        sel_ref[k:k + 1, :] = sels[k].astype(jnp.int32)
        gate_ref[k:k + 1, :] = gates[k] / denom * ROUTED_SCALE
        rank_ref[k:k + 1, :] = jnp.sum(jnp.where(row == sels[k], before, 0.0), axis=0,
                                       keepdims=True).astype(jnp.int32)
    carry_ref[...] = carry_ref[...] + jnp.sum(total, axis=1, keepdims=True)
    cnt_ref[...] = carry_ref[...].astype(jnp.int32)


def _post_attn(o, xs, mod, wo, g, rwh, rwl, rb, *, layer, n_lat, lat_seq, n_lat_batch, tm):
    n, d = o.shape
    n_exp = rwh.shape[0]
    n_lat_tiles = n_lat // tm
    tri = (jnp.arange(tm)[:, None] < jnp.arange(tm)[None, :]).astype(BF16)

    def mod_idx(i):
        return (layer, jnp.minimum(i * tm // lat_seq, n_lat_batch), 0, 0)

    tok = pl.BlockSpec((tm, d), lambda i: (i, 0))
    tok_p = pl.BlockSpec((tm, d // 2), lambda i: (i, 0))
    kt = pl.BlockSpec((TOP_K, tm), lambda i: (0, i))
    const2 = lambda i: (0, 0)
    return pl.pallas_call(
        functools.partial(_post_attn_kernel, n_x=len(xs), n_lat_tiles=n_lat_tiles),
        grid=(n // tm,),
        in_specs=[tok] + _token_specs(xs, tm, n_lat_tiles) + [
                  pl.BlockSpec((None, None, N_MOD, d), mod_idx),
                  pl.BlockSpec((d, d), const2), pl.BlockSpec((1, d), const2),
                  pl.BlockSpec((n_exp, d), const2), pl.BlockSpec((n_exp, d), const2),
                  pl.BlockSpec((n_exp, 1), const2), pl.BlockSpec((tm, tm), const2)],
        out_specs=[tok, tok_p, kt, kt, kt, pl.BlockSpec((n_exp, LANES), const2)],
        out_shape=[jax.ShapeDtypeStruct((n, d), F32), jax.ShapeDtypeStruct((n, d // 2), jnp.int32),
                   jax.ShapeDtypeStruct((TOP_K, n), jnp.int32),
                   jax.ShapeDtypeStruct((TOP_K, n), F32),
                   jax.ShapeDtypeStruct((TOP_K, n), jnp.int32),
                   jax.ShapeDtypeStruct((n_exp, LANES), jnp.int32)],
        scratch_shapes=[pltpu.VMEM((n_exp, LANES), F32)],
        compiler_params=_cparams("arbitrary"),
        name=f"post_attn{layer}",
    )(o, *xs, mod, wo, g, rwh, rwl, rb, tri)


def _plan_kernel(cnt_ref, sel_ref, rank_ref, dest_ref, te_ref, tv_ref, nu_ref, *, tmb):
    n_exp = cnt_ref.shape[0]
    sel = sel_ref[...]
    tile_start = lax.broadcasted_iota(jnp.int32, te_ref.shape, 1) * tmb
    dest = rank_ref[...]
    te = jnp.zeros(te_ref.shape, jnp.int32)
    tv = jnp.zeros(te_ref.shape, jnp.int32)
    start = jnp.zeros((1, 1), jnp.int32)
    for e in range(n_exp):
        dest = dest + jnp.where(sel == e, start, 0)
        cnt = cnt_ref[e:e + 1, 0:1]
        stop = start + ((cnt + (tmb - 1)) & -tmb)
        mine = (tile_start >= start) & (tile_start < stop)
        tv = tv + jnp.where(mine, jnp.clip(start + cnt - tile_start, 0, tmb), 0)
        start = stop
        te = te + jnp.where(start <= tile_start, 1, 0)
    te_ref[...] = jnp.minimum(te, n_exp - 1)
    tv_ref[...] = tv
    nu_ref[...] = jnp.broadcast_to(start, nu_ref.shape)
    for c in range(dest_ref.shape[0]):
        dest_ref[c] = dest[:, c * SC_CHUNK:(c + 1) * SC_CHUNK]


def _plan(counts, sel, rank, *, tmb, n_tiles):
    k_top, n = sel.shape
    assert tmb & (tmb - 1) == 0
    tn = _pick_tile(n, 16 * SC_CHUNK)
    te_w = -(-n_tiles // LANES) * LANES
    tok = pl.BlockSpec((k_top, tn), lambda i: (0, i))
    const2 = lambda i: (0, 0)
    dest, te, tv, nu = pl.pallas_call(
        functools.partial(_plan_kernel, tmb=tmb),
        grid=(n // tn,),
        in_specs=[pl.BlockSpec(counts.shape, const2), tok, tok],
        out_specs=[pl.BlockSpec((tn // SC_CHUNK, k_top, SC_CHUNK), lambda i: (i, 0, 0)),
                   pl.BlockSpec((1, te_w), const2), pl.BlockSpec((1, te_w), const2),
                   pl.BlockSpec((1, LANES), const2)],
        out_shape=[jax.ShapeDtypeStruct((n // SC_CHUNK, k_top, SC_CHUNK), jnp.int32),
                   jax.ShapeDtypeStruct((1, te_w), jnp.int32),
                   jax.ShapeDtypeStruct((1, te_w), jnp.int32),
                   jax.ShapeDtypeStruct((1, LANES), jnp.int32)],
        compiler_params=_cparams("arbitrary"),
        name="plan",
    )(counts, sel, rank)
    return dest, te[0, :n_tiles], tv[0, :n_tiles], nu[0, :1] // tmb


def _expert_kernel(te_ref, tv_ref, nu_ref, x_ref, wg_ref, wu_ref, wd_ref, y_ref, wg_b, wu_b, wd_b):
    i = pl.program_id(0)
    half = x_ref.shape[0] // 2

    def rows(r):
        return jnp.concatenate(_unpack_rows(x_ref[r * half:(r + 1) * half, :]), axis=1).astype(BF16)

    def swiglu(xs):
        gu = [(_dot(x, wg_b[...]), _dot(x, wu_b[...])) for x in xs]
        acts = [(g * _sigmoid(g) * u).astype(BF16) for g, u in gu]
        return [_pack_rows(_dot(a, wd_b[...]).astype(BF16)) for a in acts]

    @pl.when(i < nu_ref[0])
    def _():
        @pl.when((i == 0) | (te_ref[i] != te_ref[jnp.maximum(i - 1, 0)]))
        def _():
            wg_b[...] = wg_ref[...].astype(BF16)
            wu_b[...] = wu_ref[...].astype(BF16)
            wd_b[...] = wd_ref[...].astype(BF16)

        @pl.when(tv_ref[i] > half)
        def _():
            ya, yb = swiglu([rows(0), rows(1)])
            y_ref[0:half, :] = ya
            y_ref[half:2 * half, :] = yb

        @pl.when(tv_ref[i] <= half)
        def _():
            y_ref[0:half, :] = swiglu([rows(0)])[0]


def _experts(tile_expert, tile_rows, n_used, xb, wg, wu, wd, *, layer, tmb):
    cap, dp = xb.shape
    f, d = wd.shape[2:]

    def row_idx(i, te, tv, nu):
        return (jnp.minimum(i, nu[0] - 1), 0)

    def w_idx(i, te, tv, nu):
        return (layer, te[i], 0, 0)

    return pl.pallas_call(
        _expert_kernel,
        grid_spec=pltpu.PrefetchScalarGridSpec(
            num_scalar_prefetch=3,
            grid=(cap // tmb,),
            in_specs=[pl.BlockSpec((tmb, dp), row_idx),
                      pl.BlockSpec((None, None, d, f), w_idx),
                      pl.BlockSpec((None, None, d, f), w_idx),
                      pl.BlockSpec((None, None, f, d), w_idx)],
            out_specs=pl.BlockSpec((tmb, dp), row_idx),
            scratch_shapes=[pltpu.VMEM((d, f), BF16), pltpu.VMEM((d, f), BF16),
                            pltpu.VMEM((f, d), BF16)]),
        out_shape=jax.ShapeDtypeStruct((cap, dp), jnp.int32),
        compiler_params=_cparams("arbitrary"),
        name="experts",
    )(tile_expert, tile_rows, n_used, xb, wg, wu, wd)


def _combine_kernel(x_ref, h_ref, yg_ref, gt_ref, mod_ref, wgu_ref, wd_ref, fg_ref, *o_refs, f,
                    n_lat_tiles):
    h = jnp.concatenate(_unpack_rows(h_ref[...]), axis=1).astype(BF16)
    gu = _dot(h, wgu_ref[...])
    gate, up = gu[:, :f], gu[:, f:]
    shared = _dot((gate * _sigmoid(gate) * up).astype(BF16), wd_ref[...])
    lo = hi = None
    for k in range(TOP_K):
        y_lo, y_hi = _unpack_rows(yg_ref[k])
        g = gt_ref[:, k:k + 1]
        lo = g * y_lo if lo is None else lo + g * y_lo
        hi = g * y_hi if hi is None else hi + g * y_hi
    acc = shared + jnp.concatenate([lo, hi], axis=1)
    x2 = x_ref[...] + mod_ref[5:6, :] * acc
    if len(o_refs) == 1:
        o_refs[0][...] = x2
    else:
        y = _rms(x2, fg_ref[...])
        i = pl.program_id(0)

        @pl.when(i < n_lat_tiles)
        def _():
            o_refs[0][...] = y

        @pl.when(i >= n_lat_tiles)
        def _():
            o_refs[1][...] = y


def _combine(x, h, yg, gates_t, mod, wgu, wd, final_g, *, layer, n_lat, lat_seq, n_lat_batch, tm,
             final):
    n, d = x.shape
    f = wd.shape[0]
    n_lat_tiles = n_lat // tm

    def mod_idx(i):
        return (layer, jnp.minimum(i * tm // lat_seq, n_lat_batch), 0, 0)

    tok = pl.BlockSpec((tm, d), lambda i: (i, 0))
    tok_p = pl.BlockSpec((tm, d // 2), lambda i: (i, 0))
    const2 = lambda i: (0, 0)
    if final:
        out_specs = [pl.BlockSpec((tm, d), lambda i: (jnp.minimum(i, n_lat_tiles - 1), 0)),
                     pl.BlockSpec((tm, d), lambda i: (jnp.maximum(i - n_lat_tiles, 0), 0))]
        out_shape = [jax.ShapeDtypeStruct((n_lat, d), F32), jax.ShapeDtypeStruct((n - n_lat, d), F32)]
    else:
        out_specs, out_shape = [tok], [jax.ShapeDtypeStruct((n, d), F32)]
    return pl.pallas_call(
        functools.partial(_combine_kernel, f=f, n_lat_tiles=n_lat_tiles),
        grid=(n // tm,),
        in_specs=[tok, tok_p, pl.BlockSpec((TOP_K, tm, d // 2), lambda i: (0, i, 0)),
                  pl.BlockSpec((tm, TOP_K), lambda i: (i, 0)),
                  pl.BlockSpec((None, None, N_MOD, d), mod_idx),
                  pl.BlockSpec((d, 2 * f), const2), pl.BlockSpec((f, d), const2),
                  pl.BlockSpec((1, d), const2)],
        out_specs=out_specs,
        out_shape=out_shape,
        compiler_params=_cparams("arbitrary"),
        name=f"combine{layer}",
    )(x, h, yg, gates_t, mod, wgu, wd, final_g)


SC_CHUNK = 128


def _sc_workers():
    info = plsc.get_sparse_core_info()
    return info.num_cores, info.num_subcores


def _sc_dispatch(rows, dest, cap):
    n, w = rows.shape
    nc, ns = _sc_workers()
    per_w = n // (nc * ns)
    assert per_w * nc * ns == n and per_w % SC_CHUNK == 0
    mesh = plsc.VectorSubcoreMesh(core_axis_name="c", subcore_axis_name="s")

    @functools.partial(
        pl.kernel, mesh=mesh, out_type=jax.ShapeDtypeStruct((cap, w), rows.dtype),
        scratch_types=[pltpu.VMEM((TOP_K, SC_CHUNK), jnp.int32),
                       pltpu.VMEM((SC_CHUNK, w), rows.dtype)],
        name="sc_dispatch")
    def run(rows_hbm, dest_hbm, out_hbm, idx_v, rows_v):
        wid = lax.axis_index("s") * nc + lax.axis_index("c")

        @pl.loop(0, per_w // SC_CHUNK)
        def _(ci):
            base = pl.multiple_of(wid * per_w + ci * SC_CHUNK, SC_CHUNK)
            pltpu.sync_copy(dest_hbm.at[wid * (per_w // SC_CHUNK) + ci], idx_v)
            pltpu.sync_copy(rows_hbm.at[pl.ds(base, SC_CHUNK)], rows_v)
            for k in range(TOP_K):
                pltpu.sync_copy(rows_v, out_hbm.at[idx_v.at[k]])

    return run(rows, dest)


def _sc_collect(table, dest):
    n_chunks, k_top, _ = dest.shape
    n = n_chunks * SC_CHUNK
    w = table.shape[1]
    nc, ns = _sc_workers()
    per_w = n // (nc * ns)
    assert per_w * nc * ns == n and per_w % SC_CHUNK == 0
    mesh = plsc.VectorSubcoreMesh(core_axis_name="c", subcore_axis_name="s")

    @functools.partial(
        pl.kernel, mesh=mesh, out_type=jax.ShapeDtypeStruct((k_top, n, w), table.dtype),
        scratch_types=[pltpu.VMEM((TOP_K, SC_CHUNK), jnp.int32),
                       pltpu.VMEM((SC_CHUNK, w), table.dtype)],
        name="sc_collect")
    def run(table_hbm, dest_hbm, out_hbm, idx_v, rows_v):
        wid = lax.axis_index("s") * nc + lax.axis_index("c")

        @pl.loop(0, per_w // SC_CHUNK)
        def _(ci):
            base = pl.multiple_of(wid * per_w + ci * SC_CHUNK, SC_CHUNK)
            pltpu.sync_copy(dest_hbm.at[wid * (per_w // SC_CHUNK) + ci], idx_v)
            for k in range(TOP_K):
                pltpu.sync_copy(table_hbm.at[idx_v.at[k]], rows_v)
                pltpu.sync_copy(rows_v, out_hbm.at[k, pl.ds(base, SC_CHUNK)])

    return run(table, dest)


def _split_bf16(w):
    hi = w.astype(BF16)
    return hi, (w - hi.astype(F32)).astype(BF16)


def kernel(x_prompt, x_sample, cache_k, cache_v, c, c_ctx, ada_w, ada_b, norm1_g, norm2_g, w_qkv, w_o, na_rpb, diff_lambda, diff_subln_g, router_w, router_b, exp_w_gate, exp_w_up, exp_w_down, shared_w_gate, shared_w_up, shared_w_down, final_g):
    batch, seq, d = x_prompt.shape
    n_lat_batch, lat_seq, _ = x_sample.shape
    depth = w_qkv.shape[0]
    n_exp = router_w.shape[-1]
    n_lat = n_lat_batch * lat_seq
    n_ctx = batch * seq
    n = n_lat + n_ctx
    assert d % LANES == 0 and na_rpb.shape[1] * HEAD_DIM == d
    assert diff_lambda.shape[-1] == HEAD_DIM and lat_seq % GRID_W == 0
    assert lat_seq // GRID_W >= NA_ROWS and n_lat % seq == 0
    tm = _pick_tile(math.gcd(lat_seq, n_ctx), 512)
    tmb = 1024
    scale = HEAD_DIM ** -0.5 * LOG2E

    xs = [x_sample.reshape(n_lat, d), x_prompt.reshape(n_ctx, d)]
    mod_rows = -(-(n_lat_batch + 1) // 8) * 8
    cond = jnp.zeros((mod_rows, d), F32).at[:n_lat_batch].set(c).at[n_lat_batch].set(c_ctx)
    mod = _modulation(cond, ada_w, ada_b).reshape(depth, mod_rows, N_MOD, d)

    w_qkv_b = w_qkv.astype(BF16)
    w_o_b = w_o.astype(BF16)
    rope_tab = _rope_tables(lat_seq, tm)
    caches = None
    for i in range(depth):
        is_diff = i % 2 == 1
        j = i // 2
        q, k, v, *caches = _qkv(xs, mod, norm1_g.reshape(depth, 1, d), w_qkv_b,
                                rope_tab if is_diff else None, caches, layer=i, n_lat=n_lat,
                                batch=batch, seq=seq, lat_seq=lat_seq, n_lat_batch=n_lat_batch,
                                tm=tm, scale=scale)
        if is_diff:
            vt = caches.pop()
        if not is_diff:
            o = _na_attention(q, k, v, cache_k, cache_v,
                              _na_bias_table(na_rpb[j], lat_seq // GRID_W), layer=i,
                              n_lat_batch=n_lat_batch, lat_seq=lat_seq)
            o = _ctx_dense(q, k, v, o, n_lat=n_lat, seq=seq)
        else:
            lam_init = 0.8 - 0.6 * math.exp(-0.3 * i)
            lp = diff_lambda[j].astype(F32)
            lam = (jnp.exp(jnp.sum(lp[0] * lp[1])) - jnp.exp(jnp.sum(lp[2] * lp[3]))
                   + lam_init).reshape(1)
            subg = diff_subln_g[j].reshape(1, LANES).astype(F32)
            o = _lat_diff(lam, q, k, vt, cache_k, cache_v, subg, layer=i,
                          n_lat_batch=n_lat_batch, lat_seq=lat_seq, out_scale=1.0 - lam_init)
            o = _ctx_diff(lam, q, k, v, subg, o, n_lat=n_lat, seq=seq, out_scale=1.0 - lam_init)

        rwh, rwl = _split_bf16(router_w[i].T)
        xmid, h2, sel, gates, rank, counts = _post_attn(
            o, xs, mod, w_o_b[i], norm2_g[i].reshape(1, d), rwh, rwl,
            router_b[i].reshape(n_exp, 1).astype(F32), layer=i, n_lat=n_lat, lat_seq=lat_seq,
            n_lat_batch=n_lat_batch, tm=tm)

        n_tiles = -(-(n * TOP_K) // tmb) + n_exp
        cap = n_tiles * tmb
        dest, tile_expert, tile_rows, n_used = _plan(counts, sel, rank, tmb=tmb, n_tiles=n_tiles)
        xb = _sc_dispatch(h2, dest, cap)
        yb = _experts(tile_expert, tile_rows, n_used, xb, exp_w_gate, exp_w_up, exp_w_down,
                      layer=i, tmb=tmb)
        yg = _sc_collect(yb, dest)
        swgu = jnp.concatenate([shared_w_gate[i], shared_w_up[i]], axis=-1).astype(BF16)
        xs = _combine(xmid, h2, yg, gates.T, mod, swgu, shared_w_down[i].astype(BF16),
                      final_g.reshape(1, d), layer=i, n_lat=n_lat, lat_seq=lat_seq,
                      n_lat_batch=n_lat_batch, tm=tm, final=i == depth - 1)

    y_sample = xs[0].reshape(n_lat_batch, lat_seq, d)
    y_prompt = xs[1].reshape(batch, seq, d)
    return (y_prompt, y_sample, caches[0], caches[1])
```

```python
import functools
import math

import jax
import jax.numpy as jnp
from jax import lax
from jax.experimental import pallas as pl
from jax.experimental.pallas import tpu as pltpu
from jax.experimental.pallas import tpu_sc as plsc

GRID_W = 64
NA_ROWS = 8
NA_COLS = 16
TOP_K = 8
ROUTED_SCALE = 2.5
ROPE_BASE = 10000.0
EPS = 1e-6
N_MOD = 6
HEAD_DIM = 64
LANES = 128
LOG2E = math.log2(math.e)
MASKED = -1e30
MOE_RANGES = 2
QUERY_GROUPS = 2
VMEM_LIMIT = 56 * 1024 * 1024

F32 = jnp.float32
BF16 = jnp.bfloat16


def _cparams(*sem):
    return pltpu.CompilerParams(dimension_semantics=sem, vmem_limit_bytes=VMEM_LIMIT)


def _dot(a, b):
    return jnp.dot(a, b, preferred_element_type=F32)


def _dot_nt(a, b):
    return lax.dot_general(a, b, (((1,), (1,)), ((), ())), preferred_element_type=F32)


def _sigmoid(x):
    return 1.0 / (1.0 + jnp.exp(-x))


def _rms(x, g):
    return x * lax.rsqrt(jnp.mean(x * x, axis=-1, keepdims=True) + EPS) * g


def _pack_rows(xb):
    half = xb.shape[1] // 2
    u = lax.bitcast_convert_type(xb.astype(F32), jnp.uint32)
    packed = (u[:, :half] >> 16) | (u[:, half:] & jnp.uint32(0xFFFF0000))
    return lax.bitcast_convert_type(packed, jnp.int32)


def _unpack_rows(p):
    u = lax.bitcast_convert_type(p, jnp.uint32)
    return (lax.bitcast_convert_type(u << 16, F32),
            lax.bitcast_convert_type(u & jnp.uint32(0xFFFF0000), F32))


def _pick_tile(n, pref):
    t = pref
    while n % t:
        t //= 2
    return t


def _mod_kernel(c_ref, w_ref, b_ref, o_ref):
    c = c_ref[...]
    o_ref[...] = _dot(c * _sigmoid(c), w_ref[...]) + b_ref[...]


def _modulation(cond, ada_w, ada_b):
    depth, d, n6 = ada_w.shape
    rows = cond.shape[0]
    tn = _pick_tile(n6, 512)
    return pl.pallas_call(
        _mod_kernel,
        grid=(depth, n6 // tn),
        in_specs=[
            pl.BlockSpec((rows, d), lambda l, j: (0, 0)),
            pl.BlockSpec((None, d, tn), lambda l, j: (l, 0, j)),
            pl.BlockSpec((None, 1, tn), lambda l, j: (l, 0, j)),
        ],
        out_specs=pl.BlockSpec((None, rows, tn), lambda l, j: (l, 0, j)),
        out_shape=jax.ShapeDtypeStruct((depth, rows, n6), F32),
        compiler_params=_cparams("arbitrary", "arbitrary"),
        name="modulation",
    )(cond, ada_w, ada_b.reshape(depth, 1, n6))


def _read_tokens(x_refs, g, ends):
    x = x_refs[-1][...]
    for ref, end in zip(reversed(x_refs[:-1]), reversed(ends[:-1])):
        x = jnp.where(g < end, ref[...], x)
    return x


def _token_parts(xs, tm, tile0, n_tiles):
    arrays, specs, ends = [], [], []
    start = 0
    for x in xs:
        size = x.shape[0] // tm
        if start < tile0 + n_tiles and start + size > tile0:
            arrays.append(x)
            ends.append(start + size)
            specs.append(pl.BlockSpec(
                (tm, x.shape[1]), lambda i, s=start, z=size: (jnp.clip(i + tile0 - s, 0, z - 1), 0)))
        start += size
    return arrays, specs, ends


def _qkv_kernel(*refs, d, scale, rope, n_lat_tiles, ends, n_alias):
    n_x = len(ends)
    x_refs = refs[:n_x]
    mod_ref, g_ref, w_ref = refs[n_x:n_x + 3]
    rest = refs[n_x + 3:]
    if rope:
        rope_ref, wvt_ref, rest = rest[0], rest[1], rest[2:]
    rest = rest[n_alias:]
    q_ref, k_ref, v_ref, kf_ref, vf_ref = rest[:5]
    i = pl.program_id(0)
    x = _read_tokens(x_refs, i, ends)
    h = _rms(x, g_ref[...]) * (1.0 + mod_ref[1:2, :]) + mod_ref[0:1, :]
    hb = h.astype(BF16)

    def rot(t):
        if not rope:
            return t
        cols = []
        for j in range(d // LANES):
            tb = t[:, j * LANES:(j + 1) * LANES]
            cols.append(tb * rope_ref[0]
                        + pltpu.roll(tb, LANES - HEAD_DIM // 2, axis=1) * rope_ref[1]
                        + pltpu.roll(tb, HEAD_DIM // 2, axis=1) * rope_ref[2])
        return jnp.concatenate(cols, axis=1)

    q = _dot(hb, w_ref[:, 0:d])
    q_ref[...] = (rot(q) * scale).astype(BF16)
    k = _dot(hb, w_ref[:, d:2 * d])
    k_ref[...] = rot(k).astype(BF16)
    if rope:
        @pl.when(i < n_lat_tiles)
        def _():
            rest[5][...] = _dot_nt(wvt_ref[...], hb).astype(BF16)

    def context_values():
        v = _dot(hb, w_ref[:, 2 * d:3 * d])
        v_ref[...] = v.astype(BF16)
        return v

    if not rope:
        v = context_values()

    @pl.when(i >= n_lat_tiles)
    def _():
        kf_ref[...] = k.reshape(kf_ref.shape)
        vf_ref[...] = (context_values() if rope else v).reshape(vf_ref.shape)


def _qkv(xs, mod, g, w, rope_tab, caches, *, layer, n_lat, batch, seq, lat_seq, n_lat_batch, tm,
         scale):
    d = xs[0].shape[1]
    depth = w.shape[0]
    n = n_lat + batch * seq
    n_lat_tiles = n_lat // tm
    rope = rope_tab is not None
    assert tm % seq == 0
    x_arrays, x_specs, ends = _token_parts(xs, tm, 0, n // tm)

    def mod_idx(i):
        return (layer, jnp.minimum(i * tm // lat_seq, n_lat_batch), 0, 0)

    in_specs = x_specs + [
        pl.BlockSpec((None, None, N_MOD, d), mod_idx),
        pl.BlockSpec((None, 1, d), lambda i: (layer, 0, 0)),
        pl.BlockSpec((None, d, 3 * d), lambda i: (layer, 0, 0)),
    ]
    args = x_arrays + [mod, g, w]
    if rope:
        seq_tiles = lat_seq // tm
        in_specs.append(pl.BlockSpec(
            (3, tm, LANES),
            lambda i: (0, jnp.where(i < n_lat_tiles, i % seq_tiles, seq_tiles), 0)))
        in_specs.append(pl.BlockSpec((d, d), lambda i: (0, 0)))
        args += [rope_tab, w[layer, :, 2 * d:].T]
    aliases = {}
    if caches is not None:
        aliases = {len(args): 3, len(args) + 1: 4}
        in_specs += [pl.BlockSpec(memory_space=pl.ANY)] * 2
        args += list(caches)
    tok = pl.BlockSpec((tm, d), lambda i: (i, 0))
    cache_spec = pl.BlockSpec((tm // seq, None, seq, d),
                              lambda i: (jnp.maximum(i - n_lat_tiles, 0), layer, 0, 0))
    out_specs = [tok, tok, tok, cache_spec, cache_spec]
    out_shape = ([jax.ShapeDtypeStruct((n, d), BF16)] * 3
                 + [jax.ShapeDtypeStruct((batch, depth, seq, d), F32)] * 2)
    if rope:
        out_specs.append(pl.BlockSpec((None, d, tm),
                                      lambda i: (jnp.minimum(i, n_lat_tiles - 1), 0, 0)))
        out_shape.append(jax.ShapeDtypeStruct((n_lat_tiles, d, tm), BF16))
    return pl.pallas_call(
        functools.partial(_qkv_kernel, d=d, scale=scale, rope=rope, n_lat_tiles=n_lat_tiles,
                          ends=tuple(ends), n_alias=len(aliases)),
        grid=(n // tm,),
        in_specs=in_specs,
        out_specs=out_specs,
        out_shape=out_shape,
        input_output_aliases=aliases,
        compiler_params=_cparams("arbitrary"),
        name=f"qkv{layer}",
    )(*args)


def _rope_tables(lat_seq, tm):
    n_freq = HEAD_DIM // 4
    inv = ROPE_BASE ** (-jnp.arange(n_freq, dtype=F32) / n_freq)
    pos = jnp.arange(lat_seq)
    row = (pos // GRID_W).astype(F32)
    col = (pos % GRID_W).astype(F32)
    ang = jnp.concatenate([row[:, None] * inv, col[:, None] * inv], axis=-1)
    reps = LANES // (HEAD_DIM // 2)
    cos = jnp.tile(jnp.cos(ang), (1, reps))
    sin = jnp.tile(jnp.sin(ang), (1, reps))
    first_half = (jnp.arange(LANES) % HEAD_DIM) < HEAD_DIM // 2
    s_next = jnp.where(first_half, -sin, 0.0)
    s_prev = jnp.where(first_half, 0.0, sin)
    ident = jnp.stack([jnp.ones((tm, LANES), F32), jnp.zeros((tm, LANES), F32),
                       jnp.zeros((tm, LANES), F32)])
    return jnp.concatenate([jnp.stack([cos, s_next, s_prev]), ident], axis=1)


def _lane_lo():
    return lax.broadcasted_iota(jnp.int32, (1, LANES), 1) < HEAD_DIM


def _ctx_dense_kernel(q_ref, k_ref, v_ref, o_all_ref, o_ref, *, d):
    del o_all_ref
    lo = _lane_lo()
    cols = []
    for j in range(d // LANES):
        blk = slice(j * LANES, (j + 1) * LANES)
        q2, k2, v2 = q_ref[:, blk], k_ref[:, blk], v_ref[:, blk]
        outs = []
        for qm in (jnp.where(lo, q2, 0), jnp.where(lo, 0, q2)):
            s = _dot_nt(qm, k2)
            p = jnp.exp2(s - jnp.max(s, axis=-1, keepdims=True))
            outs.append(_dot(p.astype(BF16), v2) / jnp.sum(p, axis=-1, keepdims=True))
        cols.append(jnp.where(lo, outs[0], outs[1]).astype(BF16))
    o_ref[...] = jnp.concatenate(cols, axis=1)


def _ctx_dense(q, k, v, o_all, *, n_lat, seq):
    n, d = q.shape
    off = n_lat // seq
    spec = pl.BlockSpec((seq, d), lambda b: (off + b, 0))
    return pl.pallas_call(
        functools.partial(_ctx_dense_kernel, d=d),
        grid=((n - n_lat) // seq,),
        in_specs=[spec, spec, spec, pl.BlockSpec(memory_space=pl.ANY)],
        out_specs=spec,
        out_shape=jax.ShapeDtypeStruct((n, d), BF16),
        input_output_aliases={3: 0},
        compiler_params=_cparams("arbitrary"),
        name="ctx_dense_attn",
    )(q, k, v, o_all)


def _subln(o, g_ref, out_scale):
    return _rms(o, g_ref[...]) * out_scale


def _ctx_diff_kernel(lam_ref, q_ref, k_ref, v_ref, g_ref, o_all_ref, o_ref, *, d, out_scale):
    del o_all_ref
    lo = _lane_lo()
    lam = lam_ref[0]
    cols = []
    for j in range(d // LANES):
        blk = slice(j * LANES, (j + 1) * LANES)
        q2, k2, v2 = q_ref[:, blk], k_ref[:, blk], v_ref[:, blk]
        ps = []
        for qm in (jnp.where(lo, q2, 0), jnp.where(lo, 0, q2)):
            s = _dot_nt(qm, k2)
            p = jnp.exp2(s - jnp.max(s, axis=-1, keepdims=True))
            ps.append(p / jnp.sum(p, axis=-1, keepdims=True))
        o = _dot((ps[0] - lam * ps[1]).astype(BF16), v2)
        cols.append(_subln(o, g_ref, out_scale).astype(BF16))
    o_ref[...] = jnp.concatenate(cols, axis=1)


def _ctx_diff(lam, q, k, v, subg, o_all, *, n_lat, seq, out_scale):
    n, d = q.shape
    off = n_lat // seq
    spec = pl.BlockSpec((seq, d), lambda b: (off + b, 0))
    return pl.pallas_call(
        functools.partial(_ctx_diff_kernel, d=d, out_scale=out_scale),
        grid=((n - n_lat) // seq,),
        in_specs=[pl.BlockSpec(memory_space=pltpu.SMEM), spec, spec, spec,
                  pl.BlockSpec((1, LANES), lambda b: (0, 0)), pl.BlockSpec(memory_space=pl.ANY)],
        out_specs=spec,
        out_shape=jax.ShapeDtypeStruct((n, d), BF16),
        input_output_aliases={5: 0},
        compiler_params=_cparams("arbitrary"),
        name="ctx_diff_attn",
    )(lam, q, k, v, subg, o_all)


def _na_kernel(q_ref, k_ref, v_ref, ck_ref, cv_ref, bias_ref, o_ref, *, d, rows):
    lo = _lane_lo()
    r = pl.program_id(1)
    win = NA_ROWS * GRID_W
    start = pl.multiple_of(jnp.clip(r - NA_ROWS // 2, 0, rows - NA_ROWS) * GRID_W, GRID_W)
    n_heads = 2 * (d // LANES)

    def blk(h):
        return slice(h // 2 * LANES, (h // 2 + 1) * LANES)

    scores = []
    for h in range(n_heads):
        q2 = q_ref[:, blk(h)]
        qm = jnp.where(lo, q2, 0) if h % 2 == 0 else jnp.where(lo, 0, q2)
        scores.append((_dot_nt(qm, k_ref[pl.ds(start, win), blk(h)]) + bias_ref[h],
                       _dot_nt(qm, ck_ref[:, blk(h)].astype(BF16))))
    probs = []
    for s_loc, s_ctx in scores:
        m = jnp.maximum(jnp.max(s_loc, axis=-1, keepdims=True),
                        jnp.max(s_ctx, axis=-1, keepdims=True))
        p_loc = jnp.exp2(s_loc - m)
        p_ctx = jnp.exp2(s_ctx - m)
        l = jnp.sum(p_loc, axis=-1, keepdims=True) + jnp.sum(p_ctx, axis=-1, keepdims=True)
        probs.append((p_loc.astype(BF16), p_ctx.astype(BF16), l))
    outs = []
    for h, (p_loc, p_ctx, l) in enumerate(probs):
        outs.append((_dot(p_loc, v_ref[pl.ds(start, win), blk(h)])
                     + _dot(p_ctx, cv_ref[:, blk(h)].astype(BF16))) / l)
    o_ref[...] = jnp.concatenate(
        [jnp.where(lo, outs[h], outs[h + 1]).astype(BF16) for h in range(0, n_heads, 2)], axis=1)


def _na_bias_table(rpb, rows):
    h = rpb.shape[0]
    delta = jnp.arange(NA_ROWS)
    kr = jnp.arange(NA_ROWS)
    c = jnp.arange(GRID_W)
    kc = jnp.arange(GRID_W)
    off_r = kr[None, :] - delta[:, None] + (NA_ROWS - 1)
    c0 = jnp.clip(c - NA_COLS // 2, 0, GRID_W - NA_COLS)
    off_c = kc[None, :] - c[:, None] + (NA_COLS - 1)
    inside = (kc[None, :] >= c0[:, None]) & (kc[None, :] < c0[:, None] + NA_COLS)
    oh_r = jax.nn.one_hot(off_r, 2 * NA_ROWS - 1, dtype=F32)
    oh_c = jax.nn.one_hot(off_c, 2 * NA_COLS - 1, dtype=F32)
    t = jnp.einsum('hrs,dkr,cqs->dhckq', rpb.astype(F32), oh_r, oh_c,
                   precision=lax.Precision.HIGHEST)
    t = jnp.where(inside[None, None, :, None, :], t * LOG2E, MASKED)
    return t.reshape(NA_ROWS, h, GRID_W, NA_ROWS * GRID_W)


def _na_attention(q, k, v, ck, cv, bias, *, layer, n_lat_batch, lat_seq):
    n, d = q.shape
    rows = lat_seq // GRID_W
    past = ck.shape[2]
    n_heads = bias.shape[1]

    def delta_idx(b, r):
        return (r - jnp.clip(r - NA_ROWS // 2, 0, rows - NA_ROWS), 0, 0, 0)

    kv_spec = pl.BlockSpec((lat_seq, d), lambda b, r: (b, 0))
    c_spec = pl.BlockSpec((None, None, past, d), lambda b, r: (b, layer, 0, 0))
    return pl.pallas_call(
        functools.partial(_na_kernel, d=d, rows=rows),
        grid=(n_lat_batch, rows),
        in_specs=[pl.BlockSpec((GRID_W, d), lambda b, r: (b * rows + r, 0)), kv_spec, kv_spec,
                  c_spec, c_spec,
                  pl.BlockSpec((None, n_heads, GRID_W, NA_ROWS * GRID_W), delta_idx)],
        out_specs=pl.BlockSpec((GRID_W, d), lambda b, r: (b * rows + r, 0)),
        out_shape=jax.ShapeDtypeStruct((n, d), BF16),
        compiler_params=_cparams("arbitrary", "arbitrary"),
        name="na_attn",
    )(q, k, v, ck, cv, bias)


def _lat_diff_kernel(lam_ref, q_ref, k_ref, vt_ref, ck_ref, cv_ref, g_ref, o_ref, *, out_scale):
    lo = _lane_lo()
    lam = lam_ref[0]
    q2 = q_ref[...]
    tq = q2.shape[0]
    tg = tq // QUERY_GROUPS
    qs = []
    for g in range(QUERY_GROUPS):
        qg = q2[g * tg:(g + 1) * tg, :]
        qs += [jnp.where(lo, qg, 0), jnp.where(lo, 0, qg)]

    def softmax_step(state, s):
        m, l, acc = state
        m_new = jnp.maximum(m, jnp.max(s, axis=0, keepdims=True))
        alpha = jnp.exp2(m - m_new)
        p = jnp.exp2(s - m_new)
        return m_new, alpha * l + jnp.sum(p, axis=0, keepdims=True), alpha * acc, p.astype(BF16)

    def scores(kt):
        return [_dot_nt(kt, qm) for qm in qs]

    def absorb(states, ss, vt):
        out = []
        for g in range(0, len(qs), 2):
            stepped = [softmax_step(states[g + j], ss[g + j]) for j in range(2)]
            out += [(m, l, acc + _dot(vt, p)) for m, l, acc, p in stepped]
        return out

    tile = vt_ref.shape[2]
    n_tiles = vt_ref.shape[0]

    def keys(t):
        return k_ref[t * tile:(t + 1) * tile, :] if t < n_tiles else ck_ref[...].astype(BF16)

    def values_t(t):
        return vt_ref[t] if t < n_tiles else cv_ref[...].T.astype(BF16)

    init = (jnp.full((1, tg), -jnp.inf, F32), jnp.zeros((1, tg), F32), jnp.zeros((LANES, tg), F32))
    states = [init] * len(qs)
    ss = scores(keys(0))
    for t in range(n_tiles + 1):
        ss_next = scores(keys(t + 1)) if t < n_tiles else None
        states = absorb(states, ss, values_t(t))
        ss = ss_next
    o = jnp.concatenate([states[g][2] / states[g][1] - lam * (states[g + 1][2] / states[g + 1][1])
                         for g in range(0, len(qs), 2)], axis=1).T
    o_ref[...] = _subln(o, g_ref, out_scale).astype(BF16)


def _lat_diff(lam, q, k, vt, ck, cv, subg, *, layer, n_lat_batch, lat_seq, out_scale):
    n, d = q.shape
    past = ck.shape[2]
    tk = vt.shape[2]
    tq = _pick_tile(lat_seq, 512)
    qt = lat_seq // tq
    kt = lat_seq // tk
    c_spec = pl.BlockSpec((None, None, past, LANES), lambda b, h, i: (b, layer, 0, h))
    q_spec = pl.BlockSpec((tq, LANES), lambda b, h, i: (b * qt + i, h))
    return pl.pallas_call(
        functools.partial(_lat_diff_kernel, out_scale=out_scale),
        grid=(n_lat_batch, d // LANES, qt),
        in_specs=[pl.BlockSpec(memory_space=pltpu.SMEM), q_spec,
                  pl.BlockSpec((lat_seq, LANES), lambda b, h, i: (b, h)),
                  pl.BlockSpec((kt, LANES, tk), lambda b, h, i: (b, h, 0)),
                  c_spec, c_spec, pl.BlockSpec((1, LANES), lambda b, h, i: (0, 0))],
        out_specs=q_spec,
        out_shape=jax.ShapeDtypeStruct((n, d), BF16),
        compiler_params=_cparams("arbitrary", "arbitrary", "arbitrary"),
        name="lat_diff_attn",
    )(lam, q, k, vt, ck, cv, subg)


def _post_attn_kernel(o_ref, *refs, ends, tile0):
    n_x = len(ends)
    x_refs = refs[:n_x]
    (mod_ref, wo_ref, g_ref, rwh_ref, rwl_ref, rb_ref, tri_ref,
     xmid_ref, h_ref, sel_ref, gate_ref, rank_ref, cnt_ref, carry_ref) = refs[n_x:]
    i = pl.program_id(0)

    @pl.when(i == 0)
    def _():
        carry_ref[...] = jnp.zeros_like(carry_ref)

    x1 = _read_tokens(x_refs, i + tile0, ends) + mod_ref[2:3, :] * _dot(o_ref[...], wo_ref[...])
    xmid_ref[...] = x1
    h = _rms(x1, g_ref[...]) * (1.0 + mod_ref[4:5, :]) + mod_ref[3:4, :]
    hb = h.astype(BF16)
    h_ref[...] = _pack_rows(hb)
    h_lo = (h - hb.astype(F32)).astype(BF16)
    logits = _dot_nt(rwh_ref[...], hb) + (_dot_nt(rwl_ref[...], hb) + _dot_nt(rwh_ref[...], h_lo))
    scores = _sigmoid(logits)
    n_exp, tm = scores.shape
    biased = scores + rb_ref[...]
    row = lax.broadcasted_iota(jnp.int32, (n_exp, tm), 0).astype(F32)
    total = jnp.zeros((n_exp, tm), F32)
    sels, gates = [], []
    for _ in range(TOP_K):
        m = jnp.max(biased, axis=0, keepdims=True)
        idx = jnp.min(jnp.where(biased == m, row, float(n_exp)), axis=0, keepdims=True)
        hit = row == idx
        gates.append(jnp.sum(jnp.where(hit, scores, 0.0), axis=0, keepdims=True))
        biased = jnp.where(hit, -jnp.inf, biased)
        total = total + jnp.where(hit, 1.0, 0.0)
        sels.append(idx)
    denom = gates[0]
    for g in gates[1:]:
        denom = denom + g
    before = _dot(total.astype(BF16), tri_ref[...]) + carry_ref[:, 0:1]
    for k in range(TOP_K):
        sel_ref[k:k + 1, :] = sels[k].astype(jnp.int32)
        gate_ref[k:k + 1, :] = gates[k] / denom * ROUTED_SCALE
        rank_ref[k:k + 1, :] = jnp.sum(jnp.where(row == sels[k], before, 0.0), axis=0,
                                       keepdims=True).astype(jnp.int32)
    carry_ref[...] = carry_ref[...] + jnp.sum(total, axis=1, keepdims=True)
    cnt_ref[...] = carry_ref[...].astype(jnp.int32)


def _post_attn(o, xs, mod, wo, g, rwh, rwl, rb, *, layer, tile0, n_tiles, lat_seq, n_lat_batch, tm):
    d = o.shape[1]
    n = n_tiles * tm
    n_exp = rwh.shape[0]
    tri = (jnp.arange(tm)[:, None] < jnp.arange(tm)[None, :]).astype(BF16)
    x_arrays, x_specs, ends = _token_parts(xs, tm, tile0, n_tiles)

    def mod_idx(i):
        return (layer, jnp.minimum((i + tile0) * tm // lat_seq, n_lat_batch), 0, 0)

    tok = pl.BlockSpec((tm, d), lambda i: (i, 0))
    tok_p = pl.BlockSpec((tm, d // 2), lambda i: (i, 0))
    kt = pl.BlockSpec((TOP_K, tm), lambda i: (0, i))
    const2 = lambda i: (0, 0)
    return pl.pallas_call(
        functools.partial(_post_attn_kernel, ends=tuple(ends), tile0=tile0),
        grid=(n_tiles,),
        in_specs=[pl.BlockSpec((tm, d), lambda i: (i + tile0, 0))] + x_specs + [
                  pl.BlockSpec((None, None, N_MOD, d), mod_idx),
                  pl.BlockSpec((d, d), const2), pl.BlockSpec((1, d), const2),
                  pl.BlockSpec((n_exp, d), const2), pl.BlockSpec((n_exp, d), const2),
                  pl.BlockSpec((n_exp, 1), const2), pl.BlockSpec((tm, tm), const2)],
        out_specs=[tok, tok_p, kt, kt, kt, pl.BlockSpec((n_exp, LANES), const2)],
        out_shape=[jax.ShapeDtypeStruct((n, d), F32), jax.ShapeDtypeStruct((n, d // 2), jnp.int32),
                   jax.ShapeDtypeStruct((TOP_K, n), jnp.int32),
                   jax.ShapeDtypeStruct((TOP_K, n), F32),
                   jax.ShapeDtypeStruct((TOP_K, n), jnp.int32),
                   jax.ShapeDtypeStruct((n_exp, LANES), jnp.int32)],
        scratch_shapes=[pltpu.VMEM((n_exp, LANES), F32)],
        compiler_params=_cparams("arbitrary"),
        name=f"post_attn{layer}",
    )(o, *x_arrays, mod, wo, g, rwh, rwl, rb, tri)


def _plan_kernel(cnt_ref, sel_ref, rank_ref, dest_ref, te_ref, tv_ref, nu_ref, *, tmb):
    n_exp = cnt_ref.shape[0]
    sel = sel_ref[...]
    tile_start = lax.broadcasted_iota(jnp.int32, te_ref.shape, 1) * tmb
    dest = rank_ref[...]
    te = jnp.zeros(te_ref.shape, jnp.int32)
    tv = jnp.zeros(te_ref.shape, jnp.int32)
    start = jnp.zeros((1, 1), jnp.int32)
    for e in range(n_exp):
        dest = dest + jnp.where(sel == e, start, 0)
        cnt = cnt_ref[e:e + 1, 0:1]
        stop = start + ((cnt + (tmb - 1)) & -tmb)
        mine = (tile_start >= start) & (tile_start < stop)
        tv = tv + jnp.where(mine, jnp.clip(start + cnt - tile_start, 0, tmb), 0)
        start = stop
        te = te + jnp.where(start <= tile_start, 1, 0)
    te_ref[...] = jnp.minimum(te, n_exp - 1)
    tv_ref[...] = tv
    nu_ref[...] = jnp.broadcast_to(start, nu_ref.shape)
    for c in range(dest_ref.shape[0]):
        dest_ref[c] = dest[:, c * SC_CHUNK:(c + 1) * SC_CHUNK]


def _plan(counts, sel, rank, *, tmb, n_tiles):
    k_top, n = sel.shape
    assert tmb & (tmb - 1) == 0
    tn = _pick_tile(n, 16 * SC_CHUNK)
    te_w = -(-n_tiles // LANES) * LANES
    tok = pl.BlockSpec((k_top, tn), lambda i: (0, i))
    const2 = lambda i: (0, 0)
    dest, te, tv, nu = pl.pallas_call(
        functools.partial(_plan_kernel, tmb=tmb),
        grid=(n // tn,),
        in_specs=[pl.BlockSpec(counts.shape, const2), tok, tok],
        out_specs=[pl.BlockSpec((tn // SC_CHUNK, k_top, SC_CHUNK), lambda i: (i, 0, 0)),
                   pl.BlockSpec((1, te_w), const2), pl.BlockSpec((1, te_w), const2),
                   pl.BlockSpec((1, LANES), const2)],
        out_shape=[jax.ShapeDtypeStruct((n // SC_CHUNK, k_top, SC_CHUNK), jnp.int32),
                   jax.ShapeDtypeStruct((1, te_w), jnp.int32),
                   jax.ShapeDtypeStruct((1, te_w), jnp.int32),
                   jax.ShapeDtypeStruct((1, LANES), jnp.int32)],
        compiler_params=_cparams("arbitrary"),
        name="plan",
    )(counts, sel, rank)
    return dest, te[0, :n_tiles], tv[0, :n_tiles], nu[0, :1] // tmb


def _expert_kernel(te_ref, tv_ref, nu_ref, x_ref, wg_ref, wu_ref, wd_ref, y_ref, wg_b, wu_b, wd_b):
    i = pl.program_id(0)
    half = x_ref.shape[0] // 2

    def rows(r):
        return jnp.concatenate(_unpack_rows(x_ref[r * half:(r + 1) * half, :]), axis=1).astype(BF16)

    def swiglu(xs):
        gu = [(_dot(x, wg_b[...]), _dot(x, wu_b[...])) for x in xs]
        acts = [(g * _sigmoid(g) * u).astype(BF16) for g, u in gu]
        return [_pack_rows(_dot(a, wd_b[...]).astype(BF16)) for a in acts]

    @pl.when(i < nu_ref[0])
    def _():
        @pl.when((i == 0) | (te_ref[i] != te_ref[jnp.maximum(i - 1, 0)]))
        def _():
            wg_b[...] = wg_ref[...].astype(BF16)
            wu_b[...] = wu_ref[...].astype(BF16)
            wd_b[...] = wd_ref[...].astype(BF16)

        @pl.when(tv_ref[i] > half)
        def _():
            ya, yb = swiglu([rows(0), rows(1)])
            y_ref[0:half, :] = ya
            y_ref[half:2 * half, :] = yb

        @pl.when(tv_ref[i] <= half)
        def _():
            y_ref[0:half, :] = swiglu([rows(0)])[0]


def _experts(tile_expert, tile_rows, n_used, xb, wg, wu, wd, *, layer, tmb):
    cap, dp = xb.shape
    f, d = wd.shape[2:]

    def row_idx(i, te, tv, nu):
        return (jnp.minimum(i, nu[0] - 1), 0)

    def w_idx(i, te, tv, nu):
        return (layer, te[i], 0, 0)

    return pl.pallas_call(
        _expert_kernel,
        grid_spec=pltpu.PrefetchScalarGridSpec(
            num_scalar_prefetch=3,
            grid=(cap // tmb,),
            in_specs=[pl.BlockSpec((tmb, dp), row_idx),
                      pl.BlockSpec((None, None, d, f), w_idx),
                      pl.BlockSpec((None, None, d, f), w_idx),
                      pl.BlockSpec((None, None, f, d), w_idx)],
            out_specs=pl.BlockSpec((tmb, dp), row_idx),
            scratch_shapes=[pltpu.VMEM((d, f), BF16), pltpu.VMEM((d, f), BF16),
                            pltpu.VMEM((f, d), BF16)]),
        out_shape=jax.ShapeDtypeStruct((cap, dp), jnp.int32),
        compiler_params=_cparams("arbitrary"),
        name="experts",
    )(tile_expert, tile_rows, n_used, xb, wg, wu, wd)


def _combine_kernel(x_ref, h_ref, yg_ref, gt_ref, mod_ref, wgu_ref, wd_ref, fg_ref, *refs, f, final,
                    n_alias, tile0, n_lat_tiles):
    o_refs = refs[n_alias:]
    h = jnp.concatenate(_unpack_rows(h_ref[...]), axis=1).astype(BF16)
    gu = _dot(h, wgu_ref[...])
    gate, up = gu[:, :f], gu[:, f:]
    shared = _dot((gate * _sigmoid(gate) * up).astype(BF16), wd_ref[...])
    lo = hi = None
    for k in range(TOP_K):
        y_lo, y_hi = _unpack_rows(yg_ref[k])
        g = gt_ref[:, k:k + 1]
        lo = g * y_lo if lo is None else lo + g * y_lo
        hi = g * y_hi if hi is None else hi + g * y_hi
    acc = shared + jnp.concatenate([lo, hi], axis=1)
    x2 = x_ref[...] + mod_ref[5:6, :] * acc
    if not final:
        o_refs[0][...] = x2
    elif len(o_refs) == 1:
        o_refs[0][...] = _rms(x2, fg_ref[...])
    else:
        y = _rms(x2, fg_ref[...])
        tile = pl.program_id(0) + tile0

        @pl.when(tile < n_lat_tiles)
        def _():
            o_refs[0][...] = y

        @pl.when(tile >= n_lat_tiles)
        def _():
            o_refs[1][...] = y


def _combine(x, h, yg, gates_t, mod, wgu, wd, final_g, y_lat, *, layer, tile0, n_lat, n_ctx, lat_seq,
             n_lat_batch, tm, final):
    n, d = x.shape
    f = wd.shape[0]
    n_tiles = n // tm
    n_lat_tiles = n_lat // tm

    def mod_idx(i):
        return (layer, jnp.minimum((i + tile0) * tm // lat_seq, n_lat_batch), 0, 0)

    tok = pl.BlockSpec((tm, d), lambda i: (i, 0))
    tok_p = pl.BlockSpec((tm, d // 2), lambda i: (i, 0))
    const2 = lambda i: (0, 0)
    in_specs = [tok, tok_p, pl.BlockSpec((TOP_K, tm, d // 2), lambda i: (0, i, 0)),
                pl.BlockSpec((tm, TOP_K), lambda i: (i, 0)),
                pl.BlockSpec((None, None, N_MOD, d), mod_idx),
                pl.BlockSpec((d, 2 * f), const2), pl.BlockSpec((f, d), const2),
                pl.BlockSpec((1, d), const2)]
    args = [x, h, yg, gates_t, mod, wgu, wd, final_g]
    aliases = {}
    has_lat = has_ctx = False
    if not final:
        out_specs, out_shape = [tok], [jax.ShapeDtypeStruct((n, d), F32)]
    else:
        has_lat = tile0 < n_lat_tiles
        has_ctx = tile0 + n_tiles > n_lat_tiles
        out_specs, out_shape = [], []
        if has_lat:
            out_specs.append(pl.BlockSpec(
                (tm, d), lambda i: (jnp.minimum(i + tile0, n_lat_tiles - 1), 0)))
            out_shape.append(jax.ShapeDtypeStruct((n_lat, d), F32))
            if y_lat is not None:
                aliases = {len(args): 0}
                in_specs.append(pl.BlockSpec(memory_space=pl.ANY))
                args.append(y_lat)
        if has_ctx:
            out_specs.append(pl.BlockSpec(
                (tm, d), lambda i: (jnp.maximum(i + tile0 - n_lat_tiles, 0), 0)))
            out_shape.append(jax.ShapeDtypeStruct((n_ctx, d), F32))
    outs = pl.pallas_call(
        functools.partial(_combine_kernel, f=f, final=final, n_alias=len(aliases), tile0=tile0,
                          n_lat_tiles=n_lat_tiles),
        grid=(n_tiles,),
        in_specs=in_specs,
        out_specs=out_specs,
        out_shape=out_shape,
        input_output_aliases=aliases,
        compiler_params=_cparams("arbitrary"),
        name=f"combine{layer}",
    )(*args)
    if not final:
        return outs[0]
    return (outs[0] if has_lat else y_lat), (outs[-1] if has_ctx else None)


SC_CHUNK = 128


def _sc_workers():
    info = plsc.get_sparse_core_info()
    return info.num_cores, info.num_subcores


def _sc_dispatch(rows, dest, cap):
    n, w = rows.shape
    nc, ns = _sc_workers()
    per_w = n // (nc * ns)
    assert per_w * nc * ns == n and per_w % SC_CHUNK == 0
    mesh = plsc.VectorSubcoreMesh(core_axis_name="c", subcore_axis_name="s")

    @functools.partial(
        pl.kernel, mesh=mesh, out_type=jax.ShapeDtypeStruct((cap, w), rows.dtype),
        scratch_types=[pltpu.VMEM((TOP_K, SC_CHUNK), jnp.int32),
                       pltpu.VMEM((SC_CHUNK, w), rows.dtype)],
        name="sc_dispatch")
    def run(rows_hbm, dest_hbm, out_hbm, idx_v, rows_v):
        wid = lax.axis_index("s") * nc + lax.axis_index("c")

        @pl.loop(0, per_w // SC_CHUNK)
        def _(ci):
            base = pl.multiple_of(wid * per_w + ci * SC_CHUNK, SC_CHUNK)
            pltpu.sync_copy(dest_hbm.at[wid * (per_w // SC_CHUNK) + ci], idx_v)
            pltpu.sync_copy(rows_hbm.at[pl.ds(base, SC_CHUNK)], rows_v)
            for k in range(TOP_K):
                pltpu.sync_copy(rows_v, out_hbm.at[idx_v.at[k]])

    return run(rows, dest)


def _sc_collect(table, dest):
    n_chunks, k_top, _ = dest.shape
    n = n_chunks * SC_CHUNK
    w = table.shape[1]
    nc, ns = _sc_workers()
    per_w = n // (nc * ns)
    assert per_w * nc * ns == n and per_w % SC_CHUNK == 0
    mesh = plsc.VectorSubcoreMesh(core_axis_name="c", subcore_axis_name="s")

    @functools.partial(
        pl.kernel, mesh=mesh, out_type=jax.ShapeDtypeStruct((k_top, n, w), table.dtype),
        scratch_types=[pltpu.VMEM((TOP_K, SC_CHUNK), jnp.int32),
                       pltpu.VMEM((SC_CHUNK, w), table.dtype)],
        name="sc_collect")
    def run(table_hbm, dest_hbm, out_hbm, idx_v, rows_v):
        wid = lax.axis_index("s") * nc + lax.axis_index("c")

        @pl.loop(0, per_w // SC_CHUNK)
        def _(ci):
            base = pl.multiple_of(wid * per_w + ci * SC_CHUNK, SC_CHUNK)
            pltpu.sync_copy(dest_hbm.at[wid * (per_w // SC_CHUNK) + ci], idx_v)
            for k in range(TOP_K):
                pltpu.sync_copy(table_hbm.at[idx_v.at[k]], rows_v)
                pltpu.sync_copy(rows_v, out_hbm.at[k, pl.ds(base, SC_CHUNK)])

    return run(table, dest)


def _split_bf16(w):
    hi = w.astype(BF16)
    return hi, (w - hi.astype(F32)).astype(BF16)


def kernel(x_prompt, x_sample, cache_k, cache_v, c, c_ctx, ada_w, ada_b, norm1_g, norm2_g, w_qkv, w_o, na_rpb, diff_lambda, diff_subln_g, router_w, router_b, exp_w_gate, exp_w_up, exp_w_down, shared_w_gate, shared_w_up, shared_w_down, final_g):
    batch, seq, d = x_prompt.shape
    n_lat_batch, lat_seq, _ = x_sample.shape
    depth = w_qkv.shape[0]
    n_exp = router_w.shape[-1]
    n_lat = n_lat_batch * lat_seq
    n_ctx = batch * seq
    n = n_lat + n_ctx
    assert d % LANES == 0 and na_rpb.shape[1] * HEAD_DIM == d
    assert diff_lambda.shape[-1] == HEAD_DIM and lat_seq % GRID_W == 0
    assert lat_seq // GRID_W >= NA_ROWS and n_lat % seq == 0
    tm = _pick_tile(math.gcd(lat_seq, n_ctx), 512)
    tmb = 1024
    n_all = n // tm
    ranges = [(r * n_all // MOE_RANGES, (r + 1) * n_all // MOE_RANGES - r * n_all // MOE_RANGES)
              for r in range(MOE_RANGES)]
    scale = HEAD_DIM ** -0.5 * LOG2E

    xs = [x_sample.reshape(n_lat, d), x_prompt.reshape(n_ctx, d)]
    mod_rows = -(-(n_lat_batch + 1) // 8) * 8
    cond = jnp.zeros((mod_rows, d), F32).at[:n_lat_batch].set(c).at[n_lat_batch].set(c_ctx)
    mod = _modulation(cond, ada_w, ada_b).reshape(depth, mod_rows, N_MOD, d)

    w_qkv_b = w_qkv.astype(BF16)
    w_o_b = w_o.astype(BF16)
    rope_tab = _rope_tables(lat_seq, tm)
    caches = None
    for i in range(depth):
        is_diff = i % 2 == 1
        j = i // 2
        q, k, v, *caches = _qkv(xs, mod, norm1_g.reshape(depth, 1, d), w_qkv_b,
                                rope_tab if is_diff else None, caches, layer=i, n_lat=n_lat,
                                batch=batch, seq=seq, lat_seq=lat_seq, n_lat_batch=n_lat_batch,
                                tm=tm, scale=scale)
        if is_diff:
            vt = caches.pop()
        if not is_diff:
            o = _na_attention(q, k, v, cache_k, cache_v,
                              _na_bias_table(na_rpb[j], lat_seq // GRID_W), layer=i,
                              n_lat_batch=n_lat_batch, lat_seq=lat_seq)
            o = _ctx_dense(q, k, v, o, n_lat=n_lat, seq=seq)
        else:
            lam_init = 0.8 - 0.6 * math.exp(-0.3 * i)
            lp = diff_lambda[j].astype(F32)
            lam = (jnp.exp(jnp.sum(lp[0] * lp[1])) - jnp.exp(jnp.sum(lp[2] * lp[3]))
                   + lam_init).reshape(1)
            subg = diff_subln_g[j].reshape(1, LANES).astype(F32)
            o = _lat_diff(lam, q, k, vt, cache_k, cache_v, subg, layer=i,
                          n_lat_batch=n_lat_batch, lat_seq=lat_seq, out_scale=1.0 - lam_init)
            o = _ctx_diff(lam, q, k, v, subg, o, n_lat=n_lat, seq=seq, out_scale=1.0 - lam_init)

        rwh, rwl = _split_bf16(router_w[i].T)
        swgu = jnp.concatenate([shared_w_gate[i], shared_w_up[i]], axis=-1).astype(BF16)
        final = i == depth - 1
        routed = []
        for tile0, nt in ranges:
            xmid, h2, sel, gates, rank, counts = _post_attn(
                o, xs, mod, w_o_b[i], norm2_g[i].reshape(1, d), rwh, rwl,
                router_b[i].reshape(n_exp, 1).astype(F32), layer=i, tile0=tile0, n_tiles=nt,
                lat_seq=lat_seq, n_lat_batch=n_lat_batch, tm=tm)
            n_tiles = -(-(nt * tm * TOP_K) // tmb) + n_exp
            dest, tile_expert, tile_rows, n_used = _plan(counts, sel, rank, tmb=tmb, n_tiles=n_tiles)
            xb = _sc_dispatch(h2, dest, n_tiles * tmb)
            routed.append((xmid, h2, gates, dest, tile_expert, tile_rows, n_used, xb))
        ybs = [_experts(te, tr, nu, xb, exp_w_gate, exp_w_up, exp_w_down, layer=i, tmb=tmb)
               for (_, _, _, _, te, tr, nu, xb) in routed]
        ygs = [_sc_collect(yb, r[3]) for yb, r in zip(ybs, routed)]
        xs, y_lat, y_ctx = [], None, None
        for (tile0, nt), (xmid, h2, gates, *_), yg in zip(ranges, routed, ygs):
            out = _combine(xmid, h2, yg, gates.T, mod, swgu, shared_w_down[i].astype(BF16),
                           final_g.reshape(1, d), y_lat, layer=i, tile0=tile0, n_lat=n_lat,
                           n_ctx=n_ctx, lat_seq=lat_seq, n_lat_batch=n_lat_batch, tm=tm, final=final)
            if final:
                y_lat, y_ctx = out[0], (out[1] if out[1] is not None else y_ctx)
            else:
                xs.append(out)

    y_sample = y_lat.reshape(n_lat_batch, lat_seq, d)
    y_prompt = y_ctx.reshape(batch, seq, d)
    return (y_prompt, y_sample, caches[0], caches[1])
```

```python
import functools
import math

import jax
import jax.numpy as jnp
from jax import lax
from jax.experimental import pallas as pl
from jax.experimental.pallas import tpu as pltpu
from jax.experimental.pallas import tpu_sc as plsc

GRID_W = 64
NA_ROWS = 8
NA_COLS = 16
TOP_K = 8
ROUTED_SCALE = 2.5
ROPE_BASE = 10000.0
EPS = 1e-6
N_MOD = 6
HEAD_DIM = 64
LANES = 128
LOG2E = math.log2(math.e)
MASKED = -1e30
NA_HEAD_GROUP = 4
MOE_RANGES = 2
QUERY_GROUPS = 4
VMEM_LIMIT = 56 * 1024 * 1024

F32 = jnp.float32
BF16 = jnp.bfloat16


def _cparams(*sem):
    return pltpu.CompilerParams(dimension_semantics=sem, vmem_limit_bytes=VMEM_LIMIT)


def _dot(a, b):
    return jnp.dot(a, b, preferred_element_type=F32)


def _dot_nt(a, b):
    return lax.dot_general(a, b, (((1,), (1,)), ((), ())), preferred_element_type=F32)


def _sigmoid(x):
    return 1.0 / (1.0 + jnp.exp(-x))


def _rms(x, g):
    return x * lax.rsqrt(jnp.mean(x * x, axis=-1, keepdims=True) + EPS) * g


def _pack_rows(xb):
    half = xb.shape[1] // 2
    u = lax.bitcast_convert_type(xb.astype(F32), jnp.uint32)
    packed = (u[:, :half] >> 16) | (u[:, half:] & jnp.uint32(0xFFFF0000))
    return lax.bitcast_convert_type(packed, jnp.int32)


def _unpack_rows(p):
    u = lax.bitcast_convert_type(p, jnp.uint32)
    return (lax.bitcast_convert_type(u << 16, F32),
            lax.bitcast_convert_type(u & jnp.uint32(0xFFFF0000), F32))


def _pick_tile(n, pref):
    t = pref
    while n % t:
        t //= 2
    return t


def _mod_kernel(c_ref, w_ref, b_ref, o_ref):
    c = c_ref[...]
    o_ref[...] = _dot(c * _sigmoid(c), w_ref[...]) + b_ref[...]


def _modulation(cond, ada_w, ada_b):
    depth, d, n6 = ada_w.shape
    rows = cond.shape[0]
    tn = _pick_tile(n6, 512)
    return pl.pallas_call(
        _mod_kernel,
        grid=(depth, n6 // tn),
        in_specs=[
            pl.BlockSpec((rows, d), lambda l, j: (0, 0)),
            pl.BlockSpec((None, d, tn), lambda l, j: (l, 0, j)),
            pl.BlockSpec((None, 1, tn), lambda l, j: (l, 0, j)),
        ],
        out_specs=pl.BlockSpec((None, rows, tn), lambda l, j: (l, 0, j)),
        out_shape=jax.ShapeDtypeStruct((depth, rows, n6), F32),
        compiler_params=_cparams("arbitrary", "arbitrary"),
        name="modulation",
    )(cond, ada_w, ada_b.reshape(depth, 1, n6))


def _read_tokens(x_refs, g, ends):
    x = x_refs[-1][...]
    for ref, end in zip(reversed(x_refs[:-1]), reversed(ends[:-1])):
        x = jnp.where(g < end, ref[...], x)
    return x


def _token_parts(xs, tm, tile0, n_tiles):
    arrays, specs, ends = [], [], []
    start = 0
    for x in xs:
        size = x.shape[0] // tm
        if start < tile0 + n_tiles and start + size > tile0:
            arrays.append(x)
            ends.append(start + size)
            specs.append(pl.BlockSpec(
                (tm, x.shape[1]), lambda i, s=start, z=size: (jnp.clip(i + tile0 - s, 0, z - 1), 0)))
        start += size
    return arrays, specs, ends


def _qkv_kernel(*refs, d, scale, rope, n_lat_tiles, ends, n_alias):
    n_x = len(ends)
    x_refs = refs[:n_x]
    mod_ref, g_ref, w_ref = refs[n_x:n_x + 3]
    rest = refs[n_x + 3:]
    if rope:
        rope_ref, wvt_ref, rest = rest[0], rest[1], rest[2:]
    rest = rest[n_alias:]
    q_ref, k_ref, v_ref, kf_ref, vf_ref = rest[:5]
    i = pl.program_id(0)
    x = _read_tokens(x_refs, i, ends)
    h = _rms(x, g_ref[...]) * (1.0 + mod_ref[1:2, :]) + mod_ref[0:1, :]
    hb = h.astype(BF16)

    def rot(t):
        if not rope:
            return t
        cols = []
        for j in range(d // LANES):
            tb = t[:, j * LANES:(j + 1) * LANES]
            cols.append(tb * rope_ref[0]
                        + pltpu.roll(tb, LANES - HEAD_DIM // 2, axis=1) * rope_ref[1]
                        + pltpu.roll(tb, HEAD_DIM // 2, axis=1) * rope_ref[2])
        return jnp.concatenate(cols, axis=1)

    q = _dot(hb, w_ref[:, 0:d])
    q_ref[...] = (rot(q) * scale).astype(BF16)
    k = _dot(hb, w_ref[:, d:2 * d])
    k_ref[...] = rot(k).astype(BF16)
    if rope:
        @pl.when(i < n_lat_tiles)
        def _():
            rest[5][...] = _dot_nt(wvt_ref[...], hb).astype(BF16)

    def context_values():
        v = _dot(hb, w_ref[:, 2 * d:3 * d])
        v_ref[...] = v.astype(BF16)
        return v

    if not rope:
        v = context_values()

    @pl.when(i >= n_lat_tiles)
    def _():
        kf_ref[...] = k.reshape(kf_ref.shape)
        vf_ref[...] = (context_values() if rope else v).reshape(vf_ref.shape)


def _qkv(xs, mod, g, w, rope_tab, caches, *, layer, n_lat, batch, seq, lat_seq, n_lat_batch, tm,
         scale):
    d = xs[0].shape[1]
    depth = w.shape[0]
    n = n_lat + batch * seq
    n_lat_tiles = n_lat // tm
    rope = rope_tab is not None
    assert tm % seq == 0
    x_arrays, x_specs, ends = _token_parts(xs, tm, 0, n // tm)

    def mod_idx(i):
        return (layer, jnp.minimum(i * tm // lat_seq, n_lat_batch), 0, 0)

    in_specs = x_specs + [
        pl.BlockSpec((None, None, N_MOD, d), mod_idx),
        pl.BlockSpec((None, 1, d), lambda i: (layer, 0, 0)),
        pl.BlockSpec((None, d, 3 * d), lambda i: (layer, 0, 0)),
    ]
    args = x_arrays + [mod, g, w]
    if rope:
        seq_tiles = lat_seq // tm
        in_specs.append(pl.BlockSpec(
            (3, tm, LANES),
            lambda i: (0, jnp.where(i < n_lat_tiles, i % seq_tiles, seq_tiles), 0)))
        in_specs.append(pl.BlockSpec((d, d), lambda i: (0, 0)))
        args += [rope_tab, w[layer, :, 2 * d:].T]
    aliases = {}
    if caches is not None:
        aliases = {len(args): 3, len(args) + 1: 4}
        in_specs += [pl.BlockSpec(memory_space=pl.ANY)] * 2
        args += list(caches)
    tok = pl.BlockSpec((tm, d), lambda i: (i, 0))
    cache_spec = pl.BlockSpec((tm // seq, None, seq, d),
                              lambda i: (jnp.maximum(i - n_lat_tiles, 0), layer, 0, 0))
    out_specs = [tok, tok, tok, cache_spec, cache_spec]
    out_shape = ([jax.ShapeDtypeStruct((n, d), BF16)] * 3
                 + [jax.ShapeDtypeStruct((batch, depth, seq, d), F32)] * 2)
    if rope:
        out_specs.append(pl.BlockSpec((None, d, tm),
                                      lambda i: (jnp.minimum(i, n_lat_tiles - 1), 0, 0)))
        out_shape.append(jax.ShapeDtypeStruct((n_lat_tiles, d, tm), BF16))
    return pl.pallas_call(
        functools.partial(_qkv_kernel, d=d, scale=scale, rope=rope, n_lat_tiles=n_lat_tiles,
                          ends=tuple(ends), n_alias=len(aliases)),
        grid=(n // tm,),
        in_specs=in_specs,
        out_specs=out_specs,
        out_shape=out_shape,
        input_output_aliases=aliases,
        compiler_params=_cparams("arbitrary"),
        name=f"qkv{layer}",
    )(*args)


def _rope_tables(lat_seq, tm):
    n_freq = HEAD_DIM // 4
    inv = ROPE_BASE ** (-jnp.arange(n_freq, dtype=F32) / n_freq)
    pos = jnp.arange(lat_seq)
    row = (pos // GRID_W).astype(F32)
    col = (pos % GRID_W).astype(F32)
    ang = jnp.concatenate([row[:, None] * inv, col[:, None] * inv], axis=-1)
    reps = LANES // (HEAD_DIM // 2)
    cos = jnp.tile(jnp.cos(ang), (1, reps))
    sin = jnp.tile(jnp.sin(ang), (1, reps))
    first_half = (jnp.arange(LANES) % HEAD_DIM) < HEAD_DIM // 2
    s_next = jnp.where(first_half, -sin, 0.0)
    s_prev = jnp.where(first_half, 0.0, sin)
    ident = jnp.stack([jnp.ones((tm, LANES), F32), jnp.zeros((tm, LANES), F32),
                       jnp.zeros((tm, LANES), F32)])
    return jnp.concatenate([jnp.stack([cos, s_next, s_prev]), ident], axis=1)


def _lane_lo():
    return lax.broadcasted_iota(jnp.int32, (1, LANES), 1) < HEAD_DIM


def _ctx_dense_kernel(q_ref, k_ref, v_ref, o_all_ref, o_ref, *, d):
    del o_all_ref
    lo = _lane_lo()
    cols = []
    for j in range(d // LANES):
        blk = slice(j * LANES, (j + 1) * LANES)
        q2, k2, v2 = q_ref[:, blk], k_ref[:, blk], v_ref[:, blk]
        outs = []
        for qm in (jnp.where(lo, q2, 0), jnp.where(lo, 0, q2)):
            s = _dot_nt(qm, k2)
            p = jnp.exp2(s - jnp.max(s, axis=-1, keepdims=True))
            outs.append(_dot(p.astype(BF16), v2) / jnp.sum(p, axis=-1, keepdims=True))
        cols.append(jnp.where(lo, outs[0], outs[1]).astype(BF16))
    o_ref[...] = jnp.concatenate(cols, axis=1)


def _ctx_dense(q, k, v, o_all, *, n_lat, seq):
    n, d = q.shape
    off = n_lat // seq
    spec = pl.BlockSpec((seq, d), lambda b: (off + b, 0))
    return pl.pallas_call(
        functools.partial(_ctx_dense_kernel, d=d),
        grid=((n - n_lat) // seq,),
        in_specs=[spec, spec, spec, pl.BlockSpec(memory_space=pl.ANY)],
        out_specs=spec,
        out_shape=jax.ShapeDtypeStruct((n, d), BF16),
        input_output_aliases={3: 0},
        compiler_params=_cparams("arbitrary"),
        name="ctx_dense_attn",
    )(q, k, v, o_all)


def _subln(o, g_ref, out_scale):
    return _rms(o, g_ref[...]) * out_scale


def _ctx_diff_kernel(lam_ref, q_ref, k_ref, v_ref, g_ref, o_all_ref, o_ref, *, d, out_scale):
    del o_all_ref
    lo = _lane_lo()
    lam = lam_ref[0]
    cols = []
    for j in range(d // LANES):
        blk = slice(j * LANES, (j + 1) * LANES)
        q2, k2, v2 = q_ref[:, blk], k_ref[:, blk], v_ref[:, blk]
        ps = []
        for qm in (jnp.where(lo, q2, 0), jnp.where(lo, 0, q2)):
            s = _dot_nt(qm, k2)
            p = jnp.exp2(s - jnp.max(s, axis=-1, keepdims=True))
            ps.append(p / jnp.sum(p, axis=-1, keepdims=True))
        o = _dot((ps[0] - lam * ps[1]).astype(BF16), v2)
        cols.append(_subln(o, g_ref, out_scale).astype(BF16))
    o_ref[...] = jnp.concatenate(cols, axis=1)


def _ctx_diff(lam, q, k, v, subg, o_all, *, n_lat, seq, out_scale):
    n, d = q.shape
    off = n_lat // seq
    spec = pl.BlockSpec((seq, d), lambda b: (off + b, 0))
    return pl.pallas_call(
        functools.partial(_ctx_diff_kernel, d=d, out_scale=out_scale),
        grid=((n - n_lat) // seq,),
        in_specs=[pl.BlockSpec(memory_space=pltpu.SMEM), spec, spec, spec,
                  pl.BlockSpec((1, LANES), lambda b: (0, 0)), pl.BlockSpec(memory_space=pl.ANY)],
        out_specs=spec,
        out_shape=jax.ShapeDtypeStruct((n, d), BF16),
        input_output_aliases={5: 0},
        compiler_params=_cparams("arbitrary"),
        name="ctx_diff_attn",
    )(lam, q, k, v, subg, o_all)


def _na_kernel(q_ref, k_ref, v_ref, ck_ref, cv_ref, bias_ref, o_ref, *, d, rows):
    lo = _lane_lo()
    r = pl.program_id(1)
    win = NA_ROWS * GRID_W
    start = pl.multiple_of(jnp.clip(r - NA_ROWS // 2, 0, rows - NA_ROWS) * GRID_W, GRID_W)
    n_heads = 2 * (d // LANES)

    def blk(h):
        return slice(h // 2 * LANES, (h // 2 + 1) * LANES)

    def scores(heads):
        out = []
        for h in heads:
            q2 = q_ref[:, blk(h)]
            qm = jnp.where(lo, q2, 0) if h % 2 == 0 else jnp.where(lo, 0, q2)
            out.append((_dot_nt(qm, k_ref[pl.ds(start, win), blk(h)]) + bias_ref[h],
                        _dot_nt(qm, ck_ref[:, blk(h)].astype(BF16))))
        return out

    def attend(heads, ss):
        probs = []
        for s_loc, s_ctx in ss:
            m = jnp.maximum(jnp.max(s_loc, axis=-1, keepdims=True),
                            jnp.max(s_ctx, axis=-1, keepdims=True))
            p_loc = jnp.exp2(s_loc - m)
            p_ctx = jnp.exp2(s_ctx - m)
            l = jnp.sum(p_loc, axis=-1, keepdims=True) + jnp.sum(p_ctx, axis=-1, keepdims=True)
            probs.append((p_loc.astype(BF16), p_ctx.astype(BF16), l))
        return [(_dot(p_loc, v_ref[pl.ds(start, win), blk(h)])
                 + _dot(p_ctx, cv_ref[:, blk(h)].astype(BF16))) / l
                for h, (p_loc, p_ctx, l) in zip(heads, probs)]

    groups = [list(range(g, min(g + NA_HEAD_GROUP, n_heads))) for g in range(0, n_heads, NA_HEAD_GROUP)]
    outs = []
    ss = scores(groups[0])
    for gi, heads in enumerate(groups):
        ss_next = scores(groups[gi + 1]) if gi + 1 < len(groups) else None
        outs += attend(heads, ss)
        ss = ss_next
    o_ref[...] = jnp.concatenate(
        [jnp.where(lo, outs[h], outs[h + 1]).astype(BF16) for h in range(0, n_heads, 2)], axis=1)


def _na_bias_table(rpb, rows):
    h = rpb.shape[0]
    delta = jnp.arange(NA_ROWS)
    kr = jnp.arange(NA_ROWS)
    c = jnp.arange(GRID_W)
    kc = jnp.arange(GRID_W)
    off_r = kr[None, :] - delta[:, None] + (NA_ROWS - 1)
    c0 = jnp.clip(c - NA_COLS // 2, 0, GRID_W - NA_COLS)
    off_c = kc[None, :] - c[:, None] + (NA_COLS - 1)
    inside = (kc[None, :] >= c0[:, None]) & (kc[None, :] < c0[:, None] + NA_COLS)
    oh_r = jax.nn.one_hot(off_r, 2 * NA_ROWS - 1, dtype=F32)
    oh_c = jax.nn.one_hot(off_c, 2 * NA_COLS - 1, dtype=F32)
    t = jnp.einsum('hrs,dkr,cqs->dhckq', rpb.astype(F32), oh_r, oh_c,
                   precision=lax.Precision.HIGHEST)
    t = jnp.where(inside[None, None, :, None, :], t * LOG2E, MASKED)
    return t.reshape(NA_ROWS, h, GRID_W, NA_ROWS * GRID_W)


def _na_attention(q, k, v, ck, cv, bias, *, layer, n_lat_batch, lat_seq):
    n, d = q.shape
    rows = lat_seq // GRID_W
    past = ck.shape[2]
    n_heads = bias.shape[1]

    def delta_idx(b, r):
        return (r - jnp.clip(r - NA_ROWS // 2, 0, rows - NA_ROWS), 0, 0, 0)

    kv_spec = pl.BlockSpec((lat_seq, d), lambda b, r: (b, 0))
    c_spec = pl.BlockSpec((None, None, past, d), lambda b, r: (b, layer, 0, 0))
    return pl.pallas_call(
        functools.partial(_na_kernel, d=d, rows=rows),
        grid=(n_lat_batch, rows),
        in_specs=[pl.BlockSpec((GRID_W, d), lambda b, r: (b * rows + r, 0)), kv_spec, kv_spec,
                  c_spec, c_spec,
                  pl.BlockSpec((None, n_heads, GRID_W, NA_ROWS * GRID_W), delta_idx)],
        out_specs=pl.BlockSpec((GRID_W, d), lambda b, r: (b * rows + r, 0)),
        out_shape=jax.ShapeDtypeStruct((n, d), BF16),
        compiler_params=_cparams("arbitrary", "arbitrary"),
        name="na_attn",
    )(q, k, v, ck, cv, bias)


def _lat_diff_kernel(lam_ref, q_ref, k_ref, vt_ref, ck_ref, cv_ref, g_ref, o_ref, *, out_scale):
    lo = _lane_lo()
    lam = lam_ref[0]
    q2 = q_ref[...]
    tq = q2.shape[0]
    tg = tq // QUERY_GROUPS
    qs = []
    for g in range(QUERY_GROUPS):
        qg = q2[g * tg:(g + 1) * tg, :]
        qs += [jnp.where(lo, qg, 0), jnp.where(lo, 0, qg)]

    def softmax_step(state, s):
        m, l, acc = state
        m_new = jnp.maximum(m, jnp.max(s, axis=0, keepdims=True))
        alpha = jnp.exp2(m - m_new)
        p = jnp.exp2(s - m_new)
        return m_new, alpha * l + jnp.sum(p, axis=0, keepdims=True), alpha * acc, p.astype(BF16)

    def scores(kt):
        return [_dot_nt(kt, qm) for qm in qs]

    def absorb(states, ss, vt):
        out = []
        for g in range(0, len(qs), 2):
            stepped = [softmax_step(states[g + j], ss[g + j]) for j in range(2)]
            out += [(m, l, acc + _dot(vt, p)) for m, l, acc, p in stepped]
        return out

    tile = vt_ref.shape[2]
    n_tiles = vt_ref.shape[0]

    def keys(t):
        return k_ref[t * tile:(t + 1) * tile, :] if t < n_tiles else ck_ref[...].astype(BF16)

    def values_t(t):
        return vt_ref[t] if t < n_tiles else cv_ref[...].T.astype(BF16)

    init = (jnp.full((1, tg), -jnp.inf, F32), jnp.zeros((1, tg), F32), jnp.zeros((LANES, tg), F32))
    states = [init] * len(qs)
    ss = scores(keys(0))
    for t in range(n_tiles + 1):
        ss_next = scores(keys(t + 1)) if t < n_tiles else None
        states = absorb(states, ss, values_t(t))
        ss = ss_next
    o = jnp.concatenate([states[g][2] / states[g][1] - lam * (states[g + 1][2] / states[g + 1][1])
                         for g in range(0, len(qs), 2)], axis=1).T
    o_ref[...] = _subln(o, g_ref, out_scale).astype(BF16)


def _lat_diff(lam, q, k, vt, ck, cv, subg, *, layer, n_lat_batch, lat_seq, out_scale):
    n, d = q.shape
    past = ck.shape[2]
    tk = vt.shape[2]
    tq = _pick_tile(lat_seq, QUERY_GROUPS * 256)
    qt = lat_seq // tq
    kt = lat_seq // tk
    c_spec = pl.BlockSpec((None, None, past, LANES), lambda b, h, i: (b, layer, 0, h))
    q_spec = pl.BlockSpec((tq, LANES), lambda b, h, i: (b * qt + i, h))
    return pl.pallas_call(
        functools.partial(_lat_diff_kernel, out_scale=out_scale),
        grid=(n_lat_batch, d // LANES, qt),
        in_specs=[pl.BlockSpec(memory_space=pltpu.SMEM), q_spec,
                  pl.BlockSpec((lat_seq, LANES), lambda b, h, i: (b, h)),
                  pl.BlockSpec((kt, LANES, tk), lambda b, h, i: (b, h, 0)),
                  c_spec, c_spec, pl.BlockSpec((1, LANES), lambda b, h, i: (0, 0))],
        out_specs=q_spec,
        out_shape=jax.ShapeDtypeStruct((n, d), BF16),
        compiler_params=_cparams("arbitrary", "arbitrary", "arbitrary"),
        name="lat_diff_attn",
    )(lam, q, k, vt, ck, cv, subg)


def _post_attn_kernel(o_ref, *refs, ends, tile0):
    n_x = len(ends)
    x_refs = refs[:n_x]
    (mod_ref, wo_ref, g_ref, rwh_ref, rwl_ref, rb_ref, tri_ref,
     xmid_ref, h_ref, sel_ref, gate_ref, rank_ref, cnt_ref, carry_ref) = refs[n_x:]
    i = pl.program_id(0)

    @pl.when(i == 0)
    def _():
        carry_ref[...] = jnp.zeros_like(carry_ref)

    x1 = _read_tokens(x_refs, i + tile0, ends) + mod_ref[2:3, :] * _dot(o_ref[...], wo_ref[...])
    xmid_ref[...] = x1
    h = _rms(x1, g_ref[...]) * (1.0 + mod_ref[4:5, :]) + mod_ref[3:4, :]
    hb = h.astype(BF16)
    h_ref[...] = _pack_rows(hb)
    h_lo = (h - hb.astype(F32)).astype(BF16)
    logits = _dot_nt(rwh_ref[...], hb) + (_dot_nt(rwl_ref[...], hb) + _dot_nt(rwh_ref[...], h_lo))
    scores = _sigmoid(logits)
    n_exp, tm = scores.shape
    biased = scores + rb_ref[...]
    row = lax.broadcasted_iota(jnp.int32, (n_exp, tm), 0).astype(F32)
    total = jnp.zeros((n_exp, tm), F32)
    sels, gates = [], []
    for _ in range(TOP_K):
        m = jnp.max(biased, axis=0, keepdims=True)
        idx = jnp.min(jnp.where(biased == m, row, float(n_exp)), axis=0, keepdims=True)
        hit = row == idx
        gates.append(jnp.sum(jnp.where(hit, scores, 0.0), axis=0, keepdims=True))
        biased = jnp.where(hit, -jnp.inf, biased)
        total = total + jnp.where(hit, 1.0, 0.0)
        sels.append(idx)
    denom = gates[0]
    for g in gates[1:]:
        denom = denom + g
    before = _dot(total.astype(BF16), tri_ref[...]) + carry_ref[:, 0:1]
    for k in range(TOP_K):
        sel_ref[k:k + 1, :] = sels[k].astype(jnp.int32)
        gate_ref[k:k + 1, :] = gates[k] / denom * ROUTED_SCALE
        rank_ref[k:k + 1, :] = jnp.sum(jnp.where(row == sels[k], before, 0.0), axis=0,
                                       keepdims=True).astype(jnp.int32)
    carry_ref[...] = carry_ref[...] + jnp.sum(total, axis=1, keepdims=True)
    cnt_ref[...] = carry_ref[...].astype(jnp.int32)


def _post_attn(o, xs, mod, wo, g, rwh, rwl, rb, *, layer, tile0, n_tiles, lat_seq, n_lat_batch, tm):
    d = o.shape[1]
    n = n_tiles * tm
    n_exp = rwh.shape[0]
    tri = (jnp.arange(tm)[:, None] < jnp.arange(tm)[None, :]).astype(BF16)
    x_arrays, x_specs, ends = _token_parts(xs, tm, tile0, n_tiles)

    def mod_idx(i):
        return (layer, jnp.minimum((i + tile0) * tm // lat_seq, n_lat_batch), 0, 0)

    tok = pl.BlockSpec((tm, d), lambda i: (i, 0))
    tok_p = pl.BlockSpec((tm, d // 2), lambda i: (i, 0))
    kt = pl.BlockSpec((TOP_K, tm), lambda i: (0, i))
    const2 = lambda i: (0, 0)
    return pl.pallas_call(
        functools.partial(_post_attn_kernel, ends=tuple(ends), tile0=tile0),
        grid=(n_tiles,),
        in_specs=[pl.BlockSpec((tm, d), lambda i: (i + tile0, 0))] + x_specs + [
                  pl.BlockSpec((None, None, N_MOD, d), mod_idx),
                  pl.BlockSpec((d, d), const2), pl.BlockSpec((1, d), const2),
                  pl.BlockSpec((n_exp, d), const2), pl.BlockSpec((n_exp, d), const2),
                  pl.BlockSpec((n_exp, 1), const2), pl.BlockSpec((tm, tm), const2)],
        out_specs=[tok, tok_p, kt, kt, kt, pl.BlockSpec((n_exp, LANES), const2)],
        out_shape=[jax.ShapeDtypeStruct((n, d), F32), jax.ShapeDtypeStruct((n, d // 2), jnp.int32),
                   jax.ShapeDtypeStruct((TOP_K, n), jnp.int32),
                   jax.ShapeDtypeStruct((TOP_K, n), F32),
                   jax.ShapeDtypeStruct((TOP_K, n), jnp.int32),
                   jax.ShapeDtypeStruct((n_exp, LANES), jnp.int32)],
        scratch_shapes=[pltpu.VMEM((n_exp, LANES), F32)],
        compiler_params=_cparams("arbitrary"),
        name=f"post_attn{layer}",
    )(o, *x_arrays, mod, wo, g, rwh, rwl, rb, tri)


def _plan_kernel(cnt_ref, sel_ref, rank_ref, dest_ref, te_ref, tv_ref, nu_ref, *, tmb):
    n_exp = cnt_ref.shape[0]
    sel = sel_ref[...]
    tile_start = lax.broadcasted_iota(jnp.int32, te_ref.shape, 1) * tmb
    dest = rank_ref[...]
    te = jnp.zeros(te_ref.shape, jnp.int32)
    tv = jnp.zeros(te_ref.shape, jnp.int32)
    start = jnp.zeros((1, 1), jnp.int32)
    for e in range(n_exp):
        dest = dest + jnp.where(sel == e, start, 0)
        cnt = cnt_ref[e:e + 1, 0:1]
        stop = start + ((cnt + (tmb - 1)) & -tmb)
        mine = (tile_start >= start) & (tile_start < stop)
        tv = tv + jnp.where(mine, jnp.clip(start + cnt - tile_start, 0, tmb), 0)
        start = stop
        te = te + jnp.where(start <= tile_start, 1, 0)
    te_ref[...] = jnp.minimum(te, n_exp - 1)
    tv_ref[...] = tv
    nu_ref[...] = jnp.broadcast_to(start, nu_ref.shape)
    for c in range(dest_ref.shape[0]):
        dest_ref[c] = dest[:, c * SC_CHUNK:(c + 1) * SC_CHUNK]


def _plan(counts, sel, rank, *, tmb, n_tiles):
    k_top, n = sel.shape
    assert tmb & (tmb - 1) == 0
    tn = _pick_tile(n, 16 * SC_CHUNK)
    te_w = -(-n_tiles // LANES) * LANES
    tok = pl.BlockSpec((k_top, tn), lambda i: (0, i))
    const2 = lambda i: (0, 0)
    dest, te, tv, nu = pl.pallas_call(
        functools.partial(_plan_kernel, tmb=tmb),
        grid=(n // tn,),
        in_specs=[pl.BlockSpec(counts.shape, const2), tok, tok],
        out_specs=[pl.BlockSpec((tn // SC_CHUNK, k_top, SC_CHUNK), lambda i: (i, 0, 0)),
                   pl.BlockSpec((1, te_w), const2), pl.BlockSpec((1, te_w), const2),
                   pl.BlockSpec((1, LANES), const2)],
        out_shape=[jax.ShapeDtypeStruct((n // SC_CHUNK, k_top, SC_CHUNK), jnp.int32),
                   jax.ShapeDtypeStruct((1, te_w), jnp.int32),
                   jax.ShapeDtypeStruct((1, te_w), jnp.int32),
                   jax.ShapeDtypeStruct((1, LANES), jnp.int32)],
        compiler_params=_cparams("arbitrary"),
        name="plan",
    )(counts, sel, rank)
    return dest, te[0, :n_tiles], tv[0, :n_tiles], nu[0, :1] // tmb


def _expert_kernel(te_ref, tv_ref, nu_ref, x_ref, wg_ref, wu_ref, wd_ref, y_ref, wg_b, wu_b, wd_b):
    i = pl.program_id(0)
    half = x_ref.shape[0] // 2

    def rows(r):
        return jnp.concatenate(_unpack_rows(x_ref[r * half:(r + 1) * half, :]), axis=1).astype(BF16)

    def swiglu(xs):
        gu = [(_dot(x, wg_b[...]), _dot(x, wu_b[...])) for x in xs]
        acts = [(g * _sigmoid(g) * u).astype(BF16) for g, u in gu]
        return [_pack_rows(_dot(a, wd_b[...]).astype(BF16)) for a in acts]

    @pl.when(i < nu_ref[0])
    def _():
        @pl.when((i == 0) | (te_ref[i] != te_ref[jnp.maximum(i - 1, 0)]))
        def _():
            wg_b[...] = wg_ref[...].astype(BF16)
            wu_b[...] = wu_ref[...].astype(BF16)
            wd_b[...] = wd_ref[...].astype(BF16)

        @pl.when(tv_ref[i] > half)
        def _():
            ya, yb = swiglu([rows(0), rows(1)])
            y_ref[0:half, :] = ya
            y_ref[half:2 * half, :] = yb

        @pl.when(tv_ref[i] <= half)
        def _():
            y_ref[0:half, :] = swiglu([rows(0)])[0]


def _experts(tile_expert, tile_rows, n_used, xb, wg, wu, wd, *, layer, tmb):
    cap, dp = xb.shape
    f, d = wd.shape[2:]

    def row_idx(i, te, tv, nu):
        return (jnp.minimum(i, nu[0] - 1), 0)

    def w_idx(i, te, tv, nu):
        return (layer, te[i], 0, 0)

    return pl.pallas_call(
        _expert_kernel,
        grid_spec=pltpu.PrefetchScalarGridSpec(
            num_scalar_prefetch=3,
            grid=(cap // tmb,),
            in_specs=[pl.BlockSpec((tmb, dp), row_idx),
                      pl.BlockSpec((None, None, d, f), w_idx),
                      pl.BlockSpec((None, None, d, f), w_idx),
                      pl.BlockSpec((None, None, f, d), w_idx)],
            out_specs=pl.BlockSpec((tmb, dp), row_idx),
            scratch_shapes=[pltpu.VMEM((d, f), BF16), pltpu.VMEM((d, f), BF16),
                            pltpu.VMEM((f, d), BF16)]),
        out_shape=jax.ShapeDtypeStruct((cap, dp), jnp.int32),
        compiler_params=_cparams("arbitrary"),
        name="experts",
    )(tile_expert, tile_rows, n_used, xb, wg, wu, wd)


def _combine_kernel(x_ref, h_ref, yg_ref, gt_ref, mod_ref, wgu_ref, wd_ref, fg_ref, *refs, f, final,
                    n_alias, tile0, n_lat_tiles):
    o_refs = refs[n_alias:]
    h = jnp.concatenate(_unpack_rows(h_ref[...]), axis=1).astype(BF16)
    gu = _dot(h, wgu_ref[...])
    gate, up = gu[:, :f], gu[:, f:]
    shared = _dot((gate * _sigmoid(gate) * up).astype(BF16), wd_ref[...])
    lo = hi = None
    for k in range(TOP_K):
        y_lo, y_hi = _unpack_rows(yg_ref[k])
        g = gt_ref[:, k:k + 1]
        lo = g * y_lo if lo is None else lo + g * y_lo
        hi = g * y_hi if hi is None else hi + g * y_hi
    acc = shared + jnp.concatenate([lo, hi], axis=1)
    x2 = x_ref[...] + mod_ref[5:6, :] * acc
    if not final:
        o_refs[0][...] = x2
    elif len(o_refs) == 1:
        o_refs[0][...] = _rms(x2, fg_ref[...])
    else:
        y = _rms(x2, fg_ref[...])
        tile = pl.program_id(0) + tile0

        @pl.when(tile < n_lat_tiles)
        def _():
            o_refs[0][...] = y

        @pl.when(tile >= n_lat_tiles)
        def _():
            o_refs[1][...] = y


def _combine(x, h, yg, gates_t, mod, wgu, wd, final_g, y_lat, *, layer, tile0, n_lat, n_ctx, lat_seq,
             n_lat_batch, tm, final):
    n, d = x.shape
    f = wd.shape[0]
    n_tiles = n // tm
    n_lat_tiles = n_lat // tm

    def mod_idx(i):
        return (layer, jnp.minimum((i + tile0) * tm // lat_seq, n_lat_batch), 0, 0)

    tok = pl.BlockSpec((tm, d), lambda i: (i, 0))
    tok_p = pl.BlockSpec((tm, d // 2), lambda i: (i, 0))
    const2 = lambda i: (0, 0)
    in_specs = [tok, tok_p, pl.BlockSpec((TOP_K, tm, d // 2), lambda i: (0, i, 0)),
                pl.BlockSpec((tm, TOP_K), lambda i: (i, 0)),
                pl.BlockSpec((None, None, N_MOD, d), mod_idx),
                pl.BlockSpec((d, 2 * f), const2), pl.BlockSpec((f, d), const2),
                pl.BlockSpec((1, d), const2)]
    args = [x, h, yg, gates_t, mod, wgu, wd, final_g]
    aliases = {}
    has_lat = has_ctx = False
    if not final:
        out_specs, out_shape = [tok], [jax.ShapeDtypeStruct((n, d), F32)]
    else:
        has_lat = tile0 < n_lat_tiles
        has_ctx = tile0 + n_tiles > n_lat_tiles
        out_specs, out_shape = [], []
        if has_lat:
            out_specs.append(pl.BlockSpec(
                (tm, d), lambda i: (jnp.minimum(i + tile0, n_lat_tiles - 1), 0)))
            out_shape.append(jax.ShapeDtypeStruct((n_lat, d), F32))
            if y_lat is not None:
                aliases = {len(args): 0}
                in_specs.append(pl.BlockSpec(memory_space=pl.ANY))
                args.append(y_lat)
        if has_ctx:
            out_specs.append(pl.BlockSpec(
                (tm, d), lambda i: (jnp.maximum(i + tile0 - n_lat_tiles, 0), 0)))
            out_shape.append(jax.ShapeDtypeStruct((n_ctx, d), F32))
    outs = pl.pallas_call(
        functools.partial(_combine_kernel, f=f, final=final, n_alias=len(aliases), tile0=tile0,
                          n_lat_tiles=n_lat_tiles),
        grid=(n_tiles,),
        in_specs=in_specs,
        out_specs=out_specs,
        out_shape=out_shape,
        input_output_aliases=aliases,
        compiler_params=_cparams("arbitrary"),
        name=f"combine{layer}",
    )(*args)
    if not final:
        return outs[0]
    return (outs[0] if has_lat else y_lat), (outs[-1] if has_ctx else None)


SC_CHUNK = 128


def _sc_workers():
    info = plsc.get_sparse_core_info()
    return info.num_cores, info.num_subcores


def _sc_dispatch(rows, dest, cap):
    n, w = rows.shape
    nc, ns = _sc_workers()
    per_w = n // (nc * ns)
    assert per_w * nc * ns == n and per_w % SC_CHUNK == 0
    mesh = plsc.VectorSubcoreMesh(core_axis_name="c", subcore_axis_name="s")

    @functools.partial(
        pl.kernel, mesh=mesh, out_type=jax.ShapeDtypeStruct((cap, w), rows.dtype),
        scratch_types=[pltpu.VMEM((TOP_K, SC_CHUNK), jnp.int32),
                       pltpu.VMEM((SC_CHUNK, w), rows.dtype)],
        name="sc_dispatch")
    def run(rows_hbm, dest_hbm, out_hbm, idx_v, rows_v):
        wid = lax.axis_index("s") * nc + lax.axis_index("c")

        @pl.loop(0, per_w // SC_CHUNK)
        def _(ci):
            base = pl.multiple_of(wid * per_w + ci * SC_CHUNK, SC_CHUNK)
            pltpu.sync_copy(dest_hbm.at[wid * (per_w // SC_CHUNK) + ci], idx_v)
            pltpu.sync_copy(rows_hbm.at[pl.ds(base, SC_CHUNK)], rows_v)
            for k in range(TOP_K):
                pltpu.sync_copy(rows_v, out_hbm.at[idx_v.at[k]])

    return run(rows, dest)


def _sc_collect(table, dest):
    n_chunks, k_top, _ = dest.shape
    n = n_chunks * SC_CHUNK
    w = table.shape[1]
    nc, ns = _sc_workers()
    per_w = n // (nc * ns)
    assert per_w * nc * ns == n and per_w % SC_CHUNK == 0
    mesh = plsc.VectorSubcoreMesh(core_axis_name="c", subcore_axis_name="s")

    @functools.partial(
        pl.kernel, mesh=mesh, out_type=jax.ShapeDtypeStruct((k_top, n, w), table.dtype),
        scratch_types=[pltpu.VMEM((TOP_K, SC_CHUNK), jnp.int32),
                       pltpu.VMEM((SC_CHUNK, w), table.dtype)],
        name="sc_collect")
    def run(table_hbm, dest_hbm, out_hbm, idx_v, rows_v):
        wid = lax.axis_index("s") * nc + lax.axis_index("c")

        @pl.loop(0, per_w // SC_CHUNK)
        def _(ci):
            base = pl.multiple_of(wid * per_w + ci * SC_CHUNK, SC_CHUNK)
            pltpu.sync_copy(dest_hbm.at[wid * (per_w // SC_CHUNK) + ci], idx_v)
            for k in range(TOP_K):
                pltpu.sync_copy(table_hbm.at[idx_v.at[k]], rows_v)
                pltpu.sync_copy(rows_v, out_hbm.at[k, pl.ds(base, SC_CHUNK)])

    return run(table, dest)


def _split_bf16(w):
    hi = w.astype(BF16)
    return hi, (w - hi.astype(F32)).astype(BF16)


def kernel(x_prompt, x_sample, cache_k, cache_v, c, c_ctx, ada_w, ada_b, norm1_g, norm2_g, w_qkv, w_o, na_rpb, diff_lambda, diff_subln_g, router_w, router_b, exp_w_gate, exp_w_up, exp_w_down, shared_w_gate, shared_w_up, shared_w_down, final_g):
    batch, seq, d = x_prompt.shape
    n_lat_batch, lat_seq, _ = x_sample.shape
    depth = w_qkv.shape[0]
    n_exp = router_w.shape[-1]
    n_lat = n_lat_batch * lat_seq
    n_ctx = batch * seq
    n = n_lat + n_ctx
    assert d % LANES == 0 and na_rpb.shape[1] * HEAD_DIM == d
    assert diff_lambda.shape[-1] == HEAD_DIM and lat_seq % GRID_W == 0
    assert lat_seq // GRID_W >= NA_ROWS and n_lat % seq == 0
    tm = _pick_tile(math.gcd(lat_seq, n_ctx), 512)
    tmb = 1024
    n_all = n // tm
    ranges = [(r * n_all // MOE_RANGES, (r + 1) * n_all // MOE_RANGES - r * n_all // MOE_RANGES)
              for r in range(MOE_RANGES)]
    scale = HEAD_DIM ** -0.5 * LOG2E

    xs = [x_sample.reshape(n_lat, d), x_prompt.reshape(n_ctx, d)]
    mod_rows = -(-(n_lat_batch + 1) // 8) * 8
    cond = jnp.zeros((mod_rows, d), F32).at[:n_lat_batch].set(c).at[n_lat_batch].set(c_ctx)
    mod = _modulation(cond, ada_w, ada_b).reshape(depth, mod_rows, N_MOD, d)

    w_qkv_b = w_qkv.astype(BF16)
    w_o_b = w_o.astype(BF16)
    rope_tab = _rope_tables(lat_seq, tm)
    caches = None
    for i in range(depth):
        is_diff = i % 2 == 1
        j = i // 2
        q, k, v, *caches = _qkv(xs, mod, norm1_g.reshape(depth, 1, d), w_qkv_b,
                                rope_tab if is_diff else None, caches, layer=i, n_lat=n_lat,
                                batch=batch, seq=seq, lat_seq=lat_seq, n_lat_batch=n_lat_batch,
                                tm=tm, scale=scale)
        if is_diff:
            vt = caches.pop()
        if not is_diff:
            o = _na_attention(q, k, v, cache_k, cache_v,
                              _na_bias_table(na_rpb[j], lat_seq // GRID_W), layer=i,
                              n_lat_batch=n_lat_batch, lat_seq=lat_seq)
            o = _ctx_dense(q, k, v, o, n_lat=n_lat, seq=seq)
        else:
            lam_init = 0.8 - 0.6 * math.exp(-0.3 * i)
            lp = diff_lambda[j].astype(F32)
            lam = (jnp.exp(jnp.sum(lp[0] * lp[1])) - jnp.exp(jnp.sum(lp[2] * lp[3]))
                   + lam_init).reshape(1)
            subg = diff_subln_g[j].reshape(1, LANES).astype(F32)
            o = _lat_diff(lam, q, k, vt, cache_k, cache_v, subg, layer=i,
                          n_lat_batch=n_lat_batch, lat_seq=lat_seq, out_scale=1.0 - lam_init)
            o = _ctx_diff(lam, q, k, v, subg, o, n_lat=n_lat, seq=seq, out_scale=1.0 - lam_init)

        rwh, rwl = _split_bf16(router_w[i].T)
        swgu = jnp.concatenate([shared_w_gate[i], shared_w_up[i]], axis=-1).astype(BF16)
        final = i == depth - 1
        routed = []
        for tile0, nt in ranges:
            xmid, h2, sel, gates, rank, counts = _post_attn(
                o, xs, mod, w_o_b[i], norm2_g[i].reshape(1, d), rwh, rwl,
                router_b[i].reshape(n_exp, 1).astype(F32), layer=i, tile0=tile0, n_tiles=nt,
                lat_seq=lat_seq, n_lat_batch=n_lat_batch, tm=tm)
            n_tiles = -(-(nt * tm * TOP_K) // tmb) + n_exp
            dest, tile_expert, tile_rows, n_used = _plan(counts, sel, rank, tmb=tmb, n_tiles=n_tiles)
            xb = _sc_dispatch(h2, dest, n_tiles * tmb)
            routed.append((xmid, h2, gates, dest, tile_expert, tile_rows, n_used, xb))
        ybs = [_experts(te, tr, nu, xb, exp_w_gate, exp_w_up, exp_w_down, layer=i, tmb=tmb)
               for (_, _, _, _, te, tr, nu, xb) in routed]
        ygs = [_sc_collect(yb, r[3]) for yb, r in zip(ybs, routed)]
        xs, y_lat, y_ctx = [], None, None
        for (tile0, nt), (xmid, h2, gates, *_), yg in zip(ranges, routed, ygs):
            out = _combine(xmid, h2, yg, gates.T, mod, swgu, shared_w_down[i].astype(BF16),
                           final_g.reshape(1, d), y_lat, layer=i, tile0=tile0, n_lat=n_lat,
                           n_ctx=n_ctx, lat_seq=lat_seq, n_lat_batch=n_lat_batch, tm=tm, final=final)
            if final:
                y_lat, y_ctx = out[0], (out[1] if out[1] is not None else y_ctx)
            else:
                xs.append(out)

    y_sample = y_lat.reshape(n_lat_batch, lat_seq, d)
    y_prompt = y_ctx.reshape(batch, seq, d)
    return (y_prompt, y_sample, caches[0], caches[1])
```

```python
import functools
import math

import jax
import jax.numpy as jnp
from jax import lax
from jax.experimental import pallas as pl
from jax.experimental.pallas import tpu as pltpu
from jax.experimental.pallas import tpu_sc as plsc

GRID_W = 64
NA_ROWS = 8
NA_COLS = 16
TOP_K = 8
ROUTED_SCALE = 2.5
ROPE_BASE = 10000.0
EPS = 1e-6
N_MOD = 6
HEAD_DIM = 64
LANES = 128
LOG2E = math.log2(math.e)
MASKED = -1e30
NA_HEAD_GROUP = 16
MOE_RANGES = 2
QUERY_GROUPS = 4
VMEM_LIMIT = 56 * 1024 * 1024

F32 = jnp.float32
BF16 = jnp.bfloat16


def _cparams(*sem):
    return pltpu.CompilerParams(dimension_semantics=sem, vmem_limit_bytes=VMEM_LIMIT)


def _dot(a, b):
    return jnp.dot(a, b, preferred_element_type=F32)


def _dot_nt(a, b):
    return lax.dot_general(a, b, (((1,), (1,)), ((), ())), preferred_element_type=F32)


def _sigmoid(x):
    return 1.0 / (1.0 + jnp.exp(-x))


def _rms(x, g):
    return x * lax.rsqrt(jnp.mean(x * x, axis=-1, keepdims=True) + EPS) * g


def _pack_rows(xb):
    half = xb.shape[1] // 2
    u = lax.bitcast_convert_type(xb.astype(F32), jnp.uint32)
    packed = (u[:, :half] >> 16) | (u[:, half:] & jnp.uint32(0xFFFF0000))
    return lax.bitcast_convert_type(packed, jnp.int32)


def _unpack_rows(p):
    u = lax.bitcast_convert_type(p, jnp.uint32)
    return (lax.bitcast_convert_type(u << 16, F32),
            lax.bitcast_convert_type(u & jnp.uint32(0xFFFF0000), F32))


def _pick_tile(n, pref):
    t = pref
    while n % t:
        t //= 2
    return t


def _mod_kernel(c_ref, w_ref, b_ref, o_ref):
    c = c_ref[...]
    o_ref[...] = _dot(c * _sigmoid(c), w_ref[...]) + b_ref[...]


def _modulation(cond, ada_w, ada_b):
    depth, d, n6 = ada_w.shape
    rows = cond.shape[0]
    tn = _pick_tile(n6, 512)
    return pl.pallas_call(
        _mod_kernel,
        grid=(depth, n6 // tn),
        in_specs=[
            pl.BlockSpec((rows, d), lambda l, j: (0, 0)),
            pl.BlockSpec((None, d, tn), lambda l, j: (l, 0, j)),
            pl.BlockSpec((None, 1, tn), lambda l, j: (l, 0, j)),
        ],
        out_specs=pl.BlockSpec((None, rows, tn), lambda l, j: (l, 0, j)),
        out_shape=jax.ShapeDtypeStruct((depth, rows, n6), F32),
        compiler_params=_cparams("arbitrary", "arbitrary"),
        name="modulation",
    )(cond, ada_w, ada_b.reshape(depth, 1, n6))


def _read_tokens(x_refs, g, ends):
    x = x_refs[-1][...]
    for ref, end in zip(reversed(x_refs[:-1]), reversed(ends[:-1])):
        x = jnp.where(g < end, ref[...], x)
    return x


def _token_parts(xs, tm, tile0, n_tiles):
    arrays, specs, ends = [], [], []
    start = 0
    for x in xs:
        size = x.shape[0] // tm
        if start < tile0 + n_tiles and start + size > tile0:
            arrays.append(x)
            ends.append(start + size)
            specs.append(pl.BlockSpec(
                (tm, x.shape[1]), lambda i, s=start, z=size: (jnp.clip(i + tile0 - s, 0, z - 1), 0)))
        start += size
    return arrays, specs, ends


def _qkv_kernel(*refs, d, scale, rope, n_lat_tiles, ends, n_alias):
    n_x = len(ends)
    x_refs = refs[:n_x]
    mod_ref, g_ref, w_ref = refs[n_x:n_x + 3]
    rest = refs[n_x + 3:]
    if rope:
        rope_ref, wvt_ref, rest = rest[0], rest[1], rest[2:]
    rest = rest[n_alias:]
    q_ref, k_ref, v_ref, kf_ref, vf_ref = rest[:5]
    i = pl.program_id(0)
    x = _read_tokens(x_refs, i, ends)
    h = _rms(x, g_ref[...]) * (1.0 + mod_ref[1:2, :]) + mod_ref[0:1, :]
    hb = h.astype(BF16)

    def rot(t):
        if not rope:
            return t
        cols = []
        for j in range(d // LANES):
            tb = t[:, j * LANES:(j + 1) * LANES]
            cols.append(tb * rope_ref[0]
                        + pltpu.roll(tb, LANES - HEAD_DIM // 2, axis=1) * rope_ref[1]
                        + pltpu.roll(tb, HEAD_DIM // 2, axis=1) * rope_ref[2])
        return jnp.concatenate(cols, axis=1)

    q = _dot(hb, w_ref[:, 0:d])
    q_ref[...] = (rot(q) * scale).astype(BF16)
    k = _dot(hb, w_ref[:, d:2 * d])
    k_ref[...] = rot(k).astype(BF16)
    if rope:
        @pl.when(i < n_lat_tiles)
        def _():
            rest[5][...] = _dot_nt(wvt_ref[...], hb).astype(BF16)

    def context_values():
        v = _dot(hb, w_ref[:, 2 * d:3 * d])
        v_ref[...] = v.astype(BF16)
        return v

    if not rope:
        v = context_values()

    @pl.when(i >= n_lat_tiles)
    def _():
        kf_ref[...] = k.reshape(kf_ref.shape)
        vf_ref[...] = (context_values() if rope else v).reshape(vf_ref.shape)


def _qkv(xs, mod, g, w, rope_tab, caches, *, layer, n_lat, batch, seq, lat_seq, n_lat_batch, tm,
         scale):
    d = xs[0].shape[1]
    depth = w.shape[0]
    n = n_lat + batch * seq
    n_lat_tiles = n_lat // tm
    rope = rope_tab is not None
    assert tm % seq == 0
    x_arrays, x_specs, ends = _token_parts(xs, tm, 0, n // tm)

    def mod_idx(i):
        return (layer, jnp.minimum(i * tm // lat_seq, n_lat_batch), 0, 0)

    in_specs = x_specs + [
        pl.BlockSpec((None, None, N_MOD, d), mod_idx),
        pl.BlockSpec((None, 1, d), lambda i: (layer, 0, 0)),
        pl.BlockSpec((None, d, 3 * d), lambda i: (layer, 0, 0)),
    ]
    args = x_arrays + [mod, g, w]
    if rope:
        seq_tiles = lat_seq // tm
        in_specs.append(pl.BlockSpec(
            (3, tm, LANES),
            lambda i: (0, jnp.where(i < n_lat_tiles, i % seq_tiles, seq_tiles), 0)))
        in_specs.append(pl.BlockSpec((d, d), lambda i: (0, 0)))
        args += [rope_tab, w[layer, :, 2 * d:].T]
    aliases = {}
    if caches is not None:
        aliases = {len(args): 3, len(args) + 1: 4}
        in_specs += [pl.BlockSpec(memory_space=pl.ANY)] * 2
        args += list(caches)
    tok = pl.BlockSpec((tm, d), lambda i: (i, 0))
    cache_spec = pl.BlockSpec((tm // seq, None, seq, d),
                              lambda i: (jnp.maximum(i - n_lat_tiles, 0), layer, 0, 0))
    out_specs = [tok, tok, tok, cache_spec, cache_spec]
    out_shape = ([jax.ShapeDtypeStruct((n, d), BF16)] * 3
                 + [jax.ShapeDtypeStruct((batch, depth, seq, d), F32)] * 2)
    if rope:
        out_specs.append(pl.BlockSpec((None, d, tm),
                                      lambda i: (jnp.minimum(i, n_lat_tiles - 1), 0, 0)))
        out_shape.append(jax.ShapeDtypeStruct((n_lat_tiles, d, tm), BF16))
    return pl.pallas_call(
        functools.partial(_qkv_kernel, d=d, scale=scale, rope=rope, n_lat_tiles=n_lat_tiles,
                          ends=tuple(ends), n_alias=len(aliases)),
        grid=(n // tm,),
        in_specs=in_specs,
        out_specs=out_specs,
        out_shape=out_shape,
        input_output_aliases=aliases,
        compiler_params=_cparams("arbitrary"),
        name=f"qkv{layer}",
    )(*args)


def _rope_tables(lat_seq, tm):
    n_freq = HEAD_DIM // 4
    inv = ROPE_BASE ** (-jnp.arange(n_freq, dtype=F32) / n_freq)
    pos = jnp.arange(lat_seq)
    row = (pos // GRID_W).astype(F32)
    col = (pos % GRID_W).astype(F32)
    ang = jnp.concatenate([row[:, None] * inv, col[:, None] * inv], axis=-1)
    reps = LANES // (HEAD_DIM // 2)
    cos = jnp.tile(jnp.cos(ang), (1, reps))
    sin = jnp.tile(jnp.sin(ang), (1, reps))
    first_half = (jnp.arange(LANES) % HEAD_DIM) < HEAD_DIM // 2
    s_next = jnp.where(first_half, -sin, 0.0)
    s_prev = jnp.where(first_half, 0.0, sin)
    ident = jnp.stack([jnp.ones((tm, LANES), F32), jnp.zeros((tm, LANES), F32),
                       jnp.zeros((tm, LANES), F32)])
    return jnp.concatenate([jnp.stack([cos, s_next, s_prev]), ident], axis=1)


def _lane_lo():
    return lax.broadcasted_iota(jnp.int32, (1, LANES), 1) < HEAD_DIM


def _ctx_dense_kernel(q_ref, k_ref, v_ref, o_all_ref, o_ref, *, d):
    del o_all_ref
    lo = _lane_lo()
    cols = []
    for j in range(d // LANES):
        blk = slice(j * LANES, (j + 1) * LANES)
        q2, k2, v2 = q_ref[:, blk], k_ref[:, blk], v_ref[:, blk]
        outs = []
        for qm in (jnp.where(lo, q2, 0), jnp.where(lo, 0, q2)):
            s = _dot_nt(qm, k2)
            p = jnp.exp2(s - jnp.max(s, axis=-1, keepdims=True))
            outs.append(_dot(p.astype(BF16), v2) / jnp.sum(p, axis=-1, keepdims=True))
        cols.append(jnp.where(lo, outs[0], outs[1]).astype(BF16))
    o_ref[...] = jnp.concatenate(cols, axis=1)


def _ctx_dense(q, k, v, o_all, *, n_lat, seq):
    n, d = q.shape
    off = n_lat // seq
    spec = pl.BlockSpec((seq, d), lambda b: (off + b, 0))
    return pl.pallas_call(
        functools.partial(_ctx_dense_kernel, d=d),
        grid=((n - n_lat) // seq,),
        in_specs=[spec, spec, spec, pl.BlockSpec(memory_space=pl.ANY)],
        out_specs=spec,
        out_shape=jax.ShapeDtypeStruct((n, d), BF16),
        input_output_aliases={3: 0},
        compiler_params=_cparams("arbitrary"),
        name="ctx_dense_attn",
    )(q, k, v, o_all)


def _subln(o, g_ref, out_scale):
    return _rms(o, g_ref[...]) * out_scale


def _ctx_diff_kernel(lam_ref, q_ref, k_ref, v_ref, g_ref, o_all_ref, o_ref, *, d, out_scale):
    del o_all_ref
    lo = _lane_lo()
    lam = lam_ref[0]
    cols = []
    for j in range(d // LANES):
        blk = slice(j * LANES, (j + 1) * LANES)
        q2, k2, v2 = q_ref[:, blk], k_ref[:, blk], v_ref[:, blk]
        ps = []
        for qm in (jnp.where(lo, q2, 0), jnp.where(lo, 0, q2)):
            s = _dot_nt(qm, k2)
            p = jnp.exp2(s - jnp.max(s, axis=-1, keepdims=True))
            ps.append(p / jnp.sum(p, axis=-1, keepdims=True))
        o = _dot((ps[0] - lam * ps[1]).astype(BF16), v2)
        cols.append(_subln(o, g_ref, out_scale).astype(BF16))
    o_ref[...] = jnp.concatenate(cols, axis=1)


def _ctx_diff(lam, q, k, v, subg, o_all, *, n_lat, seq, out_scale):
    n, d = q.shape
    off = n_lat // seq
    spec = pl.BlockSpec((seq, d), lambda b: (off + b, 0))
    return pl.pallas_call(
        functools.partial(_ctx_diff_kernel, d=d, out_scale=out_scale),
        grid=((n - n_lat) // seq,),
        in_specs=[pl.BlockSpec(memory_space=pltpu.SMEM), spec, spec, spec,
                  pl.BlockSpec((1, LANES), lambda b: (0, 0)), pl.BlockSpec(memory_space=pl.ANY)],
        out_specs=spec,
        out_shape=jax.ShapeDtypeStruct((n, d), BF16),
        input_output_aliases={5: 0},
        compiler_params=_cparams("arbitrary"),
        name="ctx_diff_attn",
    )(lam, q, k, v, subg, o_all)


def _na_kernel(q_ref, k_ref, v_ref, ck_ref, cv_ref, bias_ref, o_ref, *, d, rows):
    lo = _lane_lo()
    r = pl.program_id(1)
    win = NA_ROWS * GRID_W
    start = pl.multiple_of(jnp.clip(r - NA_ROWS // 2, 0, rows - NA_ROWS) * GRID_W, GRID_W)
    n_heads = 2 * (d // LANES)

    def blk(h):
        return slice(h // 2 * LANES, (h // 2 + 1) * LANES)

    def scores(heads):
        out = []
        for h in heads:
            q2 = q_ref[:, blk(h)]
            qm = jnp.where(lo, q2, 0) if h % 2 == 0 else jnp.where(lo, 0, q2)
            out.append((_dot_nt(qm, k_ref[pl.ds(start, win), blk(h)]) + bias_ref[h],
                        _dot_nt(qm, ck_ref[:, blk(h)].astype(BF16))))
        return out

    def attend(heads, ss):
        probs = []
        for s_loc, s_ctx in ss:
            m = jnp.maximum(jnp.max(s_loc, axis=-1, keepdims=True),
                            jnp.max(s_ctx, axis=-1, keepdims=True))
            p_loc = jnp.exp2(s_loc - m)
            p_ctx = jnp.exp2(s_ctx - m)
            l = jnp.sum(p_loc, axis=-1, keepdims=True) + jnp.sum(p_ctx, axis=-1, keepdims=True)
            probs.append((p_loc.astype(BF16), p_ctx.astype(BF16), l))
        return [(_dot(p_loc, v_ref[pl.ds(start, win), blk(h)])
                 + _dot(p_ctx, cv_ref[:, blk(h)].astype(BF16))) / l
                for h, (p_loc, p_ctx, l) in zip(heads, probs)]

    groups = [list(range(g, min(g + NA_HEAD_GROUP, n_heads))) for g in range(0, n_heads, NA_HEAD_GROUP)]
    outs = []
    ss = scores(groups[0])
    for gi, heads in enumerate(groups):
        ss_next = scores(groups[gi + 1]) if gi + 1 < len(groups) else None
        outs += attend(heads, ss)
        ss = ss_next
    o_ref[...] = jnp.concatenate(
        [jnp.where(lo, outs[h], outs[h + 1]).astype(BF16) for h in range(0, n_heads, 2)], axis=1)


def _na_bias_table(rpb, rows):
    h = rpb.shape[0]
    delta = jnp.arange(NA_ROWS)
    kr = jnp.arange(NA_ROWS)
    c = jnp.arange(GRID_W)
    kc = jnp.arange(GRID_W)
    off_r = kr[None, :] - delta[:, None] + (NA_ROWS - 1)
    c0 = jnp.clip(c - NA_COLS // 2, 0, GRID_W - NA_COLS)
    off_c = kc[None, :] - c[:, None] + (NA_COLS - 1)
    inside = (kc[None, :] >= c0[:, None]) & (kc[None, :] < c0[:, None] + NA_COLS)
    oh_r = jax.nn.one_hot(off_r, 2 * NA_ROWS - 1, dtype=F32)
    oh_c = jax.nn.one_hot(off_c, 2 * NA_COLS - 1, dtype=F32)
    t = jnp.einsum('hrs,dkr,cqs->dhckq', rpb.astype(F32), oh_r, oh_c,
                   precision=lax.Precision.HIGHEST)
    t = jnp.where(inside[None, None, :, None, :], t * LOG2E, MASKED)
    return t.reshape(NA_ROWS, h, GRID_W, NA_ROWS * GRID_W)


def _na_attention(q, k, v, ck, cv, bias, *, layer, n_lat_batch, lat_seq):
    n, d = q.shape
    rows = lat_seq // GRID_W
    past = ck.shape[2]
    n_heads = bias.shape[1]

    def delta_idx(b, r):
        return (r - jnp.clip(r - NA_ROWS // 2, 0, rows - NA_ROWS), 0, 0, 0)

    kv_spec = pl.BlockSpec((lat_seq, d), lambda b, r: (b, 0))
    c_spec = pl.BlockSpec((None, None, past, d), lambda b, r: (b, layer, 0, 0))
    return pl.pallas_call(
        functools.partial(_na_kernel, d=d, rows=rows),
        grid=(n_lat_batch, rows),
        in_specs=[pl.BlockSpec((GRID_W, d), lambda b, r: (b * rows + r, 0)), kv_spec, kv_spec,
                  c_spec, c_spec,
                  pl.BlockSpec((None, n_heads, GRID_W, NA_ROWS * GRID_W), delta_idx)],
        out_specs=pl.BlockSpec((GRID_W, d), lambda b, r: (b * rows + r, 0)),
        out_shape=jax.ShapeDtypeStruct((n, d), BF16),
        compiler_params=_cparams("arbitrary", "arbitrary"),
        name="na_attn",
    )(q, k, v, ck, cv, bias)


def _lat_diff_kernel(lam_ref, q_ref, k_ref, vt_ref, ck_ref, cv_ref, g_ref, o_ref, *, out_scale):
    lo = _lane_lo()
    lam = lam_ref[0]
    q2 = q_ref[...]
    tq = q2.shape[0]
    tg = tq // QUERY_GROUPS
    qs = []
    for g in range(QUERY_GROUPS):
        qg = q2[g * tg:(g + 1) * tg, :]
        qs += [jnp.where(lo, qg, 0), jnp.where(lo, 0, qg)]

    def softmax_step(state, s):
        m, l, acc = state
        m_new = jnp.maximum(m, jnp.max(s, axis=0, keepdims=True))
        alpha = jnp.exp2(m - m_new)
        p = jnp.exp2(s - m_new)
        return m_new, alpha * l + jnp.sum(p, axis=0, keepdims=True), alpha * acc, p.astype(BF16)

    def scores(kt):
        return [_dot_nt(kt, qm) for qm in qs]

    def absorb(states, ss, vt):
        out = []
        for g in range(0, len(qs), 2):
            stepped = [softmax_step(states[g + j], ss[g + j]) for j in range(2)]
            out += [(m, l, acc + _dot(vt, p)) for m, l, acc, p in stepped]
        return out

    tile = vt_ref.shape[2]
    n_tiles = vt_ref.shape[0]

    def keys(t):
        return k_ref[t * tile:(t + 1) * tile, :] if t < n_tiles else ck_ref[...].astype(BF16)

    def values_t(t):
        return vt_ref[t] if t < n_tiles else cv_ref[...].T.astype(BF16)

    init = (jnp.full((1, tg), -jnp.inf, F32), jnp.zeros((1, tg), F32), jnp.zeros((LANES, tg), F32))
    states = [init] * len(qs)
    ss = scores(keys(0))
    for t in range(n_tiles + 1):
        ss_next = scores(keys(t + 1)) if t < n_tiles else None
        states = absorb(states, ss, values_t(t))
        ss = ss_next
    o = jnp.concatenate([states[g][2] / states[g][1] - lam * (states[g + 1][2] / states[g + 1][1])
                         for g in range(0, len(qs), 2)], axis=1).T
    o_ref[...] = _subln(o, g_ref, out_scale).astype(BF16)


def _lat_diff(lam, q, k, vt, ck, cv, subg, *, layer, n_lat_batch, lat_seq, out_scale):
    n, d = q.shape
    past = ck.shape[2]
    tk = vt.shape[2]
    tq = _pick_tile(lat_seq, QUERY_GROUPS * 256)
    qt = lat_seq // tq
    kt = lat_seq // tk
    c_spec = pl.BlockSpec((None, None, past, LANES), lambda b, h, i: (b, layer, 0, h))
    q_spec = pl.BlockSpec((tq, LANES), lambda b, h, i: (b * qt + i, h))
    return pl.pallas_call(
        functools.partial(_lat_diff_kernel, out_scale=out_scale),
        grid=(n_lat_batch, d // LANES, qt),
        in_specs=[pl.BlockSpec(memory_space=pltpu.SMEM), q_spec,
                  pl.BlockSpec((lat_seq, LANES), lambda b, h, i: (b, h)),
                  pl.BlockSpec((kt, LANES, tk), lambda b, h, i: (b, h, 0)),
                  c_spec, c_spec, pl.BlockSpec((1, LANES), lambda b, h, i: (0, 0))],
        out_specs=q_spec,
        out_shape=jax.ShapeDtypeStruct((n, d), BF16),
        compiler_params=_cparams("arbitrary", "arbitrary", "arbitrary"),
        name="lat_diff_attn",
    )(lam, q, k, vt, ck, cv, subg)


def _post_attn_kernel(o_ref, *refs, ends, tile0):
    n_x = len(ends)
    x_refs = refs[:n_x]
    (mod_ref, wo_ref, g_ref, rwh_ref, rwl_ref, rb_ref, tri_ref,
     xmid_ref, h_ref, sel_ref, gate_ref, rank_ref, cnt_ref, carry_ref) = refs[n_x:]
    i = pl.program_id(0)

    @pl.when(i == 0)
    def _():
        carry_ref[...] = jnp.zeros_like(carry_ref)

    x1 = _read_tokens(x_refs, i + tile0, ends) + mod_ref[2:3, :] * _dot(o_ref[...], wo_ref[...])
    xmid_ref[...] = x1
    h = _rms(x1, g_ref[...]) * (1.0 + mod_ref[4:5, :]) + mod_ref[3:4, :]
    hb = h.astype(BF16)
    h_ref[...] = _pack_rows(hb)
    h_lo = (h - hb.astype(F32)).astype(BF16)
    logits = _dot_nt(rwh_ref[...], hb) + (_dot_nt(rwl_ref[...], hb) + _dot_nt(rwh_ref[...], h_lo))
    scores = _sigmoid(logits)
    n_exp, tm = scores.shape
    biased = scores + rb_ref[...]
    row = lax.broadcasted_iota(jnp.int32, (n_exp, tm), 0).astype(F32)
    total = jnp.zeros((n_exp, tm), F32)
    sels, gates = [], []
    for _ in range(TOP_K):
        m = jnp.max(biased, axis=0, keepdims=True)
        idx = jnp.min(jnp.where(biased == m, row, float(n_exp)), axis=0, keepdims=True)
        hit = row == idx
        gates.append(jnp.sum(jnp.where(hit, scores, 0.0), axis=0, keepdims=True))
        biased = jnp.where(hit, -jnp.inf, biased)
        total = total + jnp.where(hit, 1.0, 0.0)
        sels.append(idx)
    denom = gates[0]
    for g in gates[1:]:
        denom = denom + g
    before = _dot(total.astype(BF16), tri_ref[...]) + carry_ref[:, 0:1]
    for k in range(TOP_K):
        sel_ref[k:k + 1, :] = sels[k].astype(jnp.int32)
        gate_ref[k:k + 1, :] = gates[k] / denom * ROUTED_SCALE
        rank_ref[k:k + 1, :] = jnp.sum(jnp.where(row == sels[k], before, 0.0), axis=0,
                                       keepdims=True).astype(jnp.int32)
    carry_ref[...] = carry_ref[...] + jnp.sum(total, axis=1, keepdims=True)
    cnt_ref[...] = carry_ref[...].astype(jnp.int32)


def _post_attn(o, xs, mod, wo, g, rwh, rwl, rb, *, layer, tile0, n_tiles, lat_seq, n_lat_batch, tm):
    d = o.shape[1]
    n = n_tiles * tm
    n_exp = rwh.shape[0]
    tri = (jnp.arange(tm)[:, None] < jnp.arange(tm)[None, :]).astype(BF16)
    x_arrays, x_specs, ends = _token_parts(xs, tm, tile0, n_tiles)

    def mod_idx(i):
        return (layer, jnp.minimum((i + tile0) * tm // lat_seq, n_lat_batch), 0, 0)

    tok = pl.BlockSpec((tm, d), lambda i: (i, 0))
    tok_p = pl.BlockSpec((tm, d // 2), lambda i: (i, 0))
    kt = pl.BlockSpec((TOP_K, tm), lambda i: (0, i))
    const2 = lambda i: (0, 0)
    return pl.pallas_call(
        functools.partial(_post_attn_kernel, ends=tuple(ends), tile0=tile0),
        grid=(n_tiles,),
        in_specs=[pl.BlockSpec((tm, d), lambda i: (i + tile0, 0))] + x_specs + [
                  pl.BlockSpec((None, None, N_MOD, d), mod_idx),
                  pl.BlockSpec((d, d), const2), pl.BlockSpec((1, d), const2),
                  pl.BlockSpec((n_exp, d), const2), pl.BlockSpec((n_exp, d), const2),
                  pl.BlockSpec((n_exp, 1), const2), pl.BlockSpec((tm, tm), const2)],
        out_specs=[tok, tok_p, kt, kt, kt, pl.BlockSpec((n_exp, LANES), const2)],
        out_shape=[jax.ShapeDtypeStruct((n, d), F32), jax.ShapeDtypeStruct((n, d // 2), jnp.int32),
                   jax.ShapeDtypeStruct((TOP_K, n), jnp.int32),
                   jax.ShapeDtypeStruct((TOP_K, n), F32),
                   jax.ShapeDtypeStruct((TOP_K, n), jnp.int32),
                   jax.ShapeDtypeStruct((n_exp, LANES), jnp.int32)],
        scratch_shapes=[pltpu.VMEM((n_exp, LANES), F32)],
        compiler_params=_cparams("arbitrary"),
        name=f"post_attn{layer}",
    )(o, *x_arrays, mod, wo, g, rwh, rwl, rb, tri)


def _plan_kernel(cnt_ref, sel_ref, rank_ref, dest_ref, te_ref, tv_ref, nu_ref, *, tmb):
    n_exp = cnt_ref.shape[0]
    sel = sel_ref[...]
    tile_start = lax.broadcasted_iota(jnp.int32, te_ref.shape, 1) * tmb
    dest = rank_ref[...]
    te = jnp.zeros(te_ref.shape, jnp.int32)
    tv = jnp.zeros(te_ref.shape, jnp.int32)
    start = jnp.zeros((1, 1), jnp.int32)
    for e in range(n_exp):
        dest = dest + jnp.where(sel == e, start, 0)
        cnt = cnt_ref[e:e + 1, 0:1]
        stop = start + ((cnt + (tmb - 1)) & -tmb)
        mine = (tile_start >= start) & (tile_start < stop)
        tv = tv + jnp.where(mine, jnp.clip(start + cnt - tile_start, 0, tmb), 0)
        start = stop
        te = te + jnp.where(start <= tile_start, 1, 0)
    te_ref[...] = jnp.minimum(te, n_exp - 1)
    tv_ref[...] = tv
    nu_ref[...] = jnp.broadcast_to(start, nu_ref.shape)
    for c in range(dest_ref.shape[0]):
        dest_ref[c] = dest[:, c * SC_CHUNK:(c + 1) * SC_CHUNK]


def _plan(counts, sel, rank, *, tmb, n_tiles):
    k_top, n = sel.shape
    assert tmb & (tmb - 1) == 0
    tn = _pick_tile(n, 16 * SC_CHUNK)
    te_w = -(-n_tiles // LANES) * LANES
    tok = pl.BlockSpec((k_top, tn), lambda i: (0, i))
    const2 = lambda i: (0, 0)
    dest, te, tv, nu = pl.pallas_call(
        functools.partial(_plan_kernel, tmb=tmb),
        grid=(n // tn,),
        in_specs=[pl.BlockSpec(counts.shape, const2), tok, tok],
        out_specs=[pl.BlockSpec((tn // SC_CHUNK, k_top, SC_CHUNK), lambda i: (i, 0, 0)),
                   pl.BlockSpec((1, te_w), const2), pl.BlockSpec((1, te_w), const2),
                   pl.BlockSpec((1, LANES), const2)],
        out_shape=[jax.ShapeDtypeStruct((n // SC_CHUNK, k_top, SC_CHUNK), jnp.int32),
                   jax.ShapeDtypeStruct((1, te_w), jnp.int32),
                   jax.ShapeDtypeStruct((1, te_w), jnp.int32),
                   jax.ShapeDtypeStruct((1, LANES), jnp.int32)],
        compiler_params=_cparams("arbitrary"),
        name="plan",
    )(counts, sel, rank)
    return dest, te[0, :n_tiles], tv[0, :n_tiles], nu[0, :1] // tmb


def _expert_kernel(te_ref, tv_ref, nu_ref, x_ref, wg_ref, wu_ref, wd_ref, y_ref, wg_b, wu_b, wd_b):
    i = pl.program_id(0)
    half = x_ref.shape[0] // 2

    def rows(r):
        return jnp.concatenate(_unpack_rows(x_ref[r * half:(r + 1) * half, :]), axis=1).astype(BF16)

    def swiglu(xs):
        gu = [(_dot(x, wg_b[...]), _dot(x, wu_b[...])) for x in xs]
        acts = [(g * _sigmoid(g) * u).astype(BF16) for g, u in gu]
        return [_pack_rows(_dot(a, wd_b[...]).astype(BF16)) for a in acts]

    @pl.when(i < nu_ref[0])
    def _():
        @pl.when((i == 0) | (te_ref[i] != te_ref[jnp.maximum(i - 1, 0)]))
        def _():
            wg_b[...] = wg_ref[...].astype(BF16)
            wu_b[...] = wu_ref[...].astype(BF16)
            wd_b[...] = wd_ref[...].astype(BF16)

        @pl.when(tv_ref[i] > half)
        def _():
            ya, yb = swiglu([rows(0), rows(1)])
            y_ref[0:half, :] = ya
            y_ref[half:2 * half, :] = yb

        @pl.when(tv_ref[i] <= half)
        def _():
            y_ref[0:half, :] = swiglu([rows(0)])[0]


def _experts(tile_expert, tile_rows, n_used, xb, wg, wu, wd, *, layer, tmb):
    cap, dp = xb.shape
    f, d = wd.shape[2:]

    def row_idx(i, te, tv, nu):
        return (jnp.minimum(i, nu[0] - 1), 0)

    def w_idx(i, te, tv, nu):
        return (layer, te[i], 0, 0)

    return pl.pallas_call(
        _expert_kernel,
        grid_spec=pltpu.PrefetchScalarGridSpec(
            num_scalar_prefetch=3,
            grid=(cap // tmb,),
            in_specs=[pl.BlockSpec((tmb, dp), row_idx),
                      pl.BlockSpec((None, None, d, f), w_idx),
                      pl.BlockSpec((None, None, d, f), w_idx),
                      pl.BlockSpec((None, None, f, d), w_idx)],
            out_specs=pl.BlockSpec((tmb, dp), row_idx),
            scratch_shapes=[pltpu.VMEM((d, f), BF16), pltpu.VMEM((d, f), BF16),
                            pltpu.VMEM((f, d), BF16)]),
        out_shape=jax.ShapeDtypeStruct((cap, dp), jnp.int32),
        compiler_params=_cparams("arbitrary"),
        name="experts",
    )(tile_expert, tile_rows, n_used, xb, wg, wu, wd)


def _combine_kernel(x_ref, h_ref, routed_ref, mod_ref, wgu_ref, wd_ref, fg_ref, *refs, f, final,
                    n_alias, tile0, n_lat_tiles):
    o_refs = refs[n_alias:]
    h = jnp.concatenate(_unpack_rows(h_ref[...]), axis=1).astype(BF16)
    gu = _dot(h, wgu_ref[...])
    gate, up = gu[:, :f], gu[:, f:]
    shared = _dot((gate * _sigmoid(gate) * up).astype(BF16), wd_ref[...])
    x2 = x_ref[...] + mod_ref[5:6, :] * (shared + routed_ref[...])
    if not final:
        o_refs[0][...] = x2
    elif len(o_refs) == 1:
        o_refs[0][...] = _rms(x2, fg_ref[...])
    else:
        y = _rms(x2, fg_ref[...])
        tile = pl.program_id(0) + tile0

        @pl.when(tile < n_lat_tiles)
        def _():
            o_refs[0][...] = y

        @pl.when(tile >= n_lat_tiles)
        def _():
            o_refs[1][...] = y


def _combine(x, h, routed, mod, wgu, wd, final_g, y_lat, *, layer, tile0, n_lat, n_ctx, lat_seq,
             n_lat_batch, tm, final):
    n, d = x.shape
    f = wd.shape[0]
    n_tiles = n // tm
    n_lat_tiles = n_lat // tm

    def mod_idx(i):
        return (layer, jnp.minimum((i + tile0) * tm // lat_seq, n_lat_batch), 0, 0)

    tok = pl.BlockSpec((tm, d), lambda i: (i, 0))
    tok_p = pl.BlockSpec((tm, d // 2), lambda i: (i, 0))
    const2 = lambda i: (0, 0)
    in_specs = [tok, tok_p, tok,
                pl.BlockSpec((None, None, N_MOD, d), mod_idx),
                pl.BlockSpec((d, 2 * f), const2), pl.BlockSpec((f, d), const2),
                pl.BlockSpec((1, d), const2)]
    args = [x, h, routed, mod, wgu, wd, final_g]
    aliases = {}
    has_lat = has_ctx = False
    if not final:
        out_specs, out_shape = [tok], [jax.ShapeDtypeStruct((n, d), F32)]
    else:
        has_lat = tile0 < n_lat_tiles
        has_ctx = tile0 + n_tiles > n_lat_tiles
        out_specs, out_shape = [], []
        if has_lat:
            out_specs.append(pl.BlockSpec(
                (tm, d), lambda i: (jnp.minimum(i + tile0, n_lat_tiles - 1), 0)))
            out_shape.append(jax.ShapeDtypeStruct((n_lat, d), F32))
            if y_lat is not None:
                aliases = {len(args): 0}
                in_specs.append(pl.BlockSpec(memory_space=pl.ANY))
                args.append(y_lat)
        if has_ctx:
            out_specs.append(pl.BlockSpec(
                (tm, d), lambda i: (jnp.maximum(i + tile0 - n_lat_tiles, 0), 0)))
            out_shape.append(jax.ShapeDtypeStruct((n_ctx, d), F32))
    outs = pl.pallas_call(
        functools.partial(_combine_kernel, f=f, final=final, n_alias=len(aliases), tile0=tile0,
                          n_lat_tiles=n_lat_tiles),
        grid=(n_tiles,),
        in_specs=in_specs,
        out_specs=out_specs,
        out_shape=out_shape,
        input_output_aliases=aliases,
        compiler_params=_cparams("arbitrary"),
        name=f"combine{layer}",
    )(*args)
    if not final:
        return outs[0]
    return (outs[0] if has_lat else y_lat), (outs[-1] if has_ctx else None)


SC_CHUNK = 128


def _sc_workers():
    info = plsc.get_sparse_core_info()
    return info.num_cores, info.num_subcores


def _sc_dispatch(rows, dest, cap):
    n, w = rows.shape
    nc, ns = _sc_workers()
    per_w = n // (nc * ns)
    assert per_w * nc * ns == n and per_w % SC_CHUNK == 0
    mesh = plsc.VectorSubcoreMesh(core_axis_name="c", subcore_axis_name="s")

    @functools.partial(
        pl.kernel, mesh=mesh, out_type=jax.ShapeDtypeStruct((cap, w), rows.dtype),
        scratch_types=[pltpu.VMEM((TOP_K, SC_CHUNK), jnp.int32),
                       pltpu.VMEM((SC_CHUNK, w), rows.dtype)],
        name="sc_dispatch")
    def run(rows_hbm, dest_hbm, out_hbm, idx_v, rows_v):
        wid = lax.axis_index("s") * nc + lax.axis_index("c")

        @pl.loop(0, per_w // SC_CHUNK)
        def _(ci):
            base = pl.multiple_of(wid * per_w + ci * SC_CHUNK, SC_CHUNK)
            pltpu.sync_copy(dest_hbm.at[wid * (per_w // SC_CHUNK) + ci], idx_v)
            pltpu.sync_copy(rows_hbm.at[pl.ds(base, SC_CHUNK)], rows_v)
            for k in range(TOP_K):
                pltpu.sync_copy(rows_v, out_hbm.at[idx_v.at[k]])

    return run(rows, dest)


SC_TOK = 8
SC_LANES = 16


def _sc_combine(table, dest, gates):
    n_chunks, k_top, _ = dest.shape
    n = n_chunks * SC_CHUNK
    w = table.shape[1]
    nc, ns = _sc_workers()
    per_w = n // (nc * ns)
    assert per_w * nc * ns == n and per_w % SC_CHUNK == 0 and w % (16 * SC_LANES) == 0
    gb = jnp.broadcast_to(
        gates.reshape(k_top, n_chunks, SC_CHUNK).transpose(1, 0, 2)[..., None],
        (n_chunks, k_top, SC_CHUNK, SC_LANES)).reshape(n_chunks, k_top, SC_CHUNK // SC_TOK,
                                                        SC_TOK * SC_LANES)
    mesh = plsc.VectorSubcoreMesh(core_axis_name="c", subcore_axis_name="s")

    @functools.partial(
        pl.kernel, mesh=mesh, out_type=jax.ShapeDtypeStruct((n, 2 * w), F32),
        scratch_types=[pltpu.VMEM((k_top, SC_CHUNK), jnp.int32),
                       pltpu.VMEM((k_top, SC_CHUNK // SC_TOK, SC_TOK * SC_LANES), F32),
                       pltpu.VMEM((k_top, SC_TOK, w), table.dtype),
                       pltpu.VMEM((SC_TOK, 2 * w), F32)],
        compiler_params=pltpu.CompilerParams(needs_layout_passes=False),
        name="sc_combine")
    def run(table_hbm, dest_hbm, gb_hbm, out_hbm, idx_v, g_v, rows_v, out_v):
        wid = lax.axis_index("s") * nc + lax.axis_index("c")

        @pl.loop(0, per_w // SC_CHUNK)
        def _(ci):
            chunk = wid * (per_w // SC_CHUNK) + ci
            pltpu.sync_copy(dest_hbm.at[chunk], idx_v)
            pltpu.sync_copy(gb_hbm.at[chunk], g_v)

            @pl.loop(0, SC_CHUNK // SC_TOK)
            def _(bi):
                off = pl.multiple_of(bi * SC_TOK, SC_TOK)
                for k in range(k_top):
                    pltpu.sync_copy(table_hbm.at[idx_v.at[k, pl.ds(off, SC_TOK)]], rows_v.at[k])

                @pl.loop(0, SC_TOK)
                def _(j):
                    for base in range(0, w, 16 * SC_LANES):
                        acc = None
                        for k in range(k_top):
                            g = g_v[k, bi, pl.ds(j * SC_LANES, SC_LANES)]
                            terms = []
                            for v in range(16):
                                word = rows_v[k, j, pl.ds(base + v * SC_LANES, SC_LANES)]
                                lo = lax.bitcast_convert_type(lax.shift_left(word, 16), F32)
                                hi = lax.bitcast_convert_type(word & jnp.int32(-65536), F32)
                                terms.append((g * lo, g * hi))
                            acc = terms if acc is None else [
                                (a + x, b + y) for (a, b), (x, y) in zip(acc, terms)]
                        for v, (a, b) in enumerate(acc):
                            out_v[j, pl.ds(base + v * SC_LANES, SC_LANES)] = a
                            out_v[j, pl.ds(w + base + v * SC_LANES, SC_LANES)] = b

                row0 = pl.multiple_of(chunk * SC_CHUNK + off, SC_TOK)
                pltpu.sync_copy(out_v, out_hbm.at[pl.ds(row0, SC_TOK)])

    return run(table, dest, gb)


def _split_bf16(w):
    hi = w.astype(BF16)
    return hi, (w - hi.astype(F32)).astype(BF16)


def kernel(x_prompt, x_sample, cache_k, cache_v, c, c_ctx, ada_w, ada_b, norm1_g, norm2_g, w_qkv, w_o, na_rpb, diff_lambda, diff_subln_g, router_w, router_b, exp_w_gate, exp_w_up, exp_w_down, shared_w_gate, shared_w_up, shared_w_down, final_g):
    batch, seq, d = x_prompt.shape
    n_lat_batch, lat_seq, _ = x_sample.shape
    depth = w_qkv.shape[0]
    n_exp = router_w.shape[-1]
    n_lat = n_lat_batch * lat_seq
    n_ctx = batch * seq
    n = n_lat + n_ctx
    assert d % LANES == 0 and na_rpb.shape[1] * HEAD_DIM == d
    assert diff_lambda.shape[-1] == HEAD_DIM and lat_seq % GRID_W == 0
    assert lat_seq // GRID_W >= NA_ROWS and n_lat % seq == 0
    tm = _pick_tile(math.gcd(lat_seq, n_ctx), 512)
    tmb = 1024
    n_all = n // tm
    ranges = [(r * n_all // MOE_RANGES, (r + 1) * n_all // MOE_RANGES - r * n_all // MOE_RANGES)
              for r in range(MOE_RANGES)]
    scale = HEAD_DIM ** -0.5 * LOG2E

    xs = [x_sample.reshape(n_lat, d), x_prompt.reshape(n_ctx, d)]
    mod_rows = -(-(n_lat_batch + 1) // 8) * 8
    cond = jnp.zeros((mod_rows, d), F32).at[:n_lat_batch].set(c).at[n_lat_batch].set(c_ctx)
    mod = _modulation(cond, ada_w, ada_b).reshape(depth, mod_rows, N_MOD, d)

    w_qkv_b = w_qkv.astype(BF16)
    w_o_b = w_o.astype(BF16)
    rope_tab = _rope_tables(lat_seq, tm)
    caches = None
    for i in range(depth):
        is_diff = i % 2 == 1
        j = i // 2
        q, k, v, *caches = _qkv(xs, mod, norm1_g.reshape(depth, 1, d), w_qkv_b,
                                rope_tab if is_diff else None, caches, layer=i, n_lat=n_lat,
                                batch=batch, seq=seq, lat_seq=lat_seq, n_lat_batch=n_lat_batch,
                                tm=tm, scale=scale)
        if is_diff:
            vt = caches.pop()
        if not is_diff:
            o = _na_attention(q, k, v, cache_k, cache_v,
                              _na_bias_table(na_rpb[j], lat_seq // GRID_W), layer=i,
                              n_lat_batch=n_lat_batch, lat_seq=lat_seq)
            o = _ctx_dense(q, k, v, o, n_lat=n_lat, seq=seq)
        else:
            lam_init = 0.8 - 0.6 * math.exp(-0.3 * i)
            lp = diff_lambda[j].astype(F32)
            lam = (jnp.exp(jnp.sum(lp[0] * lp[1])) - jnp.exp(jnp.sum(lp[2] * lp[3]))
                   + lam_init).reshape(1)
            subg = diff_subln_g[j].reshape(1, LANES).astype(F32)
            o = _lat_diff(lam, q, k, vt, cache_k, cache_v, subg, layer=i,
                          n_lat_batch=n_lat_batch, lat_seq=lat_seq, out_scale=1.0 - lam_init)
            o = _ctx_diff(lam, q, k, v, subg, o, n_lat=n_lat, seq=seq, out_scale=1.0 - lam_init)

        rwh, rwl = _split_bf16(router_w[i].T)
        swgu = jnp.concatenate([shared_w_gate[i], shared_w_up[i]], axis=-1).astype(BF16)
        final = i == depth - 1
        routed = []
        for tile0, nt in ranges:
            xmid, h2, sel, gates, rank, counts = _post_attn(
                o, xs, mod, w_o_b[i], norm2_g[i].reshape(1, d), rwh, rwl,
                router_b[i].reshape(n_exp, 1).astype(F32), layer=i, tile0=tile0, n_tiles=nt,
                lat_seq=lat_seq, n_lat_batch=n_lat_batch, tm=tm)
            n_tiles = -(-(nt * tm * TOP_K) // tmb) + n_exp
            dest, tile_expert, tile_rows, n_used = _plan(counts, sel, rank, tmb=tmb, n_tiles=n_tiles)
            xb = _sc_dispatch(h2, dest, n_tiles * tmb)
            routed.append((xmid, h2, gates, dest, tile_expert, tile_rows, n_used, xb))
        ybs = [_experts(te, tr, nu, xb, exp_w_gate, exp_w_up, exp_w_down, layer=i, tmb=tmb)
               for (_, _, _, _, te, tr, nu, xb) in routed]
        sums = [_sc_combine(yb, r[3], r[2]) for yb, r in zip(ybs, routed)]
        xs, y_lat, y_ctx = [], None, None
        for (tile0, nt), (xmid, h2, *_), rsum in zip(ranges, routed, sums):
            out = _combine(xmid, h2, rsum, mod, swgu, shared_w_down[i].astype(BF16),
                           final_g.reshape(1, d), y_lat, layer=i, tile0=tile0, n_lat=n_lat,
                           n_ctx=n_ctx, lat_seq=lat_seq, n_lat_batch=n_lat_batch, tm=tm, final=final)
            if final:
                y_lat, y_ctx = out[0], (out[1] if out[1] is not None else y_ctx)
            else:
                xs.append(out)

    y_sample = y_lat.reshape(n_lat_batch, lat_seq, d)
    y_prompt = y_ctx.reshape(batch, seq, d)
    return (y_prompt, y_sample, caches[0], caches[1])
```

```python
import functools
import math

import jax
import jax.numpy as jnp
from jax import lax
from jax.experimental import pallas as pl
from jax.experimental.pallas import tpu as pltpu
from jax.experimental.pallas import tpu_sc as plsc

GRID_W = 64
NA_ROWS = 8
NA_COLS = 16
TOP_K = 8
ROUTED_SCALE = 2.5
ROPE_BASE = 10000.0
EPS = 1e-6
N_MOD = 6
HEAD_DIM = 64
LANES = 128
LOG2E = math.log2(math.e)
MASKED = -1e30
NA_HEAD_GROUP = 16
MOE_RANGES = 2
QUERY_GROUPS = 4
VMEM_LIMIT = 56 * 1024 * 1024

F32 = jnp.float32
BF16 = jnp.bfloat16


def _cparams(*sem):
    return pltpu.CompilerParams(dimension_semantics=sem, vmem_limit_bytes=VMEM_LIMIT)


def _dot(a, b):
    return jnp.dot(a, b, preferred_element_type=F32)


def _dot_nt(a, b):
    return lax.dot_general(a, b, (((1,), (1,)), ((), ())), preferred_element_type=F32)


def _sigmoid(x):
    return 1.0 / (1.0 + jnp.exp(-x))


def _rms(x, g):
    return x * lax.rsqrt(jnp.mean(x * x, axis=-1, keepdims=True) + EPS) * g


def _pack_rows(xb):
    half = xb.shape[1] // 2
    u = lax.bitcast_convert_type(xb.astype(F32), jnp.uint32)
    packed = (u[:, :half] >> 16) | (u[:, half:] & jnp.uint32(0xFFFF0000))
    return lax.bitcast_convert_type(packed, jnp.int32)


def _unpack_rows(p):
    u = lax.bitcast_convert_type(p, jnp.uint32)
    return (lax.bitcast_convert_type(u << 16, F32),
            lax.bitcast_convert_type(u & jnp.uint32(0xFFFF0000), F32))


def _pick_tile(n, pref):
    t = pref
    while n % t:
        t //= 2
    return t


def _mod_kernel(c_ref, w_ref, b_ref, o_ref):
    c = c_ref[...]
    o_ref[...] = _dot(c * _sigmoid(c), w_ref[...]) + b_ref[...]


def _modulation(cond, ada_w, ada_b):
    depth, d, n6 = ada_w.shape
    rows = cond.shape[0]
    tn = _pick_tile(n6, 512)
    return pl.pallas_call(
        _mod_kernel,
        grid=(depth, n6 // tn),
        in_specs=[
            pl.BlockSpec((rows, d), lambda l, j: (0, 0)),
            pl.BlockSpec((None, d, tn), lambda l, j: (l, 0, j)),
            pl.BlockSpec((None, 1, tn), lambda l, j: (l, 0, j)),
        ],
        out_specs=pl.BlockSpec((None, rows, tn), lambda l, j: (l, 0, j)),
        out_shape=jax.ShapeDtypeStruct((depth, rows, n6), F32),
        compiler_params=_cparams("arbitrary", "arbitrary"),
        name="modulation",
    )(cond, ada_w, ada_b.reshape(depth, 1, n6))


def _read_tokens(x_refs, g, ends):
    x = x_refs[-1][...]
    for ref, end in zip(reversed(x_refs[:-1]), reversed(ends[:-1])):
        x = jnp.where(g < end, ref[...], x)
    return x


def _token_parts(xs, tm, tile0, n_tiles):
    arrays, specs, ends = [], [], []
    start = 0
    for x in xs:
        size = x.shape[0] // tm
        if start < tile0 + n_tiles and start + size > tile0:
            arrays.append(x)
            ends.append(start + size)
            specs.append(pl.BlockSpec(
                (tm, x.shape[1]), lambda i, s=start, z=size: (jnp.clip(i + tile0 - s, 0, z - 1), 0)))
        start += size
    return arrays, specs, ends


def _qkv_kernel(*refs, d, scale, rope, n_lat_tiles, ends, n_alias):
    n_x = len(ends)
    x_refs = refs[:n_x]
    mod_ref, g_ref, w_ref = refs[n_x:n_x + 3]
    rest = refs[n_x + 3:]
    if rope:
        rope_ref, wvt_ref, rest = rest[0], rest[1], rest[2:]
    rest = rest[n_alias:]
    q_ref, k_ref, v_ref, kf_ref, vf_ref = rest[:5]
    i = pl.program_id(0)
    x = _read_tokens(x_refs, i, ends)
    h = _rms(x, g_ref[...]) * (1.0 + mod_ref[1:2, :]) + mod_ref[0:1, :]
    hb = h.astype(BF16)

    def rot(t):
        if not rope:
            return t
        cols = []
        for j in range(d // LANES):
            tb = t[:, j * LANES:(j + 1) * LANES]
            cols.append(tb * rope_ref[0]
                        + pltpu.roll(tb, LANES - HEAD_DIM // 2, axis=1) * rope_ref[1]
                        + pltpu.roll(tb, HEAD_DIM // 2, axis=1) * rope_ref[2])
        return jnp.concatenate(cols, axis=1)

    q = _dot(hb, w_ref[:, 0:d])
    q_ref[...] = (rot(q) * scale).astype(BF16)
    k = _dot(hb, w_ref[:, d:2 * d])
    k_ref[...] = rot(k).astype(BF16)
    if rope:
        @pl.when(i < n_lat_tiles)
        def _():
            rest[5][...] = _dot_nt(wvt_ref[...], hb).astype(BF16)

    def context_values():
        v = _dot(hb, w_ref[:, 2 * d:3 * d])
        v_ref[...] = v.astype(BF16)
        return v

    if not rope:
        v = context_values()

    @pl.when(i >= n_lat_tiles)
    def _():
        kf_ref[...] = k.reshape(kf_ref.shape)
        vf_ref[...] = (context_values() if rope else v).reshape(vf_ref.shape)


def _qkv(xs, mod, g, w, rope_tab, caches, *, layer, n_lat, batch, seq, lat_seq, n_lat_batch, tm,
         scale):
    d = xs[0].shape[1]
    depth = w.shape[0]
    n = n_lat + batch * seq
    n_lat_tiles = n_lat // tm
    rope = rope_tab is not None
    assert tm % seq == 0
    x_arrays, x_specs, ends = _token_parts(xs, tm, 0, n // tm)

    def mod_idx(i):
        return (layer, jnp.minimum(i * tm // lat_seq, n_lat_batch), 0, 0)

    in_specs = x_specs + [
        pl.BlockSpec((None, None, N_MOD, d), mod_idx),
        pl.BlockSpec((None, 1, d), lambda i: (layer, 0, 0)),
        pl.BlockSpec((None, d, 3 * d), lambda i: (layer, 0, 0)),
    ]
    args = x_arrays + [mod, g, w]
    if rope:
        seq_tiles = lat_seq // tm
        in_specs.append(pl.BlockSpec(
            (3, tm, LANES),
            lambda i: (0, jnp.where(i < n_lat_tiles, i % seq_tiles, seq_tiles), 0)))
        in_specs.append(pl.BlockSpec((d, d), lambda i: (0, 0)))
        args += [rope_tab, w[layer, :, 2 * d:].T]
    aliases = {}
    if caches is not None:
        aliases = {len(args): 3, len(args) + 1: 4}
        in_specs += [pl.BlockSpec(memory_space=pl.ANY)] * 2
        args += list(caches)
    tok = pl.BlockSpec((tm, d), lambda i: (i, 0))
    cache_spec = pl.BlockSpec((tm // seq, None, seq, d),
                              lambda i: (jnp.maximum(i - n_lat_tiles, 0), layer, 0, 0))
    out_specs = [tok, tok, tok, cache_spec, cache_spec]
    out_shape = ([jax.ShapeDtypeStruct((n, d), BF16)] * 3
                 + [jax.ShapeDtypeStruct((batch, depth, seq, d), F32)] * 2)
    if rope:
        out_specs.append(pl.BlockSpec((None, d, tm),
                                      lambda i: (jnp.minimum(i, n_lat_tiles - 1), 0, 0)))
        out_shape.append(jax.ShapeDtypeStruct((n_lat_tiles, d, tm), BF16))
    return pl.pallas_call(
        functools.partial(_qkv_kernel, d=d, scale=scale, rope=rope, n_lat_tiles=n_lat_tiles,
                          ends=tuple(ends), n_alias=len(aliases)),
        grid=(n // tm,),
        in_specs=in_specs,
        out_specs=out_specs,
        out_shape=out_shape,
        input_output_aliases=aliases,
        compiler_params=_cparams("arbitrary"),
        name=f"qkv{layer}",
    )(*args)


def _rope_tables(lat_seq, tm):
    n_freq = HEAD_DIM // 4
    inv = ROPE_BASE ** (-jnp.arange(n_freq, dtype=F32) / n_freq)
    pos = jnp.arange(lat_seq)
    row = (pos // GRID_W).astype(F32)
    col = (pos % GRID_W).astype(F32)
    ang = jnp.concatenate([row[:, None] * inv, col[:, None] * inv], axis=-1)
    reps = LANES // (HEAD_DIM // 2)
    cos = jnp.tile(jnp.cos(ang), (1, reps))
    sin = jnp.tile(jnp.sin(ang), (1, reps))
    first_half = (jnp.arange(LANES) % HEAD_DIM) < HEAD_DIM // 2
    s_next = jnp.where(first_half, -sin, 0.0)
    s_prev = jnp.where(first_half, 0.0, sin)
    ident = jnp.stack([jnp.ones((tm, LANES), F32), jnp.zeros((tm, LANES), F32),
                       jnp.zeros((tm, LANES), F32)])
    return jnp.concatenate([jnp.stack([cos, s_next, s_prev]), ident], axis=1)


def _lane_lo():
    return lax.broadcasted_iota(jnp.int32, (1, LANES), 1) < HEAD_DIM


def _ctx_dense_kernel(q_ref, k_ref, v_ref, o_all_ref, o_ref, *, d):
    del o_all_ref
    lo = _lane_lo()
    cols = []
    for j in range(d // LANES):
        blk = slice(j * LANES, (j + 1) * LANES)
        q2, k2, v2 = q_ref[:, blk], k_ref[:, blk], v_ref[:, blk]
        outs = []
        for qm in (jnp.where(lo, q2, 0), jnp.where(lo, 0, q2)):
            s = _dot_nt(qm, k2)
            p = jnp.exp2(s - jnp.max(s, axis=-1, keepdims=True))
            outs.append(_dot(p.astype(BF16), v2) / jnp.sum(p, axis=-1, keepdims=True))
        cols.append(jnp.where(lo, outs[0], outs[1]).astype(BF16))
    o_ref[...] = jnp.concatenate(cols, axis=1)


def _ctx_dense(q, k, v, o_all, *, n_lat, seq):
    n, d = q.shape
    off = n_lat // seq
    spec = pl.BlockSpec((seq, d), lambda b: (off + b, 0))
    return pl.pallas_call(
        functools.partial(_ctx_dense_kernel, d=d),
        grid=((n - n_lat) // seq,),
        in_specs=[spec, spec, spec, pl.BlockSpec(memory_space=pl.ANY)],
        out_specs=spec,
        out_shape=jax.ShapeDtypeStruct((n, d), BF16),
        input_output_aliases={3: 0},
        compiler_params=_cparams("arbitrary"),
        name="ctx_dense_attn",
    )(q, k, v, o_all)


def _subln(o, g_ref, out_scale):
    return _rms(o, g_ref[...]) * out_scale


def _ctx_diff_kernel(lam_ref, q_ref, k_ref, v_ref, g_ref, o_all_ref, o_ref, *, d, out_scale):
    del o_all_ref
    lo = _lane_lo()
    lam = lam_ref[0]
    cols = []
    for j in range(d // LANES):
        blk = slice(j * LANES, (j + 1) * LANES)
        q2, k2, v2 = q_ref[:, blk], k_ref[:, blk], v_ref[:, blk]
        ps = []
        for qm in (jnp.where(lo, q2, 0), jnp.where(lo, 0, q2)):
            s = _dot_nt(qm, k2)
            p = jnp.exp2(s - jnp.max(s, axis=-1, keepdims=True))
            ps.append(p / jnp.sum(p, axis=-1, keepdims=True))
        o = _dot((ps[0] - lam * ps[1]).astype(BF16), v2)
        cols.append(_subln(o, g_ref, out_scale).astype(BF16))
    o_ref[...] = jnp.concatenate(cols, axis=1)


def _ctx_diff(lam, q, k, v, subg, o_all, *, n_lat, seq, out_scale):
    n, d = q.shape
    off = n_lat // seq
    spec = pl.BlockSpec((seq, d), lambda b: (off + b, 0))
    return pl.pallas_call(
        functools.partial(_ctx_diff_kernel, d=d, out_scale=out_scale),
        grid=((n - n_lat) // seq,),
        in_specs=[pl.BlockSpec(memory_space=pltpu.SMEM), spec, spec, spec,
                  pl.BlockSpec((1, LANES), lambda b: (0, 0)), pl.BlockSpec(memory_space=pl.ANY)],
        out_specs=spec,
        out_shape=jax.ShapeDtypeStruct((n, d), BF16),
        input_output_aliases={5: 0},
        compiler_params=_cparams("arbitrary"),
        name="ctx_diff_attn",
    )(lam, q, k, v, subg, o_all)


def _na_kernel(q_ref, k_ref, v_ref, ck_ref, cv_ref, bias_ref, o_ref, *, d, rows):
    lo = _lane_lo()
    r = pl.program_id(1)
    win = NA_ROWS * GRID_W
    start = pl.multiple_of(jnp.clip(r - NA_ROWS // 2, 0, rows - NA_ROWS) * GRID_W, GRID_W)
    n_heads = 2 * (d // LANES)

    def blk(h):
        return slice(h // 2 * LANES, (h // 2 + 1) * LANES)

    def scores(heads):
        out = []
        for h in heads:
            q2 = q_ref[:, blk(h)]
            qm = jnp.where(lo, q2, 0) if h % 2 == 0 else jnp.where(lo, 0, q2)
            out.append((_dot_nt(qm, k_ref[pl.ds(start, win), blk(h)]) + bias_ref[h],
                        _dot_nt(qm, ck_ref[:, blk(h)].astype(BF16))))
        return out

    def attend(heads, ss):
        probs = []
        for s_loc, s_ctx in ss:
            m = jnp.maximum(jnp.max(s_loc, axis=-1, keepdims=True),
                            jnp.max(s_ctx, axis=-1, keepdims=True))
            p_loc = jnp.exp2(s_loc - m)
            p_ctx = jnp.exp2(s_ctx - m)
            l = jnp.sum(p_loc, axis=-1, keepdims=True) + jnp.sum(p_ctx, axis=-1, keepdims=True)
            probs.append((p_loc.astype(BF16), p_ctx.astype(BF16), l))
        return [(_dot(p_loc, v_ref[pl.ds(start, win), blk(h)])
                 + _dot(p_ctx, cv_ref[:, blk(h)].astype(BF16))) / l
                for h, (p_loc, p_ctx, l) in zip(heads, probs)]

    groups = [list(range(g, min(g + NA_HEAD_GROUP, n_heads))) for g in range(0, n_heads, NA_HEAD_GROUP)]
    outs = []
    ss = scores(groups[0])
    for gi, heads in enumerate(groups):
        ss_next = scores(groups[gi + 1]) if gi + 1 < len(groups) else None
        outs += attend(heads, ss)
        ss = ss_next
    o_ref[...] = jnp.concatenate(
        [jnp.where(lo, outs[h], outs[h + 1]).astype(BF16) for h in range(0, n_heads, 2)], axis=1)


def _na_bias_table(rpb, rows):
    h = rpb.shape[0]
    delta = jnp.arange(NA_ROWS)
    kr = jnp.arange(NA_ROWS)
    c = jnp.arange(GRID_W)
    kc = jnp.arange(GRID_W)
    off_r = kr[None, :] - delta[:, None] + (NA_ROWS - 1)
    c0 = jnp.clip(c - NA_COLS // 2, 0, GRID_W - NA_COLS)
    off_c = kc[None, :] - c[:, None] + (NA_COLS - 1)
    inside = (kc[None, :] >= c0[:, None]) & (kc[None, :] < c0[:, None] + NA_COLS)
    oh_r = jax.nn.one_hot(off_r, 2 * NA_ROWS - 1, dtype=F32)
    oh_c = jax.nn.one_hot(off_c, 2 * NA_COLS - 1, dtype=F32)
    t = jnp.einsum('hrs,dkr,cqs->dhckq', rpb.astype(F32), oh_r, oh_c,
                   precision=lax.Precision.HIGHEST)
    t = jnp.where(inside[None, None, :, None, :], t * LOG2E, MASKED)
    return t.reshape(NA_ROWS, h, GRID_W, NA_ROWS * GRID_W)


def _na_attention(q, k, v, ck, cv, bias, *, layer, n_lat_batch, lat_seq):
    n, d = q.shape
    rows = lat_seq // GRID_W
    past = ck.shape[2]
    n_heads = bias.shape[1]

    def delta_idx(b, r):
        return (r - jnp.clip(r - NA_ROWS // 2, 0, rows - NA_ROWS), 0, 0, 0)

    kv_spec = pl.BlockSpec((lat_seq, d), lambda b, r: (b, 0))
    c_spec = pl.BlockSpec((None, None, past, d), lambda b, r: (b, layer, 0, 0))
    return pl.pallas_call(
        functools.partial(_na_kernel, d=d, rows=rows),
        grid=(n_lat_batch, rows),
        in_specs=[pl.BlockSpec((GRID_W, d), lambda b, r: (b * rows + r, 0)), kv_spec, kv_spec,
                  c_spec, c_spec,
                  pl.BlockSpec((None, n_heads, GRID_W, NA_ROWS * GRID_W), delta_idx)],
        out_specs=pl.BlockSpec((GRID_W, d), lambda b, r: (b * rows + r, 0)),
        out_shape=jax.ShapeDtypeStruct((n, d), BF16),
        compiler_params=_cparams("arbitrary", "arbitrary"),
        name="na_attn",
    )(q, k, v, ck, cv, bias)


def _lat_diff_kernel(lam_ref, q_ref, k_ref, vt_ref, ck_ref, cv_ref, g_ref, o_ref, *, out_scale):
    lo = _lane_lo()
    lam = lam_ref[0]
    q2 = q_ref[...]
    tq = q2.shape[0]
    tg = tq // QUERY_GROUPS
    qs = []
    for g in range(QUERY_GROUPS):
        qg = q2[g * tg:(g + 1) * tg, :]
        qs += [jnp.where(lo, qg, 0), jnp.where(lo, 0, qg)]

    def softmax_step(state, s):
        m, l, acc = state
        m_new = jnp.maximum(m, jnp.max(s, axis=0, keepdims=True))
        alpha = jnp.exp2(m - m_new)
        p = jnp.exp2(s - m_new)
        return m_new, alpha * l + jnp.sum(p, axis=0, keepdims=True), alpha * acc, p.astype(BF16)

    def scores(kt):
        return [_dot_nt(kt, qm) for qm in qs]

    def absorb(states, ss, vt):
        out = []
        for g in range(0, len(qs), 2):
            stepped = [softmax_step(states[g + j], ss[g + j]) for j in range(2)]
            out += [(m, l, acc + _dot(vt, p)) for m, l, acc, p in stepped]
        return out

    tile = vt_ref.shape[2]
    n_tiles = vt_ref.shape[0]

    def keys(t):
        return k_ref[t * tile:(t + 1) * tile, :] if t < n_tiles else ck_ref[...].astype(BF16)

    def values_t(t):
        return vt_ref[t] if t < n_tiles else cv_ref[...].T.astype(BF16)

    init = (jnp.full((1, tg), -jnp.inf, F32), jnp.zeros((1, tg), F32), jnp.zeros((LANES, tg), F32))
    states = [init] * len(qs)
    ss = scores(keys(0))
    for t in range(n_tiles + 1):
        ss_next = scores(keys(t + 1)) if t < n_tiles else None
        states = absorb(states, ss, values_t(t))
        ss = ss_next
    o = jnp.concatenate([states[g][2] / states[g][1] - lam * (states[g + 1][2] / states[g + 1][1])
                         for g in range(0, len(qs), 2)], axis=1).T
    o_ref[...] = _subln(o, g_ref, out_scale).astype(BF16)


def _lat_diff(lam, q, k, vt, ck, cv, subg, *, layer, n_lat_batch, lat_seq, out_scale):
    n, d = q.shape
    past = ck.shape[2]
    tk = vt.shape[2]
    tq = _pick_tile(lat_seq, QUERY_GROUPS * 256)
    qt = lat_seq // tq
    kt = lat_seq // tk
    c_spec = pl.BlockSpec((None, None, past, LANES), lambda b, h, i: (b, layer, 0, h))
    q_spec = pl.BlockSpec((tq, LANES), lambda b, h, i: (b * qt + i, h))
    return pl.pallas_call(
        functools.partial(_lat_diff_kernel, out_scale=out_scale),
        grid=(n_lat_batch, d // LANES, qt),
        in_specs=[pl.BlockSpec(memory_space=pltpu.SMEM), q_spec,
                  pl.BlockSpec((lat_seq, LANES), lambda b, h, i: (b, h)),
                  pl.BlockSpec((kt, LANES, tk), lambda b, h, i: (b, h, 0)),
                  c_spec, c_spec, pl.BlockSpec((1, LANES), lambda b, h, i: (0, 0))],
        out_specs=q_spec,
        out_shape=jax.ShapeDtypeStruct((n, d), BF16),
        compiler_params=_cparams("arbitrary", "arbitrary", "arbitrary"),
        name="lat_diff_attn",
    )(lam, q, k, vt, ck, cv, subg)


def _post_attn_kernel(o_ref, *refs, ends, tile0):
    n_x = len(ends)
    x_refs = refs[:n_x]
    (mod_ref, wo_ref, g_ref, rwh_ref, rwl_ref, rb_ref, tri_ref,
     xmid_ref, h_ref, sel_ref, gate_ref, rank_ref, cnt_ref, carry_ref) = refs[n_x:]
    i = pl.program_id(0)

    @pl.when(i == 0)
    def _():
        carry_ref[...] = jnp.zeros_like(carry_ref)

    x1 = _read_tokens(x_refs, i + tile0, ends) + mod_ref[2:3, :] * _dot(o_ref[...], wo_ref[...])
    xmid_ref[...] = x1
    h = _rms(x1, g_ref[...]) * (1.0 + mod_ref[4:5, :]) + mod_ref[3:4, :]
    hb = h.astype(BF16)
    h_ref[...] = _pack_rows(hb)
    h_lo = (h - hb.astype(F32)).astype(BF16)
    logits = _dot_nt(rwh_ref[...], hb) + (_dot_nt(rwl_ref[...], hb) + _dot_nt(rwh_ref[...], h_lo))
    scores = _sigmoid(logits)
    n_exp, tm = scores.shape
    biased = scores + rb_ref[...]
    row = lax.broadcasted_iota(jnp.int32, (n_exp, tm), 0).astype(F32)
    total = jnp.zeros((n_exp, tm), F32)
    sels, gates = [], []
    for _ in range(TOP_K):
        m = jnp.max(biased, axis=0, keepdims=True)
        idx = jnp.min(jnp.where(biased == m, row, float(n_exp)), axis=0, keepdims=True)
        hit = row == idx
        gates.append(jnp.sum(jnp.where(hit, scores, 0.0), axis=0, keepdims=True))
        biased = jnp.where(hit, -jnp.inf, biased)
        total = total + jnp.where(hit, 1.0, 0.0)
        sels.append(idx)
    denom = gates[0]
    for g in gates[1:]:
        denom = denom + g
    before = _dot(total.astype(BF16), tri_ref[...]) + carry_ref[:, 0:1]
    for k in range(TOP_K):
        sel_ref[k:k + 1, :] = sels[k].astype(jnp.int32)
        gate_ref[k:k + 1, :] = gates[k] / denom * ROUTED_SCALE
        rank_ref[k:k + 1, :] = jnp.sum(jnp.where(row == sels[k], before, 0.0), axis=0,
                                       keepdims=True).astype(jnp.int32)
    carry_ref[...] = carry_ref[...] + jnp.sum(total, axis=1, keepdims=True)
    cnt_ref[...] = carry_ref[...].astype(jnp.int32)


def _post_attn(o, xs, mod, wo, g, rwh, rwl, rb, *, layer, tile0, n_tiles, lat_seq, n_lat_batch, tm):
    d = o.shape[1]
    n = n_tiles * tm
    n_exp = rwh.shape[0]
    tri = (jnp.arange(tm)[:, None] < jnp.arange(tm)[None, :]).astype(BF16)
    x_arrays, x_specs, ends = _token_parts(xs, tm, tile0, n_tiles)

    def mod_idx(i):
        return (layer, jnp.minimum((i + tile0) * tm // lat_seq, n_lat_batch), 0, 0)

    tok = pl.BlockSpec((tm, d), lambda i: (i, 0))
    tok_p = pl.BlockSpec((tm, d // 2), lambda i: (i, 0))
    kt = pl.BlockSpec((TOP_K, tm), lambda i: (0, i))
    const2 = lambda i: (0, 0)
    return pl.pallas_call(
        functools.partial(_post_attn_kernel, ends=tuple(ends), tile0=tile0),
        grid=(n_tiles,),
        in_specs=[pl.BlockSpec((tm, d), lambda i: (i + tile0, 0))] + x_specs + [
                  pl.BlockSpec((None, None, N_MOD, d), mod_idx),
                  pl.BlockSpec((d, d), const2), pl.BlockSpec((1, d), const2),
                  pl.BlockSpec((n_exp, d), const2), pl.BlockSpec((n_exp, d), const2),
                  pl.BlockSpec((n_exp, 1), const2), pl.BlockSpec((tm, tm), const2)],
        out_specs=[tok, tok_p, kt, kt, kt, pl.BlockSpec((n_exp, LANES), const2)],
        out_shape=[jax.ShapeDtypeStruct((n, d), F32), jax.ShapeDtypeStruct((n, d // 2), jnp.int32),
                   jax.ShapeDtypeStruct((TOP_K, n), jnp.int32),
                   jax.ShapeDtypeStruct((TOP_K, n), F32),
                   jax.ShapeDtypeStruct((TOP_K, n), jnp.int32),
                   jax.ShapeDtypeStruct((n_exp, LANES), jnp.int32)],
        scratch_shapes=[pltpu.VMEM((n_exp, LANES), F32)],
        compiler_params=_cparams("arbitrary"),
        name=f"post_attn{layer}",
    )(o, *x_arrays, mod, wo, g, rwh, rwl, rb, tri)


def _plan_kernel(cnt_ref, sel_ref, rank_ref, dest_ref, te_ref, tv_ref, nu_ref, *, tmb):
    n_exp = cnt_ref.shape[0]
    sel = sel_ref[...]
    tile_start = lax.broadcasted_iota(jnp.int32, te_ref.shape, 1) * tmb
    dest = rank_ref[...]
    te = jnp.zeros(te_ref.shape, jnp.int32)
    tv = jnp.zeros(te_ref.shape, jnp.int32)
    start = jnp.zeros((1, 1), jnp.int32)
    for e in range(n_exp):
        dest = dest + jnp.where(sel == e, start, 0)
        cnt = cnt_ref[e:e + 1, 0:1]
        stop = start + ((cnt + (tmb - 1)) & -tmb)
        mine = (tile_start >= start) & (tile_start < stop)
        tv = tv + jnp.where(mine, jnp.clip(start + cnt - tile_start, 0, tmb), 0)
        start = stop
        te = te + jnp.where(start <= tile_start, 1, 0)
    te_ref[...] = jnp.minimum(te, n_exp - 1)
    tv_ref[...] = tv
    nu_ref[...] = jnp.broadcast_to(start, nu_ref.shape)
    for c in range(dest_ref.shape[0]):
        dest_ref[c] = dest[:, c * SC_CHUNK:(c + 1) * SC_CHUNK]


def _plan(counts, sel, rank, *, tmb, n_tiles):
    k_top, n = sel.shape
    assert tmb & (tmb - 1) == 0
    tn = _pick_tile(n, 16 * SC_CHUNK)
    te_w = -(-n_tiles // LANES) * LANES
    tok = pl.BlockSpec((k_top, tn), lambda i: (0, i))
    const2 = lambda i: (0, 0)
    dest, te, tv, nu = pl.pallas_call(
        functools.partial(_plan_kernel, tmb=tmb),
        grid=(n // tn,),
        in_specs=[pl.BlockSpec(counts.shape, const2), tok, tok],
        out_specs=[pl.BlockSpec((tn // SC_CHUNK, k_top, SC_CHUNK), lambda i: (i, 0, 0)),
                   pl.BlockSpec((1, te_w), const2), pl.BlockSpec((1, te_w), const2),
                   pl.BlockSpec((1, LANES), const2)],
        out_shape=[jax.ShapeDtypeStruct((n // SC_CHUNK, k_top, SC_CHUNK), jnp.int32),
                   jax.ShapeDtypeStruct((1, te_w), jnp.int32),
                   jax.ShapeDtypeStruct((1, te_w), jnp.int32),
                   jax.ShapeDtypeStruct((1, LANES), jnp.int32)],
        compiler_params=_cparams("arbitrary"),
        name="plan",
    )(counts, sel, rank)
    return dest, te[0, :n_tiles], tv[0, :n_tiles], nu[0, :1] // tmb


def _expert_kernel(te_ref, tv_ref, nu_ref, x_ref, wg_ref, wu_ref, wd_ref, y_ref, wg_b, wu_b, wd_b):
    i = pl.program_id(0)
    half = x_ref.shape[0] // 2

    def rows(r):
        return jnp.concatenate(_unpack_rows(x_ref[r * half:(r + 1) * half, :]), axis=1).astype(BF16)

    def swiglu(xs):
        gu = [(_dot(x, wg_b[...]), _dot(x, wu_b[...])) for x in xs]
        acts = [(g * _sigmoid(g) * u).astype(BF16) for g, u in gu]
        return [_pack_rows(_dot(a, wd_b[...]).astype(BF16)) for a in acts]

    @pl.when(i < nu_ref[0])
    def _():
        @pl.when((i == 0) | (te_ref[i] != te_ref[jnp.maximum(i - 1, 0)]))
        def _():
            wg_b[...] = wg_ref[...].astype(BF16)
            wu_b[...] = wu_ref[...].astype(BF16)
            wd_b[...] = wd_ref[...].astype(BF16)

        @pl.when(tv_ref[i] > half)
        def _():
            ya, yb = swiglu([rows(0), rows(1)])
            y_ref[0:half, :] = ya
            y_ref[half:2 * half, :] = yb

        @pl.when(tv_ref[i] <= half)
        def _():
            y_ref[0:half, :] = swiglu([rows(0)])[0]


def _experts(tile_expert, tile_rows, n_used, xb, wg, wu, wd, *, layer, tmb):
    cap, dp = xb.shape
    f, d = wd.shape[2:]

    def row_idx(i, te, tv, nu):
        return (jnp.minimum(i, nu[0] - 1), 0)

    def w_idx(i, te, tv, nu):
        return (layer, te[i], 0, 0)

    return pl.pallas_call(
        _expert_kernel,
        grid_spec=pltpu.PrefetchScalarGridSpec(
            num_scalar_prefetch=3,
            grid=(cap // tmb,),
            in_specs=[pl.BlockSpec((tmb, dp), row_idx),
                      pl.BlockSpec((None, None, d, f), w_idx),
                      pl.BlockSpec((None, None, d, f), w_idx),
                      pl.BlockSpec((None, None, f, d), w_idx)],
            out_specs=pl.BlockSpec((tmb, dp), row_idx),
            scratch_shapes=[pltpu.VMEM((d, f), BF16), pltpu.VMEM((d, f), BF16),
                            pltpu.VMEM((f, d), BF16)]),
        out_shape=jax.ShapeDtypeStruct((cap, dp), jnp.int32),
        compiler_params=_cparams("arbitrary"),
        name="experts",
    )(tile_expert, tile_rows, n_used, xb, wg, wu, wd)


def _combine_kernel(x_ref, h_ref, routed_ref, mod_ref, wgu_ref, wd_ref, fg_ref, *refs, f, final,
                    n_alias, tile0, n_lat_tiles):
    o_refs = refs[n_alias:]
    h = jnp.concatenate(_unpack_rows(h_ref[...]), axis=1).astype(BF16)
    gu = _dot(h, wgu_ref[...])
    gate, up = gu[:, :f], gu[:, f:]
    shared = _dot((gate * _sigmoid(gate) * up).astype(BF16), wd_ref[...])
    x2 = x_ref[...] + mod_ref[5:6, :] * (shared + routed_ref[...])
    if not final:
        o_refs[0][...] = x2
    elif len(o_refs) == 1:
        o_refs[0][...] = _rms(x2, fg_ref[...])
    else:
        y = _rms(x2, fg_ref[...])
        tile = pl.program_id(0) + tile0

        @pl.when(tile < n_lat_tiles)
        def _():
            o_refs[0][...] = y

        @pl.when(tile >= n_lat_tiles)
        def _():
            o_refs[1][...] = y


def _combine(x, h, routed, mod, wgu, wd, final_g, y_lat, *, layer, tile0, n_lat, n_ctx, lat_seq,
             n_lat_batch, tm, final):
    n, d = x.shape
    f = wd.shape[0]
    n_tiles = n // tm
    n_lat_tiles = n_lat // tm

    def mod_idx(i):
        return (layer, jnp.minimum((i + tile0) * tm // lat_seq, n_lat_batch), 0, 0)

    tok = pl.BlockSpec((tm, d), lambda i: (i, 0))
    tok_p = pl.BlockSpec((tm, d // 2), lambda i: (i, 0))
    const2 = lambda i: (0, 0)
    in_specs = [tok, tok_p, tok,
                pl.BlockSpec((None, None, N_MOD, d), mod_idx),
                pl.BlockSpec((d, 2 * f), const2), pl.BlockSpec((f, d), const2),
                pl.BlockSpec((1, d), const2)]
    args = [x, h, routed, mod, wgu, wd, final_g]
    aliases = {}
    has_lat = has_ctx = False
    if not final:
        out_specs, out_shape = [tok], [jax.ShapeDtypeStruct((n, d), F32)]
    else:
        has_lat = tile0 < n_lat_tiles
        has_ctx = tile0 + n_tiles > n_lat_tiles
        out_specs, out_shape = [], []
        if has_lat:
            out_specs.append(pl.BlockSpec(
                (tm, d), lambda i: (jnp.minimum(i + tile0, n_lat_tiles - 1), 0)))
            out_shape.append(jax.ShapeDtypeStruct((n_lat, d), F32))
            if y_lat is not None:
                aliases = {len(args): 0}
                in_specs.append(pl.BlockSpec(memory_space=pl.ANY))
                args.append(y_lat)
        if has_ctx:
            out_specs.append(pl.BlockSpec(
                (tm, d), lambda i: (jnp.maximum(i + tile0 - n_lat_tiles, 0), 0)))
            out_shape.append(jax.ShapeDtypeStruct((n_ctx, d), F32))
    outs = pl.pallas_call(
        functools.partial(_combine_kernel, f=f, final=final, n_alias=len(aliases), tile0=tile0,
                          n_lat_tiles=n_lat_tiles),
        grid=(n_tiles,),
        in_specs=in_specs,
        out_specs=out_specs,
        out_shape=out_shape,
        input_output_aliases=aliases,
        compiler_params=_cparams("arbitrary"),
        name=f"combine{layer}",
    )(*args)
    if not final:
        return outs[0]
    return (outs[0] if has_lat else y_lat), (outs[-1] if has_ctx else None)


SC_CHUNK = 128


def _sc_workers():
    info = plsc.get_sparse_core_info()
    return info.num_cores, info.num_subcores


def _sc_dispatch(rows, dest, cap):
    n, w = rows.shape
    nc, ns = _sc_workers()
    per_w = n // (nc * ns)
    assert per_w * nc * ns == n and per_w % SC_CHUNK == 0
    mesh = plsc.VectorSubcoreMesh(core_axis_name="c", subcore_axis_name="s")

    @functools.partial(
        pl.kernel, mesh=mesh, out_type=jax.ShapeDtypeStruct((cap, w), rows.dtype),
        scratch_types=[pltpu.VMEM((TOP_K, SC_CHUNK), jnp.int32),
                       pltpu.VMEM((SC_CHUNK, w), rows.dtype)],
        name="sc_dispatch")
    def run(rows_hbm, dest_hbm, out_hbm, idx_v, rows_v):
        wid = lax.axis_index("s") * nc + lax.axis_index("c")

        @pl.loop(0, per_w // SC_CHUNK)
        def _(ci):
            base = pl.multiple_of(wid * per_w + ci * SC_CHUNK, SC_CHUNK)
            pltpu.sync_copy(dest_hbm.at[wid * (per_w // SC_CHUNK) + ci], idx_v)
            pltpu.sync_copy(rows_hbm.at[pl.ds(base, SC_CHUNK)], rows_v)
            for k in range(TOP_K):
                pltpu.sync_copy(rows_v, out_hbm.at[idx_v.at[k]])

    return run(rows, dest)


SC_TOK = 8
SC_LANES = 16


def _sc_combine(table, dest, gates):
    n_chunks, k_top, _ = dest.shape
    n = n_chunks * SC_CHUNK
    w = table.shape[1]
    nc, ns = _sc_workers()
    per_w = n // (nc * ns)
    assert per_w * nc * ns == n and per_w % SC_CHUNK == 0 and w % (16 * SC_LANES) == 0
    gb = jnp.broadcast_to(
        gates.reshape(k_top, n_chunks, SC_CHUNK).transpose(1, 0, 2)[..., None],
        (n_chunks, k_top, SC_CHUNK, SC_LANES)).reshape(n_chunks, k_top, SC_CHUNK // SC_TOK,
                                                        SC_TOK * SC_LANES)
    mesh = plsc.VectorSubcoreMesh(core_axis_name="c", subcore_axis_name="s")

    n_blocks = SC_CHUNK // SC_TOK

    @functools.partial(
        pl.kernel, mesh=mesh, out_type=jax.ShapeDtypeStruct((n, 2 * w), F32),
        scratch_types=[pltpu.VMEM((k_top, SC_CHUNK), jnp.int32),
                       pltpu.VMEM((k_top, n_blocks, SC_TOK * SC_LANES), F32),
                       pltpu.VMEM((2, k_top, SC_TOK, w), table.dtype),
                       pltpu.VMEM((2, SC_TOK, 2 * w), F32),
                       pltpu.SemaphoreType.DMA((2,)), pltpu.SemaphoreType.DMA((2,))],
        compiler_params=pltpu.CompilerParams(needs_layout_passes=False),
        name="sc_combine")
    def run(table_hbm, dest_hbm, gb_hbm, out_hbm, idx_v, g_v, rows_v, out_v, sem_in, sem_out):
        wid = lax.axis_index("s") * nc + lax.axis_index("c")

        def gathers(blk, slot):
            off = pl.multiple_of(blk * SC_TOK, SC_TOK)
            return [pltpu.make_async_copy(table_hbm.at[idx_v.at[k, pl.ds(off, SC_TOK)]],
                                          rows_v.at[slot, k], sem_in.at[slot]) for k in range(k_top)]

        def write_back(chunk, blk, slot):
            row0 = pl.multiple_of(chunk * SC_CHUNK + blk * SC_TOK, SC_TOK)
            return pltpu.make_async_copy(out_v.at[slot], out_hbm.at[pl.ds(row0, SC_TOK)],
                                         sem_out.at[slot])

        def weighted_sum(blk, slot):
            @pl.loop(0, SC_TOK)
            def _(j):
                for base in range(0, w, 16 * SC_LANES):
                    acc = None
                    for k in range(k_top):
                        g = g_v[k, blk, pl.ds(j * SC_LANES, SC_LANES)]
                        terms = []
                        for v in range(16):
                            word = rows_v[slot, k, j, pl.ds(base + v * SC_LANES, SC_LANES)]
                            lo = lax.bitcast_convert_type(lax.shift_left(word, 16), F32)
                            hi = lax.bitcast_convert_type(word & jnp.int32(-65536), F32)
                            terms.append((g * lo, g * hi))
                        acc = terms if acc is None else [
                            (a + x, b + y) for (a, b), (x, y) in zip(acc, terms)]
                    for v, (a, b) in enumerate(acc):
                        out_v[slot, j, pl.ds(base + v * SC_LANES, SC_LANES)] = a
                        out_v[slot, j, pl.ds(w + base + v * SC_LANES, SC_LANES)] = b

        @pl.loop(0, per_w // SC_CHUNK)
        def _(ci):
            chunk = wid * (per_w // SC_CHUNK) + ci
            pltpu.sync_copy(dest_hbm.at[chunk], idx_v)
            pltpu.sync_copy(gb_hbm.at[chunk], g_v)
            for cp in gathers(0, 0):
                cp.start()

            @pl.loop(0, n_blocks, step=2)
            def _(b0):
                for slot in range(2):
                    blk = b0 + slot
                    for cp in gathers(blk, slot):
                        cp.wait()

                    @pl.when(blk + 1 < n_blocks)
                    def _():
                        for cp in gathers(blk + 1, 1 - slot):
                            cp.start()

                    @pl.when(blk >= 2)
                    def _():
                        write_back(chunk, blk - 2, slot).wait()

                    weighted_sum(blk, slot)
                    write_back(chunk, blk, slot).start()

            for slot in range(2):
                write_back(chunk, n_blocks - 2 + slot, slot).wait()


    return run(table, dest, gb)


def _split_bf16(w):
    hi = w.astype(BF16)
    return hi, (w - hi.astype(F32)).astype(BF16)


def kernel(x_prompt, x_sample, cache_k, cache_v, c, c_ctx, ada_w, ada_b, norm1_g, norm2_g, w_qkv, w_o, na_rpb, diff_lambda, diff_subln_g, router_w, router_b, exp_w_gate, exp_w_up, exp_w_down, shared_w_gate, shared_w_up, shared_w_down, final_g):
    batch, seq, d = x_prompt.shape
    n_lat_batch, lat_seq, _ = x_sample.shape
    depth = w_qkv.shape[0]
    n_exp = router_w.shape[-1]
    n_lat = n_lat_batch * lat_seq
    n_ctx = batch * seq
    n = n_lat + n_ctx
    assert d % LANES == 0 and na_rpb.shape[1] * HEAD_DIM == d
    assert diff_lambda.shape[-1] == HEAD_DIM and lat_seq % GRID_W == 0
    assert lat_seq // GRID_W >= NA_ROWS and n_lat % seq == 0
    tm = _pick_tile(math.gcd(lat_seq, n_ctx), 512)
    tmb = 1024
    n_all = n // tm
    ranges = [(r * n_all // MOE_RANGES, (r + 1) * n_all // MOE_RANGES - r * n_all // MOE_RANGES)
              for r in range(MOE_RANGES)]
    scale = HEAD_DIM ** -0.5 * LOG2E

    xs = [x_sample.reshape(n_lat, d), x_prompt.reshape(n_ctx, d)]
    mod_rows = -(-(n_lat_batch + 1) // 8) * 8
    cond = jnp.zeros((mod_rows, d), F32).at[:n_lat_batch].set(c).at[n_lat_batch].set(c_ctx)
    mod = _modulation(cond, ada_w, ada_b).reshape(depth, mod_rows, N_MOD, d)

    w_qkv_b = w_qkv.astype(BF16)
    w_o_b = w_o.astype(BF16)
    rope_tab = _rope_tables(lat_seq, tm)
    caches = None
    for i in range(depth):
        is_diff = i % 2 == 1
        j = i // 2
        q, k, v, *caches = _qkv(xs, mod, norm1_g.reshape(depth, 1, d), w_qkv_b,
                                rope_tab if is_diff else None, caches, layer=i, n_lat=n_lat,
                                batch=batch, seq=seq, lat_seq=lat_seq, n_lat_batch=n_lat_batch,
                                tm=tm, scale=scale)
        if is_diff:
            vt = caches.pop()
        if not is_diff:
            o = _na_attention(q, k, v, cache_k, cache_v,
                              _na_bias_table(na_rpb[j], lat_seq // GRID_W), layer=i,
                              n_lat_batch=n_lat_batch, lat_seq=lat_seq)
            o = _ctx_dense(q, k, v, o, n_lat=n_lat, seq=seq)
        else:
            lam_init = 0.8 - 0.6 * math.exp(-0.3 * i)
            lp = diff_lambda[j].astype(F32)
            lam = (jnp.exp(jnp.sum(lp[0] * lp[1])) - jnp.exp(jnp.sum(lp[2] * lp[3]))
                   + lam_init).reshape(1)
            subg = diff_subln_g[j].reshape(1, LANES).astype(F32)
            o = _lat_diff(lam, q, k, vt, cache_k, cache_v, subg, layer=i,
                          n_lat_batch=n_lat_batch, lat_seq=lat_seq, out_scale=1.0 - lam_init)
            o = _ctx_diff(lam, q, k, v, subg, o, n_lat=n_lat, seq=seq, out_scale=1.0 - lam_init)

        rwh, rwl = _split_bf16(router_w[i].T)
        swgu = jnp.concatenate([shared_w_gate[i], shared_w_up[i]], axis=-1).astype(BF16)
        final = i == depth - 1
        routed = []
        for tile0, nt in ranges:
            xmid, h2, sel, gates, rank, counts = _post_attn(
                o, xs, mod, w_o_b[i], norm2_g[i].reshape(1, d), rwh, rwl,
                router_b[i].reshape(n_exp, 1).astype(F32), layer=i, tile0=tile0, n_tiles=nt,
                lat_seq=lat_seq, n_lat_batch=n_lat_batch, tm=tm)
            n_tiles = -(-(nt * tm * TOP_K) // tmb) + n_exp
            dest, tile_expert, tile_rows, n_used = _plan(counts, sel, rank, tmb=tmb, n_tiles=n_tiles)
            xb = _sc_dispatch(h2, dest, n_tiles * tmb)
            routed.append((xmid, h2, gates, dest, tile_expert, tile_rows, n_used, xb))
        ybs = [_experts(te, tr, nu, xb, exp_w_gate, exp_w_up, exp_w_down, layer=i, tmb=tmb)
               for (_, _, _, _, te, tr, nu, xb) in routed]
        sums = [_sc_combine(yb, r[3], r[2]) for yb, r in zip(ybs, routed)]
        xs, y_lat, y_ctx = [], None, None
        for (tile0, nt), (xmid, h2, *_), rsum in zip(ranges, routed, sums):
            out = _combine(xmid, h2, rsum, mod, swgu, shared_w_down[i].astype(BF16),
                           final_g.reshape(1, d), y_lat, layer=i, tile0=tile0, n_lat=n_lat,
                           n_ctx=n_ctx, lat_seq=lat_seq, n_lat_batch=n_lat_batch, tm=tm, final=final)
            if final:
                y_lat, y_ctx = out[0], (out[1] if out[1] is not None else y_ctx)
            else:
                xs.append(out)

    y_sample = y_lat.reshape(n_lat_batch, lat_seq, d)
    y_prompt = y_ctx.reshape(batch, seq, d)
    return (y_prompt, y_sample, caches[0], caches[1])
```

```python
import functools
import math

import jax
import jax.numpy as jnp
from jax import lax
from jax.experimental import pallas as pl
from jax.experimental.pallas import tpu as pltpu
from jax.experimental.pallas import tpu_sc as plsc

GRID_W = 64
NA_ROWS = 8
NA_COLS = 16
TOP_K = 8
ROUTED_SCALE = 2.5
ROPE_BASE = 10000.0
EPS = 1e-6
N_MOD = 6
HEAD_DIM = 64
LANES = 128
LOG2E = math.log2(math.e)
MASKED = -1e30
NA_QROWS = 2
NA_WIN = NA_ROWS + NA_QROWS - 1
MOE_RANGES = 2
QUERY_GROUPS = 4
VMEM_LIMIT = 56 * 1024 * 1024

F32 = jnp.float32
BF16 = jnp.bfloat16


def _cparams(*sem):
    return pltpu.CompilerParams(dimension_semantics=sem, vmem_limit_bytes=VMEM_LIMIT)


def _dot(a, b):
    return jnp.dot(a, b, preferred_element_type=F32)


def _dot_nt(a, b):
    return lax.dot_general(a, b, (((1,), (1,)), ((), ())), preferred_element_type=F32)


def _sigmoid(x):
    return 1.0 / (1.0 + jnp.exp(-x))


def _rms(x, g):
    return x * lax.rsqrt(jnp.mean(x * x, axis=-1, keepdims=True) + EPS) * g


def _pack_rows(xb):
    half = xb.shape[1] // 2
    u = lax.bitcast_convert_type(xb.astype(F32), jnp.uint32)
    packed = (u[:, :half] >> 16) | (u[:, half:] & jnp.uint32(0xFFFF0000))
    return lax.bitcast_convert_type(packed, jnp.int32)


def _unpack_rows(p):
    u = lax.bitcast_convert_type(p, jnp.uint32)
    return (lax.bitcast_convert_type(u << 16, F32),
            lax.bitcast_convert_type(u & jnp.uint32(0xFFFF0000), F32))


def _pick_tile(n, pref):
    t = pref
    while n % t:
        t //= 2
    return t


def _mod_kernel(c_ref, w_ref, b_ref, o_ref):
    c = c_ref[...]
    o_ref[...] = _dot(c * _sigmoid(c), w_ref[...]) + b_ref[...]


def _modulation(cond, ada_w, ada_b):
    depth, d, n6 = ada_w.shape
    rows = cond.shape[0]
    tn = _pick_tile(n6, 512)
    return pl.pallas_call(
        _mod_kernel,
        grid=(depth, n6 // tn),
        in_specs=[
            pl.BlockSpec((rows, d), lambda l, j: (0, 0)),
            pl.BlockSpec((None, d, tn), lambda l, j: (l, 0, j)),
            pl.BlockSpec((None, 1, tn), lambda l, j: (l, 0, j)),
        ],
        out_specs=pl.BlockSpec((None, rows, tn), lambda l, j: (l, 0, j)),
        out_shape=jax.ShapeDtypeStruct((depth, rows, n6), F32),
        compiler_params=_cparams("arbitrary", "arbitrary"),
        name="modulation",
    )(cond, ada_w, ada_b.reshape(depth, 1, n6))


def _read_tokens(x_refs, g, ends):
    x = x_refs[-1][...]
    for ref, end in zip(reversed(x_refs[:-1]), reversed(ends[:-1])):
        x = jnp.where(g < end, ref[...], x)
    return x


def _token_parts(xs, tm, tile0, n_tiles):
    arrays, specs, ends = [], [], []
    start = 0
    for x in xs:
        size = x.shape[0] // tm
        if start < tile0 + n_tiles and start + size > tile0:
            arrays.append(x)
            ends.append(start + size)
            specs.append(pl.BlockSpec(
                (tm, x.shape[1]), lambda i, s=start, z=size: (jnp.clip(i + tile0 - s, 0, z - 1), 0)))
        start += size
    return arrays, specs, ends


def _qkv_kernel(*refs, d, scale, rope, n_lat_tiles, ends, n_alias):
    n_x = len(ends)
    x_refs = refs[:n_x]
    mod_ref, g_ref, w_ref = refs[n_x:n_x + 3]
    rest = refs[n_x + 3:]
    if rope:
        rope_ref, wvt_ref, rest = rest[0], rest[1], rest[2:]
    rest = rest[n_alias:]
    q_ref, k_ref, v_ref, kf_ref, vf_ref = rest[:5]
    i = pl.program_id(0)
    x = _read_tokens(x_refs, i, ends)
    h = _rms(x, g_ref[...]) * (1.0 + mod_ref[1:2, :]) + mod_ref[0:1, :]
    hb = h.astype(BF16)

    def rot(t):
        if not rope:
            return t
        cols = []
        for j in range(d // LANES):
            tb = t[:, j * LANES:(j + 1) * LANES]
            cols.append(tb * rope_ref[0]
                        + pltpu.roll(tb, LANES - HEAD_DIM // 2, axis=1) * rope_ref[1]
                        + pltpu.roll(tb, HEAD_DIM // 2, axis=1) * rope_ref[2])
        return jnp.concatenate(cols, axis=1)

    q = _dot(hb, w_ref[:, 0:d])
    q_ref[...] = (rot(q) * scale).astype(BF16)
    k = _dot(hb, w_ref[:, d:2 * d])
    k_ref[...] = rot(k).astype(BF16)
    if rope:
        @pl.when(i < n_lat_tiles)
        def _():
            rest[5][...] = _dot_nt(wvt_ref[...], hb).astype(BF16)

    def context_values():
        v = _dot(hb, w_ref[:, 2 * d:3 * d])
        v_ref[...] = v.astype(BF16)
        return v

    if not rope:
        v = context_values()

    @pl.when(i >= n_lat_tiles)
    def _():
        kf_ref[...] = k.reshape(kf_ref.shape)
        vf_ref[...] = (context_values() if rope else v).reshape(vf_ref.shape)


def _qkv(xs, mod, g, w, rope_tab, caches, *, layer, n_lat, batch, seq, lat_seq, n_lat_batch, tm,
         scale):
    d = xs[0].shape[1]
    depth = w.shape[0]
    n = n_lat + batch * seq
    n_lat_tiles = n_lat // tm
    rope = rope_tab is not None
    assert tm % seq == 0
    x_arrays, x_specs, ends = _token_parts(xs, tm, 0, n // tm)

    def mod_idx(i):
        return (layer, jnp.minimum(i * tm // lat_seq, n_lat_batch), 0, 0)

    in_specs = x_specs + [
        pl.BlockSpec((None, None, N_MOD, d), mod_idx),
        pl.BlockSpec((None, 1, d), lambda i: (layer, 0, 0)),
        pl.BlockSpec((None, d, 3 * d), lambda i: (layer, 0, 0)),
    ]
    args = x_arrays + [mod, g, w]
    if rope:
        seq_tiles = lat_seq // tm
        in_specs.append(pl.BlockSpec(
            (3, tm, LANES),
            lambda i: (0, jnp.where(i < n_lat_tiles, i % seq_tiles, seq_tiles), 0)))
        in_specs.append(pl.BlockSpec((d, d), lambda i: (0, 0)))
        args += [rope_tab, w[layer, :, 2 * d:].T]
    aliases = {}
    if caches is not None:
        aliases = {len(args): 3, len(args) + 1: 4}
        in_specs += [pl.BlockSpec(memory_space=pl.ANY)] * 2
        args += list(caches)
    tok = pl.BlockSpec((tm, d), lambda i: (i, 0))
    cache_spec = pl.BlockSpec((tm // seq, None, seq, d),
                              lambda i: (jnp.maximum(i - n_lat_tiles, 0), layer, 0, 0))
    out_specs = [tok, tok, tok, cache_spec, cache_spec]
    out_shape = ([jax.ShapeDtypeStruct((n, d), BF16)] * 3
                 + [jax.ShapeDtypeStruct((batch, depth, seq, d), F32)] * 2)
    if rope:
        out_specs.append(pl.BlockSpec((None, d, tm),
                                      lambda i: (jnp.minimum(i, n_lat_tiles - 1), 0, 0)))
        out_shape.append(jax.ShapeDtypeStruct((n_lat_tiles, d, tm), BF16))
    return pl.pallas_call(
        functools.partial(_qkv_kernel, d=d, scale=scale, rope=rope, n_lat_tiles=n_lat_tiles,
                          ends=tuple(ends), n_alias=len(aliases)),
        grid=(n // tm,),
        in_specs=in_specs,
        out_specs=out_specs,
        out_shape=out_shape,
        input_output_aliases=aliases,
        compiler_params=_cparams("arbitrary"),
        name=f"qkv{layer}",
    )(*args)


def _rope_tables(lat_seq, tm):
    n_freq = HEAD_DIM // 4
    inv = ROPE_BASE ** (-jnp.arange(n_freq, dtype=F32) / n_freq)
    pos = jnp.arange(lat_seq)
    row = (pos // GRID_W).astype(F32)
    col = (pos % GRID_W).astype(F32)
    ang = jnp.concatenate([row[:, None] * inv, col[:, None] * inv], axis=-1)
    reps = LANES // (HEAD_DIM // 2)
    cos = jnp.tile(jnp.cos(ang), (1, reps))
    sin = jnp.tile(jnp.sin(ang), (1, reps))
    first_half = (jnp.arange(LANES) % HEAD_DIM) < HEAD_DIM // 2
    s_next = jnp.where(first_half, -sin, 0.0)
    s_prev = jnp.where(first_half, 0.0, sin)
    ident = jnp.stack([jnp.ones((tm, LANES), F32), jnp.zeros((tm, LANES), F32),
                       jnp.zeros((tm, LANES), F32)])
    return jnp.concatenate([jnp.stack([cos, s_next, s_prev]), ident], axis=1)


def _lane_lo():
    return lax.broadcasted_iota(jnp.int32, (1, LANES), 1) < HEAD_DIM


def _ctx_dense_kernel(q_ref, k_ref, v_ref, o_all_ref, o_ref, *, d):
    del o_all_ref
    lo = _lane_lo()
    cols = []
    for j in range(d // LANES):
        blk = slice(j * LANES, (j + 1) * LANES)
        q2, k2, v2 = q_ref[:, blk], k_ref[:, blk], v_ref[:, blk]
        outs = []
        for qm in (jnp.where(lo, q2, 0), jnp.where(lo, 0, q2)):
            s = _dot_nt(qm, k2)
            p = jnp.exp2(s - jnp.max(s, axis=-1, keepdims=True))
            outs.append(_dot(p.astype(BF16), v2) / jnp.sum(p, axis=-1, keepdims=True))
        cols.append(jnp.where(lo, outs[0], outs[1]).astype(BF16))
    o_ref[...] = jnp.concatenate(cols, axis=1)


def _ctx_dense(q, k, v, o_all, *, n_lat, seq):
    n, d = q.shape
    off = n_lat // seq
    spec = pl.BlockSpec((seq, d), lambda b: (off + b, 0))
    return pl.pallas_call(
        functools.partial(_ctx_dense_kernel, d=d),
        grid=((n - n_lat) // seq,),
        in_specs=[spec, spec, spec, pl.BlockSpec(memory_space=pl.ANY)],
        out_specs=spec,
        out_shape=jax.ShapeDtypeStruct((n, d), BF16),
        input_output_aliases={3: 0},
        compiler_params=_cparams("arbitrary"),
        name="ctx_dense_attn",
    )(q, k, v, o_all)


def _subln(o, g_ref, out_scale):
    return _rms(o, g_ref[...]) * out_scale


def _ctx_diff_kernel(lam_ref, q_ref, k_ref, v_ref, g_ref, o_all_ref, o_ref, *, d, out_scale):
    del o_all_ref
    lo = _lane_lo()
    lam = lam_ref[0]
    cols = []
    for j in range(d // LANES):
        blk = slice(j * LANES, (j + 1) * LANES)
        q2, k2, v2 = q_ref[:, blk], k_ref[:, blk], v_ref[:, blk]
        ps = []
        for qm in (jnp.where(lo, q2, 0), jnp.where(lo, 0, q2)):
            s = _dot_nt(qm, k2)
            p = jnp.exp2(s - jnp.max(s, axis=-1, keepdims=True))
            ps.append(p / jnp.sum(p, axis=-1, keepdims=True))
        o = _dot((ps[0] - lam * ps[1]).astype(BF16), v2)
        cols.append(_subln(o, g_ref, out_scale).astype(BF16))
    o_ref[...] = jnp.concatenate(cols, axis=1)


def _ctx_diff(lam, q, k, v, subg, o_all, *, n_lat, seq, out_scale):
    n, d = q.shape
    off = n_lat // seq
    spec = pl.BlockSpec((seq, d), lambda b: (off + b, 0))
    return pl.pallas_call(
        functools.partial(_ctx_diff_kernel, d=d, out_scale=out_scale),
        grid=((n - n_lat) // seq,),
        in_specs=[pl.BlockSpec(memory_space=pltpu.SMEM), spec, spec, spec,
                  pl.BlockSpec((1, LANES), lambda b: (0, 0)), pl.BlockSpec(memory_space=pl.ANY)],
        out_specs=spec,
        out_shape=jax.ShapeDtypeStruct((n, d), BF16),
        input_output_aliases={5: 0},
        compiler_params=_cparams("arbitrary"),
        name="ctx_diff_attn",
    )(lam, q, k, v, subg, o_all)


def _na_windows(rows):
    starts, classes, geoms = [], [], []
    for r0 in range(0, rows, NA_QROWS):
        first = [min(max(r0 + i - NA_ROWS // 2, 0), rows - NA_ROWS) for i in range(NA_QROWS)]
        ws = min(first[0], rows - NA_WIN)
        geom = tuple(
            tuple((ws + u) - (r0 + i) + NA_ROWS - 1 if first[i] <= ws + u < first[i] + NA_ROWS else -1
                  for u in range(NA_WIN)) for i in range(NA_QROWS))
        if geom not in geoms:
            geoms.append(geom)
        starts.append(ws)
        classes.append(geoms.index(geom))
    return starts, classes, geoms


def _na_kernel(ws_ref, cls_ref, q_ref, k_ref, v_ref, ck_ref, cv_ref, bias_ref, o_ref, *, d):
    del cls_ref
    lo = _lane_lo()
    win = NA_WIN * GRID_W
    start = pl.multiple_of(ws_ref[pl.program_id(1)] * GRID_W, GRID_W)
    n_heads = 2 * (d // LANES)

    def blk(h):
        return slice(h // 2 * LANES, (h // 2 + 1) * LANES)

    scores = []
    for h in range(n_heads):
        q2 = q_ref[:, blk(h)]
        qm = jnp.where(lo, q2, 0) if h % 2 == 0 else jnp.where(lo, 0, q2)
        scores.append((_dot_nt(qm, k_ref[pl.ds(start, win), blk(h)]) + bias_ref[h],
                       _dot_nt(qm, ck_ref[:, blk(h)].astype(BF16))))
    probs = []
    for s_loc, s_ctx in scores:
        m = jnp.maximum(jnp.max(s_loc, axis=-1, keepdims=True),
                        jnp.max(s_ctx, axis=-1, keepdims=True))
        p_loc = jnp.exp2(s_loc - m)
        p_ctx = jnp.exp2(s_ctx - m)
        l = jnp.sum(p_loc, axis=-1, keepdims=True) + jnp.sum(p_ctx, axis=-1, keepdims=True)
        probs.append((p_loc.astype(BF16), p_ctx.astype(BF16), l))
    outs = [(_dot(p_loc, v_ref[pl.ds(start, win), blk(h)])
             + _dot(p_ctx, cv_ref[:, blk(h)].astype(BF16))) / l
            for h, (p_loc, p_ctx, l) in enumerate(probs)]
    o_ref[...] = jnp.concatenate(
        [jnp.where(lo, outs[h], outs[h + 1]).astype(BF16) for h in range(0, n_heads, 2)], axis=1)


def _na_bias_table(rpb, geoms):
    h = rpb.shape[0]
    off_r = jnp.asarray(geoms, jnp.int32)
    c = jnp.arange(GRID_W)
    kc = jnp.arange(GRID_W)
    c0 = jnp.clip(c - NA_COLS // 2, 0, GRID_W - NA_COLS)
    off_c = kc[None, :] - c[:, None] + (NA_COLS - 1)
    inside = (kc[None, :] >= c0[:, None]) & (kc[None, :] < c0[:, None] + NA_COLS)
    oh_r = jax.nn.one_hot(off_r, 2 * NA_ROWS - 1, dtype=F32)
    oh_c = jax.nn.one_hot(off_c, 2 * NA_COLS - 1, dtype=F32)
    t = jnp.einsum('hrs,ziur,cqs->zhicuq', rpb.astype(F32), oh_r, oh_c,
                   precision=lax.Precision.HIGHEST)
    keep = (off_r >= 0)[:, None, :, None, :, None] & inside[None, None, None, :, None, :]
    t = jnp.where(keep, t * LOG2E, MASKED)
    return t.reshape(len(geoms), h, NA_QROWS * GRID_W, NA_WIN * GRID_W)


def _na_attention(q, k, v, ck, cv, rpb, *, layer, n_lat_batch, lat_seq):
    n, d = q.shape
    rows = lat_seq // GRID_W
    past = ck.shape[2]
    n_heads = rpb.shape[0]
    assert rows % NA_QROWS == 0 and rows >= NA_WIN
    starts, classes, geoms = _na_windows(rows)
    bias = _na_bias_table(rpb, geoms)
    groups = rows // NA_QROWS
    tq = NA_QROWS * GRID_W
    kv_spec = pl.BlockSpec((lat_seq, d), lambda b, r, ws, cls: (b, 0))
    c_spec = pl.BlockSpec((None, None, past, d), lambda b, r, ws, cls: (b, layer, 0, 0))
    q_spec = pl.BlockSpec((tq, d), lambda b, r, ws, cls: (b * groups + r, 0))
    return pl.pallas_call(
        functools.partial(_na_kernel, d=d),
        grid_spec=pltpu.PrefetchScalarGridSpec(
            num_scalar_prefetch=2,
            grid=(n_lat_batch, groups),
            in_specs=[q_spec, kv_spec, kv_spec, c_spec, c_spec,
                      pl.BlockSpec((None, n_heads, tq, NA_WIN * GRID_W),
                                   lambda b, r, ws, cls: (cls[r], 0, 0, 0))],
            out_specs=q_spec),
        out_shape=jax.ShapeDtypeStruct((n, d), BF16),
        compiler_params=_cparams("arbitrary", "arbitrary"),
        name="na_attn",
    )(jnp.asarray(starts, jnp.int32), jnp.asarray(classes, jnp.int32), q, k, v, ck, cv, bias)


def _lat_diff_kernel(lam_ref, q_ref, k_ref, vt_ref, ck_ref, cv_ref, g_ref, o_ref, *, out_scale):
    lo = _lane_lo()
    lam = lam_ref[0]
    q2 = q_ref[...]
    tq = q2.shape[0]
    tg = tq // QUERY_GROUPS
    qs = []
    for g in range(QUERY_GROUPS):
        qg = q2[g * tg:(g + 1) * tg, :]
        qs += [jnp.where(lo, qg, 0), jnp.where(lo, 0, qg)]

    def softmax_step(state, s):
        m, l, acc = state
        m_new = jnp.maximum(m, jnp.max(s, axis=0, keepdims=True))
        alpha = jnp.exp2(m - m_new)
        p = jnp.exp2(s - m_new)
        return m_new, alpha * l + jnp.sum(p, axis=0, keepdims=True), alpha * acc, p.astype(BF16)

    def scores(kt):
        return [_dot_nt(kt, qm) for qm in qs]

    def absorb(states, ss, vt):
        out = []
        for g in range(0, len(qs), 2):
            stepped = [softmax_step(states[g + j], ss[g + j]) for j in range(2)]
            out += [(m, l, acc + _dot(vt, p)) for m, l, acc, p in stepped]
        return out

    tile = vt_ref.shape[2]
    n_tiles = vt_ref.shape[0]

    def keys(t):
        return k_ref[t * tile:(t + 1) * tile, :] if t < n_tiles else ck_ref[...].astype(BF16)

    def values_t(t):
        return vt_ref[t] if t < n_tiles else cv_ref[...].T.astype(BF16)

    init = (jnp.full((1, tg), -jnp.inf, F32), jnp.zeros((1, tg), F32), jnp.zeros((LANES, tg), F32))
    states = [init] * len(qs)
    ss = scores(keys(0))
    for t in range(n_tiles + 1):
        ss_next = scores(keys(t + 1)) if t < n_tiles else None
        states = absorb(states, ss, values_t(t))
        ss = ss_next
    o = jnp.concatenate([states[g][2] / states[g][1] - lam * (states[g + 1][2] / states[g + 1][1])
                         for g in range(0, len(qs), 2)], axis=1).T
    o_ref[...] = _subln(o, g_ref, out_scale).astype(BF16)


def _lat_diff(lam, q, k, vt, ck, cv, subg, *, layer, n_lat_batch, lat_seq, out_scale):
    n, d = q.shape
    past = ck.shape[2]
    tk = vt.shape[2]
    tq = _pick_tile(lat_seq, QUERY_GROUPS * 256)
    qt = lat_seq // tq
    kt = lat_seq // tk
    c_spec = pl.BlockSpec((None, None, past, LANES), lambda b, h, i: (b, layer, 0, h))
    q_spec = pl.BlockSpec((tq, LANES), lambda b, h, i: (b * qt + i, h))
    return pl.pallas_call(
        functools.partial(_lat_diff_kernel, out_scale=out_scale),
        grid=(n_lat_batch, d // LANES, qt),
        in_specs=[pl.BlockSpec(memory_space=pltpu.SMEM), q_spec,
                  pl.BlockSpec((lat_seq, LANES), lambda b, h, i: (b, h)),
                  pl.BlockSpec((kt, LANES, tk), lambda b, h, i: (b, h, 0)),
                  c_spec, c_spec, pl.BlockSpec((1, LANES), lambda b, h, i: (0, 0))],
        out_specs=q_spec,
        out_shape=jax.ShapeDtypeStruct((n, d), BF16),
        compiler_params=_cparams("arbitrary", "arbitrary", "arbitrary"),
        name="lat_diff_attn",
    )(lam, q, k, vt, ck, cv, subg)


def _post_attn_kernel(o_ref, *refs, ends, tile0):
    n_x = len(ends)
    x_refs = refs[:n_x]
    (mod_ref, wo_ref, g_ref, rwh_ref, rwl_ref, rb_ref, tri_ref,
     xmid_ref, h_ref, sel_ref, gate_ref, rank_ref, cnt_ref, carry_ref) = refs[n_x:]
    i = pl.program_id(0)

    @pl.when(i == 0)
    def _():
        carry_ref[...] = jnp.zeros_like(carry_ref)

    x1 = _read_tokens(x_refs, i + tile0, ends) + mod_ref[2:3, :] * _dot(o_ref[...], wo_ref[...])
    xmid_ref[...] = x1
    h = _rms(x1, g_ref[...]) * (1.0 + mod_ref[4:5, :]) + mod_ref[3:4, :]
    hb = h.astype(BF16)
    h_ref[...] = _pack_rows(hb)
    h_lo = (h - hb.astype(F32)).astype(BF16)
    logits = _dot_nt(rwh_ref[...], hb) + (_dot_nt(rwl_ref[...], hb) + _dot_nt(rwh_ref[...], h_lo))
    scores = _sigmoid(logits)
    n_exp, tm = scores.shape
    biased = scores + rb_ref[...]
    row = lax.broadcasted_iota(jnp.int32, (n_exp, tm), 0).astype(F32)
    total = jnp.zeros((n_exp, tm), F32)
    sels, gates = [], []
    for _ in range(TOP_K):
        m = jnp.max(biased, axis=0, keepdims=True)
        idx = jnp.min(jnp.where(biased == m, row, float(n_exp)), axis=0, keepdims=True)
        hit = row == idx
        gates.append(jnp.sum(jnp.where(hit, scores, 0.0), axis=0, keepdims=True))
        biased = jnp.where(hit, -jnp.inf, biased)
        total = total + jnp.where(hit, 1.0, 0.0)
        sels.append(idx)
    denom = gates[0]
    for g in gates[1:]:
        denom = denom + g
    before = _dot(total.astype(BF16), tri_ref[...]) + carry_ref[:, 0:1]
    for k in range(TOP_K):
        sel_ref[k:k + 1, :] = sels[k].astype(jnp.int32)
        gate_ref[k:k + 1, :] = gates[k] / denom * ROUTED_SCALE
        rank_ref[k:k + 1, :] = jnp.sum(jnp.where(row == sels[k], before, 0.0), axis=0,
                                       keepdims=True).astype(jnp.int32)
    carry_ref[...] = carry_ref[...] + jnp.sum(total, axis=1, keepdims=True)
    cnt_ref[...] = carry_ref[...].astype(jnp.int32)


def _post_attn(o, xs, mod, wo, g, rwh, rwl, rb, *, layer, tile0, n_tiles, lat_seq, n_lat_batch, tm):
    d = o.shape[1]
    n = n_tiles * tm
    n_exp = rwh.shape[0]
    tri = (jnp.arange(tm)[:, None] < jnp.arange(tm)[None, :]).astype(BF16)
    x_arrays, x_specs, ends = _token_parts(xs, tm, tile0, n_tiles)

    def mod_idx(i):
        return (layer, jnp.minimum((i + tile0) * tm // lat_seq, n_lat_batch), 0, 0)

    tok = pl.BlockSpec((tm, d), lambda i: (i, 0))
    tok_p = pl.BlockSpec((tm, d // 2), lambda i: (i, 0))
    kt = pl.BlockSpec((TOP_K, tm), lambda i: (0, i))
    const2 = lambda i: (0, 0)
    return pl.pallas_call(
        functools.partial(_post_attn_kernel, ends=tuple(ends), tile0=tile0),
        grid=(n_tiles,),
        in_specs=[pl.BlockSpec((tm, d), lambda i: (i + tile0, 0))] + x_specs + [
                  pl.BlockSpec((None, None, N_MOD, d), mod_idx),
                  pl.BlockSpec((d, d), const2), pl.BlockSpec((1, d), const2),
                  pl.BlockSpec((n_exp, d), const2), pl.BlockSpec((n_exp, d), const2),
                  pl.BlockSpec((n_exp, 1), const2), pl.BlockSpec((tm, tm), const2)],
        out_specs=[tok, tok_p, kt, kt, kt, pl.BlockSpec((n_exp, LANES), const2)],
        out_shape=[jax.ShapeDtypeStruct((n, d), F32), jax.ShapeDtypeStruct((n, d // 2), jnp.int32),
                   jax.ShapeDtypeStruct((TOP_K, n), jnp.int32),
                   jax.ShapeDtypeStruct((TOP_K, n), F32),
                   jax.ShapeDtypeStruct((TOP_K, n), jnp.int32),
                   jax.ShapeDtypeStruct((n_exp, LANES), jnp.int32)],
        scratch_shapes=[pltpu.VMEM((n_exp, LANES), F32)],
        compiler_params=_cparams("arbitrary"),
        name=f"post_attn{layer}",
    )(o, *x_arrays, mod, wo, g, rwh, rwl, rb, tri)


def _plan_kernel(cnt_ref, sel_ref, rank_ref, dest_ref, te_ref, tv_ref, nu_ref, *, tmb):
    n_exp = cnt_ref.shape[0]
    sel = sel_ref[...]
    tile_start = lax.broadcasted_iota(jnp.int32, te_ref.shape, 1) * tmb
    dest = rank_ref[...]
    te = jnp.zeros(te_ref.shape, jnp.int32)
    tv = jnp.zeros(te_ref.shape, jnp.int32)
    start = jnp.zeros((1, 1), jnp.int32)
    for e in range(n_exp):
        dest = dest + jnp.where(sel == e, start, 0)
        cnt = cnt_ref[e:e + 1, 0:1]
        stop = start + ((cnt + (tmb - 1)) & -tmb)
        mine = (tile_start >= start) & (tile_start < stop)
        tv = tv + jnp.where(mine, jnp.clip(start + cnt - tile_start, 0, tmb), 0)
        start = stop
        te = te + jnp.where(start <= tile_start, 1, 0)
    te_ref[...] = jnp.minimum(te, n_exp - 1)
    tv_ref[...] = tv
    nu_ref[...] = jnp.broadcast_to(start, nu_ref.shape)
    for c in range(dest_ref.shape[0]):
        dest_ref[c] = dest[:, c * SC_CHUNK:(c + 1) * SC_CHUNK]


def _plan(counts, sel, rank, *, tmb, n_tiles):
    k_top, n = sel.shape
    assert tmb & (tmb - 1) == 0
    tn = _pick_tile(n, 16 * SC_CHUNK)
    te_w = -(-n_tiles // LANES) * LANES
    tok = pl.BlockSpec((k_top, tn), lambda i: (0, i))
    const2 = lambda i: (0, 0)
    dest, te, tv, nu = pl.pallas_call(
        functools.partial(_plan_kernel, tmb=tmb),
        grid=(n // tn,),
        in_specs=[pl.BlockSpec(counts.shape, const2), tok, tok],
        out_specs=[pl.BlockSpec((tn // SC_CHUNK, k_top, SC_CHUNK), lambda i: (i, 0, 0)),
                   pl.BlockSpec((1, te_w), const2), pl.BlockSpec((1, te_w), const2),
                   pl.BlockSpec((1, LANES), const2)],
        out_shape=[jax.ShapeDtypeStruct((n // SC_CHUNK, k_top, SC_CHUNK), jnp.int32),
                   jax.ShapeDtypeStruct((1, te_w), jnp.int32),
                   jax.ShapeDtypeStruct((1, te_w), jnp.int32),
                   jax.ShapeDtypeStruct((1, LANES), jnp.int32)],
        compiler_params=_cparams("arbitrary"),
        name="plan",
    )(counts, sel, rank)
    return dest, te[0, :n_tiles], tv[0, :n_tiles], nu[0, :1] // tmb


def _expert_kernel(te_ref, tv_ref, nu_ref, x_ref, wg_ref, wu_ref, wd_ref, y_ref, wg_b, wu_b, wd_b):
    i = pl.program_id(0)
    half = x_ref.shape[0] // 2

    def rows(r):
        return jnp.concatenate(_unpack_rows(x_ref[r * half:(r + 1) * half, :]), axis=1).astype(BF16)

    def swiglu(xs):
        gu = [(_dot(x, wg_b[...]), _dot(x, wu_b[...])) for x in xs]
        acts = [(g * _sigmoid(g) * u).astype(BF16) for g, u in gu]
        return [_pack_rows(_dot(a, wd_b[...]).astype(BF16)) for a in acts]

    @pl.when(i < nu_ref[0])
    def _():
        @pl.when((i == 0) | (te_ref[i] != te_ref[jnp.maximum(i - 1, 0)]))
        def _():
            wg_b[...] = wg_ref[...].astype(BF16)
            wu_b[...] = wu_ref[...].astype(BF16)
            wd_b[...] = wd_ref[...].astype(BF16)

        @pl.when(tv_ref[i] > half)
        def _():
            ya, yb = swiglu([rows(0), rows(1)])
            y_ref[0:half, :] = ya
            y_ref[half:2 * half, :] = yb

        @pl.when(tv_ref[i] <= half)
        def _():
            y_ref[0:half, :] = swiglu([rows(0)])[0]


def _experts(tile_expert, tile_rows, n_used, xb, wg, wu, wd, *, layer, tmb):
    cap, dp = xb.shape
    f, d = wd.shape[2:]

    def row_idx(i, te, tv, nu):
        return (jnp.minimum(i, nu[0] - 1), 0)

    def w_idx(i, te, tv, nu):
        return (layer, te[i], 0, 0)

    return pl.pallas_call(
        _expert_kernel,
        grid_spec=pltpu.PrefetchScalarGridSpec(
            num_scalar_prefetch=3,
            grid=(cap // tmb,),
            in_specs=[pl.BlockSpec((tmb, dp), row_idx),
                      pl.BlockSpec((None, None, d, f), w_idx),
                      pl.BlockSpec((None, None, d, f), w_idx),
                      pl.BlockSpec((None, None, f, d), w_idx)],
            out_specs=pl.BlockSpec((tmb, dp), row_idx),
            scratch_shapes=[pltpu.VMEM((d, f), BF16), pltpu.VMEM((d, f), BF16),
                            pltpu.VMEM((f, d), BF16)]),
        out_shape=jax.ShapeDtypeStruct((cap, dp), jnp.int32),
        compiler_params=_cparams("arbitrary"),
        name="experts",
    )(tile_expert, tile_rows, n_used, xb, wg, wu, wd)


def _combine_kernel(x_ref, h_ref, routed_ref, mod_ref, wgu_ref, wd_ref, fg_ref, *refs, f, final,
                    n_alias, tile0, n_lat_tiles):
    o_refs = refs[n_alias:]
    h = jnp.concatenate(_unpack_rows(h_ref[...]), axis=1).astype(BF16)
    gu = _dot(h, wgu_ref[...])
    gate, up = gu[:, :f], gu[:, f:]
    shared = _dot((gate * _sigmoid(gate) * up).astype(BF16), wd_ref[...])
    x2 = x_ref[...] + mod_ref[5:6, :] * (shared + routed_ref[...])
    if not final:
        o_refs[0][...] = x2
    elif len(o_refs) == 1:
        o_refs[0][...] = _rms(x2, fg_ref[...])
    else:
        y = _rms(x2, fg_ref[...])
        tile = pl.program_id(0) + tile0

        @pl.when(tile < n_lat_tiles)
        def _():
            o_refs[0][...] = y

        @pl.when(tile >= n_lat_tiles)
        def _():
            o_refs[1][...] = y


def _combine(x, h, routed, mod, wgu, wd, final_g, y_lat, *, layer, tile0, n_lat, n_ctx, lat_seq,
             n_lat_batch, tm, final):
    n, d = x.shape
    f = wd.shape[0]
    n_tiles = n // tm
    n_lat_tiles = n_lat // tm

    def mod_idx(i):
        return (layer, jnp.minimum((i + tile0) * tm // lat_seq, n_lat_batch), 0, 0)

    tok = pl.BlockSpec((tm, d), lambda i: (i, 0))
    tok_p = pl.BlockSpec((tm, d // 2), lambda i: (i, 0))
    const2 = lambda i: (0, 0)
    in_specs = [tok, tok_p, tok,
                pl.BlockSpec((None, None, N_MOD, d), mod_idx),
                pl.BlockSpec((d, 2 * f), const2), pl.BlockSpec((f, d), const2),
                pl.BlockSpec((1, d), const2)]
    args = [x, h, routed, mod, wgu, wd, final_g]
    aliases = {}
    has_lat = has_ctx = False
    if not final:
        out_specs, out_shape = [tok], [jax.ShapeDtypeStruct((n, d), F32)]
    else:
        has_lat = tile0 < n_lat_tiles
        has_ctx = tile0 + n_tiles > n_lat_tiles
        out_specs, out_shape = [], []
        if has_lat:
            out_specs.append(pl.BlockSpec(
                (tm, d), lambda i: (jnp.minimum(i + tile0, n_lat_tiles - 1), 0)))
            out_shape.append(jax.ShapeDtypeStruct((n_lat, d), F32))
            if y_lat is not None:
                aliases = {len(args): 0}
                in_specs.append(pl.BlockSpec(memory_space=pl.ANY))
                args.append(y_lat)
        if has_ctx:
            out_specs.append(pl.BlockSpec(
                (tm, d), lambda i: (jnp.maximum(i + tile0 - n_lat_tiles, 0), 0)))
            out_shape.append(jax.ShapeDtypeStruct((n_ctx, d), F32))
    outs = pl.pallas_call(
        functools.partial(_combine_kernel, f=f, final=final, n_alias=len(aliases), tile0=tile0,
                          n_lat_tiles=n_lat_tiles),
        grid=(n_tiles,),
        in_specs=in_specs,
        out_specs=out_specs,
        out_shape=out_shape,
        input_output_aliases=aliases,
        compiler_params=_cparams("arbitrary"),
        name=f"combine{layer}",
    )(*args)
    if not final:
        return outs[0]
    return (outs[0] if has_lat else y_lat), (outs[-1] if has_ctx else None)


SC_CHUNK = 128


def _sc_workers():
    info = plsc.get_sparse_core_info()
    return info.num_cores, info.num_subcores


def _sc_dispatch(rows, dest, cap):
    n, w = rows.shape
    nc, ns = _sc_workers()
    per_w = n // (nc * ns)
    assert per_w * nc * ns == n and per_w % SC_CHUNK == 0
    mesh = plsc.VectorSubcoreMesh(core_axis_name="c", subcore_axis_name="s")

    @functools.partial(
        pl.kernel, mesh=mesh, out_type=jax.ShapeDtypeStruct((cap, w), rows.dtype),
        scratch_types=[pltpu.VMEM((TOP_K, SC_CHUNK), jnp.int32),
                       pltpu.VMEM((SC_CHUNK, w), rows.dtype)],
        name="sc_dispatch")
    def run(rows_hbm, dest_hbm, out_hbm, idx_v, rows_v):
        wid = lax.axis_index("s") * nc + lax.axis_index("c")

        @pl.loop(0, per_w // SC_CHUNK)
        def _(ci):
            base = pl.multiple_of(wid * per_w + ci * SC_CHUNK, SC_CHUNK)
            pltpu.sync_copy(dest_hbm.at[wid * (per_w // SC_CHUNK) + ci], idx_v)
            pltpu.sync_copy(rows_hbm.at[pl.ds(base, SC_CHUNK)], rows_v)
            for k in range(TOP_K):
                pltpu.sync_copy(rows_v, out_hbm.at[idx_v.at[k]])

    return run(rows, dest)


SC_TOK = 8
SC_LANES = 16


def _sc_combine(table, dest, gates):
    n_chunks, k_top, _ = dest.shape
    n = n_chunks * SC_CHUNK
    w = table.shape[1]
    nc, ns = _sc_workers()
    per_w = n // (nc * ns)
    assert per_w * nc * ns == n and per_w % SC_CHUNK == 0 and w % (16 * SC_LANES) == 0
    gb = jnp.broadcast_to(
        gates.reshape(k_top, n_chunks, SC_CHUNK).transpose(1, 0, 2)[..., None],
        (n_chunks, k_top, SC_CHUNK, SC_LANES)).reshape(n_chunks, k_top, SC_CHUNK // SC_TOK,
                                                        SC_TOK * SC_LANES)
    mesh = plsc.VectorSubcoreMesh(core_axis_name="c", subcore_axis_name="s")

    n_blocks = SC_CHUNK // SC_TOK

    @functools.partial(
        pl.kernel, mesh=mesh, out_type=jax.ShapeDtypeStruct((n, 2 * w), F32),
        scratch_types=[pltpu.VMEM((k_top, SC_CHUNK), jnp.int32),
                       pltpu.VMEM((k_top, n_blocks, SC_TOK * SC_LANES), F32),
                       pltpu.VMEM((2, k_top, SC_TOK, w), table.dtype),
                       pltpu.VMEM((2, SC_TOK, 2 * w), F32),
                       pltpu.SemaphoreType.DMA((2,)), pltpu.SemaphoreType.DMA((2,))],
        compiler_params=pltpu.CompilerParams(needs_layout_passes=False),
        name="sc_combine")
    def run(table_hbm, dest_hbm, gb_hbm, out_hbm, idx_v, g_v, rows_v, out_v, sem_in, sem_out):
        wid = lax.axis_index("s") * nc + lax.axis_index("c")

        def gathers(blk, slot):
            off = pl.multiple_of(blk * SC_TOK, SC_TOK)
            return [pltpu.make_async_copy(table_hbm.at[idx_v.at[k, pl.ds(off, SC_TOK)]],
                                          rows_v.at[slot, k], sem_in.at[slot]) for k in range(k_top)]

        def write_back(chunk, blk, slot):
            row0 = pl.multiple_of(chunk * SC_CHUNK + blk * SC_TOK, SC_TOK)
            return pltpu.make_async_copy(out_v.at[slot], out_hbm.at[pl.ds(row0, SC_TOK)],
                                         sem_out.at[slot])

        def weighted_sum(blk, slot):
            @pl.loop(0, SC_TOK)
            def _(j):
                for base in range(0, w, 16 * SC_LANES):
                    acc = None
                    for k in range(k_top):
                        g = g_v[k, blk, pl.ds(j * SC_LANES, SC_LANES)]
                        terms = []
                        for v in range(16):
                            word = rows_v[slot, k, j, pl.ds(base + v * SC_LANES, SC_LANES)]
                            lo = lax.bitcast_convert_type(lax.shift_left(word, 16), F32)
                            hi = lax.bitcast_convert_type(word & jnp.int32(-65536), F32)
                            terms.append((g * lo, g * hi))
                        acc = terms if acc is None else [
                            (a + x, b + y) for (a, b), (x, y) in zip(acc, terms)]
                    for v, (a, b) in enumerate(acc):
                        out_v[slot, j, pl.ds(base + v * SC_LANES, SC_LANES)] = a
                        out_v[slot, j, pl.ds(w + base + v * SC_LANES, SC_LANES)] = b

        @pl.loop(0, per_w // SC_CHUNK)
        def _(ci):
            chunk = wid * (per_w // SC_CHUNK) + ci
            pltpu.sync_copy(dest_hbm.at[chunk], idx_v)
            pltpu.sync_copy(gb_hbm.at[chunk], g_v)
            for cp in gathers(0, 0):
                cp.start()

            @pl.loop(0, n_blocks, step=2)
            def _(b0):
                for slot in range(2):
                    blk = b0 + slot
                    for cp in gathers(blk, slot):
                        cp.wait()

                    @pl.when(blk + 1 < n_blocks)
                    def _():
                        for cp in gathers(blk + 1, 1 - slot):
                            cp.start()

                    @pl.when(blk >= 2)
                    def _():
                        write_back(chunk, blk - 2, slot).wait()

                    weighted_sum(blk, slot)
                    write_back(chunk, blk, slot).start()

            for slot in range(2):
                write_back(chunk, n_blocks - 2 + slot, slot).wait()


    return run(table, dest, gb)


def _split_bf16(w):
    hi = w.astype(BF16)
    return hi, (w - hi.astype(F32)).astype(BF16)


def kernel(x_prompt, x_sample, cache_k, cache_v, c, c_ctx, ada_w, ada_b, norm1_g, norm2_g, w_qkv, w_o, na_rpb, diff_lambda, diff_subln_g, router_w, router_b, exp_w_gate, exp_w_up, exp_w_down, shared_w_gate, shared_w_up, shared_w_down, final_g):
    batch, seq, d = x_prompt.shape
    n_lat_batch, lat_seq, _ = x_sample.shape
    depth = w_qkv.shape[0]
    n_exp = router_w.shape[-1]
    n_lat = n_lat_batch * lat_seq
    n_ctx = batch * seq
    n = n_lat + n_ctx
    assert d % LANES == 0 and na_rpb.shape[1] * HEAD_DIM == d
    assert diff_lambda.shape[-1] == HEAD_DIM and lat_seq % GRID_W == 0
    assert n_lat % seq == 0
    tm = _pick_tile(math.gcd(lat_seq, n_ctx), 512)
    tmb = 1024
    n_all = n // tm
    ranges = [(r * n_all // MOE_RANGES, (r + 1) * n_all // MOE_RANGES - r * n_all // MOE_RANGES)
              for r in range(MOE_RANGES)]
    scale = HEAD_DIM ** -0.5 * LOG2E

    xs = [x_sample.reshape(n_lat, d), x_prompt.reshape(n_ctx, d)]
    mod_rows = -(-(n_lat_batch + 1) // 8) * 8
    cond = jnp.zeros((mod_rows, d), F32).at[:n_lat_batch].set(c).at[n_lat_batch].set(c_ctx)
    mod = _modulation(cond, ada_w, ada_b).reshape(depth, mod_rows, N_MOD, d)

    w_qkv_b = w_qkv.astype(BF16)
    w_o_b = w_o.astype(BF16)
    rope_tab = _rope_tables(lat_seq, tm)
    caches = None
    for i in range(depth):
        is_diff = i % 2 == 1
        j = i // 2
        q, k, v, *caches = _qkv(xs, mod, norm1_g.reshape(depth, 1, d), w_qkv_b,
                                rope_tab if is_diff else None, caches, layer=i, n_lat=n_lat,
                                batch=batch, seq=seq, lat_seq=lat_seq, n_lat_batch=n_lat_batch,
                                tm=tm, scale=scale)
        if is_diff:
            vt = caches.pop()
        if not is_diff:
            o = _na_attention(q, k, v, cache_k, cache_v, na_rpb[j], layer=i,
                              n_lat_batch=n_lat_batch, lat_seq=lat_seq)
            o = _ctx_dense(q, k, v, o, n_lat=n_lat, seq=seq)
        else:
            lam_init = 0.8 - 0.6 * math.exp(-0.3 * i)
            lp = diff_lambda[j].astype(F32)
            lam = (jnp.exp(jnp.sum(lp[0] * lp[1])) - jnp.exp(jnp.sum(lp[2] * lp[3]))
                   + lam_init).reshape(1)
            subg = diff_subln_g[j].reshape(1, LANES).astype(F32)
            o = _lat_diff(lam, q, k, vt, cache_k, cache_v, subg, layer=i,
                          n_lat_batch=n_lat_batch, lat_seq=lat_seq, out_scale=1.0 - lam_init)
            o = _ctx_diff(lam, q, k, v, subg, o, n_lat=n_lat, seq=seq, out_scale=1.0 - lam_init)

        rwh, rwl = _split_bf16(router_w[i].T)
        swgu = jnp.concatenate([shared_w_gate[i], shared_w_up[i]], axis=-1).astype(BF16)
        final = i == depth - 1
        routed = []
        for tile0, nt in ranges:
            xmid, h2, sel, gates, rank, counts = _post_attn(
                o, xs, mod, w_o_b[i], norm2_g[i].reshape(1, d), rwh, rwl,
                router_b[i].reshape(n_exp, 1).astype(F32), layer=i, tile0=tile0, n_tiles=nt,
                lat_seq=lat_seq, n_lat_batch=n_lat_batch, tm=tm)
            n_tiles = -(-(nt * tm * TOP_K) // tmb) + n_exp
            dest, tile_expert, tile_rows, n_used = _plan(counts, sel, rank, tmb=tmb, n_tiles=n_tiles)
            xb = _sc_dispatch(h2, dest, n_tiles * tmb)
            routed.append((xmid, h2, gates, dest, tile_expert, tile_rows, n_used, xb))
        ybs = [_experts(te, tr, nu, xb, exp_w_gate, exp_w_up, exp_w_down, layer=i, tmb=tmb)
               for (_, _, _, _, te, tr, nu, xb) in routed]
        sums = [_sc_combine(yb, r[3], r[2]) for yb, r in zip(ybs, routed)]
        xs, y_lat, y_ctx = [], None, None
        for (tile0, nt), (xmid, h2, *_), rsum in zip(ranges, routed, sums):
            out = _combine(xmid, h2, rsum, mod, swgu, shared_w_down[i].astype(BF16),
                           final_g.reshape(1, d), y_lat, layer=i, tile0=tile0, n_lat=n_lat,
                           n_ctx=n_ctx, lat_seq=lat_seq, n_lat_batch=n_lat_batch, tm=tm, final=final)
            if final:
                y_lat, y_ctx = out[0], (out[1] if out[1] is not None else y_ctx)
            else:
                xs.append(out)

    y_sample = y_lat.reshape(n_lat_batch, lat_seq, d)
    y_prompt = y_ctx.reshape(batch, seq, d)
    return (y_prompt, y_sample, caches[0], caches[1])
```

```python
import functools
import math

import jax
import jax.numpy as jnp
from jax import lax
from jax.experimental import pallas as pl
from jax.experimental.pallas import tpu as pltpu
from jax.experimental.pallas import tpu_sc as plsc

GRID_W = 64
NA_ROWS = 8
NA_COLS = 16
TOP_K = 8
ROUTED_SCALE = 2.5
ROPE_BASE = 10000.0
EPS = 1e-6
N_MOD = 6
HEAD_DIM = 64
LANES = 128
LOG2E = math.log2(math.e)
MASKED = -1e30
NA_QROWS = 2
NA_WIN = NA_ROWS + NA_QROWS - 1
MOE_RANGES = 2
QUERY_GROUPS = 4
VMEM_LIMIT = 56 * 1024 * 1024

F32 = jnp.float32
BF16 = jnp.bfloat16


def _cparams(*sem):
    return pltpu.CompilerParams(dimension_semantics=sem, vmem_limit_bytes=VMEM_LIMIT)


def _dot(a, b):
    return jnp.dot(a, b, preferred_element_type=F32)


def _dot_nt(a, b):
    return lax.dot_general(a, b, (((1,), (1,)), ((), ())), preferred_element_type=F32)


def _sigmoid(x):
    return 1.0 / (1.0 + jnp.exp(-x))


def _rms(x, g):
    return x * lax.rsqrt(jnp.mean(x * x, axis=-1, keepdims=True) + EPS) * g


def _pack_rows(xb):
    half = xb.shape[1] // 2
    u = lax.bitcast_convert_type(xb.astype(F32), jnp.uint32)
    packed = (u[:, :half] >> 16) | (u[:, half:] & jnp.uint32(0xFFFF0000))
    return lax.bitcast_convert_type(packed, jnp.int32)


def _unpack_rows(p):
    u = lax.bitcast_convert_type(p, jnp.uint32)
    return (lax.bitcast_convert_type(u << 16, F32),
            lax.bitcast_convert_type(u & jnp.uint32(0xFFFF0000), F32))


def _pick_tile(n, pref):
    t = pref
    while n % t:
        t //= 2
    return t


def _mod_kernel(c_ref, w_ref, b_ref, o_ref):
    c = c_ref[...]
    o_ref[...] = _dot(c * _sigmoid(c), w_ref[...]) + b_ref[...]


def _modulation(cond, ada_w, ada_b):
    depth, d, n6 = ada_w.shape
    rows = cond.shape[0]
    tn = _pick_tile(n6, 512)
    return pl.pallas_call(
        _mod_kernel,
        grid=(depth, n6 // tn),
        in_specs=[
            pl.BlockSpec((rows, d), lambda l, j: (0, 0)),
            pl.BlockSpec((None, d, tn), lambda l, j: (l, 0, j)),
            pl.BlockSpec((None, 1, tn), lambda l, j: (l, 0, j)),
        ],
        out_specs=pl.BlockSpec((None, rows, tn), lambda l, j: (l, 0, j)),
        out_shape=jax.ShapeDtypeStruct((depth, rows, n6), F32),
        compiler_params=_cparams("arbitrary", "arbitrary"),
        name="modulation",
    )(cond, ada_w, ada_b.reshape(depth, 1, n6))


def _read_tokens(x_refs, g, ends):
    x = x_refs[-1][...]
    for ref, end in zip(reversed(x_refs[:-1]), reversed(ends[:-1])):
        x = jnp.where(g < end, ref[...], x)
    return x


def _token_parts(xs, tm, tile0, n_tiles):
    arrays, specs, ends = [], [], []
    start = 0
    for x in xs:
        size = x.shape[0] // tm
        if start < tile0 + n_tiles and start + size > tile0:
            arrays.append(x)
            ends.append(start + size)
            specs.append(pl.BlockSpec(
                (tm, x.shape[1]), lambda i, s=start, z=size: (jnp.clip(i + tile0 - s, 0, z - 1), 0)))
        start += size
    return arrays, specs, ends


def _qkv_kernel(*refs, d, scale, rope, n_lat_tiles, ends, n_alias):
    n_x = len(ends)
    x_refs = refs[:n_x]
    mod_ref, g_ref, w_ref = refs[n_x:n_x + 3]
    rest = refs[n_x + 3:]
    if rope:
        rope_ref, wvt_ref, rest = rest[0], rest[1], rest[2:]
    rest = rest[n_alias:]
    q_ref, k_ref, v_ref, kf_ref, vf_ref = rest[:5]
    i = pl.program_id(0)
    x = _read_tokens(x_refs, i, ends)
    h = _rms(x, g_ref[...]) * (1.0 + mod_ref[1:2, :]) + mod_ref[0:1, :]
    hb = h.astype(BF16)

    def rot(t):
        if not rope:
            return t
        cols = []
        for j in range(d // LANES):
            tb = t[:, j * LANES:(j + 1) * LANES]
            cols.append(tb * rope_ref[0]
                        + pltpu.roll(tb, LANES - HEAD_DIM // 2, axis=1) * rope_ref[1]
                        + pltpu.roll(tb, HEAD_DIM // 2, axis=1) * rope_ref[2])
        return jnp.concatenate(cols, axis=1)

    q = _dot(hb, w_ref[:, 0:d])
    q_ref[...] = (rot(q) * scale).astype(BF16)
    k = _dot(hb, w_ref[:, d:2 * d])
    k_ref[...] = rot(k).astype(BF16)
    if rope:
        @pl.when(i < n_lat_tiles)
        def _():
            rest[5][...] = _dot_nt(wvt_ref[...], hb).astype(BF16)

    def context_values():
        v = _dot(hb, w_ref[:, 2 * d:3 * d])
        v_ref[...] = v.astype(BF16)
        return v

    if not rope:
        v = context_values()

    @pl.when(i >= n_lat_tiles)
    def _():
        kf_ref[...] = k.reshape(kf_ref.shape)
        vf_ref[...] = (context_values() if rope else v).reshape(vf_ref.shape)


def _qkv(xs, mod, g, w, rope_tab, caches, *, layer, n_lat, batch, seq, lat_seq, n_lat_batch, tm,
         scale):
    d = xs[0].shape[1]
    depth = w.shape[0]
    n = n_lat + batch * seq
    n_lat_tiles = n_lat // tm
    rope = rope_tab is not None
    assert tm % seq == 0
    x_arrays, x_specs, ends = _token_parts(xs, tm, 0, n // tm)

    def mod_idx(i):
        return (layer, jnp.minimum(i * tm // lat_seq, n_lat_batch), 0, 0)

    in_specs = x_specs + [
        pl.BlockSpec((None, None, N_MOD, d), mod_idx),
        pl.BlockSpec((None, 1, d), lambda i: (layer, 0, 0)),
        pl.BlockSpec((None, d, 3 * d), lambda i: (layer, 0, 0)),
    ]
    args = x_arrays + [mod, g, w]
    if rope:
        seq_tiles = lat_seq // tm
        in_specs.append(pl.BlockSpec(
            (3, tm, LANES),
            lambda i: (0, jnp.where(i < n_lat_tiles, i % seq_tiles, seq_tiles), 0)))
        in_specs.append(pl.BlockSpec((d, d), lambda i: (0, 0)))
        args += [rope_tab, w[layer, :, 2 * d:].T]
    aliases = {}
    if caches is not None:
        aliases = {len(args): 3, len(args) + 1: 4}
        in_specs += [pl.BlockSpec(memory_space=pl.ANY)] * 2
        args += list(caches)
    tok = pl.BlockSpec((tm, d), lambda i: (i, 0))
    cache_spec = pl.BlockSpec((tm // seq, None, seq, d),
                              lambda i: (jnp.maximum(i - n_lat_tiles, 0), layer, 0, 0))
    out_specs = [tok, tok, tok, cache_spec, cache_spec]
    out_shape = ([jax.ShapeDtypeStruct((n, d), BF16)] * 3
                 + [jax.ShapeDtypeStruct((batch, depth, seq, d), F32)] * 2)
    if rope:
        out_specs.append(pl.BlockSpec((None, d, tm),
                                      lambda i: (jnp.minimum(i, n_lat_tiles - 1), 0, 0)))
        out_shape.append(jax.ShapeDtypeStruct((n_lat_tiles, d, tm), BF16))
    return pl.pallas_call(
        functools.partial(_qkv_kernel, d=d, scale=scale, rope=rope, n_lat_tiles=n_lat_tiles,
                          ends=tuple(ends), n_alias=len(aliases)),
        grid=(n // tm,),
        in_specs=in_specs,
        out_specs=out_specs,
        out_shape=out_shape,
        input_output_aliases=aliases,
        compiler_params=_cparams("arbitrary"),
        name=f"qkv{layer}",
    )(*args)


def _rope_tables(lat_seq, tm):
    n_freq = HEAD_DIM // 4
    inv = ROPE_BASE ** (-jnp.arange(n_freq, dtype=F32) / n_freq)
    pos = jnp.arange(lat_seq)
    row = (pos // GRID_W).astype(F32)
    col = (pos % GRID_W).astype(F32)
    ang = jnp.concatenate([row[:, None] * inv, col[:, None] * inv], axis=-1)
    reps = LANES // (HEAD_DIM // 2)
    cos = jnp.tile(jnp.cos(ang), (1, reps))
    sin = jnp.tile(jnp.sin(ang), (1, reps))
    first_half = (jnp.arange(LANES) % HEAD_DIM) < HEAD_DIM // 2
    s_next = jnp.where(first_half, -sin, 0.0)
    s_prev = jnp.where(first_half, 0.0, sin)
    ident = jnp.stack([jnp.ones((tm, LANES), F32), jnp.zeros((tm, LANES), F32),
                       jnp.zeros((tm, LANES), F32)])
    return jnp.concatenate([jnp.stack([cos, s_next, s_prev]), ident], axis=1)


def _lane_lo():
    return lax.broadcasted_iota(jnp.int32, (1, LANES), 1) < HEAD_DIM


def _ctx_dense_kernel(q_ref, k_ref, v_ref, o_all_ref, o_ref, *, d):
    del o_all_ref
    lo = _lane_lo()
    cols = []
    for j in range(d // LANES):
        blk = slice(j * LANES, (j + 1) * LANES)
        q2, k2, v2 = q_ref[:, blk], k_ref[:, blk], v_ref[:, blk]
        outs = []
        for qm in (jnp.where(lo, q2, 0), jnp.where(lo, 0, q2)):
            s = _dot_nt(qm, k2)
            p = jnp.exp2(s - jnp.max(s, axis=-1, keepdims=True))
            outs.append(_dot(p.astype(BF16), v2) / jnp.sum(p, axis=-1, keepdims=True))
        cols.append(jnp.where(lo, outs[0], outs[1]).astype(BF16))
    o_ref[...] = jnp.concatenate(cols, axis=1)


def _ctx_dense(q, k, v, o_all, *, n_lat, seq):
    n, d = q.shape
    off = n_lat // seq
    spec = pl.BlockSpec((seq, d), lambda b: (off + b, 0))
    return pl.pallas_call(
        functools.partial(_ctx_dense_kernel, d=d),
        grid=((n - n_lat) // seq,),
        in_specs=[spec, spec, spec, pl.BlockSpec(memory_space=pl.ANY)],
        out_specs=spec,
        out_shape=jax.ShapeDtypeStruct((n, d), BF16),
        input_output_aliases={3: 0},
        compiler_params=_cparams("arbitrary"),
        name="ctx_dense_attn",
    )(q, k, v, o_all)


def _subln(o, g_ref, out_scale):
    return _rms(o, g_ref[...]) * out_scale


def _ctx_diff_kernel(lam_ref, q_ref, k_ref, v_ref, g_ref, o_all_ref, o_ref, *, d, out_scale):
    del o_all_ref
    lo = _lane_lo()
    lam = lam_ref[0]
    cols = []
    for j in range(d // LANES):
        blk = slice(j * LANES, (j + 1) * LANES)
        q2, k2, v2 = q_ref[:, blk], k_ref[:, blk], v_ref[:, blk]
        ps = []
        for qm in (jnp.where(lo, q2, 0), jnp.where(lo, 0, q2)):
            s = _dot_nt(qm, k2)
            p = jnp.exp2(s - jnp.max(s, axis=-1, keepdims=True))
            ps.append(p / jnp.sum(p, axis=-1, keepdims=True))
        o = _dot((ps[0] - lam * ps[1]).astype(BF16), v2)
        cols.append(_subln(o, g_ref, out_scale).astype(BF16))
    o_ref[...] = jnp.concatenate(cols, axis=1)


def _ctx_diff(lam, q, k, v, subg, o_all, *, n_lat, seq, out_scale):
    n, d = q.shape
    off = n_lat // seq
    spec = pl.BlockSpec((seq, d), lambda b: (off + b, 0))
    return pl.pallas_call(
        functools.partial(_ctx_diff_kernel, d=d, out_scale=out_scale),
        grid=((n - n_lat) // seq,),
        in_specs=[pl.BlockSpec(memory_space=pltpu.SMEM), spec, spec, spec,
                  pl.BlockSpec((1, LANES), lambda b: (0, 0)), pl.BlockSpec(memory_space=pl.ANY)],
        out_specs=spec,
        out_shape=jax.ShapeDtypeStruct((n, d), BF16),
        input_output_aliases={5: 0},
        compiler_params=_cparams("arbitrary"),
        name="ctx_diff_attn",
    )(lam, q, k, v, subg, o_all)


def _na_windows(rows):
    starts, classes, geoms = [], [], []
    for r0 in range(0, rows, NA_QROWS):
        first = [min(max(r0 + i - NA_ROWS // 2, 0), rows - NA_ROWS) for i in range(NA_QROWS)]
        ws = min(first[0], rows - NA_WIN)
        geom = tuple(
            tuple((ws + u) - (r0 + i) + NA_ROWS - 1 if first[i] <= ws + u < first[i] + NA_ROWS else -1
                  for u in range(NA_WIN)) for i in range(NA_QROWS))
        if geom not in geoms:
            geoms.append(geom)
        starts.append(ws)
        classes.append(geoms.index(geom))
    return starts, classes, geoms


def _na_kernel(ws_ref, cls_ref, q_ref, k_ref, v_ref, ck_ref, cv_ref, bias_ref, o_ref, *, d):
    del cls_ref
    lo = _lane_lo()
    win = NA_WIN * GRID_W
    start = pl.multiple_of(ws_ref[pl.program_id(1)] * GRID_W, GRID_W)
    n_heads = 2 * (d // LANES)

    def blk(h):
        return slice(h // 2 * LANES, (h // 2 + 1) * LANES)

    scores = []
    for h in range(n_heads):
        q2 = q_ref[:, blk(h)]
        qm = jnp.where(lo, q2, 0) if h % 2 == 0 else jnp.where(lo, 0, q2)
        scores.append((_dot_nt(qm, k_ref[pl.ds(start, win), blk(h)]) + bias_ref[h],
                       _dot_nt(qm, ck_ref[:, blk(h)].astype(BF16))))
    probs = []
    for s_loc, s_ctx in scores:
        m = jnp.maximum(jnp.max(s_loc, axis=-1, keepdims=True),
                        jnp.max(s_ctx, axis=-1, keepdims=True))
        p_loc = jnp.exp2(s_loc - m)
        p_ctx = jnp.exp2(s_ctx - m)
        l = jnp.sum(p_loc, axis=-1, keepdims=True) + jnp.sum(p_ctx, axis=-1, keepdims=True)
        probs.append((p_loc.astype(BF16), p_ctx.astype(BF16), l))
    outs = [(_dot(p_loc, v_ref[pl.ds(start, win), blk(h)])
             + _dot(p_ctx, cv_ref[:, blk(h)].astype(BF16))) / l
            for h, (p_loc, p_ctx, l) in enumerate(probs)]
    o_ref[...] = jnp.concatenate(
        [jnp.where(lo, outs[h], outs[h + 1]).astype(BF16) for h in range(0, n_heads, 2)], axis=1)


def _na_bias_table(rpb, geoms):
    h = rpb.shape[0]
    c = jnp.arange(GRID_W)
    kc = jnp.arange(GRID_W)
    c0 = jnp.clip(c - NA_COLS // 2, 0, GRID_W - NA_COLS)
    off_c = kc[None, :] - c[:, None] + (NA_COLS - 1)
    inside = (kc[None, :] >= c0[:, None]) & (kc[None, :] < c0[:, None] + NA_COLS)
    oh_c = jax.nn.one_hot(off_c, 2 * NA_COLS - 1, dtype=F32)
    blocks = jnp.einsum('hrs,cqs->rhcq', rpb.astype(F32), oh_c, precision=lax.Precision.HIGHEST)
    blocks = jnp.where(inside, blocks * LOG2E, MASKED)
    masked = jnp.full((h, GRID_W, GRID_W), MASKED, F32)
    return jnp.stack([
        jnp.concatenate([
            jnp.concatenate([blocks[r] if r >= 0 else masked for r in row], axis=-1)
            for row in geom], axis=-2)
        for geom in geoms])


def _na_attention(q, k, v, ck, cv, rpb, *, layer, n_lat_batch, lat_seq):
    n, d = q.shape
    rows = lat_seq // GRID_W
    past = ck.shape[2]
    n_heads = rpb.shape[0]
    assert rows % NA_QROWS == 0 and rows >= NA_WIN
    starts, classes, geoms = _na_windows(rows)
    bias = _na_bias_table(rpb, geoms)
    groups = rows // NA_QROWS
    tq = NA_QROWS * GRID_W
    kv_spec = pl.BlockSpec((lat_seq, d), lambda b, r, ws, cls: (b, 0))
    c_spec = pl.BlockSpec((None, None, past, d), lambda b, r, ws, cls: (b, layer, 0, 0))
    q_spec = pl.BlockSpec((tq, d), lambda b, r, ws, cls: (b * groups + r, 0))
    return pl.pallas_call(
        functools.partial(_na_kernel, d=d),
        grid_spec=pltpu.PrefetchScalarGridSpec(
            num_scalar_prefetch=2,
            grid=(n_lat_batch, groups),
            in_specs=[q_spec, kv_spec, kv_spec, c_spec, c_spec,
                      pl.BlockSpec((None, n_heads, tq, NA_WIN * GRID_W),
                                   lambda b, r, ws, cls: (cls[r], 0, 0, 0))],
            out_specs=q_spec),
        out_shape=jax.ShapeDtypeStruct((n, d), BF16),
        compiler_params=_cparams("arbitrary", "arbitrary"),
        name="na_attn",
    )(jnp.asarray(starts, jnp.int32), jnp.asarray(classes, jnp.int32), q, k, v, ck, cv, bias)


def _lat_diff_kernel(lam_ref, q_ref, k_ref, vt_ref, ck_ref, cv_ref, g_ref, o_ref, *, out_scale):
    lo = _lane_lo()
    lam = lam_ref[0]
    q2 = q_ref[...]
    tq = q2.shape[0]
    tg = tq // QUERY_GROUPS
    qs = []
    for g in range(QUERY_GROUPS):
        qg = q2[g * tg:(g + 1) * tg, :]
        qs += [jnp.where(lo, qg, 0), jnp.where(lo, 0, qg)]

    def softmax_step(state, s):
        m, l, acc = state
        m_new = jnp.maximum(m, jnp.max(s, axis=0, keepdims=True))
        alpha = jnp.exp2(m - m_new)
        p = jnp.exp2(s - m_new)
        return m_new, alpha * l + jnp.sum(p, axis=0, keepdims=True), alpha * acc, p.astype(BF16)

    def scores(kt):
        return [_dot_nt(kt, qm) for qm in qs]

    def absorb(states, ss, vt):
        out = []
        for g in range(0, len(qs), 2):
            stepped = [softmax_step(states[g + j], ss[g + j]) for j in range(2)]
            out += [(m, l, acc + _dot(vt, p)) for m, l, acc, p in stepped]
        return out

    tile = vt_ref.shape[2]
    n_tiles = vt_ref.shape[0]

    def keys(t):
        return k_ref[t * tile:(t + 1) * tile, :] if t < n_tiles else ck_ref[...].astype(BF16)

    def values_t(t):
        return vt_ref[t] if t < n_tiles else cv_ref[...].T.astype(BF16)

    init = (jnp.full((1, tg), -jnp.inf, F32), jnp.zeros((1, tg), F32), jnp.zeros((LANES, tg), F32))
    states = [init] * len(qs)
    ss = scores(keys(0))
    for t in range(n_tiles + 1):
        ss_next = scores(keys(t + 1)) if t < n_tiles else None
        states = absorb(states, ss, values_t(t))
        ss = ss_next
    o = jnp.concatenate([states[g][2] / states[g][1] - lam * (states[g + 1][2] / states[g + 1][1])
                         for g in range(0, len(qs), 2)], axis=1).T
    o_ref[...] = _subln(o, g_ref, out_scale).astype(BF16)


def _lat_diff(lam, q, k, vt, ck, cv, subg, *, layer, n_lat_batch, lat_seq, out_scale):
    n, d = q.shape
    past = ck.shape[2]
    tk = vt.shape[2]
    tq = _pick_tile(lat_seq, QUERY_GROUPS * 256)
    qt = lat_seq // tq
    kt = lat_seq // tk
    c_spec = pl.BlockSpec((None, None, past, LANES), lambda b, h, i: (b, layer, 0, h))
    q_spec = pl.BlockSpec((tq, LANES), lambda b, h, i: (b * qt + i, h))
    return pl.pallas_call(
        functools.partial(_lat_diff_kernel, out_scale=out_scale),
        grid=(n_lat_batch, d // LANES, qt),
        in_specs=[pl.BlockSpec(memory_space=pltpu.SMEM), q_spec,
                  pl.BlockSpec((lat_seq, LANES), lambda b, h, i: (b, h)),
                  pl.BlockSpec((kt, LANES, tk), lambda b, h, i: (b, h, 0)),
                  c_spec, c_spec, pl.BlockSpec((1, LANES), lambda b, h, i: (0, 0))],
        out_specs=q_spec,
        out_shape=jax.ShapeDtypeStruct((n, d), BF16),
        compiler_params=_cparams("arbitrary", "arbitrary", "arbitrary"),
        name="lat_diff_attn",
    )(lam, q, k, vt, ck, cv, subg)


def _post_attn_kernel(o_ref, *refs, ends, tile0):
    n_x = len(ends)
    x_refs = refs[:n_x]
    (mod_ref, wo_ref, g_ref, rwh_ref, rwl_ref, rb_ref, tri_ref,
     xmid_ref, h_ref, sel_ref, gate_ref, rank_ref, cnt_ref, carry_ref) = refs[n_x:]
    i = pl.program_id(0)

    @pl.when(i == 0)
    def _():
        carry_ref[...] = jnp.zeros_like(carry_ref)

    x1 = _read_tokens(x_refs, i + tile0, ends) + mod_ref[2:3, :] * _dot(o_ref[...], wo_ref[...])
    xmid_ref[...] = x1
    h = _rms(x1, g_ref[...]) * (1.0 + mod_ref[4:5, :]) + mod_ref[3:4, :]
    hb = h.astype(BF16)
    h_ref[...] = _pack_rows(hb)
    h_lo = (h - hb.astype(F32)).astype(BF16)
    logits = _dot_nt(rwh_ref[...], hb) + (_dot_nt(rwl_ref[...], hb) + _dot_nt(rwh_ref[...], h_lo))
    scores = _sigmoid(logits)
    n_exp, tm = scores.shape
    biased = scores + rb_ref[...]
    row = lax.broadcasted_iota(jnp.int32, (n_exp, tm), 0).astype(F32)
    total = jnp.zeros((n_exp, tm), F32)
    sels, gates = [], []
    for _ in range(TOP_K):
        m = jnp.max(biased, axis=0, keepdims=True)
        idx = jnp.min(jnp.where(biased == m, row, float(n_exp)), axis=0, keepdims=True)
        hit = row == idx
        gates.append(jnp.sum(jnp.where(hit, scores, 0.0), axis=0, keepdims=True))
        biased = jnp.where(hit, -jnp.inf, biased)
        total = total + jnp.where(hit, 1.0, 0.0)
        sels.append(idx)
    denom = gates[0]
    for g in gates[1:]:
        denom = denom + g
    before = _dot(total.astype(BF16), tri_ref[...]) + carry_ref[:, 0:1]
    for k in range(TOP_K):
        sel_ref[k:k + 1, :] = sels[k].astype(jnp.int32)
        gate_ref[k:k + 1, :] = gates[k] / denom * ROUTED_SCALE
        rank_ref[k:k + 1, :] = jnp.sum(jnp.where(row == sels[k], before, 0.0), axis=0,
                                       keepdims=True).astype(jnp.int32)
    carry_ref[...] = carry_ref[...] + jnp.sum(total, axis=1, keepdims=True)
    cnt_ref[...] = carry_ref[...].astype(jnp.int32)


def _post_attn(o, xs, mod, wo, g, rwh, rwl, rb, *, layer, tile0, n_tiles, lat_seq, n_lat_batch, tm):
    d = o.shape[1]
    n = n_tiles * tm
    n_exp = rwh.shape[0]
    tri = (jnp.arange(tm)[:, None] < jnp.arange(tm)[None, :]).astype(BF16)
    x_arrays, x_specs, ends = _token_parts(xs, tm, tile0, n_tiles)

    def mod_idx(i):
        return (layer, jnp.minimum((i + tile0) * tm // lat_seq, n_lat_batch), 0, 0)

    tok = pl.BlockSpec((tm, d), lambda i: (i, 0))
    tok_p = pl.BlockSpec((tm, d // 2), lambda i: (i, 0))
    kt = pl.BlockSpec((TOP_K, tm), lambda i: (0, i))
    const2 = lambda i: (0, 0)
    return pl.pallas_call(
        functools.partial(_post_attn_kernel, ends=tuple(ends), tile0=tile0),
        grid=(n_tiles,),
        in_specs=[pl.BlockSpec((tm, d), lambda i: (i + tile0, 0))] + x_specs + [
                  pl.BlockSpec((None, None, N_MOD, d), mod_idx),
                  pl.BlockSpec((d, d), const2), pl.BlockSpec((1, d), const2),
                  pl.BlockSpec((n_exp, d), const2), pl.BlockSpec((n_exp, d), const2),
                  pl.BlockSpec((n_exp, 1), const2), pl.BlockSpec((tm, tm), const2)],
        out_specs=[tok, tok_p, kt, kt, kt, pl.BlockSpec((n_exp, LANES), const2)],
        out_shape=[jax.ShapeDtypeStruct((n, d), F32), jax.ShapeDtypeStruct((n, d // 2), jnp.int32),
                   jax.ShapeDtypeStruct((TOP_K, n), jnp.int32),
                   jax.ShapeDtypeStruct((TOP_K, n), F32),
                   jax.ShapeDtypeStruct((TOP_K, n), jnp.int32),
                   jax.ShapeDtypeStruct((n_exp, LANES), jnp.int32)],
        scratch_shapes=[pltpu.VMEM((n_exp, LANES), F32)],
        compiler_params=_cparams("arbitrary"),
        name=f"post_attn{layer}",
    )(o, *x_arrays, mod, wo, g, rwh, rwl, rb, tri)


def _plan_kernel(cnt_ref, sel_ref, rank_ref, dest_ref, te_ref, tv_ref, nu_ref, *, tmb):
    n_exp = cnt_ref.shape[0]
    sel = sel_ref[...]
    tile_start = lax.broadcasted_iota(jnp.int32, te_ref.shape, 1) * tmb
    dest = rank_ref[...]
    te = jnp.zeros(te_ref.shape, jnp.int32)
    tv = jnp.zeros(te_ref.shape, jnp.int32)
    start = jnp.zeros((1, 1), jnp.int32)
    for e in range(n_exp):
        dest = dest + jnp.where(sel == e, start, 0)
        cnt = cnt_ref[e:e + 1, 0:1]
        stop = start + ((cnt + (tmb - 1)) & -tmb)
        mine = (tile_start >= start) & (tile_start < stop)
        tv = tv + jnp.where(mine, jnp.clip(start + cnt - tile_start, 0, tmb), 0)
        start = stop
        te = te + jnp.where(start <= tile_start, 1, 0)
    te_ref[...] = jnp.minimum(te, n_exp - 1)
    tv_ref[...] = tv
    nu_ref[...] = jnp.broadcast_to(start, nu_ref.shape)
    for c in range(dest_ref.shape[0]):
        dest_ref[c] = dest[:, c * SC_CHUNK:(c + 1) * SC_CHUNK]


def _plan(counts, sel, rank, *, tmb, n_tiles):
    k_top, n = sel.shape
    assert tmb & (tmb - 1) == 0
    tn = _pick_tile(n, 16 * SC_CHUNK)
    te_w = -(-n_tiles // LANES) * LANES
    tok = pl.BlockSpec((k_top, tn), lambda i: (0, i))
    const2 = lambda i: (0, 0)
    dest, te, tv, nu = pl.pallas_call(
        functools.partial(_plan_kernel, tmb=tmb),
        grid=(n // tn,),
        in_specs=[pl.BlockSpec(counts.shape, const2), tok, tok],
        out_specs=[pl.BlockSpec((tn // SC_CHUNK, k_top, SC_CHUNK), lambda i: (i, 0, 0)),
                   pl.BlockSpec((1, te_w), const2), pl.BlockSpec((1, te_w), const2),
                   pl.BlockSpec((1, LANES), const2)],
        out_shape=[jax.ShapeDtypeStruct((n // SC_CHUNK, k_top, SC_CHUNK), jnp.int32),
                   jax.ShapeDtypeStruct((1, te_w), jnp.int32),
                   jax.ShapeDtypeStruct((1, te_w), jnp.int32),
                   jax.ShapeDtypeStruct((1, LANES), jnp.int32)],
        compiler_params=_cparams("arbitrary"),
        name="plan",
    )(counts, sel, rank)
    return dest, te[0, :n_tiles], tv[0, :n_tiles], nu[0, :1] // tmb


def _expert_kernel(te_ref, tv_ref, nu_ref, x_ref, wg_ref, wu_ref, wd_ref, y_ref, wg_b, wu_b, wd_b):
    i = pl.program_id(0)
    half = x_ref.shape[0] // 2

    def rows(r):
        return jnp.concatenate(_unpack_rows(x_ref[r * half:(r + 1) * half, :]), axis=1).astype(BF16)

    def swiglu(xs):
        gu = [(_dot(x, wg_b[...]), _dot(x, wu_b[...])) for x in xs]
        acts = [(g * _sigmoid(g) * u).astype(BF16) for g, u in gu]
        return [_pack_rows(_dot(a, wd_b[...]).astype(BF16)) for a in acts]

    @pl.when(i < nu_ref[0])
    def _():
        @pl.when((i == 0) | (te_ref[i] != te_ref[jnp.maximum(i - 1, 0)]))
        def _():
            wg_b[...] = wg_ref[...].astype(BF16)
            wu_b[...] = wu_ref[...].astype(BF16)
            wd_b[...] = wd_ref[...].astype(BF16)

        @pl.when(tv_ref[i] > half)
        def _():
            ya, yb = swiglu([rows(0), rows(1)])
            y_ref[0:half, :] = ya
            y_ref[half:2 * half, :] = yb

        @pl.when(tv_ref[i] <= half)
        def _():
            y_ref[0:half, :] = swiglu([rows(0)])[0]


def _experts(tile_expert, tile_rows, n_used, xb, wg, wu, wd, *, layer, tmb):
    cap, dp = xb.shape
    f, d = wd.shape[2:]

    def row_idx(i, te, tv, nu):
        return (jnp.minimum(i, nu[0] - 1), 0)

    def w_idx(i, te, tv, nu):
        return (layer, te[i], 0, 0)

    return pl.pallas_call(
        _expert_kernel,
        grid_spec=pltpu.PrefetchScalarGridSpec(
            num_scalar_prefetch=3,
            grid=(cap // tmb,),
            in_specs=[pl.BlockSpec((tmb, dp), row_idx),
                      pl.BlockSpec((None, None, d, f), w_idx),
                      pl.BlockSpec((None, None, d, f), w_idx),
                      pl.BlockSpec((None, None, f, d), w_idx)],
            out_specs=pl.BlockSpec((tmb, dp), row_idx),
            scratch_shapes=[pltpu.VMEM((d, f), BF16), pltpu.VMEM((d, f), BF16),
                            pltpu.VMEM((f, d), BF16)]),
        out_shape=jax.ShapeDtypeStruct((cap, dp), jnp.int32),
        compiler_params=_cparams("arbitrary"),
        name="experts",
    )(tile_expert, tile_rows, n_used, xb, wg, wu, wd)


def _combine_kernel(x_ref, h_ref, routed_ref, mod_ref, wgu_ref, wd_ref, fg_ref, *refs, f, final,
                    n_alias, tile0, n_lat_tiles):
    o_refs = refs[n_alias:]
    h = jnp.concatenate(_unpack_rows(h_ref[...]), axis=1).astype(BF16)
    gu = _dot(h, wgu_ref[...])
    gate, up = gu[:, :f], gu[:, f:]
    shared = _dot((gate * _sigmoid(gate) * up).astype(BF16), wd_ref[...])
    x2 = x_ref[...] + mod_ref[5:6, :] * (shared + routed_ref[...])
    if not final:
        o_refs[0][...] = x2
    elif len(o_refs) == 1:
        o_refs[0][...] = _rms(x2, fg_ref[...])
    else:
        y = _rms(x2, fg_ref[...])
        tile = pl.program_id(0) + tile0

        @pl.when(tile < n_lat_tiles)
        def _():
            o_refs[0][...] = y

        @pl.when(tile >= n_lat_tiles)
        def _():
            o_refs[1][...] = y


def _combine(x, h, routed, mod, wgu, wd, final_g, y_lat, *, layer, tile0, n_lat, n_ctx, lat_seq,
             n_lat_batch, tm, final):
    n, d = x.shape
    f = wd.shape[0]
    n_tiles = n // tm
    n_lat_tiles = n_lat // tm

    def mod_idx(i):
        return (layer, jnp.minimum((i + tile0) * tm // lat_seq, n_lat_batch), 0, 0)

    tok = pl.BlockSpec((tm, d), lambda i: (i, 0))
    tok_p = pl.BlockSpec((tm, d // 2), lambda i: (i, 0))
    const2 = lambda i: (0, 0)
    in_specs = [tok, tok_p, tok,
                pl.BlockSpec((None, None, N_MOD, d), mod_idx),
                pl.BlockSpec((d, 2 * f), const2), pl.BlockSpec((f, d), const2),
                pl.BlockSpec((1, d), const2)]
    args = [x, h, routed, mod, wgu, wd, final_g]
    aliases = {}
    has_lat = has_ctx = False
    if not final:
        out_specs, out_shape = [tok], [jax.ShapeDtypeStruct((n, d), F32)]
    else:
        has_lat = tile0 < n_lat_tiles
        has_ctx = tile0 + n_tiles > n_lat_tiles
        out_specs, out_shape = [], []
        if has_lat:
            out_specs.append(pl.BlockSpec(
                (tm, d), lambda i: (jnp.minimum(i + tile0, n_lat_tiles - 1), 0)))
            out_shape.append(jax.ShapeDtypeStruct((n_lat, d), F32))
            if y_lat is not None:
                aliases = {len(args): 0}
                in_specs.append(pl.BlockSpec(memory_space=pl.ANY))
                args.append(y_lat)
        if has_ctx:
            out_specs.append(pl.BlockSpec(
                (tm, d), lambda i: (jnp.maximum(i + tile0 - n_lat_tiles, 0), 0)))
            out_shape.append(jax.ShapeDtypeStruct((n_ctx, d), F32))
    outs = pl.pallas_call(
        functools.partial(_combine_kernel, f=f, final=final, n_alias=len(aliases), tile0=tile0,
                          n_lat_tiles=n_lat_tiles),
        grid=(n_tiles,),
        in_specs=in_specs,
        out_specs=out_specs,
        out_shape=out_shape,
        input_output_aliases=aliases,
        compiler_params=_cparams("arbitrary"),
        name=f"combine{layer}",
    )(*args)
    if not final:
        return outs[0]
    return (outs[0] if has_lat else y_lat), (outs[-1] if has_ctx else None)


SC_CHUNK = 128


def _sc_workers():
    info = plsc.get_sparse_core_info()
    return info.num_cores, info.num_subcores


def _sc_dispatch(rows, dest, cap):
    n, w = rows.shape
    nc, ns = _sc_workers()
    per_w = n // (nc * ns)
    assert per_w * nc * ns == n and per_w % SC_CHUNK == 0
    mesh = plsc.VectorSubcoreMesh(core_axis_name="c", subcore_axis_name="s")

    @functools.partial(
        pl.kernel, mesh=mesh, out_type=jax.ShapeDtypeStruct((cap, w), rows.dtype),
        scratch_types=[pltpu.VMEM((TOP_K, SC_CHUNK), jnp.int32),
                       pltpu.VMEM((SC_CHUNK, w), rows.dtype)],
        name="sc_dispatch")
    def run(rows_hbm, dest_hbm, out_hbm, idx_v, rows_v):
        wid = lax.axis_index("s") * nc + lax.axis_index("c")

        @pl.loop(0, per_w // SC_CHUNK)
        def _(ci):
            base = pl.multiple_of(wid * per_w + ci * SC_CHUNK, SC_CHUNK)
            pltpu.sync_copy(dest_hbm.at[wid * (per_w // SC_CHUNK) + ci], idx_v)
            pltpu.sync_copy(rows_hbm.at[pl.ds(base, SC_CHUNK)], rows_v)
            for k in range(TOP_K):
                pltpu.sync_copy(rows_v, out_hbm.at[idx_v.at[k]])

    return run(rows, dest)


SC_TOK = 8
SC_LANES = 16


def _sc_combine(table, dest, gates):
    n_chunks, k_top, _ = dest.shape
    n = n_chunks * SC_CHUNK
    w = table.shape[1]
    nc, ns = _sc_workers()
    per_w = n // (nc * ns)
    assert per_w * nc * ns == n and per_w % SC_CHUNK == 0 and w % (16 * SC_LANES) == 0
    gb = jnp.broadcast_to(
        gates.reshape(k_top, n_chunks, SC_CHUNK).transpose(1, 0, 2)[..., None],
        (n_chunks, k_top, SC_CHUNK, SC_LANES)).reshape(n_chunks, k_top, SC_CHUNK // SC_TOK,
                                                        SC_TOK * SC_LANES)
    mesh = plsc.VectorSubcoreMesh(core_axis_name="c", subcore_axis_name="s")

    n_blocks = SC_CHUNK // SC_TOK

    @functools.partial(
        pl.kernel, mesh=mesh, out_type=jax.ShapeDtypeStruct((n, 2 * w), F32),
        scratch_types=[pltpu.VMEM((k_top, SC_CHUNK), jnp.int32),
                       pltpu.VMEM((k_top, n_blocks, SC_TOK * SC_LANES), F32),
                       pltpu.VMEM((2, k_top, SC_TOK, w), table.dtype),
                       pltpu.VMEM((2, SC_TOK, 2 * w), F32),
                       pltpu.SemaphoreType.DMA((2,)), pltpu.SemaphoreType.DMA((2,))],
        compiler_params=pltpu.CompilerParams(needs_layout_passes=False),
        name="sc_combine")
    def run(table_hbm, dest_hbm, gb_hbm, out_hbm, idx_v, g_v, rows_v, out_v, sem_in, sem_out):
        wid = lax.axis_index("s") * nc + lax.axis_index("c")

        def gathers(blk, slot):
            off = pl.multiple_of(blk * SC_TOK, SC_TOK)
            return [pltpu.make_async_copy(table_hbm.at[idx_v.at[k, pl.ds(off, SC_TOK)]],
                                          rows_v.at[slot, k], sem_in.at[slot]) for k in range(k_top)]

        def write_back(chunk, blk, slot):
            row0 = pl.multiple_of(chunk * SC_CHUNK + blk * SC_TOK, SC_TOK)
            return pltpu.make_async_copy(out_v.at[slot], out_hbm.at[pl.ds(row0, SC_TOK)],
                                         sem_out.at[slot])

        def weighted_sum(blk, slot):
            @pl.loop(0, SC_TOK)
            def _(j):
                for base in range(0, w, 16 * SC_LANES):
                    acc = None
                    for k in range(k_top):
                        g = g_v[k, blk, pl.ds(j * SC_LANES, SC_LANES)]
                        terms = []
                        for v in range(16):
                            word = rows_v[slot, k, j, pl.ds(base + v * SC_LANES, SC_LANES)]
                            lo = lax.bitcast_convert_type(lax.shift_left(word, 16), F32)
                            hi = lax.bitcast_convert_type(word & jnp.int32(-65536), F32)
                            terms.append((g * lo, g * hi))
                        acc = terms if acc is None else [
                            (a + x, b + y) for (a, b), (x, y) in zip(acc, terms)]
                    for v, (a, b) in enumerate(acc):
                        out_v[slot, j, pl.ds(base + v * SC_LANES, SC_LANES)] = a
                        out_v[slot, j, pl.ds(w + base + v * SC_LANES, SC_LANES)] = b

        @pl.loop(0, per_w // SC_CHUNK)
        def _(ci):
            chunk = wid * (per_w // SC_CHUNK) + ci
            pltpu.sync_copy(dest_hbm.at[chunk], idx_v)
            pltpu.sync_copy(gb_hbm.at[chunk], g_v)
            for cp in gathers(0, 0):
                cp.start()

            @pl.loop(0, n_blocks, step=2)
            def _(b0):
                for slot in range(2):
                    blk = b0 + slot
                    for cp in gathers(blk, slot):
                        cp.wait()

                    @pl.when(blk + 1 < n_blocks)
                    def _():
                        for cp in gathers(blk + 1, 1 - slot):
                            cp.start()

                    @pl.when(blk >= 2)
                    def _():
                        write_back(chunk, blk - 2, slot).wait()

                    weighted_sum(blk, slot)
                    write_back(chunk, blk, slot).start()

            for slot in range(2):
                write_back(chunk, n_blocks - 2 + slot, slot).wait()


    return run(table, dest, gb)


def _split_bf16(w):
    hi = w.astype(BF16)
    return hi, (w - hi.astype(F32)).astype(BF16)


def kernel(x_prompt, x_sample, cache_k, cache_v, c, c_ctx, ada_w, ada_b, norm1_g, norm2_g, w_qkv, w_o, na_rpb, diff_lambda, diff_subln_g, router_w, router_b, exp_w_gate, exp_w_up, exp_w_down, shared_w_gate, shared_w_up, shared_w_down, final_g):
    batch, seq, d = x_prompt.shape
    n_lat_batch, lat_seq, _ = x_sample.shape
    depth = w_qkv.shape[0]
    n_exp = router_w.shape[-1]
    n_lat = n_lat_batch * lat_seq
    n_ctx = batch * seq
    n = n_lat + n_ctx
    assert d % LANES == 0 and na_rpb.shape[1] * HEAD_DIM == d
    assert diff_lambda.shape[-1] == HEAD_DIM and lat_seq % GRID_W == 0
    assert n_lat % seq == 0
    tm = _pick_tile(math.gcd(lat_seq, n_ctx), 512)
    tmb = 1024
    n_all = n // tm
    ranges = [(r * n_all // MOE_RANGES, (r + 1) * n_all // MOE_RANGES - r * n_all // MOE_RANGES)
              for r in range(MOE_RANGES)]
    scale = HEAD_DIM ** -0.5 * LOG2E

    xs = [x_sample.reshape(n_lat, d), x_prompt.reshape(n_ctx, d)]
    mod_rows = -(-(n_lat_batch + 1) // 8) * 8
    cond = jnp.zeros((mod_rows, d), F32).at[:n_lat_batch].set(c).at[n_lat_batch].set(c_ctx)
    mod = _modulation(cond, ada_w, ada_b).reshape(depth, mod_rows, N_MOD, d)

    w_qkv_b = w_qkv.astype(BF16)
    w_o_b = w_o.astype(BF16)
    rope_tab = _rope_tables(lat_seq, tm)
    caches = None
    for i in range(depth):
        is_diff = i % 2 == 1
        j = i // 2
        q, k, v, *caches = _qkv(xs, mod, norm1_g.reshape(depth, 1, d), w_qkv_b,
                                rope_tab if is_diff else None, caches, layer=i, n_lat=n_lat,
                                batch=batch, seq=seq, lat_seq=lat_seq, n_lat_batch=n_lat_batch,
                                tm=tm, scale=scale)
        if is_diff:
            vt = caches.pop()
        if not is_diff:
            o = _na_attention(q, k, v, cache_k, cache_v, na_rpb[j], layer=i,
                              n_lat_batch=n_lat_batch, lat_seq=lat_seq)
            o = _ctx_dense(q, k, v, o, n_lat=n_lat, seq=seq)
        else:
            lam_init = 0.8 - 0.6 * math.exp(-0.3 * i)
            lp = diff_lambda[j].astype(F32)
            lam = (jnp.exp(jnp.sum(lp[0] * lp[1])) - jnp.exp(jnp.sum(lp[2] * lp[3]))
                   + lam_init).reshape(1)
            subg = diff_subln_g[j].reshape(1, LANES).astype(F32)
            o = _lat_diff(lam, q, k, vt, cache_k, cache_v, subg, layer=i,
                          n_lat_batch=n_lat_batch, lat_seq=lat_seq, out_scale=1.0 - lam_init)
            o = _ctx_diff(lam, q, k, v, subg, o, n_lat=n_lat, seq=seq, out_scale=1.0 - lam_init)

        rwh, rwl = _split_bf16(router_w[i].T)
        swgu = jnp.concatenate([shared_w_gate[i], shared_w_up[i]], axis=-1).astype(BF16)
        final = i == depth - 1
        routed = []
        for tile0, nt in ranges:
            xmid, h2, sel, gates, rank, counts = _post_attn(
                o, xs, mod, w_o_b[i], norm2_g[i].reshape(1, d), rwh, rwl,
                router_b[i].reshape(n_exp, 1).astype(F32), layer=i, tile0=tile0, n_tiles=nt,
                lat_seq=lat_seq, n_lat_batch=n_lat_batch, tm=tm)
            n_tiles = -(-(nt * tm * TOP_K) // tmb) + n_exp
            dest, tile_expert, tile_rows, n_used = _plan(counts, sel, rank, tmb=tmb, n_tiles=n_tiles)
            xb = _sc_dispatch(h2, dest, n_tiles * tmb)
            routed.append((xmid, h2, gates, dest, tile_expert, tile_rows, n_used, xb))
        ybs = [_experts(te, tr, nu, xb, exp_w_gate, exp_w_up, exp_w_down, layer=i, tmb=tmb)
               for (_, _, _, _, te, tr, nu, xb) in routed]
        sums = [_sc_combine(yb, r[3], r[2]) for yb, r in zip(ybs, routed)]
        xs, y_lat, y_ctx = [], None, None
        for (tile0, nt), (xmid, h2, *_), rsum in zip(ranges, routed, sums):
            out = _combine(xmid, h2, rsum, mod, swgu, shared_w_down[i].astype(BF16),
                           final_g.reshape(1, d), y_lat, layer=i, tile0=tile0, n_lat=n_lat,
                           n_ctx=n_ctx, lat_seq=lat_seq, n_lat_batch=n_lat_batch, tm=tm, final=final)
            if final:
                y_lat, y_ctx = out[0], (out[1] if out[1] is not None else y_ctx)
            else:
                xs.append(out)

    y_sample = y_lat.reshape(n_lat_batch, lat_seq, d)
    y_prompt = y_ctx.reshape(batch, seq, d)
    return (y_prompt, y_sample, caches[0], caches[1])
```

```python
import functools
import math

import jax
import jax.numpy as jnp
from jax import lax
from jax.experimental import pallas as pl
from jax.experimental.pallas import tpu as pltpu
from jax.experimental.pallas import tpu_sc as plsc

GRID_W = 64
NA_ROWS = 8
NA_COLS = 16
TOP_K = 8
ROUTED_SCALE = 2.5
ROPE_BASE = 10000.0
EPS = 1e-6
N_MOD = 6
HEAD_DIM = 64
LANES = 128
LOG2E = math.log2(math.e)
MASKED = -1e30
NA_QROWS = 2
NA_WIN = NA_ROWS + NA_QROWS - 1
POST_SPLIT = 2
MOE_SHARES = (3, 2)
QUERY_GROUPS = 4
VMEM_LIMIT = 56 * 1024 * 1024

F32 = jnp.float32
BF16 = jnp.bfloat16


def _cparams(*sem):
    return pltpu.CompilerParams(dimension_semantics=sem, vmem_limit_bytes=VMEM_LIMIT)


def _dot(a, b):
    return jnp.dot(a, b, preferred_element_type=F32)


def _dot_nt(a, b):
    return lax.dot_general(a, b, (((1,), (1,)), ((), ())), preferred_element_type=F32)


def _sigmoid(x):
    return 1.0 / (1.0 + jnp.exp(-x))


def _rms(x, g):
    return x * lax.rsqrt(jnp.mean(x * x, axis=-1, keepdims=True) + EPS) * g


def _pack_rows(xb):
    half = xb.shape[1] // 2
    u = lax.bitcast_convert_type(xb.astype(F32), jnp.uint32)
    packed = (u[:, :half] >> 16) | (u[:, half:] & jnp.uint32(0xFFFF0000))
    return lax.bitcast_convert_type(packed, jnp.int32)


def _unpack_rows(p):
    u = lax.bitcast_convert_type(p, jnp.uint32)
    return (lax.bitcast_convert_type(u << 16, F32),
            lax.bitcast_convert_type(u & jnp.uint32(0xFFFF0000), F32))


def _pick_tile(n, pref):
    t = pref
    while n % t:
        t //= 2
    return t


def _mod_kernel(c_ref, w_ref, b_ref, o_ref):
    c = c_ref[...]
    o_ref[...] = _dot(c * _sigmoid(c), w_ref[...]) + b_ref[...]


def _modulation(cond, ada_w, ada_b):
    depth, d, n6 = ada_w.shape
    rows = cond.shape[0]
    tn = _pick_tile(n6, 512)
    return pl.pallas_call(
        _mod_kernel,
        grid=(depth, n6 // tn),
        in_specs=[
            pl.BlockSpec((rows, d), lambda l, j: (0, 0)),
            pl.BlockSpec((None, d, tn), lambda l, j: (l, 0, j)),
            pl.BlockSpec((None, 1, tn), lambda l, j: (l, 0, j)),
        ],
        out_specs=pl.BlockSpec((None, rows, tn), lambda l, j: (l, 0, j)),
        out_shape=jax.ShapeDtypeStruct((depth, rows, n6), F32),
        compiler_params=_cparams("arbitrary", "arbitrary"),
        name="modulation",
    )(cond, ada_w, ada_b.reshape(depth, 1, n6))


def _read_tokens(x_refs, g, ends):
    x = x_refs[-1][...]
    for ref, end in zip(reversed(x_refs[:-1]), reversed(ends[:-1])):
        x = jnp.where(g < end, ref[...], x)
    return x


def _token_parts(xs, tm, tile0, n_tiles):
    arrays, specs, ends = [], [], []
    start = 0
    for x in xs:
        size = x.shape[0] // tm
        if start < tile0 + n_tiles and start + size > tile0:
            arrays.append(x)
            ends.append(start + size)
            specs.append(pl.BlockSpec(
                (tm, x.shape[1]), lambda i, s=start, z=size: (jnp.clip(i + tile0 - s, 0, z - 1), 0)))
        start += size
    return arrays, specs, ends


def _qkv_kernel(*refs, d, scale, rope, n_lat_tiles, ends, n_alias):
    n_x = len(ends)
    x_refs = refs[:n_x]
    mod_ref, g_ref, w_ref = refs[n_x:n_x + 3]
    rest = refs[n_x + 3:]
    if rope:
        rope_ref, wvt_ref, rest = rest[0], rest[1], rest[2:]
    rest = rest[n_alias:]
    q_ref, k_ref, v_ref, kf_ref, vf_ref = rest[:5]
    i = pl.program_id(0)
    x = _read_tokens(x_refs, i, ends)
    h = _rms(x, g_ref[...]) * (1.0 + mod_ref[1:2, :]) + mod_ref[0:1, :]
    hb = h.astype(BF16)

    def rot(t):
        if not rope:
            return t
        cols = []
        for j in range(d // LANES):
            tb = t[:, j * LANES:(j + 1) * LANES]
            cols.append(tb * rope_ref[0]
                        + pltpu.roll(tb, LANES - HEAD_DIM // 2, axis=1) * rope_ref[1]
                        + pltpu.roll(tb, HEAD_DIM // 2, axis=1) * rope_ref[2])
        return jnp.concatenate(cols, axis=1)

    q = _dot(hb, w_ref[:, 0:d])
    q_ref[...] = (rot(q) * scale).astype(BF16)
    k = _dot(hb, w_ref[:, d:2 * d])
    k_ref[...] = rot(k).astype(BF16)
    if rope:
        @pl.when(i < n_lat_tiles)
        def _():
            rest[5][...] = _dot_nt(wvt_ref[...], hb).astype(BF16)

    def context_values():
        v = _dot(hb, w_ref[:, 2 * d:3 * d])
        v_ref[...] = v.astype(BF16)
        return v

    if not rope:
        v = context_values()

    @pl.when(i >= n_lat_tiles)
    def _():
        kf_ref[...] = k.reshape(kf_ref.shape)
        vf_ref[...] = (context_values() if rope else v).reshape(vf_ref.shape)


def _qkv(xs, mod, g, w, rope_tab, caches, *, layer, n_lat, batch, seq, lat_seq, n_lat_batch, tm,
         scale):
    d = xs[0].shape[1]
    depth = w.shape[0]
    n = n_lat + batch * seq
    n_lat_tiles = n_lat // tm
    rope = rope_tab is not None
    assert tm % seq == 0
    x_arrays, x_specs, ends = _token_parts(xs, tm, 0, n // tm)

    def mod_idx(i):
        return (layer, jnp.minimum(i * tm // lat_seq, n_lat_batch), 0, 0)

    in_specs = x_specs + [
        pl.BlockSpec((None, None, N_MOD, d), mod_idx),
        pl.BlockSpec((None, 1, d), lambda i: (layer, 0, 0)),
        pl.BlockSpec((None, d, 3 * d), lambda i: (layer, 0, 0)),
    ]
    args = x_arrays + [mod, g, w]
    if rope:
        seq_tiles = lat_seq // tm
        in_specs.append(pl.BlockSpec(
            (3, tm, LANES),
            lambda i: (0, jnp.where(i < n_lat_tiles, i % seq_tiles, seq_tiles), 0)))
        in_specs.append(pl.BlockSpec((d, d), lambda i: (0, 0)))
        args += [rope_tab, w[layer, :, 2 * d:].T]
    aliases = {}
    if caches is not None:
        aliases = {len(args): 3, len(args) + 1: 4}
        in_specs += [pl.BlockSpec(memory_space=pl.ANY)] * 2
        args += list(caches)
    tok = pl.BlockSpec((tm, d), lambda i: (i, 0))
    cache_spec = pl.BlockSpec((tm // seq, None, seq, d),
                              lambda i: (jnp.maximum(i - n_lat_tiles, 0), layer, 0, 0))
    out_specs = [tok, tok, tok, cache_spec, cache_spec]
    out_shape = ([jax.ShapeDtypeStruct((n, d), BF16)] * 3
                 + [jax.ShapeDtypeStruct((batch, depth, seq, d), F32)] * 2)
    if rope:
        out_specs.append(pl.BlockSpec((None, d, tm),
                                      lambda i: (jnp.minimum(i, n_lat_tiles - 1), 0, 0)))
        out_shape.append(jax.ShapeDtypeStruct((n_lat_tiles, d, tm), BF16))
    return pl.pallas_call(
        functools.partial(_qkv_kernel, d=d, scale=scale, rope=rope, n_lat_tiles=n_lat_tiles,
                          ends=tuple(ends), n_alias=len(aliases)),
        grid=(n // tm,),
        in_specs=in_specs,
        out_specs=out_specs,
        out_shape=out_shape,
        input_output_aliases=aliases,
        compiler_params=_cparams("arbitrary"),
        name=f"qkv{layer}",
    )(*args)


def _rope_tables(lat_seq, tm):
    n_freq = HEAD_DIM // 4
    inv = ROPE_BASE ** (-jnp.arange(n_freq, dtype=F32) / n_freq)
    pos = jnp.arange(lat_seq)
    row = (pos // GRID_W).astype(F32)
    col = (pos % GRID_W).astype(F32)
    ang = jnp.concatenate([row[:, None] * inv, col[:, None] * inv], axis=-1)
    reps = LANES // (HEAD_DIM // 2)
    cos = jnp.tile(jnp.cos(ang), (1, reps))
    sin = jnp.tile(jnp.sin(ang), (1, reps))
    first_half = (jnp.arange(LANES) % HEAD_DIM) < HEAD_DIM // 2
    s_next = jnp.where(first_half, -sin, 0.0)
    s_prev = jnp.where(first_half, 0.0, sin)
    ident = jnp.stack([jnp.ones((tm, LANES), F32), jnp.zeros((tm, LANES), F32),
                       jnp.zeros((tm, LANES), F32)])
    return jnp.concatenate([jnp.stack([cos, s_next, s_prev]), ident], axis=1)


def _lane_lo():
    return lax.broadcasted_iota(jnp.int32, (1, LANES), 1) < HEAD_DIM


def _ctx_dense_kernel(q_ref, k_ref, v_ref, o_all_ref, o_ref, *, d):
    del o_all_ref
    lo = _lane_lo()
    blks = [slice(j * LANES, (j + 1) * LANES) for j in range(d // LANES)]
    scores = []
    for blk in blks:
        q2 = q_ref[:, blk]
        scores += [_dot_nt(jnp.where(lo, q2, 0), k_ref[:, blk]),
                   _dot_nt(jnp.where(lo, 0, q2), k_ref[:, blk])]
    probs = []
    for s in scores:
        p = jnp.exp2(s - jnp.max(s, axis=-1, keepdims=True))
        probs.append((p.astype(BF16), jnp.sum(p, axis=-1, keepdims=True)))
    outs = [_dot(p, v_ref[:, blks[h // 2]]) / l for h, (p, l) in enumerate(probs)]
    o_ref[...] = jnp.concatenate(
        [jnp.where(lo, outs[h], outs[h + 1]).astype(BF16) for h in range(0, len(outs), 2)], axis=1)


def _ctx_dense(q, k, v, o_all, *, n_lat, seq):
    n, d = q.shape
    off = n_lat // seq
    spec = pl.BlockSpec((seq, d), lambda b: (off + b, 0))
    return pl.pallas_call(
        functools.partial(_ctx_dense_kernel, d=d),
        grid=((n - n_lat) // seq,),
        in_specs=[spec, spec, spec, pl.BlockSpec(memory_space=pl.ANY)],
        out_specs=spec,
        out_shape=jax.ShapeDtypeStruct((n, d), BF16),
        input_output_aliases={3: 0},
        compiler_params=_cparams("arbitrary"),
        name="ctx_dense_attn",
    )(q, k, v, o_all)


def _subln(o, g_ref, out_scale):
    return _rms(o, g_ref[...]) * out_scale


def _ctx_diff_kernel(lam_ref, q_ref, k_ref, v_ref, g_ref, o_all_ref, o_ref, *, d, out_scale):
    del o_all_ref
    lo = _lane_lo()
    lam = lam_ref[0]
    blks = [slice(j * LANES, (j + 1) * LANES) for j in range(d // LANES)]
    scores = []
    for blk in blks:
        q2 = q_ref[:, blk]
        scores.append((_dot_nt(jnp.where(lo, q2, 0), k_ref[:, blk]),
                       _dot_nt(jnp.where(lo, 0, q2), k_ref[:, blk])))
    probs = []
    for pair in scores:
        ps = []
        for s in pair:
            p = jnp.exp2(s - jnp.max(s, axis=-1, keepdims=True))
            ps.append(p / jnp.sum(p, axis=-1, keepdims=True))
        probs.append((ps[0] - lam * ps[1]).astype(BF16))
    outs = [_dot(p, v_ref[:, blk]) for p, blk in zip(probs, blks)]
    o_ref[...] = jnp.concatenate([_subln(o, g_ref, out_scale).astype(BF16) for o in outs], axis=1)


def _ctx_diff(lam, q, k, v, subg, o_all, *, n_lat, seq, out_scale):
    n, d = q.shape
    off = n_lat // seq
    spec = pl.BlockSpec((seq, d), lambda b: (off + b, 0))
    return pl.pallas_call(
        functools.partial(_ctx_diff_kernel, d=d, out_scale=out_scale),
        grid=((n - n_lat) // seq,),
        in_specs=[pl.BlockSpec(memory_space=pltpu.SMEM), spec, spec, spec,
                  pl.BlockSpec((1, LANES), lambda b: (0, 0)), pl.BlockSpec(memory_space=pl.ANY)],
        out_specs=spec,
        out_shape=jax.ShapeDtypeStruct((n, d), BF16),
        input_output_aliases={5: 0},
        compiler_params=_cparams("arbitrary"),
        name="ctx_diff_attn",
    )(lam, q, k, v, subg, o_all)


def _na_windows(rows):
    starts, classes, geoms = [], [], []
    for r0 in range(0, rows, NA_QROWS):
        first = [min(max(r0 + i - NA_ROWS // 2, 0), rows - NA_ROWS) for i in range(NA_QROWS)]
        ws = min(first[0], rows - NA_WIN)
        geom = tuple(
            tuple((ws + u) - (r0 + i) + NA_ROWS - 1 if first[i] <= ws + u < first[i] + NA_ROWS else -1
                  for u in range(NA_WIN)) for i in range(NA_QROWS))
        if geom not in geoms:
            geoms.append(geom)
        starts.append(ws)
        classes.append(geoms.index(geom))
    return starts, classes, geoms


def _na_kernel(ws_ref, cls_ref, q_ref, k_ref, v_ref, ck_ref, cv_ref, bias_ref, o_ref, *, d):
    del cls_ref
    lo = _lane_lo()
    win = NA_WIN * GRID_W
    start = pl.multiple_of(ws_ref[pl.program_id(1)] * GRID_W, GRID_W)
    n_heads = 2 * (d // LANES)

    def blk(h):
        return slice(h // 2 * LANES, (h // 2 + 1) * LANES)

    scores = []
    for h in range(n_heads):
        q2 = q_ref[:, blk(h)]
        qm = jnp.where(lo, q2, 0) if h % 2 == 0 else jnp.where(lo, 0, q2)
        scores.append((_dot_nt(qm, k_ref[pl.ds(start, win), blk(h)]) + bias_ref[h],
                       _dot_nt(qm, ck_ref[:, blk(h)].astype(BF16))))
    probs = []
    for s_loc, s_ctx in scores:
        m = jnp.maximum(jnp.max(s_loc, axis=-1, keepdims=True),
                        jnp.max(s_ctx, axis=-1, keepdims=True))
        p_loc = jnp.exp2(s_loc - m)
        p_ctx = jnp.exp2(s_ctx - m)
        l = jnp.sum(p_loc, axis=-1, keepdims=True) + jnp.sum(p_ctx, axis=-1, keepdims=True)
        probs.append((p_loc.astype(BF16), p_ctx.astype(BF16), l))
    outs = [(_dot(p_loc, v_ref[pl.ds(start, win), blk(h)])
             + _dot(p_ctx, cv_ref[:, blk(h)].astype(BF16))) / l
            for h, (p_loc, p_ctx, l) in enumerate(probs)]
    o_ref[...] = jnp.concatenate(
        [jnp.where(lo, outs[h], outs[h + 1]).astype(BF16) for h in range(0, n_heads, 2)], axis=1)


def _na_bias_table(rpb, geoms):
    h = rpb.shape[0]
    c = jnp.arange(GRID_W)
    kc = jnp.arange(GRID_W)
    c0 = jnp.clip(c - NA_COLS // 2, 0, GRID_W - NA_COLS)
    off_c = kc[None, :] - c[:, None] + (NA_COLS - 1)
    inside = (kc[None, :] >= c0[:, None]) & (kc[None, :] < c0[:, None] + NA_COLS)
    oh_c = jax.nn.one_hot(off_c, 2 * NA_COLS - 1, dtype=F32)
    blocks = jnp.einsum('hrs,cqs->rhcq', rpb.astype(F32), oh_c, precision=lax.Precision.HIGHEST)
    blocks = jnp.where(inside, blocks * LOG2E, MASKED)
    masked = jnp.full((h, GRID_W, GRID_W), MASKED, F32)
    return jnp.stack([
        jnp.concatenate([
            jnp.concatenate([blocks[r] if r >= 0 else masked for r in row], axis=-1)
            for row in geom], axis=-2)
        for geom in geoms])


def _na_attention(q, k, v, ck, cv, rpb, *, layer, n_lat_batch, lat_seq):
    n, d = q.shape
    rows = lat_seq // GRID_W
    past = ck.shape[2]
    n_heads = rpb.shape[0]
    assert rows % NA_QROWS == 0 and rows >= NA_WIN
    starts, classes, geoms = _na_windows(rows)
    bias = _na_bias_table(rpb, geoms)
    groups = rows // NA_QROWS
    tq = NA_QROWS * GRID_W
    kv_spec = pl.BlockSpec((lat_seq, d), lambda b, r, ws, cls: (b, 0))
    c_spec = pl.BlockSpec((None, None, past, d), lambda b, r, ws, cls: (b, layer, 0, 0))
    q_spec = pl.BlockSpec((tq, d), lambda b, r, ws, cls: (b * groups + r, 0))
    return pl.pallas_call(
        functools.partial(_na_kernel, d=d),
        grid_spec=pltpu.PrefetchScalarGridSpec(
            num_scalar_prefetch=2,
            grid=(n_lat_batch, groups),
            in_specs=[q_spec, kv_spec, kv_spec, c_spec, c_spec,
                      pl.BlockSpec((None, n_heads, tq, NA_WIN * GRID_W),
                                   lambda b, r, ws, cls: (cls[r], 0, 0, 0))],
            out_specs=q_spec),
        out_shape=jax.ShapeDtypeStruct((n, d), BF16),
        compiler_params=_cparams("arbitrary", "arbitrary"),
        name="na_attn",
    )(jnp.asarray(starts, jnp.int32), jnp.asarray(classes, jnp.int32), q, k, v, ck, cv, bias)


def _lat_diff_kernel(lam_ref, q_ref, k_ref, vt_ref, ck_ref, cv_ref, g_ref, o_ref, *, out_scale):
    lo = _lane_lo()
    lam = lam_ref[0]
    q2 = q_ref[...]
    tq = q2.shape[0]
    tg = tq // QUERY_GROUPS
    qs = []
    for g in range(QUERY_GROUPS):
        qg = q2[g * tg:(g + 1) * tg, :]
        qs += [jnp.where(lo, qg, 0), jnp.where(lo, 0, qg)]

    def softmax_step(state, s):
        m, l, acc = state
        m_new = jnp.maximum(m, jnp.max(s, axis=0, keepdims=True))
        alpha = jnp.exp2(m - m_new)
        p = jnp.exp2(s - m_new)
        return m_new, alpha * l + jnp.sum(p, axis=0, keepdims=True), alpha * acc, p.astype(BF16)

    def scores(kt):
        return [_dot_nt(kt, qm) for qm in qs]

    def absorb(states, ss, vt):
        out = []
        for g in range(0, len(qs), 2):
            stepped = [softmax_step(states[g + j], ss[g + j]) for j in range(2)]
            out += [(m, l, acc + _dot(vt, p)) for m, l, acc, p in stepped]
        return out

    tile = vt_ref.shape[2]
    n_tiles = vt_ref.shape[0]

    def keys(t):
        return k_ref[t * tile:(t + 1) * tile, :] if t < n_tiles else ck_ref[...].astype(BF16)

    def values_t(t):
        return vt_ref[t] if t < n_tiles else cv_ref[...].T.astype(BF16)

    init = (jnp.full((1, tg), -jnp.inf, F32), jnp.zeros((1, tg), F32), jnp.zeros((LANES, tg), F32))
    states = [init] * len(qs)
    ss = scores(keys(0))
    for t in range(n_tiles + 1):
        ss_next = scores(keys(t + 1)) if t < n_tiles else None
        states = absorb(states, ss, values_t(t))
        ss = ss_next
    o = jnp.concatenate([states[g][2] / states[g][1] - lam * (states[g + 1][2] / states[g + 1][1])
                         for g in range(0, len(qs), 2)], axis=1).T
    o_ref[...] = _subln(o, g_ref, out_scale).astype(BF16)


def _lat_diff(lam, q, k, vt, ck, cv, subg, *, layer, n_lat_batch, lat_seq, out_scale):
    n, d = q.shape
    past = ck.shape[2]
    tk = vt.shape[2]
    tq = _pick_tile(lat_seq, QUERY_GROUPS * 256)
    qt = lat_seq // tq
    kt = lat_seq // tk
    c_spec = pl.BlockSpec((None, None, past, LANES), lambda b, h, i: (b, layer, 0, h))
    q_spec = pl.BlockSpec((tq, LANES), lambda b, h, i: (b * qt + i, h))
    return pl.pallas_call(
        functools.partial(_lat_diff_kernel, out_scale=out_scale),
        grid=(n_lat_batch, d // LANES, qt),
        in_specs=[pl.BlockSpec(memory_space=pltpu.SMEM), q_spec,
                  pl.BlockSpec((lat_seq, LANES), lambda b, h, i: (b, h)),
                  pl.BlockSpec((kt, LANES, tk), lambda b, h, i: (b, h, 0)),
                  c_spec, c_spec, pl.BlockSpec((1, LANES), lambda b, h, i: (0, 0))],
        out_specs=q_spec,
        out_shape=jax.ShapeDtypeStruct((n, d), BF16),
        compiler_params=_cparams("arbitrary", "arbitrary", "arbitrary"),
        name="lat_diff_attn",
    )(lam, q, k, vt, ck, cv, subg)


def _post_attn_kernel(o_ref, *refs, ends, tile0):
    n_x = len(ends)
    x_refs = refs[:n_x]
    (mod_ref, wo_ref, g_ref, rwh_ref, rwl_ref, rb_ref, tri_ref, swgu_ref, swd_ref,
     xmid_ref, h_ref, sel_ref, gate_ref, rank_ref, cnt_ref, carry_ref) = refs[n_x:]
    i = pl.program_id(0)

    @pl.when(i == 0)
    def _():
        carry_ref[...] = jnp.zeros_like(carry_ref)

    tm = o_ref.shape[0]
    hm = tm // POST_SPLIT
    groups = [slice(r * hm, (r + 1) * hm) for r in range(POST_SPLIT)]
    x_in = _read_tokens(x_refs, i + tile0, ends)
    x1s = [x_in[rows] + mod_ref[2:3, :] * _dot(o_ref[rows, :], wo_ref[...]) for rows in groups]
    logits = []
    for rows, x1 in zip(groups, x1s):
        h = _rms(x1, g_ref[...]) * (1.0 + mod_ref[4:5, :]) + mod_ref[3:4, :]
        hb = h.astype(BF16)
        h_ref[rows, :] = _pack_rows(hb)
        gu = _dot(hb, swgu_ref[...])
        f = swd_ref.shape[0]
        shared = _dot((gu[:, :f] * _sigmoid(gu[:, :f]) * gu[:, f:]).astype(BF16), swd_ref[...])
        xmid_ref[rows, :] = x1 + mod_ref[5:6, :] * shared
        h_lo = (h - hb.astype(F32)).astype(BF16)
        logits.append(_dot_nt(rwh_ref[...], hb)
                      + (_dot_nt(rwl_ref[...], hb) + _dot_nt(rwh_ref[...], h_lo)))
    n_exp = rwh_ref.shape[0]
    row = lax.broadcasted_iota(jnp.int32, (n_exp, hm), 0).astype(F32)
    picked = []
    for lg in logits:
        scores = _sigmoid(lg)
        biased = scores + rb_ref[...]
        total = jnp.zeros((n_exp, hm), F32)
        sels, gates = [], []
        for _ in range(TOP_K):
            m = jnp.max(biased, axis=0, keepdims=True)
            idx = jnp.min(jnp.where(biased == m, row, float(n_exp)), axis=0, keepdims=True)
            hit = row == idx
            gates.append(jnp.sum(jnp.where(hit, scores, 0.0), axis=0, keepdims=True))
            biased = jnp.where(hit, -jnp.inf, biased)
            total = total + jnp.where(hit, 1.0, 0.0)
            sels.append(idx)
        picked.append((sels, gates, total))
    carry = carry_ref[:, 0:1]
    for rows, (sels, gates, total) in zip(groups, picked):
        denom = gates[0]
        for g in gates[1:]:
            denom = denom + g
        before = _dot(total.astype(BF16), tri_ref[...]) + carry
        for k in range(TOP_K):
            sel_ref[k:k + 1, rows] = sels[k].astype(jnp.int32)
            gate_ref[k:k + 1, rows] = gates[k] / denom * ROUTED_SCALE
            rank_ref[k:k + 1, rows] = jnp.sum(jnp.where(row == sels[k], before, 0.0), axis=0,
                                              keepdims=True).astype(jnp.int32)
        carry = carry + jnp.sum(total, axis=1, keepdims=True)
    carry_ref[...] = jnp.broadcast_to(carry, carry_ref.shape)
    cnt_ref[...] = carry_ref[...].astype(jnp.int32)


def _post_attn(o, xs, mod, wo, g, rwh, rwl, rb, swgu, swd, *, layer, tile0, n_tiles, lat_seq,
               n_lat_batch, tm):
    d = o.shape[1]
    f = swd.shape[0]
    n = n_tiles * tm
    n_exp = rwh.shape[0]
    hm = tm // POST_SPLIT
    tri = (jnp.arange(hm)[:, None] < jnp.arange(hm)[None, :]).astype(BF16)
    x_arrays, x_specs, ends = _token_parts(xs, tm, tile0, n_tiles)

    def mod_idx(i):
        return (layer, jnp.minimum((i + tile0) * tm // lat_seq, n_lat_batch), 0, 0)

    tok = pl.BlockSpec((tm, d), lambda i: (i, 0))
    tok_p = pl.BlockSpec((tm, d // 2), lambda i: (i, 0))
    kt = pl.BlockSpec((TOP_K, tm), lambda i: (0, i))
    const2 = lambda i: (0, 0)
    return pl.pallas_call(
        functools.partial(_post_attn_kernel, ends=tuple(ends), tile0=tile0),
        grid=(n_tiles,),
        in_specs=[pl.BlockSpec((tm, d), lambda i: (i + tile0, 0))] + x_specs + [
                  pl.BlockSpec((None, None, N_MOD, d), mod_idx),
                  pl.BlockSpec((d, d), const2), pl.BlockSpec((1, d), const2),
                  pl.BlockSpec((n_exp, d), const2), pl.BlockSpec((n_exp, d), const2),
                  pl.BlockSpec((n_exp, 1), const2), pl.BlockSpec((hm, hm), const2),
                  pl.BlockSpec((d, 2 * f), const2), pl.BlockSpec((f, d), const2)],
        out_specs=[tok, tok_p, kt, kt, kt, pl.BlockSpec((n_exp, LANES), const2)],
        out_shape=[jax.ShapeDtypeStruct((n, d), F32), jax.ShapeDtypeStruct((n, d // 2), jnp.int32),
                   jax.ShapeDtypeStruct((TOP_K, n), jnp.int32),
                   jax.ShapeDtypeStruct((TOP_K, n), F32),
                   jax.ShapeDtypeStruct((TOP_K, n), jnp.int32),
                   jax.ShapeDtypeStruct((n_exp, LANES), jnp.int32)],
        scratch_shapes=[pltpu.VMEM((n_exp, LANES), F32)],
        compiler_params=_cparams("arbitrary"),
        name=f"post_attn{layer}",
    )(o, *x_arrays, mod, wo, g, rwh, rwl, rb, tri, swgu, swd)


def _plan_kernel(cnt_ref, sel_ref, rank_ref, dest_ref, te_ref, tv_ref, nu_ref, *, tmb):
    n_exp = cnt_ref.shape[0]
    sel = sel_ref[...]
    tile_start = lax.broadcasted_iota(jnp.int32, te_ref.shape, 1) * tmb
    dest = rank_ref[...]
    te = jnp.zeros(te_ref.shape, jnp.int32)
    tv = jnp.zeros(te_ref.shape, jnp.int32)
    start = jnp.zeros((1, 1), jnp.int32)
    for e in range(n_exp):
        dest = dest + jnp.where(sel == e, start, 0)
        cnt = cnt_ref[e:e + 1, 0:1]
        stop = start + ((cnt + (tmb - 1)) & -tmb)
        mine = (tile_start >= start) & (tile_start < stop)
        tv = tv + jnp.where(mine, jnp.clip(start + cnt - tile_start, 0, tmb), 0)
        start = stop
        te = te + jnp.where(start <= tile_start, 1, 0)
    te_ref[...] = jnp.minimum(te, n_exp - 1)
    tv_ref[...] = tv
    nu_ref[...] = jnp.broadcast_to(start, nu_ref.shape)
    for c in range(dest_ref.shape[0]):
        dest_ref[c] = dest[:, c * SC_CHUNK:(c + 1) * SC_CHUNK]


def _plan(counts, sel, rank, *, tmb, n_tiles):
    k_top, n = sel.shape
    assert tmb & (tmb - 1) == 0
    tn = _pick_tile(n, 16 * SC_CHUNK)
    te_w = -(-n_tiles // LANES) * LANES
    tok = pl.BlockSpec((k_top, tn), lambda i: (0, i))
    const2 = lambda i: (0, 0)
    dest, te, tv, nu = pl.pallas_call(
        functools.partial(_plan_kernel, tmb=tmb),
        grid=(n // tn,),
        in_specs=[pl.BlockSpec(counts.shape, const2), tok, tok],
        out_specs=[pl.BlockSpec((tn // SC_CHUNK, k_top, SC_CHUNK), lambda i: (i, 0, 0)),
                   pl.BlockSpec((1, te_w), const2), pl.BlockSpec((1, te_w), const2),
                   pl.BlockSpec((1, LANES), const2)],
        out_shape=[jax.ShapeDtypeStruct((n // SC_CHUNK, k_top, SC_CHUNK), jnp.int32),
                   jax.ShapeDtypeStruct((1, te_w), jnp.int32),
                   jax.ShapeDtypeStruct((1, te_w), jnp.int32),
                   jax.ShapeDtypeStruct((1, LANES), jnp.int32)],
        compiler_params=_cparams("arbitrary"),
        name="plan",
    )(counts, sel, rank)
    return dest, te[0, :n_tiles], tv[0, :n_tiles], nu[0, :1] // tmb


def _expert_kernel(te_ref, tv_ref, nu_ref, x_ref, wg_ref, wu_ref, wd_ref, y_ref, wg_b, wu_b, wd_b):
    i = pl.program_id(0)
    half = x_ref.shape[0] // 2

    def rows(r):
        return jnp.concatenate(_unpack_rows(x_ref[r * half:(r + 1) * half, :]), axis=1).astype(BF16)

    def swiglu(xs):
        gu = [(_dot(x, wg_b[...]), _dot(x, wu_b[...])) for x in xs]
        acts = [(g * _sigmoid(g) * u).astype(BF16) for g, u in gu]
        return [_pack_rows(_dot(a, wd_b[...]).astype(BF16)) for a in acts]

    @pl.when(i < nu_ref[0])
    def _():
        @pl.when((i == 0) | (te_ref[i] != te_ref[jnp.maximum(i - 1, 0)]))
        def _():
            wg_b[...] = wg_ref[...].astype(BF16)
            wu_b[...] = wu_ref[...].astype(BF16)
            wd_b[...] = wd_ref[...].astype(BF16)

        @pl.when(tv_ref[i] > half)
        def _():
            ya, yb = swiglu([rows(0), rows(1)])
            y_ref[0:half, :] = ya
            y_ref[half:2 * half, :] = yb

        @pl.when(tv_ref[i] <= half)
        def _():
            y_ref[0:half, :] = swiglu([rows(0)])[0]


def _experts(tile_expert, tile_rows, n_used, xb, wg, wu, wd, *, layer, tmb):
    cap, dp = xb.shape
    f, d = wd.shape[2:]

    def row_idx(i, te, tv, nu):
        return (jnp.minimum(i, nu[0] - 1), 0)

    def w_idx(i, te, tv, nu):
        return (layer, te[i], 0, 0)

    return pl.pallas_call(
        _expert_kernel,
        grid_spec=pltpu.PrefetchScalarGridSpec(
            num_scalar_prefetch=3,
            grid=(cap // tmb,),
            in_specs=[pl.BlockSpec((tmb, dp), row_idx),
                      pl.BlockSpec((None, None, d, f), w_idx),
                      pl.BlockSpec((None, None, d, f), w_idx),
                      pl.BlockSpec((None, None, f, d), w_idx)],
            out_specs=pl.BlockSpec((tmb, dp), row_idx),
            scratch_shapes=[pltpu.VMEM((d, f), BF16), pltpu.VMEM((d, f), BF16),
                            pltpu.VMEM((f, d), BF16)]),
        out_shape=jax.ShapeDtypeStruct((cap, dp), jnp.int32),
        compiler_params=_cparams("arbitrary"),
        name="experts",
    )(tile_expert, tile_rows, n_used, xb, wg, wu, wd)


def _combine_kernel(x_ref, routed_ref, mod_ref, fg_ref, *refs, final, n_alias, tile0, n_lat_tiles):
    o_refs = refs[n_alias:]
    x2 = x_ref[...] + mod_ref[5:6, :] * routed_ref[...]
    if not final:
        o_refs[0][...] = x2
    elif len(o_refs) == 1:
        o_refs[0][...] = _rms(x2, fg_ref[...])
    else:
        y = _rms(x2, fg_ref[...])
        tile = pl.program_id(0) + tile0

        @pl.when(tile < n_lat_tiles)
        def _():
            o_refs[0][...] = y

        @pl.when(tile >= n_lat_tiles)
        def _():
            o_refs[1][...] = y


def _combine(x, routed, mod, final_g, y_lat, *, layer, tile0, n_lat, n_ctx, lat_seq, n_lat_batch, tm,
             final):
    n, d = x.shape
    n_tiles = n // tm
    n_lat_tiles = n_lat // tm

    def mod_idx(i):
        return (layer, jnp.minimum((i + tile0) * tm // lat_seq, n_lat_batch), 0, 0)

    tok = pl.BlockSpec((tm, d), lambda i: (i, 0))
    in_specs = [tok, tok, pl.BlockSpec((None, None, N_MOD, d), mod_idx),
                pl.BlockSpec((1, d), lambda i: (0, 0))]
    args = [x, routed, mod, final_g]
    aliases = {}
    has_lat = has_ctx = False
    if not final:
        out_specs, out_shape = [tok], [jax.ShapeDtypeStruct((n, d), F32)]
    else:
        has_lat = tile0 < n_lat_tiles
        has_ctx = tile0 + n_tiles > n_lat_tiles
        out_specs, out_shape = [], []
        if has_lat:
            out_specs.append(pl.BlockSpec(
                (tm, d), lambda i: (jnp.minimum(i + tile0, n_lat_tiles - 1), 0)))
            out_shape.append(jax.ShapeDtypeStruct((n_lat, d), F32))
            if y_lat is not None:
                aliases = {len(args): 0}
                in_specs.append(pl.BlockSpec(memory_space=pl.ANY))
                args.append(y_lat)
        if has_ctx:
            out_specs.append(pl.BlockSpec(
                (tm, d), lambda i: (jnp.maximum(i + tile0 - n_lat_tiles, 0), 0)))
            out_shape.append(jax.ShapeDtypeStruct((n_ctx, d), F32))
    outs = pl.pallas_call(
        functools.partial(_combine_kernel, final=final, n_alias=len(aliases), tile0=tile0,
                          n_lat_tiles=n_lat_tiles),
        grid=(n_tiles,),
        in_specs=in_specs,
        out_specs=out_specs,
        out_shape=out_shape,
        input_output_aliases=aliases,
        compiler_params=_cparams("arbitrary"),
        name=f"combine{layer}",
    )(*args)
    if not final:
        return outs[0]
    return (outs[0] if has_lat else y_lat), (outs[-1] if has_ctx else None)


SC_CHUNK = 128


def _sc_workers():
    info = plsc.get_sparse_core_info()
    return info.num_cores, info.num_subcores


def _sc_dispatch(rows, dest, cap):
    n, w = rows.shape
    nc, ns = _sc_workers()
    per_w = n // (nc * ns)
    assert per_w * nc * ns == n and per_w % SC_CHUNK == 0
    mesh = plsc.VectorSubcoreMesh(core_axis_name="c", subcore_axis_name="s")

    @functools.partial(
        pl.kernel, mesh=mesh, out_type=jax.ShapeDtypeStruct((cap, w), rows.dtype),
        scratch_types=[pltpu.VMEM((TOP_K, SC_CHUNK), jnp.int32),
                       pltpu.VMEM((SC_CHUNK, w), rows.dtype)],
        name="sc_dispatch")
    def run(rows_hbm, dest_hbm, out_hbm, idx_v, rows_v):
        wid = lax.axis_index("s") * nc + lax.axis_index("c")

        @pl.loop(0, per_w // SC_CHUNK)
        def _(ci):
            base = pl.multiple_of(wid * per_w + ci * SC_CHUNK, SC_CHUNK)
            pltpu.sync_copy(dest_hbm.at[wid * (per_w // SC_CHUNK) + ci], idx_v)
            pltpu.sync_copy(rows_hbm.at[pl.ds(base, SC_CHUNK)], rows_v)
            for k in range(TOP_K):
                pltpu.sync_copy(rows_v, out_hbm.at[idx_v.at[k]])

    return run(rows, dest)


SC_TOK = 8
SC_LANES = 16


def _sc_combine(table, dest, gates):
    n_chunks, k_top, _ = dest.shape
    n = n_chunks * SC_CHUNK
    w = table.shape[1]
    nc, ns = _sc_workers()
    per_w = n // (nc * ns)
    assert per_w * nc * ns == n and per_w % SC_CHUNK == 0 and w % (16 * SC_LANES) == 0
    gb = jnp.broadcast_to(
        gates.reshape(k_top, n_chunks, SC_CHUNK).transpose(1, 0, 2)[..., None],
        (n_chunks, k_top, SC_CHUNK, SC_LANES)).reshape(n_chunks, k_top, SC_CHUNK // SC_TOK,
                                                        SC_TOK * SC_LANES)
    mesh = plsc.VectorSubcoreMesh(core_axis_name="c", subcore_axis_name="s")

    n_blocks = SC_CHUNK // SC_TOK

    @functools.partial(
        pl.kernel, mesh=mesh, out_type=jax.ShapeDtypeStruct((n, 2 * w), F32),
        scratch_types=[pltpu.VMEM((k_top, SC_CHUNK), jnp.int32),
                       pltpu.VMEM((k_top, n_blocks, SC_TOK * SC_LANES), F32),
                       pltpu.VMEM((2, k_top, SC_TOK, w), table.dtype),
                       pltpu.VMEM((2, SC_TOK, 2 * w), F32),
                       pltpu.SemaphoreType.DMA((2,)), pltpu.SemaphoreType.DMA((2,))],
        compiler_params=pltpu.CompilerParams(needs_layout_passes=False),
        name="sc_combine")
    def run(table_hbm, dest_hbm, gb_hbm, out_hbm, idx_v, g_v, rows_v, out_v, sem_in, sem_out):
        wid = lax.axis_index("s") * nc + lax.axis_index("c")

        def gathers(blk, slot):
            off = pl.multiple_of(blk * SC_TOK, SC_TOK)
            return [pltpu.make_async_copy(table_hbm.at[idx_v.at[k, pl.ds(off, SC_TOK)]],
                                          rows_v.at[slot, k], sem_in.at[slot]) for k in range(k_top)]

        def write_back(chunk, blk, slot):
            row0 = pl.multiple_of(chunk * SC_CHUNK + blk * SC_TOK, SC_TOK)
            return pltpu.make_async_copy(out_v.at[slot], out_hbm.at[pl.ds(row0, SC_TOK)],
                                         sem_out.at[slot])

        def weighted_sum(blk, slot):
            @pl.loop(0, SC_TOK)
            def _(j):
                for base in range(0, w, 16 * SC_LANES):
                    acc = None
                    for k in range(k_top):
                        g = g_v[k, blk, pl.ds(j * SC_LANES, SC_LANES)]
                        terms = []
                        for v in range(16):
                            word = rows_v[slot, k, j, pl.ds(base + v * SC_LANES, SC_LANES)]
                            lo = lax.bitcast_convert_type(lax.shift_left(word, 16), F32)
                            hi = lax.bitcast_convert_type(word & jnp.int32(-65536), F32)
                            terms.append((g * lo, g * hi))
                        acc = terms if acc is None else [
                            (a + x, b + y) for (a, b), (x, y) in zip(acc, terms)]
                    for v, (a, b) in enumerate(acc):
                        out_v[slot, j, pl.ds(base + v * SC_LANES, SC_LANES)] = a
                        out_v[slot, j, pl.ds(w + base + v * SC_LANES, SC_LANES)] = b

        @pl.loop(0, per_w // SC_CHUNK)
        def _(ci):
            chunk = wid * (per_w // SC_CHUNK) + ci
            pltpu.sync_copy(dest_hbm.at[chunk], idx_v)
            pltpu.sync_copy(gb_hbm.at[chunk], g_v)
            for cp in gathers(0, 0):
                cp.start()

            @pl.loop(0, n_blocks, step=2)
            def _(b0):
                for slot in range(2):
                    blk = b0 + slot
                    for cp in gathers(blk, slot):
                        cp.wait()

                    @pl.when(blk + 1 < n_blocks)
                    def _():
                        for cp in gathers(blk + 1, 1 - slot):
                            cp.start()

                    @pl.when(blk >= 2)
                    def _():
                        write_back(chunk, blk - 2, slot).wait()

                    weighted_sum(blk, slot)
                    write_back(chunk, blk, slot).start()

            for slot in range(2):
                write_back(chunk, n_blocks - 2 + slot, slot).wait()


    return run(table, dest, gb)


def _split_bf16(w):
    hi = w.astype(BF16)
    return hi, (w - hi.astype(F32)).astype(BF16)


def kernel(x_prompt, x_sample, cache_k, cache_v, c, c_ctx, ada_w, ada_b, norm1_g, norm2_g, w_qkv, w_o, na_rpb, diff_lambda, diff_subln_g, router_w, router_b, exp_w_gate, exp_w_up, exp_w_down, shared_w_gate, shared_w_up, shared_w_down, final_g):
    batch, seq, d = x_prompt.shape
    n_lat_batch, lat_seq, _ = x_sample.shape
    depth = w_qkv.shape[0]
    n_exp = router_w.shape[-1]
    n_lat = n_lat_batch * lat_seq
    n_ctx = batch * seq
    n = n_lat + n_ctx
    assert d % LANES == 0 and na_rpb.shape[1] * HEAD_DIM == d
    assert diff_lambda.shape[-1] == HEAD_DIM and lat_seq % GRID_W == 0
    assert n_lat % seq == 0
    tm = _pick_tile(math.gcd(lat_seq, n_ctx), 512)
    tmb = 1024
    n_all = n // tm
    bounds = [round(n_all * sum(MOE_SHARES[:r]) / sum(MOE_SHARES)) for r in range(len(MOE_SHARES) + 1)]
    ranges = [(lo, hi - lo) for lo, hi in zip(bounds, bounds[1:])]
    scale = HEAD_DIM ** -0.5 * LOG2E

    xs = [x_sample.reshape(n_lat, d), x_prompt.reshape(n_ctx, d)]
    mod_rows = -(-(n_lat_batch + 1) // 8) * 8
    cond = jnp.zeros((mod_rows, d), F32).at[:n_lat_batch].set(c).at[n_lat_batch].set(c_ctx)
    mod = _modulation(cond, ada_w, ada_b).reshape(depth, mod_rows, N_MOD, d)

    w_qkv_b = w_qkv.astype(BF16)
    w_o_b = w_o.astype(BF16)
    rope_tab = _rope_tables(lat_seq, tm)
    caches = None
    for i in range(depth):
        is_diff = i % 2 == 1
        j = i // 2
        q, k, v, *caches = _qkv(xs, mod, norm1_g.reshape(depth, 1, d), w_qkv_b,
                                rope_tab if is_diff else None, caches, layer=i, n_lat=n_lat,
                                batch=batch, seq=seq, lat_seq=lat_seq, n_lat_batch=n_lat_batch,
                                tm=tm, scale=scale)
        if is_diff:
            vt = caches.pop()
        if not is_diff:
            o = _na_attention(q, k, v, cache_k, cache_v, na_rpb[j], layer=i,
                              n_lat_batch=n_lat_batch, lat_seq=lat_seq)
            o = _ctx_dense(q, k, v, o, n_lat=n_lat, seq=seq)
        else:
            lam_init = 0.8 - 0.6 * math.exp(-0.3 * i)
            lp = diff_lambda[j].astype(F32)
            lam = (jnp.exp(jnp.sum(lp[0] * lp[1])) - jnp.exp(jnp.sum(lp[2] * lp[3]))
                   + lam_init).reshape(1)
            subg = diff_subln_g[j].reshape(1, LANES).astype(F32)
            o = _lat_diff(lam, q, k, vt, cache_k, cache_v, subg, layer=i,
                          n_lat_batch=n_lat_batch, lat_seq=lat_seq, out_scale=1.0 - lam_init)
            o = _ctx_diff(lam, q, k, v, subg, o, n_lat=n_lat, seq=seq, out_scale=1.0 - lam_init)

        rwh, rwl = _split_bf16(router_w[i].T)
        swgu = jnp.concatenate([shared_w_gate[i], shared_w_up[i]], axis=-1).astype(BF16)
        final = i == depth - 1
        routed = []
        for tile0, nt in ranges:
            xmid, h2, sel, gates, rank, counts = _post_attn(
                o, xs, mod, w_o_b[i], norm2_g[i].reshape(1, d), rwh, rwl,
                router_b[i].reshape(n_exp, 1).astype(F32), swgu, shared_w_down[i].astype(BF16),
                layer=i, tile0=tile0, n_tiles=nt, lat_seq=lat_seq, n_lat_batch=n_lat_batch, tm=tm)
            n_tiles = -(-(nt * tm * TOP_K) // tmb) + n_exp
            dest, tile_expert, tile_rows, n_used = _plan(counts, sel, rank, tmb=tmb, n_tiles=n_tiles)
            xb = _sc_dispatch(h2, dest, n_tiles * tmb)
            routed.append((xmid, h2, gates, dest, tile_expert, tile_rows, n_used, xb))
        ybs = [_experts(te, tr, nu, xb, exp_w_gate, exp_w_up, exp_w_down, layer=i, tmb=tmb)
               for (_, _, _, _, te, tr, nu, xb) in routed]
        sums = [_sc_combine(yb, r[3], r[2]) for yb, r in zip(ybs, routed)]
        xs, y_lat, y_ctx = [], None, None
        for (tile0, nt), (xmid, *_), rsum in zip(ranges, routed, sums):
            out = _combine(xmid, rsum, mod, final_g.reshape(1, d), y_lat, layer=i, tile0=tile0, n_lat=n_lat,
                           n_ctx=n_ctx, lat_seq=lat_seq, n_lat_batch=n_lat_batch, tm=tm, final=final)
            if final:
                y_lat, y_ctx = out[0], (out[1] if out[1] is not None else y_ctx)
            else:
                xs.append(out)

    y_sample = y_lat.reshape(n_lat_batch, lat_seq, d)
    y_prompt = y_ctx.reshape(batch, seq, d)
    return (y_prompt, y_sample, caches[0], caches[1])
```

```python
import functools
import math

import jax
import jax.numpy as jnp
from jax import lax
from jax.experimental import pallas as pl
from jax.experimental.pallas import tpu as pltpu
from jax.experimental.pallas import tpu_sc as plsc

GRID_W = 64
NA_ROWS = 8
NA_COLS = 16
TOP_K = 8
ROUTED_SCALE = 2.5
ROPE_BASE = 10000.0
EPS = 1e-6
N_MOD = 6
HEAD_DIM = 64
LANES = 128
LOG2E = math.log2(math.e)
MASKED = -1e30
NA_QROWS = 2
NA_WIN = NA_ROWS + NA_QROWS - 1
POST_SPLIT = 1
MOE_RANGES = 2
QUERY_GROUPS = 4
VMEM_LIMIT = 56 * 1024 * 1024

F32 = jnp.float32
BF16 = jnp.bfloat16


def _cparams(*sem):
    return pltpu.CompilerParams(dimension_semantics=sem, vmem_limit_bytes=VMEM_LIMIT)


def _dot(a, b):
    return jnp.dot(a, b, preferred_element_type=F32)


def _dot_nt(a, b):
    return lax.dot_general(a, b, (((1,), (1,)), ((), ())), preferred_element_type=F32)


def _sigmoid(x):
    return 1.0 / (1.0 + jnp.exp(-x))


def _rms(x, g):
    return x * lax.rsqrt(jnp.mean(x * x, axis=-1, keepdims=True) + EPS) * g


def _pack_rows(xb):
    half = xb.shape[1] // 2
    u = lax.bitcast_convert_type(xb.astype(F32), jnp.uint32)
    packed = (u[:, :half] >> 16) | (u[:, half:] & jnp.uint32(0xFFFF0000))
    return lax.bitcast_convert_type(packed, jnp.int32)


def _unpack_rows(p):
    u = lax.bitcast_convert_type(p, jnp.uint32)
    return (lax.bitcast_convert_type(u << 16, F32),
            lax.bitcast_convert_type(u & jnp.uint32(0xFFFF0000), F32))


def _pick_tile(n, pref):
    t = pref
    while n % t:
        t //= 2
    return t


def _mod_kernel(c_ref, w_ref, b_ref, o_ref):
    c = c_ref[...]
    o_ref[...] = _dot(c * _sigmoid(c), w_ref[...]) + b_ref[...]


def _modulation(cond, ada_w, ada_b):
    depth, d, n6 = ada_w.shape
    rows = cond.shape[0]
    tn = _pick_tile(n6, 512)
    return pl.pallas_call(
        _mod_kernel,
        grid=(depth, n6 // tn),
        in_specs=[
            pl.BlockSpec((rows, d), lambda l, j: (0, 0)),
            pl.BlockSpec((None, d, tn), lambda l, j: (l, 0, j)),
            pl.BlockSpec((None, 1, tn), lambda l, j: (l, 0, j)),
        ],
        out_specs=pl.BlockSpec((None, rows, tn), lambda l, j: (l, 0, j)),
        out_shape=jax.ShapeDtypeStruct((depth, rows, n6), F32),
        compiler_params=_cparams("arbitrary", "arbitrary"),
        name="modulation",
    )(cond, ada_w, ada_b.reshape(depth, 1, n6))


def _read_tokens(x_refs, g, ends):
    x = x_refs[-1][...]
    for ref, end in zip(reversed(x_refs[:-1]), reversed(ends[:-1])):
        x = jnp.where(g < end, ref[...], x)
    return x


def _token_parts(xs, tm, tile0, n_tiles):
    arrays, specs, ends = [], [], []
    start = 0
    for x in xs:
        size = x.shape[0] // tm
        if start < tile0 + n_tiles and start + size > tile0:
            arrays.append(x)
            ends.append(start + size)
            specs.append(pl.BlockSpec(
                (tm, x.shape[1]), lambda i, s=start, z=size: (jnp.clip(i + tile0 - s, 0, z - 1), 0)))
        start += size
    return arrays, specs, ends


def _qkv_kernel(*refs, d, scale, rope, n_lat_tiles, ends, n_alias):
    n_x = len(ends)
    x_refs = refs[:n_x]
    mod_ref, g_ref, w_ref = refs[n_x:n_x + 3]
    rest = refs[n_x + 3:]
    if rope:
        rope_ref, wvt_ref, rest = rest[0], rest[1], rest[2:]
    rest = rest[n_alias:]
    q_ref, k_ref, v_ref, kf_ref, vf_ref = rest[:5]
    i = pl.program_id(0)
    x = _read_tokens(x_refs, i, ends)
    h = _rms(x, g_ref[...]) * (1.0 + mod_ref[1:2, :]) + mod_ref[0:1, :]
    hb = h.astype(BF16)

    def rot(t):
        if not rope:
            return t
        cols = []
        for j in range(d // LANES):
            tb = t[:, j * LANES:(j + 1) * LANES]
            cols.append(tb * rope_ref[0]
                        + pltpu.roll(tb, LANES - HEAD_DIM // 2, axis=1) * rope_ref[1]
                        + pltpu.roll(tb, HEAD_DIM // 2, axis=1) * rope_ref[2])
        return jnp.concatenate(cols, axis=1)

    q = _dot(hb, w_ref[:, 0:d])
    q_ref[...] = (rot(q) * scale).astype(BF16)
    k = _dot(hb, w_ref[:, d:2 * d])
    k_ref[...] = rot(k).astype(BF16)
    if rope:
        @pl.when(i < n_lat_tiles)
        def _():
            rest[5][...] = _dot_nt(wvt_ref[...], hb).astype(BF16)

    def context_values():
        v = _dot(hb, w_ref[:, 2 * d:3 * d])
        v_ref[...] = v.astype(BF16)
        return v

    if not rope:
        v = context_values()

    @pl.when(i >= n_lat_tiles)
    def _():
        kf_ref[...] = k.reshape(kf_ref.shape)
        vf_ref[...] = (context_values() if rope else v).reshape(vf_ref.shape)


def _qkv(xs, mod, g, w, rope_tab, caches, *, layer, n_lat, batch, seq, lat_seq, n_lat_batch, tm,
         scale):
    d = xs[0].shape[1]
    depth = w.shape[0]
    n = n_lat + batch * seq
    n_lat_tiles = n_lat // tm
    rope = rope_tab is not None
    assert tm % seq == 0
    x_arrays, x_specs, ends = _token_parts(xs, tm, 0, n // tm)

    def mod_idx(i):
        return (layer, jnp.minimum(i * tm // lat_seq, n_lat_batch), 0, 0)

    in_specs = x_specs + [
        pl.BlockSpec((None, None, N_MOD, d), mod_idx),
        pl.BlockSpec((None, 1, d), lambda i: (layer, 0, 0)),
        pl.BlockSpec((None, d, 3 * d), lambda i: (layer, 0, 0)),
    ]
    args = x_arrays + [mod, g, w]
    if rope:
        seq_tiles = lat_seq // tm
        in_specs.append(pl.BlockSpec(
            (3, tm, LANES),
            lambda i: (0, jnp.where(i < n_lat_tiles, i % seq_tiles, seq_tiles), 0)))
        in_specs.append(pl.BlockSpec((d, d), lambda i: (0, 0)))
        args += [rope_tab, w[layer, :, 2 * d:].T]
    aliases = {}
    if caches is not None:
        aliases = {len(args): 3, len(args) + 1: 4}
        in_specs += [pl.BlockSpec(memory_space=pl.ANY)] * 2
        args += list(caches)
    tok = pl.BlockSpec((tm, d), lambda i: (i, 0))
    cache_spec = pl.BlockSpec((tm // seq, None, seq, d),
                              lambda i: (jnp.maximum(i - n_lat_tiles, 0), layer, 0, 0))
    out_specs = [tok, tok, tok, cache_spec, cache_spec]
    out_shape = ([jax.ShapeDtypeStruct((n, d), BF16)] * 3
                 + [jax.ShapeDtypeStruct((batch, depth, seq, d), F32)] * 2)
    if rope:
        out_specs.append(pl.BlockSpec((None, d, tm),
                                      lambda i: (jnp.minimum(i, n_lat_tiles - 1), 0, 0)))
        out_shape.append(jax.ShapeDtypeStruct((n_lat_tiles, d, tm), BF16))
    return pl.pallas_call(
        functools.partial(_qkv_kernel, d=d, scale=scale, rope=rope, n_lat_tiles=n_lat_tiles,
                          ends=tuple(ends), n_alias=len(aliases)),
        grid=(n // tm,),
        in_specs=in_specs,
        out_specs=out_specs,
        out_shape=out_shape,
        input_output_aliases=aliases,
        compiler_params=_cparams("arbitrary"),
        name=f"qkv{layer}",
    )(*args)


def _rope_tables(lat_seq, tm):
    n_freq = HEAD_DIM // 4
    inv = ROPE_BASE ** (-jnp.arange(n_freq, dtype=F32) / n_freq)
    pos = jnp.arange(lat_seq)
    row = (pos // GRID_W).astype(F32)
    col = (pos % GRID_W).astype(F32)
    ang = jnp.concatenate([row[:, None] * inv, col[:, None] * inv], axis=-1)
    reps = LANES // (HEAD_DIM // 2)
    cos = jnp.tile(jnp.cos(ang), (1, reps))
    sin = jnp.tile(jnp.sin(ang), (1, reps))
    first_half = (jnp.arange(LANES) % HEAD_DIM) < HEAD_DIM // 2
    s_next = jnp.where(first_half, -sin, 0.0)
    s_prev = jnp.where(first_half, 0.0, sin)
    ident = jnp.stack([jnp.ones((tm, LANES), F32), jnp.zeros((tm, LANES), F32),
                       jnp.zeros((tm, LANES), F32)])
    return jnp.concatenate([jnp.stack([cos, s_next, s_prev]), ident], axis=1)


def _lane_lo():
    return lax.broadcasted_iota(jnp.int32, (1, LANES), 1) < HEAD_DIM


def _ctx_dense_kernel(q_ref, k_ref, v_ref, o_all_ref, o_ref, *, d):
    del o_all_ref
    lo = _lane_lo()
    blks = [slice(j * LANES, (j + 1) * LANES) for j in range(d // LANES)]
    scores = []
    for blk in blks:
        q2 = q_ref[:, blk]
        scores += [_dot_nt(jnp.where(lo, q2, 0), k_ref[:, blk]),
                   _dot_nt(jnp.where(lo, 0, q2), k_ref[:, blk])]
    probs = []
    for s in scores:
        p = jnp.exp2(s - jnp.max(s, axis=-1, keepdims=True))
        probs.append((p.astype(BF16), jnp.sum(p, axis=-1, keepdims=True)))
    outs = [_dot(p, v_ref[:, blks[h // 2]]) / l for h, (p, l) in enumerate(probs)]
    o_ref[...] = jnp.concatenate(
        [jnp.where(lo, outs[h], outs[h + 1]).astype(BF16) for h in range(0, len(outs), 2)], axis=1)


def _ctx_dense(q, k, v, o_all, *, n_lat, seq):
    n, d = q.shape
    off = n_lat // seq
    spec = pl.BlockSpec((seq, d), lambda b: (off + b, 0))
    return pl.pallas_call(
        functools.partial(_ctx_dense_kernel, d=d),
        grid=((n - n_lat) // seq,),
        in_specs=[spec, spec, spec, pl.BlockSpec(memory_space=pl.ANY)],
        out_specs=spec,
        out_shape=jax.ShapeDtypeStruct((n, d), BF16),
        input_output_aliases={3: 0},
        compiler_params=_cparams("arbitrary"),
        name="ctx_dense_attn",
    )(q, k, v, o_all)


def _subln(o, g_ref, out_scale):
    return _rms(o, g_ref[...]) * out_scale


def _ctx_diff_kernel(lam_ref, q_ref, k_ref, v_ref, g_ref, o_all_ref, o_ref, *, d, out_scale):
    del o_all_ref
    lo = _lane_lo()
    lam = lam_ref[0]
    blks = [slice(j * LANES, (j + 1) * LANES) for j in range(d // LANES)]
    scores = []
    for blk in blks:
        q2 = q_ref[:, blk]
        scores.append((_dot_nt(jnp.where(lo, q2, 0), k_ref[:, blk]),
                       _dot_nt(jnp.where(lo, 0, q2), k_ref[:, blk])))
    probs = []
    for pair in scores:
        ps = []
        for s in pair:
            p = jnp.exp2(s - jnp.max(s, axis=-1, keepdims=True))
            ps.append(p / jnp.sum(p, axis=-1, keepdims=True))
        probs.append((ps[0] - lam * ps[1]).astype(BF16))
    outs = [_dot(p, v_ref[:, blk]) for p, blk in zip(probs, blks)]
    o_ref[...] = jnp.concatenate([_subln(o, g_ref, out_scale).astype(BF16) for o in outs], axis=1)


def _ctx_diff(lam, q, k, v, subg, o_all, *, n_lat, seq, out_scale):
    n, d = q.shape
    off = n_lat // seq
    spec = pl.BlockSpec((seq, d), lambda b: (off + b, 0))
    return pl.pallas_call(
        functools.partial(_ctx_diff_kernel, d=d, out_scale=out_scale),
        grid=((n - n_lat) // seq,),
        in_specs=[pl.BlockSpec(memory_space=pltpu.SMEM), spec, spec, spec,
                  pl.BlockSpec((1, LANES), lambda b: (0, 0)), pl.BlockSpec(memory_space=pl.ANY)],
        out_specs=spec,
        out_shape=jax.ShapeDtypeStruct((n, d), BF16),
        input_output_aliases={5: 0},
        compiler_params=_cparams("arbitrary"),
        name="ctx_diff_attn",
    )(lam, q, k, v, subg, o_all)


def _na_windows(rows):
    starts, classes, geoms = [], [], []
    for r0 in range(0, rows, NA_QROWS):
        first = [min(max(r0 + i - NA_ROWS // 2, 0), rows - NA_ROWS) for i in range(NA_QROWS)]
        ws = min(first[0], rows - NA_WIN)
        geom = tuple(
            tuple((ws + u) - (r0 + i) + NA_ROWS - 1 if first[i] <= ws + u < first[i] + NA_ROWS else -1
                  for u in range(NA_WIN)) for i in range(NA_QROWS))
        if geom not in geoms:
            geoms.append(geom)
        starts.append(ws)
        classes.append(geoms.index(geom))
    return starts, classes, geoms


def _na_kernel(ws_ref, cls_ref, q_ref, k_ref, v_ref, ck_ref, cv_ref, bias_ref, o_ref, *, d):
    del cls_ref
    lo = _lane_lo()
    win = NA_WIN * GRID_W
    start = pl.multiple_of(ws_ref[pl.program_id(1)] * GRID_W, GRID_W)
    n_heads = 2 * (d // LANES)

    def blk(h):
        return slice(h // 2 * LANES, (h // 2 + 1) * LANES)

    scores = []
    for h in range(n_heads):
        q2 = q_ref[:, blk(h)]
        qm = jnp.where(lo, q2, 0) if h % 2 == 0 else jnp.where(lo, 0, q2)
        scores.append((_dot_nt(qm, k_ref[pl.ds(start, win), blk(h)]) + bias_ref[h],
                       _dot_nt(qm, ck_ref[:, blk(h)].astype(BF16))))
    probs = []
    for s_loc, s_ctx in scores:
        m = jnp.maximum(jnp.max(s_loc, axis=-1, keepdims=True),
                        jnp.max(s_ctx, axis=-1, keepdims=True))
        p_loc = jnp.exp2(s_loc - m)
        p_ctx = jnp.exp2(s_ctx - m)
        l = jnp.sum(p_loc, axis=-1, keepdims=True) + jnp.sum(p_ctx, axis=-1, keepdims=True)
        probs.append((p_loc.astype(BF16), p_ctx.astype(BF16), l))
    outs = [(_dot(p_loc, v_ref[pl.ds(start, win), blk(h)])
             + _dot(p_ctx, cv_ref[:, blk(h)].astype(BF16))) / l
            for h, (p_loc, p_ctx, l) in enumerate(probs)]
    o_ref[...] = jnp.concatenate(
        [jnp.where(lo, outs[h], outs[h + 1]).astype(BF16) for h in range(0, n_heads, 2)], axis=1)


def _na_bias_table(rpb, geoms):
    h = rpb.shape[0]
    c = jnp.arange(GRID_W)
    kc = jnp.arange(GRID_W)
    c0 = jnp.clip(c - NA_COLS // 2, 0, GRID_W - NA_COLS)
    off_c = kc[None, :] - c[:, None] + (NA_COLS - 1)
    inside = (kc[None, :] >= c0[:, None]) & (kc[None, :] < c0[:, None] + NA_COLS)
    oh_c = jax.nn.one_hot(off_c, 2 * NA_COLS - 1, dtype=F32)
    blocks = jnp.einsum('hrs,cqs->rhcq', rpb.astype(F32), oh_c, precision=lax.Precision.HIGHEST)
    blocks = jnp.where(inside, blocks * LOG2E, MASKED)
    masked = jnp.full((h, GRID_W, GRID_W), MASKED, F32)
    return jnp.stack([
        jnp.concatenate([
            jnp.concatenate([blocks[r] if r >= 0 else masked for r in row], axis=-1)
            for row in geom], axis=-2)
        for geom in geoms])


def _na_attention(q, k, v, ck, cv, rpb, *, layer, n_lat_batch, lat_seq):
    n, d = q.shape
    rows = lat_seq // GRID_W
    past = ck.shape[2]
    n_heads = rpb.shape[0]
    assert rows % NA_QROWS == 0 and rows >= NA_WIN
    starts, classes, geoms = _na_windows(rows)
    bias = _na_bias_table(rpb, geoms)
    groups = rows // NA_QROWS
    tq = NA_QROWS * GRID_W
    kv_spec = pl.BlockSpec((lat_seq, d), lambda b, r, ws, cls: (b, 0))
    c_spec = pl.BlockSpec((None, None, past, d), lambda b, r, ws, cls: (b, layer, 0, 0))
    q_spec = pl.BlockSpec((tq, d), lambda b, r, ws, cls: (b * groups + r, 0))
    return pl.pallas_call(
        functools.partial(_na_kernel, d=d),
        grid_spec=pltpu.PrefetchScalarGridSpec(
            num_scalar_prefetch=2,
            grid=(n_lat_batch, groups),
            in_specs=[q_spec, kv_spec, kv_spec, c_spec, c_spec,
                      pl.BlockSpec((None, n_heads, tq, NA_WIN * GRID_W),
                                   lambda b, r, ws, cls: (cls[r], 0, 0, 0))],
            out_specs=q_spec),
        out_shape=jax.ShapeDtypeStruct((n, d), BF16),
        compiler_params=_cparams("arbitrary", "arbitrary"),
        name="na_attn",
    )(jnp.asarray(starts, jnp.int32), jnp.asarray(classes, jnp.int32), q, k, v, ck, cv, bias)


def _lat_diff_kernel(lam_ref, q_ref, k_ref, vt_ref, ck_ref, cv_ref, g_ref, o_ref, *, out_scale):
    lo = _lane_lo()
    lam = lam_ref[0]
    q2 = q_ref[...]
    tq = q2.shape[0]
    tg = tq // QUERY_GROUPS
    qs = []
    for g in range(QUERY_GROUPS):
        qg = q2[g * tg:(g + 1) * tg, :]
        qs += [jnp.where(lo, qg, 0), jnp.where(lo, 0, qg)]

    def softmax_step(state, s):
        m, l, acc = state
        m_new = jnp.maximum(m, jnp.max(s, axis=0, keepdims=True))
        alpha = jnp.exp2(m - m_new)
        p = jnp.exp2(s - m_new)
        return m_new, alpha * l + jnp.sum(p, axis=0, keepdims=True), alpha * acc, p.astype(BF16)

    def scores(kt):
        return [_dot_nt(kt, qm) for qm in qs]

    def absorb(states, ss, vt):
        out = []
        for g in range(0, len(qs), 2):
            stepped = [softmax_step(states[g + j], ss[g + j]) for j in range(2)]
            out += [(m, l, acc + _dot(vt, p)) for m, l, acc, p in stepped]
        return out

    tile = vt_ref.shape[2]
    n_tiles = vt_ref.shape[0]

    def keys(t):
        return k_ref[t * tile:(t + 1) * tile, :] if t < n_tiles else ck_ref[...].astype(BF16)

    def values_t(t):
        return vt_ref[t] if t < n_tiles else cv_ref[...].T.astype(BF16)

    init = (jnp.full((1, tg), -jnp.inf, F32), jnp.zeros((1, tg), F32), jnp.zeros((LANES, tg), F32))
    states = [init] * len(qs)
    ss = scores(keys(0))
    for t in range(n_tiles + 1):
        ss_next = scores(keys(t + 1)) if t < n_tiles else None
        states = absorb(states, ss, values_t(t))
        ss = ss_next
    o = jnp.concatenate([states[g][2] / states[g][1] - lam * (states[g + 1][2] / states[g + 1][1])
                         for g in range(0, len(qs), 2)], axis=1).T
    o_ref[...] = _subln(o, g_ref, out_scale).astype(BF16)


def _lat_diff(lam, q, k, vt, ck, cv, subg, *, layer, n_lat_batch, lat_seq, out_scale):
    n, d = q.shape
    past = ck.shape[2]
    tk = vt.shape[2]
    tq = _pick_tile(lat_seq, QUERY_GROUPS * 256)
    qt = lat_seq // tq
    kt = lat_seq // tk
    c_spec = pl.BlockSpec((None, None, past, LANES), lambda b, h, i: (b, layer, 0, h))
    q_spec = pl.BlockSpec((tq, LANES), lambda b, h, i: (b * qt + i, h))
    return pl.pallas_call(
        functools.partial(_lat_diff_kernel, out_scale=out_scale),
        grid=(n_lat_batch, d // LANES, qt),
        in_specs=[pl.BlockSpec(memory_space=pltpu.SMEM), q_spec,
                  pl.BlockSpec((lat_seq, LANES), lambda b, h, i: (b, h)),
                  pl.BlockSpec((kt, LANES, tk), lambda b, h, i: (b, h, 0)),
                  c_spec, c_spec, pl.BlockSpec((1, LANES), lambda b, h, i: (0, 0))],
        out_specs=q_spec,
        out_shape=jax.ShapeDtypeStruct((n, d), BF16),
        compiler_params=_cparams("arbitrary", "arbitrary", "arbitrary"),
        name="lat_diff_attn",
    )(lam, q, k, vt, ck, cv, subg)


def _post_attn_kernel(o_ref, *refs, ends, tile0):
    n_x = len(ends)
    x_refs = refs[:n_x]
    (mod_ref, wo_ref, g_ref, rwh_ref, rwl_ref, rb_ref, tri_ref,
     xmid_ref, h_ref, sel_ref, gate_ref, rank_ref, cnt_ref, carry_ref) = refs[n_x:]
    i = pl.program_id(0)

    @pl.when(i == 0)
    def _():
        carry_ref[...] = jnp.zeros_like(carry_ref)

    tm = o_ref.shape[0]
    hm = tm // POST_SPLIT
    groups = [slice(r * hm, (r + 1) * hm) for r in range(POST_SPLIT)]
    x_in = _read_tokens(x_refs, i + tile0, ends)
    x1s = [x_in[rows] + mod_ref[2:3, :] * _dot(o_ref[rows, :], wo_ref[...]) for rows in groups]
    logits = []
    for rows, x1 in zip(groups, x1s):
        xmid_ref[rows, :] = x1
        h = _rms(x1, g_ref[...]) * (1.0 + mod_ref[4:5, :]) + mod_ref[3:4, :]
        hb = h.astype(BF16)
        h_ref[rows, :] = _pack_rows(hb)
        h_lo = (h - hb.astype(F32)).astype(BF16)
        logits.append(_dot_nt(rwh_ref[...], hb)
                      + (_dot_nt(rwl_ref[...], hb) + _dot_nt(rwh_ref[...], h_lo)))
    n_exp = rwh_ref.shape[0]
    row = lax.broadcasted_iota(jnp.int32, (n_exp, hm), 0).astype(F32)
    picked = []
    for lg in logits:
        scores = _sigmoid(lg)
        biased = scores + rb_ref[...]
        total = jnp.zeros((n_exp, hm), F32)
        sels, gates = [], []
        for _ in range(TOP_K):
            m = jnp.max(biased, axis=0, keepdims=True)
            idx = jnp.min(jnp.where(biased == m, row, float(n_exp)), axis=0, keepdims=True)
            hit = row == idx
            gates.append(jnp.sum(jnp.where(hit, scores, 0.0), axis=0, keepdims=True))
            biased = jnp.where(hit, -jnp.inf, biased)
            total = total + jnp.where(hit, 1.0, 0.0)
            sels.append(idx)
        picked.append((sels, gates, total))
    carry = carry_ref[:, 0:1]
    for rows, (sels, gates, total) in zip(groups, picked):
        denom = gates[0]
        for g in gates[1:]:
            denom = denom + g
        before = _dot(total.astype(BF16), tri_ref[...]) + carry
        for k in range(TOP_K):
            sel_ref[k:k + 1, rows] = sels[k].astype(jnp.int32)
            gate_ref[k:k + 1, rows] = gates[k] / denom * ROUTED_SCALE
            rank_ref[k:k + 1, rows] = jnp.sum(jnp.where(row == sels[k], before, 0.0), axis=0,
                                              keepdims=True).astype(jnp.int32)
        carry = carry + jnp.sum(total, axis=1, keepdims=True)
    carry_ref[...] = jnp.broadcast_to(carry, carry_ref.shape)
    cnt_ref[...] = carry_ref[...].astype(jnp.int32)


def _post_attn(o, xs, mod, wo, g, rwh, rwl, rb, *, layer, tile0, n_tiles, lat_seq, n_lat_batch, tm):
    d = o.shape[1]
    n = n_tiles * tm
    n_exp = rwh.shape[0]
    hm = tm // POST_SPLIT
    tri = (jnp.arange(hm)[:, None] < jnp.arange(hm)[None, :]).astype(BF16)
    x_arrays, x_specs, ends = _token_parts(xs, tm, tile0, n_tiles)

    def mod_idx(i):
        return (layer, jnp.minimum((i + tile0) * tm // lat_seq, n_lat_batch), 0, 0)

    tok = pl.BlockSpec((tm, d), lambda i: (i, 0))
    tok_p = pl.BlockSpec((tm, d // 2), lambda i: (i, 0))
    kt = pl.BlockSpec((TOP_K, tm), lambda i: (0, i))
    const2 = lambda i: (0, 0)
    return pl.pallas_call(
        functools.partial(_post_attn_kernel, ends=tuple(ends), tile0=tile0),
        grid=(n_tiles,),
        in_specs=[pl.BlockSpec((tm, d), lambda i: (i + tile0, 0))] + x_specs + [
                  pl.BlockSpec((None, None, N_MOD, d), mod_idx),
                  pl.BlockSpec((d, d), const2), pl.BlockSpec((1, d), const2),
                  pl.BlockSpec((n_exp, d), const2), pl.BlockSpec((n_exp, d), const2),
                  pl.BlockSpec((n_exp, 1), const2), pl.BlockSpec((hm, hm), const2)],
        out_specs=[tok, tok_p, kt, kt, kt, pl.BlockSpec((n_exp, LANES), const2)],
        out_shape=[jax.ShapeDtypeStruct((n, d), F32), jax.ShapeDtypeStruct((n, d // 2), jnp.int32),
                   jax.ShapeDtypeStruct((TOP_K, n), jnp.int32),
                   jax.ShapeDtypeStruct((TOP_K, n), F32),
                   jax.ShapeDtypeStruct((TOP_K, n), jnp.int32),
                   jax.ShapeDtypeStruct((n_exp, LANES), jnp.int32)],
        scratch_shapes=[pltpu.VMEM((n_exp, LANES), F32)],
        compiler_params=_cparams("arbitrary"),
        name=f"post_attn{layer}",
    )(o, *x_arrays, mod, wo, g, rwh, rwl, rb, tri)


def _plan_kernel(cnt_ref, sel_ref, rank_ref, dest_ref, te_ref, tv_ref, nu_ref, *, tmb):
    n_exp = cnt_ref.shape[0]
    sel = sel_ref[...]
    tile_start = lax.broadcasted_iota(jnp.int32, te_ref.shape, 1) * tmb
    dest = rank_ref[...]
    te = jnp.zeros(te_ref.shape, jnp.int32)
    tv = jnp.zeros(te_ref.shape, jnp.int32)
    start = jnp.zeros((1, 1), jnp.int32)
    for e in range(n_exp):
        dest = dest + jnp.where(sel == e, start, 0)
        cnt = cnt_ref[e:e + 1, 0:1]
        stop = start + ((cnt + (tmb - 1)) & -tmb)
        mine = (tile_start >= start) & (tile_start < stop)
        tv = tv + jnp.where(mine, jnp.clip(start + cnt - tile_start, 0, tmb), 0)
        start = stop
        te = te + jnp.where(start <= tile_start, 1, 0)
    te_ref[...] = jnp.minimum(te, n_exp - 1)
    tv_ref[...] = tv
    nu_ref[...] = jnp.broadcast_to(start, nu_ref.shape)
    for c in range(dest_ref.shape[0]):
        dest_ref[c] = dest[:, c * SC_CHUNK:(c + 1) * SC_CHUNK]


def _plan(counts, sel, rank, *, tmb, n_tiles):
    k_top, n = sel.shape
    assert tmb & (tmb - 1) == 0
    tn = _pick_tile(n, 16 * SC_CHUNK)
    te_w = -(-n_tiles // LANES) * LANES
    tok = pl.BlockSpec((k_top, tn), lambda i: (0, i))
    const2 = lambda i: (0, 0)
    dest, te, tv, nu = pl.pallas_call(
        functools.partial(_plan_kernel, tmb=tmb),
        grid=(n // tn,),
        in_specs=[pl.BlockSpec(counts.shape, const2), tok, tok],
        out_specs=[pl.BlockSpec((tn // SC_CHUNK, k_top, SC_CHUNK), lambda i: (i, 0, 0)),
                   pl.BlockSpec((1, te_w), const2), pl.BlockSpec((1, te_w), const2),
                   pl.BlockSpec((1, LANES), const2)],
        out_shape=[jax.ShapeDtypeStruct((n // SC_CHUNK, k_top, SC_CHUNK), jnp.int32),
                   jax.ShapeDtypeStruct((1, te_w), jnp.int32),
                   jax.ShapeDtypeStruct((1, te_w), jnp.int32),
                   jax.ShapeDtypeStruct((1, LANES), jnp.int32)],
        compiler_params=_cparams("arbitrary"),
        name="plan",
    )(counts, sel, rank)
    return dest, te[0, :n_tiles], tv[0, :n_tiles], nu[0, :1] // tmb


def _expert_kernel(te_ref, tv_ref, nu_ref, x_ref, wg_ref, wu_ref, wd_ref, y_ref, wg_b, wu_b, wd_b):
    i = pl.program_id(0)
    half = x_ref.shape[0] // 2

    def rows(r):
        return jnp.concatenate(_unpack_rows(x_ref[r * half:(r + 1) * half, :]), axis=1).astype(BF16)

    def swiglu(xs):
        gu = [(_dot(x, wg_b[...]), _dot(x, wu_b[...])) for x in xs]
        acts = [(g * _sigmoid(g) * u).astype(BF16) for g, u in gu]
        return [_pack_rows(_dot(a, wd_b[...]).astype(BF16)) for a in acts]

    @pl.when(i < nu_ref[0])
    def _():
        @pl.when((i == 0) | (te_ref[i] != te_ref[jnp.maximum(i - 1, 0)]))
        def _():
            wg_b[...] = wg_ref[...].astype(BF16)
            wu_b[...] = wu_ref[...].astype(BF16)
            wd_b[...] = wd_ref[...].astype(BF16)

        @pl.when(tv_ref[i] > half)
        def _():
            ya, yb = swiglu([rows(0), rows(1)])
            y_ref[0:half, :] = ya
            y_ref[half:2 * half, :] = yb

        @pl.when(tv_ref[i] <= half)
        def _():
            y_ref[0:half, :] = swiglu([rows(0)])[0]


def _experts(tile_expert, tile_rows, n_used, xb, wg, wu, wd, *, layer, tmb):
    cap, dp = xb.shape
    f, d = wd.shape[2:]

    def row_idx(i, te, tv, nu):
        return (jnp.minimum(i, nu[0] - 1), 0)

    def w_idx(i, te, tv, nu):
        return (layer, te[i], 0, 0)

    return pl.pallas_call(
        _expert_kernel,
        grid_spec=pltpu.PrefetchScalarGridSpec(
            num_scalar_prefetch=3,
            grid=(cap // tmb,),
            in_specs=[pl.BlockSpec((tmb, dp), row_idx),
                      pl.BlockSpec((None, None, d, f), w_idx),
                      pl.BlockSpec((None, None, d, f), w_idx),
                      pl.BlockSpec((None, None, f, d), w_idx)],
            out_specs=pl.BlockSpec((tmb, dp), row_idx),
            scratch_shapes=[pltpu.VMEM((d, f), BF16), pltpu.VMEM((d, f), BF16),
                            pltpu.VMEM((f, d), BF16)]),
        out_shape=jax.ShapeDtypeStruct((cap, dp), jnp.int32),
        compiler_params=_cparams("arbitrary"),
        name="experts",
    )(tile_expert, tile_rows, n_used, xb, wg, wu, wd)


def _combine_kernel(x_ref, h_ref, routed_ref, mod_ref, wgu_ref, wd_ref, fg_ref, *refs, f, final,
                    n_alias, tile0, n_lat_tiles):
    o_refs = refs[n_alias:]
    h = jnp.concatenate(_unpack_rows(h_ref[...]), axis=1).astype(BF16)
    gu = _dot(h, wgu_ref[...])
    gate, up = gu[:, :f], gu[:, f:]
    shared = _dot((gate * _sigmoid(gate) * up).astype(BF16), wd_ref[...])
    x2 = x_ref[...] + mod_ref[5:6, :] * (shared + routed_ref[...])
    if not final:
        o_refs[0][...] = x2
    elif len(o_refs) == 1:
        o_refs[0][...] = _rms(x2, fg_ref[...])
    else:
        y = _rms(x2, fg_ref[...])
        tile = pl.program_id(0) + tile0

        @pl.when(tile < n_lat_tiles)
        def _():
            o_refs[0][...] = y

        @pl.when(tile >= n_lat_tiles)
        def _():
            o_refs[1][...] = y


def _combine(x, h, routed, mod, wgu, wd, final_g, y_lat, *, layer, tile0, n_lat, n_ctx, lat_seq,
             n_lat_batch, tm, final):
    n, d = x.shape
    f = wd.shape[0]
    n_tiles = n // tm
    n_lat_tiles = n_lat // tm

    def mod_idx(i):
        return (layer, jnp.minimum((i + tile0) * tm // lat_seq, n_lat_batch), 0, 0)

    tok = pl.BlockSpec((tm, d), lambda i: (i, 0))
    tok_p = pl.BlockSpec((tm, d // 2), lambda i: (i, 0))
    const2 = lambda i: (0, 0)
    in_specs = [tok, tok_p, tok,
                pl.BlockSpec((None, None, N_MOD, d), mod_idx),
                pl.BlockSpec((d, 2 * f), const2), pl.BlockSpec((f, d), const2),
                pl.BlockSpec((1, d), const2)]
    args = [x, h, routed, mod, wgu, wd, final_g]
    aliases = {}
    has_lat = has_ctx = False
    if not final:
        out_specs, out_shape = [tok], [jax.ShapeDtypeStruct((n, d), F32)]
    else:
        has_lat = tile0 < n_lat_tiles
        has_ctx = tile0 + n_tiles > n_lat_tiles
        out_specs, out_shape = [], []
        if has_lat:
            out_specs.append(pl.BlockSpec(
                (tm, d), lambda i: (jnp.minimum(i + tile0, n_lat_tiles - 1), 0)))
            out_shape.append(jax.ShapeDtypeStruct((n_lat, d), F32))
            if y_lat is not None:
                aliases = {len(args): 0}
                in_specs.append(pl.BlockSpec(memory_space=pl.ANY))
                args.append(y_lat)
        if has_ctx:
            out_specs.append(pl.BlockSpec(
                (tm, d), lambda i: (jnp.maximum(i + tile0 - n_lat_tiles, 0), 0)))
            out_shape.append(jax.ShapeDtypeStruct((n_ctx, d), F32))
    outs = pl.pallas_call(
        functools.partial(_combine_kernel, f=f, final=final, n_alias=len(aliases), tile0=tile0,
                          n_lat_tiles=n_lat_tiles),
        grid=(n_tiles,),
        in_specs=in_specs,
        out_specs=out_specs,
        out_shape=out_shape,
        input_output_aliases=aliases,
        compiler_params=_cparams("arbitrary"),
        name=f"combine{layer}",
    )(*args)
    if not final:
        return outs[0]
    return (outs[0] if has_lat else y_lat), (outs[-1] if has_ctx else None)


SC_CHUNK = 128


def _sc_workers():
    info = plsc.get_sparse_core_info()
    return info.num_cores, info.num_subcores


def _sc_dispatch(rows, dest, cap):
    n, w = rows.shape
    nc, ns = _sc_workers()
    per_w = n // (nc * ns)
    assert per_w * nc * ns == n and per_w % SC_CHUNK == 0
    mesh = plsc.VectorSubcoreMesh(core_axis_name="c", subcore_axis_name="s")

    @functools.partial(
        pl.kernel, mesh=mesh, out_type=jax.ShapeDtypeStruct((cap, w), rows.dtype),
        scratch_types=[pltpu.VMEM((TOP_K, SC_CHUNK), jnp.int32),
                       pltpu.VMEM((SC_CHUNK, w), rows.dtype)],
        name="sc_dispatch")
    def run(rows_hbm, dest_hbm, out_hbm, idx_v, rows_v):
        wid = lax.axis_index("s") * nc + lax.axis_index("c")

        @pl.loop(0, per_w // SC_CHUNK)
        def _(ci):
            base = pl.multiple_of(wid * per_w + ci * SC_CHUNK, SC_CHUNK)
            pltpu.sync_copy(dest_hbm.at[wid * (per_w // SC_CHUNK) + ci], idx_v)
            pltpu.sync_copy(rows_hbm.at[pl.ds(base, SC_CHUNK)], rows_v)
            for k in range(TOP_K):
                pltpu.sync_copy(rows_v, out_hbm.at[idx_v.at[k]])

    return run(rows, dest)


SC_TOK = 8
SC_LANES = 16


def _sc_combine(table, dest, gates):
    n_chunks, k_top, _ = dest.shape
    n = n_chunks * SC_CHUNK
    w = table.shape[1]
    nc, ns = _sc_workers()
    per_w = n // (nc * ns)
    assert per_w * nc * ns == n and per_w % SC_CHUNK == 0 and w % (16 * SC_LANES) == 0
    gb = gates.reshape(k_top, n_chunks, SC_CHUNK).transpose(1, 0, 2)
    mesh = plsc.VectorSubcoreMesh(core_axis_name="c", subcore_axis_name="s")

    n_blocks = SC_CHUNK // SC_TOK

    @functools.partial(
        pl.kernel, mesh=mesh, out_type=jax.ShapeDtypeStruct((n, 2 * w), F32),
        scratch_types=[pltpu.VMEM((k_top, SC_CHUNK), jnp.int32),
                       pltpu.VMEM((k_top, SC_CHUNK), F32),
                       pltpu.VMEM((2, k_top, SC_TOK, w), table.dtype),
                       pltpu.VMEM((2, SC_TOK, 2 * w), F32),
                       pltpu.SemaphoreType.DMA((2,)), pltpu.SemaphoreType.DMA((2,))],
        compiler_params=pltpu.CompilerParams(needs_layout_passes=False),
        name="sc_combine")
    def run(table_hbm, dest_hbm, gb_hbm, out_hbm, idx_v, g_v, rows_v, out_v, sem_in, sem_out):
        wid = lax.axis_index("s") * nc + lax.axis_index("c")

        def gathers(blk, slot):
            off = pl.multiple_of(blk * SC_TOK, SC_TOK)
            return [pltpu.make_async_copy(table_hbm.at[idx_v.at[k, pl.ds(off, SC_TOK)]],
                                          rows_v.at[slot, k], sem_in.at[slot]) for k in range(k_top)]

        def write_back(chunk, blk, slot):
            row0 = pl.multiple_of(chunk * SC_CHUNK + blk * SC_TOK, SC_TOK)
            return pltpu.make_async_copy(out_v.at[slot], out_hbm.at[pl.ds(row0, SC_TOK)],
                                         sem_out.at[slot])

        def weighted_sum(blk, slot):
            @pl.loop(0, SC_TOK)
            def _(j):
                for base in range(0, w, 16 * SC_LANES):
                    acc = None
                    for k in range(k_top):
                        tok = blk * SC_TOK + j
                        g16 = g_v[k, pl.ds(pl.multiple_of(tok // SC_LANES * SC_LANES, SC_LANES),
                                           SC_LANES)]
                        g = g16.at[jnp.full((SC_LANES,), tok % SC_LANES, jnp.int32)].get(
                            mode="promise_in_bounds")
                        terms = []
                        for v in range(16):
                            word = rows_v[slot, k, j, pl.ds(base + v * SC_LANES, SC_LANES)]
                            lo = lax.bitcast_convert_type(lax.shift_left(word, 16), F32)
                            hi = lax.bitcast_convert_type(word & jnp.int32(-65536), F32)
                            terms.append((g * lo, g * hi))
                        acc = terms if acc is None else [
                            (a + x, b + y) for (a, b), (x, y) in zip(acc, terms)]
                    for v, (a, b) in enumerate(acc):
                        out_v[slot, j, pl.ds(base + v * SC_LANES, SC_LANES)] = a
                        out_v[slot, j, pl.ds(w + base + v * SC_LANES, SC_LANES)] = b

        @pl.loop(0, per_w // SC_CHUNK)
        def _(ci):
            chunk = wid * (per_w // SC_CHUNK) + ci
            pltpu.sync_copy(dest_hbm.at[chunk], idx_v)
            pltpu.sync_copy(gb_hbm.at[chunk], g_v)
            for cp in gathers(0, 0):
                cp.start()

            @pl.loop(0, n_blocks, step=2)
            def _(b0):
                for slot in range(2):
                    blk = b0 + slot
                    for cp in gathers(blk, slot):
                        cp.wait()

                    @pl.when(blk + 1 < n_blocks)
                    def _():
                        for cp in gathers(blk + 1, 1 - slot):
                            cp.start()

                    @pl.when(blk >= 2)
                    def _():
                        write_back(chunk, blk - 2, slot).wait()

                    weighted_sum(blk, slot)
                    write_back(chunk, blk, slot).start()

            for slot in range(2):
                write_back(chunk, n_blocks - 2 + slot, slot).wait()


    return run(table, dest, gb)


def _split_bf16(w):
    hi = w.astype(BF16)
    return hi, (w - hi.astype(F32)).astype(BF16)


def kernel(x_prompt, x_sample, cache_k, cache_v, c, c_ctx, ada_w, ada_b, norm1_g, norm2_g, w_qkv, w_o, na_rpb, diff_lambda, diff_subln_g, router_w, router_b, exp_w_gate, exp_w_up, exp_w_down, shared_w_gate, shared_w_up, shared_w_down, final_g):
    batch, seq, d = x_prompt.shape
    n_lat_batch, lat_seq, _ = x_sample.shape
    depth = w_qkv.shape[0]
    n_exp = router_w.shape[-1]
    n_lat = n_lat_batch * lat_seq
    n_ctx = batch * seq
    n = n_lat + n_ctx
    assert d % LANES == 0 and na_rpb.shape[1] * HEAD_DIM == d
    assert diff_lambda.shape[-1] == HEAD_DIM and lat_seq % GRID_W == 0
    assert n_lat % seq == 0
    tm = _pick_tile(math.gcd(lat_seq, n_ctx), 512)
    tmb = 1024
    n_all = n // tm
    ranges = [(r * n_all // MOE_RANGES, (r + 1) * n_all // MOE_RANGES - r * n_all // MOE_RANGES)
              for r in range(MOE_RANGES)]
    scale = HEAD_DIM ** -0.5 * LOG2E

    xs = [x_sample.reshape(n_lat, d), x_prompt.reshape(n_ctx, d)]
    mod_rows = -(-(n_lat_batch + 1) // 8) * 8
    cond = jnp.zeros((mod_rows, d), F32).at[:n_lat_batch].set(c).at[n_lat_batch].set(c_ctx)
    mod = _modulation(cond, ada_w, ada_b).reshape(depth, mod_rows, N_MOD, d)

    w_qkv_b = w_qkv.astype(BF16)
    w_o_b = w_o.astype(BF16)
    rope_tab = _rope_tables(lat_seq, tm)
    caches = None
    for i in range(depth):
        is_diff = i % 2 == 1
        j = i // 2
        q, k, v, *caches = _qkv(xs, mod, norm1_g.reshape(depth, 1, d), w_qkv_b,
                                rope_tab if is_diff else None, caches, layer=i, n_lat=n_lat,
                                batch=batch, seq=seq, lat_seq=lat_seq, n_lat_batch=n_lat_batch,
                                tm=tm, scale=scale)
        if is_diff:
            vt = caches.pop()
        if not is_diff:
            o = _na_attention(q, k, v, cache_k, cache_v, na_rpb[j], layer=i,
                              n_lat_batch=n_lat_batch, lat_seq=lat_seq)
            o = _ctx_dense(q, k, v, o, n_lat=n_lat, seq=seq)
        else:
            lam_init = 0.8 - 0.6 * math.exp(-0.3 * i)
            lp = diff_lambda[j].astype(F32)
            lam = (jnp.exp(jnp.sum(lp[0] * lp[1])) - jnp.exp(jnp.sum(lp[2] * lp[3]))
                   + lam_init).reshape(1)
            subg = diff_subln_g[j].reshape(1, LANES).astype(F32)
            o = _lat_diff(lam, q, k, vt, cache_k, cache_v, subg, layer=i,
                          n_lat_batch=n_lat_batch, lat_seq=lat_seq, out_scale=1.0 - lam_init)
            o = _ctx_diff(lam, q, k, v, subg, o, n_lat=n_lat, seq=seq, out_scale=1.0 - lam_init)

        rwh, rwl = _split_bf16(router_w[i].T)
        swgu = jnp.concatenate([shared_w_gate[i], shared_w_up[i]], axis=-1).astype(BF16)
        final = i == depth - 1
        routed = []
        for tile0, nt in ranges:
            xmid, h2, sel, gates, rank, counts = _post_attn(
                o, xs, mod, w_o_b[i], norm2_g[i].reshape(1, d), rwh, rwl,
                router_b[i].reshape(n_exp, 1).astype(F32), layer=i, tile0=tile0, n_tiles=nt,
                lat_seq=lat_seq, n_lat_batch=n_lat_batch, tm=tm)
            n_tiles = -(-(nt * tm * TOP_K) // tmb) + n_exp
            dest, tile_expert, tile_rows, n_used = _plan(counts, sel, rank, tmb=tmb, n_tiles=n_tiles)
            xb = _sc_dispatch(h2, dest, n_tiles * tmb)
            routed.append((xmid, h2, gates, dest, tile_expert, tile_rows, n_used, xb))
        ybs = [_experts(te, tr, nu, xb, exp_w_gate, exp_w_up, exp_w_down, layer=i, tmb=tmb)
               for (_, _, _, _, te, tr, nu, xb) in routed]
        sums = [_sc_combine(yb, r[3], r[2]) for yb, r in zip(ybs, routed)]
        xs, y_lat, y_ctx = [], None, None
        for (tile0, nt), (xmid, h2, *_), rsum in zip(ranges, routed, sums):
            out = _combine(xmid, h2, rsum, mod, swgu, shared_w_down[i].astype(BF16),
                           final_g.reshape(1, d), y_lat, layer=i, tile0=tile0, n_lat=n_lat,
                           n_ctx=n_ctx, lat_seq=lat_seq, n_lat_batch=n_lat_batch, tm=tm, final=final)
            if final:
                y_lat, y_ctx = out[0], (out[1] if out[1] is not None else y_ctx)
            else:
                xs.append(out)

    y_sample = y_lat.reshape(n_lat_batch, lat_seq, d)
    y_prompt = y_ctx.reshape(batch, seq, d)
    return (y_prompt, y_sample, caches[0], caches[1])
```

```python
import functools
import math

import jax
import jax.numpy as jnp
from jax import lax
from jax.experimental import pallas as pl
from jax.experimental.pallas import tpu as pltpu
from jax.experimental.pallas import tpu_sc as plsc

GRID_W = 64
NA_ROWS = 8
NA_COLS = 16
TOP_K = 8
ROUTED_SCALE = 2.5
ROPE_BASE = 10000.0
EPS = 1e-6
N_MOD = 6
HEAD_DIM = 64
LANES = 128
LOG2E = math.log2(math.e)
MASKED = -1e30
NA_QROWS = 2
NA_WIN = NA_ROWS + NA_QROWS - 1
EXPERT_ROW_BUFFERS = 3
POST_SPLIT = 1
MOE_RANGES = 2
QUERY_GROUPS = 4
VMEM_LIMIT = 56 * 1024 * 1024

F32 = jnp.float32
BF16 = jnp.bfloat16


def _cparams(*sem):
    return pltpu.CompilerParams(dimension_semantics=sem, vmem_limit_bytes=VMEM_LIMIT)


def _dot(a, b):
    return jnp.dot(a, b, preferred_element_type=F32)


def _dot_nt(a, b):
    return lax.dot_general(a, b, (((1,), (1,)), ((), ())), preferred_element_type=F32)


def _sigmoid(x):
    return 1.0 / (1.0 + jnp.exp(-x))


def _rms(x, g):
    return x * lax.rsqrt(jnp.mean(x * x, axis=-1, keepdims=True) + EPS) * g


def _pack_rows(xb):
    half = xb.shape[1] // 2
    u = lax.bitcast_convert_type(xb.astype(F32), jnp.uint32)
    packed = (u[:, :half] >> 16) | (u[:, half:] & jnp.uint32(0xFFFF0000))
    return lax.bitcast_convert_type(packed, jnp.int32)


def _unpack_rows(p):
    u = lax.bitcast_convert_type(p, jnp.uint32)
    return (lax.bitcast_convert_type(u << 16, F32),
            lax.bitcast_convert_type(u & jnp.uint32(0xFFFF0000), F32))


def _pick_tile(n, pref):
    t = pref
    while n % t:
        t //= 2
    return t


def _mod_kernel(c_ref, w_ref, b_ref, o_ref):
    c = c_ref[...]
    o_ref[...] = _dot(c * _sigmoid(c), w_ref[...]) + b_ref[...]


def _modulation(cond, ada_w, ada_b):
    depth, d, n6 = ada_w.shape
    rows = cond.shape[0]
    tn = _pick_tile(n6, 512)
    return pl.pallas_call(
        _mod_kernel,
        grid=(depth, n6 // tn),
        in_specs=[
            pl.BlockSpec((rows, d), lambda l, j: (0, 0)),
            pl.BlockSpec((None, d, tn), lambda l, j: (l, 0, j)),
            pl.BlockSpec((None, 1, tn), lambda l, j: (l, 0, j)),
        ],
        out_specs=pl.BlockSpec((None, rows, tn), lambda l, j: (l, 0, j)),
        out_shape=jax.ShapeDtypeStruct((depth, rows, n6), F32),
        compiler_params=_cparams("arbitrary", "arbitrary"),
        name="modulation",
    )(cond, ada_w, ada_b.reshape(depth, 1, n6))


def _read_tokens(x_refs, g, ends):
    x = x_refs[-1][...]
    for ref, end in zip(reversed(x_refs[:-1]), reversed(ends[:-1])):
        x = jnp.where(g < end, ref[...], x)
    return x


def _token_parts(xs, tm, tile0, n_tiles):
    arrays, specs, ends = [], [], []
    start = 0
    for x in xs:
        size = x.shape[0] // tm
        if start < tile0 + n_tiles and start + size > tile0:
            arrays.append(x)
            ends.append(start + size)
            specs.append(pl.BlockSpec(
                (tm, x.shape[1]), lambda i, s=start, z=size: (jnp.clip(i + tile0 - s, 0, z - 1), 0)))
        start += size
    return arrays, specs, ends


def _qkv_kernel(*refs, d, scale, rope, n_lat_tiles, ends, n_alias):
    n_x = len(ends)
    x_refs = refs[:n_x]
    mod_ref, g_ref, w_ref = refs[n_x:n_x + 3]
    rest = refs[n_x + 3:]
    if rope:
        rope_ref, wvt_ref, rest = rest[0], rest[1], rest[2:]
    rest = rest[n_alias:]
    q_ref, k_ref, v_ref, kf_ref, vf_ref = rest[:5]
    i = pl.program_id(0)
    x = _read_tokens(x_refs, i, ends)
    h = _rms(x, g_ref[...]) * (1.0 + mod_ref[1:2, :]) + mod_ref[0:1, :]
    hb = h.astype(BF16)

    def rot(t):
        if not rope:
            return t
        cols = []
        for j in range(d // LANES):
            tb = t[:, j * LANES:(j + 1) * LANES]
            cols.append(tb * rope_ref[0]
                        + pltpu.roll(tb, LANES - HEAD_DIM // 2, axis=1) * rope_ref[1]
                        + pltpu.roll(tb, HEAD_DIM // 2, axis=1) * rope_ref[2])
        return jnp.concatenate(cols, axis=1)

    q = _dot(hb, w_ref[:, 0:d])
    q_ref[...] = (rot(q) * scale).astype(BF16)
    k = _dot(hb, w_ref[:, d:2 * d])
    k_ref[...] = rot(k).astype(BF16)
    if rope:
        @pl.when(i < n_lat_tiles)
        def _():
            rest[5][...] = _dot_nt(wvt_ref[...], hb).astype(BF16)

    def context_values():
        v = _dot(hb, w_ref[:, 2 * d:3 * d])
        v_ref[...] = v.astype(BF16)
        return v

    if not rope:
        v = context_values()

    @pl.when(i >= n_lat_tiles)
    def _():
        kf_ref[...] = k.reshape(kf_ref.shape)
        vf_ref[...] = (context_values() if rope else v).reshape(vf_ref.shape)


def _qkv(xs, mod, g, w, rope_tab, caches, *, layer, n_lat, batch, seq, lat_seq, n_lat_batch, tm,
         scale):
    d = xs[0].shape[1]
    depth = w.shape[0]
    n = n_lat + batch * seq
    n_lat_tiles = n_lat // tm
    rope = rope_tab is not None
    assert tm % seq == 0
    x_arrays, x_specs, ends = _token_parts(xs, tm, 0, n // tm)

    def mod_idx(i):
        return (layer, jnp.minimum(i * tm // lat_seq, n_lat_batch), 0, 0)

    in_specs = x_specs + [
        pl.BlockSpec((None, None, N_MOD, d), mod_idx),
        pl.BlockSpec((None, 1, d), lambda i: (layer, 0, 0)),
        pl.BlockSpec((None, d, 3 * d), lambda i: (layer, 0, 0)),
    ]
    args = x_arrays + [mod, g, w]
    if rope:
        seq_tiles = lat_seq // tm
        in_specs.append(pl.BlockSpec(
            (3, tm, LANES),
            lambda i: (0, jnp.where(i < n_lat_tiles, i % seq_tiles, seq_tiles), 0)))
        in_specs.append(pl.BlockSpec((d, d), lambda i: (0, 0)))
        args += [rope_tab, w[layer, :, 2 * d:].T]
    aliases = {}
    if caches is not None:
        aliases = {len(args): 3, len(args) + 1: 4}
        in_specs += [pl.BlockSpec(memory_space=pl.ANY)] * 2
        args += list(caches)
    tok = pl.BlockSpec((tm, d), lambda i: (i, 0))
    cache_spec = pl.BlockSpec((tm // seq, None, seq, d),
                              lambda i: (jnp.maximum(i - n_lat_tiles, 0), layer, 0, 0))
    out_specs = [tok, tok, tok, cache_spec, cache_spec]
    out_shape = ([jax.ShapeDtypeStruct((n, d), BF16)] * 3
                 + [jax.ShapeDtypeStruct((batch, depth, seq, d), F32)] * 2)
    if rope:
        out_specs.append(pl.BlockSpec((None, d, tm),
                                      lambda i: (jnp.minimum(i, n_lat_tiles - 1), 0, 0)))
        out_shape.append(jax.ShapeDtypeStruct((n_lat_tiles, d, tm), BF16))
    return pl.pallas_call(
        functools.partial(_qkv_kernel, d=d, scale=scale, rope=rope, n_lat_tiles=n_lat_tiles,
                          ends=tuple(ends), n_alias=len(aliases)),
        grid=(n // tm,),
        in_specs=in_specs,
        out_specs=out_specs,
        out_shape=out_shape,
        input_output_aliases=aliases,
        compiler_params=_cparams("arbitrary"),
        name=f"qkv{layer}",
    )(*args)


def _rope_tables(lat_seq, tm):
    n_freq = HEAD_DIM // 4
    inv = ROPE_BASE ** (-jnp.arange(n_freq, dtype=F32) / n_freq)
    pos = jnp.arange(lat_seq)
    row = (pos // GRID_W).astype(F32)
    col = (pos % GRID_W).astype(F32)
    ang = jnp.concatenate([row[:, None] * inv, col[:, None] * inv], axis=-1)
    reps = LANES // (HEAD_DIM // 2)
    cos = jnp.tile(jnp.cos(ang), (1, reps))
    sin = jnp.tile(jnp.sin(ang), (1, reps))
    first_half = (jnp.arange(LANES) % HEAD_DIM) < HEAD_DIM // 2
    s_next = jnp.where(first_half, -sin, 0.0)
    s_prev = jnp.where(first_half, 0.0, sin)
    ident = jnp.stack([jnp.ones((tm, LANES), F32), jnp.zeros((tm, LANES), F32),
                       jnp.zeros((tm, LANES), F32)])
    return jnp.concatenate([jnp.stack([cos, s_next, s_prev]), ident], axis=1)


def _lane_lo():
    return lax.broadcasted_iota(jnp.int32, (1, LANES), 1) < HEAD_DIM


def _ctx_dense_kernel(q_ref, k_ref, v_ref, o_all_ref, o_ref, *, d):
    del o_all_ref
    lo = _lane_lo()
    blks = [slice(j * LANES, (j + 1) * LANES) for j in range(d // LANES)]
    scores = []
    for blk in blks:
        q2 = q_ref[:, blk]
        scores += [_dot_nt(jnp.where(lo, q2, 0), k_ref[:, blk]),
                   _dot_nt(jnp.where(lo, 0, q2), k_ref[:, blk])]
    probs = []
    for s in scores:
        p = jnp.exp2(s - jnp.max(s, axis=-1, keepdims=True))
        probs.append((p.astype(BF16), jnp.sum(p, axis=-1, keepdims=True)))
    outs = [_dot(p, v_ref[:, blks[h // 2]]) / l for h, (p, l) in enumerate(probs)]
    o_ref[...] = jnp.concatenate(
        [jnp.where(lo, outs[h], outs[h + 1]).astype(BF16) for h in range(0, len(outs), 2)], axis=1)


def _ctx_dense(q, k, v, o_all, *, n_lat, seq):
    n, d = q.shape
    off = n_lat // seq
    spec = pl.BlockSpec((seq, d), lambda b: (off + b, 0))
    return pl.pallas_call(
        functools.partial(_ctx_dense_kernel, d=d),
        grid=((n - n_lat) // seq,),
        in_specs=[spec, spec, spec, pl.BlockSpec(memory_space=pl.ANY)],
        out_specs=spec,
        out_shape=jax.ShapeDtypeStruct((n, d), BF16),
        input_output_aliases={3: 0},
        compiler_params=_cparams("arbitrary"),
        name="ctx_dense_attn",
    )(q, k, v, o_all)


def _subln(o, g_ref, out_scale):
    return _rms(o, g_ref[...]) * out_scale


def _ctx_diff_kernel(lam_ref, q_ref, k_ref, v_ref, g_ref, o_all_ref, o_ref, *, d, out_scale):
    del o_all_ref
    lo = _lane_lo()
    lam = lam_ref[0]
    blks = [slice(j * LANES, (j + 1) * LANES) for j in range(d // LANES)]
    scores = []
    for blk in blks:
        q2 = q_ref[:, blk]
        scores.append((_dot_nt(jnp.where(lo, q2, 0), k_ref[:, blk]),
                       _dot_nt(jnp.where(lo, 0, q2), k_ref[:, blk])))
    probs = []
    for pair in scores:
        ps = []
        for s in pair:
            p = jnp.exp2(s - jnp.max(s, axis=-1, keepdims=True))
            ps.append(p / jnp.sum(p, axis=-1, keepdims=True))
        probs.append((ps[0] - lam * ps[1]).astype(BF16))
    outs = [_dot(p, v_ref[:, blk]) for p, blk in zip(probs, blks)]
    o_ref[...] = jnp.concatenate([_subln(o, g_ref, out_scale).astype(BF16) for o in outs], axis=1)


def _ctx_diff(lam, q, k, v, subg, o_all, *, n_lat, seq, out_scale):
    n, d = q.shape
    off = n_lat // seq
    spec = pl.BlockSpec((seq, d), lambda b: (off + b, 0))
    return pl.pallas_call(
        functools.partial(_ctx_diff_kernel, d=d, out_scale=out_scale),
        grid=((n - n_lat) // seq,),
        in_specs=[pl.BlockSpec(memory_space=pltpu.SMEM), spec, spec, spec,
                  pl.BlockSpec((1, LANES), lambda b: (0, 0)), pl.BlockSpec(memory_space=pl.ANY)],
        out_specs=spec,
        out_shape=jax.ShapeDtypeStruct((n, d), BF16),
        input_output_aliases={5: 0},
        compiler_params=_cparams("arbitrary"),
        name="ctx_diff_attn",
    )(lam, q, k, v, subg, o_all)


def _na_windows(rows):
    starts, classes, geoms = [], [], []
    for r0 in range(0, rows, NA_QROWS):
        first = [min(max(r0 + i - NA_ROWS // 2, 0), rows - NA_ROWS) for i in range(NA_QROWS)]
        ws = min(first[0], rows - NA_WIN)
        geom = tuple(
            tuple((ws + u) - (r0 + i) + NA_ROWS - 1 if first[i] <= ws + u < first[i] + NA_ROWS else -1
                  for u in range(NA_WIN)) for i in range(NA_QROWS))
        if geom not in geoms:
            geoms.append(geom)
        starts.append(ws)
        classes.append(geoms.index(geom))
    return starts, classes, geoms


def _na_kernel(ws_ref, cls_ref, q_ref, k_ref, v_ref, ck_ref, cv_ref, bias_ref, o_ref, *, d):
    del cls_ref
    lo = _lane_lo()
    win = NA_WIN * GRID_W
    start = pl.multiple_of(ws_ref[pl.program_id(1)] * GRID_W, GRID_W)
    n_heads = 2 * (d // LANES)

    def blk(h):
        return slice(h // 2 * LANES, (h // 2 + 1) * LANES)

    scores = []
    for h in range(n_heads):
        q2 = q_ref[:, blk(h)]
        qm = jnp.where(lo, q2, 0) if h % 2 == 0 else jnp.where(lo, 0, q2)
        scores.append((_dot_nt(qm, k_ref[pl.ds(start, win), blk(h)]) + bias_ref[h],
                       _dot_nt(qm, ck_ref[:, blk(h)].astype(BF16))))
    probs = []
    for s_loc, s_ctx in scores:
        m = jnp.maximum(jnp.max(s_loc, axis=-1, keepdims=True),
                        jnp.max(s_ctx, axis=-1, keepdims=True))
        p_loc = jnp.exp2(s_loc - m)
        p_ctx = jnp.exp2(s_ctx - m)
        l = jnp.sum(p_loc, axis=-1, keepdims=True) + jnp.sum(p_ctx, axis=-1, keepdims=True)
        probs.append((p_loc.astype(BF16), p_ctx.astype(BF16), l))
    outs = [(_dot(p_loc, v_ref[pl.ds(start, win), blk(h)])
             + _dot(p_ctx, cv_ref[:, blk(h)].astype(BF16))) / l
            for h, (p_loc, p_ctx, l) in enumerate(probs)]
    o_ref[...] = jnp.concatenate(
        [jnp.where(lo, outs[h], outs[h + 1]).astype(BF16) for h in range(0, n_heads, 2)], axis=1)


def _na_bias_table(rpb, geoms):
    h = rpb.shape[0]
    c = jnp.arange(GRID_W)
    kc = jnp.arange(GRID_W)
    c0 = jnp.clip(c - NA_COLS // 2, 0, GRID_W - NA_COLS)
    off_c = kc[None, :] - c[:, None] + (NA_COLS - 1)
    inside = (kc[None, :] >= c0[:, None]) & (kc[None, :] < c0[:, None] + NA_COLS)
    oh_c = jax.nn.one_hot(off_c, 2 * NA_COLS - 1, dtype=F32)
    blocks = jnp.einsum('hrs,cqs->rhcq', rpb.astype(F32), oh_c, precision=lax.Precision.HIGHEST)
    blocks = jnp.where(inside, blocks * LOG2E, MASKED)
    masked = jnp.full((h, GRID_W, GRID_W), MASKED, F32)
    return jnp.stack([
        jnp.concatenate([
            jnp.concatenate([blocks[r] if r >= 0 else masked for r in row], axis=-1)
            for row in geom], axis=-2)
        for geom in geoms])


def _na_attention(q, k, v, ck, cv, rpb, *, layer, n_lat_batch, lat_seq):
    n, d = q.shape
    rows = lat_seq // GRID_W
    past = ck.shape[2]
    n_heads = rpb.shape[0]
    assert rows % NA_QROWS == 0 and rows >= NA_WIN
    starts, classes, geoms = _na_windows(rows)
    bias = _na_bias_table(rpb, geoms)
    groups = rows // NA_QROWS
    tq = NA_QROWS * GRID_W
    kv_spec = pl.BlockSpec((lat_seq, d), lambda b, r, ws, cls: (b, 0))
    c_spec = pl.BlockSpec((None, None, past, d), lambda b, r, ws, cls: (b, layer, 0, 0))
    q_spec = pl.BlockSpec((tq, d), lambda b, r, ws, cls: (b * groups + r, 0))
    return pl.pallas_call(
        functools.partial(_na_kernel, d=d),
        grid_spec=pltpu.PrefetchScalarGridSpec(
            num_scalar_prefetch=2,
            grid=(n_lat_batch, groups),
            in_specs=[q_spec, kv_spec, kv_spec, c_spec, c_spec,
                      pl.BlockSpec((None, n_heads, tq, NA_WIN * GRID_W),
                                   lambda b, r, ws, cls: (cls[r], 0, 0, 0))],
            out_specs=q_spec),
        out_shape=jax.ShapeDtypeStruct((n, d), BF16),
        compiler_params=_cparams("arbitrary", "arbitrary"),
        name="na_attn",
    )(jnp.asarray(starts, jnp.int32), jnp.asarray(classes, jnp.int32), q, k, v, ck, cv, bias)


def _lat_diff_kernel(lam_ref, q_ref, k_ref, vt_ref, ck_ref, cv_ref, g_ref, o_ref, *, out_scale):
    lo = _lane_lo()
    lam = lam_ref[0]
    q2 = q_ref[...]
    tq = q2.shape[0]
    tg = tq // QUERY_GROUPS
    qs = []
    for g in range(QUERY_GROUPS):
        qg = q2[g * tg:(g + 1) * tg, :]
        qs += [jnp.where(lo, qg, 0), jnp.where(lo, 0, qg)]

    def softmax_step(state, s):
        m, l, acc = state
        m_new = jnp.maximum(m, jnp.max(s, axis=0, keepdims=True))
        alpha = jnp.exp2(m - m_new)
        p = jnp.exp2(s - m_new)
        return m_new, alpha * l + jnp.sum(p, axis=0, keepdims=True), alpha * acc, p.astype(BF16)

    def scores(kt):
        return [_dot_nt(kt, qm) for qm in qs]

    def absorb(states, ss, vt):
        out = []
        for g in range(0, len(qs), 2):
            stepped = [softmax_step(states[g + j], ss[g + j]) for j in range(2)]
            out += [(m, l, acc + _dot(vt, p)) for m, l, acc, p in stepped]
        return out

    tile = vt_ref.shape[2]
    n_tiles = vt_ref.shape[0]

    def keys(t):
        return k_ref[t * tile:(t + 1) * tile, :] if t < n_tiles else ck_ref[...].astype(BF16)

    def values_t(t):
        return vt_ref[t] if t < n_tiles else cv_ref[...].T.astype(BF16)

    init = (jnp.full((1, tg), -jnp.inf, F32), jnp.zeros((1, tg), F32), jnp.zeros((LANES, tg), F32))
    states = [init] * len(qs)
    ss = scores(keys(0))
    for t in range(n_tiles + 1):
        ss_next = scores(keys(t + 1)) if t < n_tiles else None
        states = absorb(states, ss, values_t(t))
        ss = ss_next
    o = jnp.concatenate([states[g][2] / states[g][1] - lam * (states[g + 1][2] / states[g + 1][1])
                         for g in range(0, len(qs), 2)], axis=1).T
    o_ref[...] = _subln(o, g_ref, out_scale).astype(BF16)


def _lat_diff(lam, q, k, vt, ck, cv, subg, *, layer, n_lat_batch, lat_seq, out_scale):
    n, d = q.shape
    past = ck.shape[2]
    tk = vt.shape[2]
    tq = _pick_tile(lat_seq, QUERY_GROUPS * 256)
    qt = lat_seq // tq
    kt = lat_seq // tk
    c_spec = pl.BlockSpec((None, None, past, LANES), lambda b, h, i: (b, layer, 0, h))
    q_spec = pl.BlockSpec((tq, LANES), lambda b, h, i: (b * qt + i, h))
    return pl.pallas_call(
        functools.partial(_lat_diff_kernel, out_scale=out_scale),
        grid=(n_lat_batch, d // LANES, qt),
        in_specs=[pl.BlockSpec(memory_space=pltpu.SMEM), q_spec,
                  pl.BlockSpec((lat_seq, LANES), lambda b, h, i: (b, h)),
                  pl.BlockSpec((kt, LANES, tk), lambda b, h, i: (b, h, 0)),
                  c_spec, c_spec, pl.BlockSpec((1, LANES), lambda b, h, i: (0, 0))],
        out_specs=q_spec,
        out_shape=jax.ShapeDtypeStruct((n, d), BF16),
        compiler_params=_cparams("arbitrary", "arbitrary", "arbitrary"),
        name="lat_diff_attn",
    )(lam, q, k, vt, ck, cv, subg)


def _post_attn_kernel(o_ref, *refs, ends, tile0):
    n_x = len(ends)
    x_refs = refs[:n_x]
    (mod_ref, wo_ref, g_ref, rwh_ref, rwl_ref, rb_ref, tri_ref,
     xmid_ref, h_ref, sel_ref, gate_ref, rank_ref, cnt_ref, carry_ref) = refs[n_x:]
    i = pl.program_id(0)

    @pl.when(i == 0)
    def _():
        carry_ref[...] = jnp.zeros_like(carry_ref)

    tm = o_ref.shape[0]
    hm = tm // POST_SPLIT
    groups = [slice(r * hm, (r + 1) * hm) for r in range(POST_SPLIT)]
    x_in = _read_tokens(x_refs, i + tile0, ends)
    x1s = [x_in[rows] + mod_ref[2:3, :] * _dot(o_ref[rows, :], wo_ref[...]) for rows in groups]
    logits = []
    for rows, x1 in zip(groups, x1s):
        xmid_ref[rows, :] = x1
        h = _rms(x1, g_ref[...]) * (1.0 + mod_ref[4:5, :]) + mod_ref[3:4, :]
        hb = h.astype(BF16)
        h_ref[rows, :] = _pack_rows(hb)
        h_lo = (h - hb.astype(F32)).astype(BF16)
        logits.append(_dot_nt(rwh_ref[...], hb)
                      + (_dot_nt(rwl_ref[...], hb) + _dot_nt(rwh_ref[...], h_lo)))
    n_exp = rwh_ref.shape[0]
    row = lax.broadcasted_iota(jnp.int32, (n_exp, hm), 0).astype(F32)
    picked = []
    for lg in logits:
        scores = _sigmoid(lg)
        biased = scores + rb_ref[...]
        total = jnp.zeros((n_exp, hm), F32)
        sels, gates = [], []
        for _ in range(TOP_K):
            m = jnp.max(biased, axis=0, keepdims=True)
            idx = jnp.min(jnp.where(biased == m, row, float(n_exp)), axis=0, keepdims=True)
            hit = row == idx
            gates.append(jnp.sum(jnp.where(hit, scores, 0.0), axis=0, keepdims=True))
            biased = jnp.where(hit, -jnp.inf, biased)
            total = total + jnp.where(hit, 1.0, 0.0)
            sels.append(idx)
        picked.append((sels, gates, total))
    carry = carry_ref[:, 0:1]
    for rows, (sels, gates, total) in zip(groups, picked):
        denom = gates[0]
        for g in gates[1:]:
            denom = denom + g
        before = _dot(total.astype(BF16), tri_ref[...]) + carry
        for k in range(TOP_K):
            sel_ref[k:k + 1, rows] = sels[k].astype(jnp.int32)
            gate_ref[k:k + 1, rows] = gates[k] / denom * ROUTED_SCALE
            rank_ref[k:k + 1, rows] = jnp.sum(jnp.where(row == sels[k], before, 0.0), axis=0,
                                              keepdims=True).astype(jnp.int32)
        carry = carry + jnp.sum(total, axis=1, keepdims=True)
    carry_ref[...] = jnp.broadcast_to(carry, carry_ref.shape)
    cnt_ref[...] = carry_ref[...].astype(jnp.int32)


def _post_attn(o, xs, mod, wo, g, rwh, rwl, rb, *, layer, tile0, n_tiles, lat_seq, n_lat_batch, tm):
    d = o.shape[1]
    n = n_tiles * tm
    n_exp = rwh.shape[0]
    hm = tm // POST_SPLIT
    tri = (jnp.arange(hm)[:, None] < jnp.arange(hm)[None, :]).astype(BF16)
    x_arrays, x_specs, ends = _token_parts(xs, tm, tile0, n_tiles)

    def mod_idx(i):
        return (layer, jnp.minimum((i + tile0) * tm // lat_seq, n_lat_batch), 0, 0)

    tok = pl.BlockSpec((tm, d), lambda i: (i, 0))
    tok_p = pl.BlockSpec((tm, d // 2), lambda i: (i, 0))
    kt = pl.BlockSpec((TOP_K, tm), lambda i: (0, i))
    const2 = lambda i: (0, 0)
    return pl.pallas_call(
        functools.partial(_post_attn_kernel, ends=tuple(ends), tile0=tile0),
        grid=(n_tiles,),
        in_specs=[pl.BlockSpec((tm, d), lambda i: (i + tile0, 0))] + x_specs + [
                  pl.BlockSpec((None, None, N_MOD, d), mod_idx),
                  pl.BlockSpec((d, d), const2), pl.BlockSpec((1, d), const2),
                  pl.BlockSpec((n_exp, d), const2), pl.BlockSpec((n_exp, d), const2),
                  pl.BlockSpec((n_exp, 1), const2), pl.BlockSpec((hm, hm), const2)],
        out_specs=[tok, tok_p, kt, kt, kt, pl.BlockSpec((n_exp, LANES), const2)],
        out_shape=[jax.ShapeDtypeStruct((n, d), F32), jax.ShapeDtypeStruct((n, d // 2), jnp.int32),
                   jax.ShapeDtypeStruct((TOP_K, n), jnp.int32),
                   jax.ShapeDtypeStruct((TOP_K, n), F32),
                   jax.ShapeDtypeStruct((TOP_K, n), jnp.int32),
                   jax.ShapeDtypeStruct((n_exp, LANES), jnp.int32)],
        scratch_shapes=[pltpu.VMEM((n_exp, LANES), F32)],
        compiler_params=_cparams("arbitrary"),
        name=f"post_attn{layer}",
    )(o, *x_arrays, mod, wo, g, rwh, rwl, rb, tri)


def _plan_kernel(cnt_ref, sel_ref, rank_ref, dest_ref, te_ref, tv_ref, nu_ref, *, tmb):
    n_exp = cnt_ref.shape[0]
    sel = sel_ref[...]
    tile_start = lax.broadcasted_iota(jnp.int32, te_ref.shape, 1) * tmb
    dest = rank_ref[...]
    te = jnp.zeros(te_ref.shape, jnp.int32)
    tv = jnp.zeros(te_ref.shape, jnp.int32)
    start = jnp.zeros((1, 1), jnp.int32)
    for e in range(n_exp):
        dest = dest + jnp.where(sel == e, start, 0)
        cnt = cnt_ref[e:e + 1, 0:1]
        stop = start + ((cnt + (tmb - 1)) & -tmb)
        mine = (tile_start >= start) & (tile_start < stop)
        tv = tv + jnp.where(mine, jnp.clip(start + cnt - tile_start, 0, tmb), 0)
        start = stop
        te = te + jnp.where(start <= tile_start, 1, 0)
    te_ref[...] = jnp.minimum(te, n_exp - 1)
    tv_ref[...] = tv
    nu_ref[...] = jnp.broadcast_to(start, nu_ref.shape)
    for c in range(dest_ref.shape[0]):
        dest_ref[c] = dest[:, c * SC_CHUNK:(c + 1) * SC_CHUNK]


def _plan(counts, sel, rank, *, tmb, n_tiles):
    k_top, n = sel.shape
    assert tmb & (tmb - 1) == 0
    tn = _pick_tile(n, 16 * SC_CHUNK)
    te_w = -(-n_tiles // LANES) * LANES
    tok = pl.BlockSpec((k_top, tn), lambda i: (0, i))
    const2 = lambda i: (0, 0)
    dest, te, tv, nu = pl.pallas_call(
        functools.partial(_plan_kernel, tmb=tmb),
        grid=(n // tn,),
        in_specs=[pl.BlockSpec(counts.shape, const2), tok, tok],
        out_specs=[pl.BlockSpec((tn // SC_CHUNK, k_top, SC_CHUNK), lambda i: (i, 0, 0)),
                   pl.BlockSpec((1, te_w), const2), pl.BlockSpec((1, te_w), const2),
                   pl.BlockSpec((1, LANES), const2)],
        out_shape=[jax.ShapeDtypeStruct((n // SC_CHUNK, k_top, SC_CHUNK), jnp.int32),
                   jax.ShapeDtypeStruct((1, te_w), jnp.int32),
                   jax.ShapeDtypeStruct((1, te_w), jnp.int32),
                   jax.ShapeDtypeStruct((1, LANES), jnp.int32)],
        compiler_params=_cparams("arbitrary"),
        name="plan",
    )(counts, sel, rank)
    return dest, te[0, :n_tiles], tv[0, :n_tiles], nu[0, :1] // tmb


def _expert_kernel(te_ref, tv_ref, nu_ref, x_hbm, wg_ref, wu_ref, wd_ref, y_ref, wg_b, wu_b, wd_b,
                   xbuf, sem):
    i = pl.program_id(0)
    n_buf, tmb, _ = xbuf.shape
    half = tmb // 2

    def x_copy(j, slot):
        return pltpu.make_async_copy(x_hbm.at[pl.ds(pl.multiple_of(j * tmb, tmb), tmb)],
                                     xbuf.at[slot], sem.at[slot])

    @pl.when(i == 0)
    def _():
        for j in range(n_buf - 1):
            @pl.when(j < nu_ref[0])
            def _():
                x_copy(j, j).start()

    ahead = i + (n_buf - 1)

    @pl.when(ahead < nu_ref[0])
    def _():
        x_copy(ahead, lax.rem(ahead, n_buf)).start()

    slot = lax.rem(i, n_buf)

    def rows(r):
        return jnp.concatenate(_unpack_rows(xbuf[slot, r * half:(r + 1) * half, :]),
                               axis=1).astype(BF16)

    def swiglu(xs):
        gu = [(_dot(x, wg_b[...]), _dot(x, wu_b[...])) for x in xs]
        acts = [(g * _sigmoid(g) * u).astype(BF16) for g, u in gu]
        return [_pack_rows(_dot(a, wd_b[...]).astype(BF16)) for a in acts]

    @pl.when(i < nu_ref[0])
    def _():
        x_copy(i, slot).wait()

        @pl.when((i == 0) | (te_ref[i] != te_ref[jnp.maximum(i - 1, 0)]))
        def _():
            wg_b[...] = wg_ref[...].astype(BF16)
            wu_b[...] = wu_ref[...].astype(BF16)
            wd_b[...] = wd_ref[...].astype(BF16)

        @pl.when(tv_ref[i] > half)
        def _():
            ya, yb = swiglu([rows(0), rows(1)])
            y_ref[0:half, :] = ya
            y_ref[half:2 * half, :] = yb

        @pl.when(tv_ref[i] <= half)
        def _():
            y_ref[0:half, :] = swiglu([rows(0)])[0]


def _experts(tile_expert, tile_rows, n_used, xb, wg, wu, wd, *, layer, tmb):
    cap, dp = xb.shape
    f, d = wd.shape[2:]

    def row_idx(i, te, tv, nu):
        return (jnp.minimum(i, nu[0] - 1), 0)

    def w_idx(i, te, tv, nu):
        return (layer, te[i], 0, 0)

    return pl.pallas_call(
        _expert_kernel,
        grid_spec=pltpu.PrefetchScalarGridSpec(
            num_scalar_prefetch=3,
            grid=(cap // tmb,),
            in_specs=[pl.BlockSpec(memory_space=pl.ANY),
                      pl.BlockSpec((None, None, d, f), w_idx),
                      pl.BlockSpec((None, None, d, f), w_idx),
                      pl.BlockSpec((None, None, f, d), w_idx)],
            out_specs=pl.BlockSpec((tmb, dp), row_idx),
            scratch_shapes=[pltpu.VMEM((d, f), BF16), pltpu.VMEM((d, f), BF16),
                            pltpu.VMEM((f, d), BF16),
                            pltpu.VMEM((EXPERT_ROW_BUFFERS, tmb, dp), jnp.int32),
                            pltpu.SemaphoreType.DMA((EXPERT_ROW_BUFFERS,))]),
        out_shape=jax.ShapeDtypeStruct((cap, dp), jnp.int32),
        compiler_params=_cparams("arbitrary"),
        name="experts",
    )(tile_expert, tile_rows, n_used, xb, wg, wu, wd)


def _combine_kernel(x_ref, h_ref, routed_ref, mod_ref, wgu_ref, wd_ref, fg_ref, *refs, f, final,
                    n_alias, tile0, n_lat_tiles):
    o_refs = refs[n_alias:]
    h = jnp.concatenate(_unpack_rows(h_ref[...]), axis=1).astype(BF16)
    gu = _dot(h, wgu_ref[...])
    gate, up = gu[:, :f], gu[:, f:]
    shared = _dot((gate * _sigmoid(gate) * up).astype(BF16), wd_ref[...])
    x2 = x_ref[...] + mod_ref[5:6, :] * (shared + routed_ref[...])
    if not final:
        o_refs[0][...] = x2
    elif len(o_refs) == 1:
        o_refs[0][...] = _rms(x2, fg_ref[...])
    else:
        y = _rms(x2, fg_ref[...])
        tile = pl.program_id(0) + tile0

        @pl.when(tile < n_lat_tiles)
        def _():
            o_refs[0][...] = y

        @pl.when(tile >= n_lat_tiles)
        def _():
            o_refs[1][...] = y


def _combine(x, h, routed, mod, wgu, wd, final_g, y_lat, *, layer, tile0, n_lat, n_ctx, lat_seq,
             n_lat_batch, tm, final):
    n, d = x.shape
    f = wd.shape[0]
    n_tiles = n // tm
    n_lat_tiles = n_lat // tm

    def mod_idx(i):
        return (layer, jnp.minimum((i + tile0) * tm // lat_seq, n_lat_batch), 0, 0)

    tok = pl.BlockSpec((tm, d), lambda i: (i, 0))
    tok_p = pl.BlockSpec((tm, d // 2), lambda i: (i, 0))
    const2 = lambda i: (0, 0)
    in_specs = [tok, tok_p, tok,
                pl.BlockSpec((None, None, N_MOD, d), mod_idx),
                pl.BlockSpec((d, 2 * f), const2), pl.BlockSpec((f, d), const2),
                pl.BlockSpec((1, d), const2)]
    args = [x, h, routed, mod, wgu, wd, final_g]
    aliases = {}
    has_lat = has_ctx = False
    if not final:
        out_specs, out_shape = [tok], [jax.ShapeDtypeStruct((n, d), F32)]
    else:
        has_lat = tile0 < n_lat_tiles
        has_ctx = tile0 + n_tiles > n_lat_tiles
        out_specs, out_shape = [], []
        if has_lat:
            out_specs.append(pl.BlockSpec(
                (tm, d), lambda i: (jnp.minimum(i + tile0, n_lat_tiles - 1), 0)))
            out_shape.append(jax.ShapeDtypeStruct((n_lat, d), F32))
            if y_lat is not None:
                aliases = {len(args): 0}
                in_specs.append(pl.BlockSpec(memory_space=pl.ANY))
                args.append(y_lat)
        if has_ctx:
            out_specs.append(pl.BlockSpec(
                (tm, d), lambda i: (jnp.maximum(i + tile0 - n_lat_tiles, 0), 0)))
            out_shape.append(jax.ShapeDtypeStruct((n_ctx, d), F32))
    outs = pl.pallas_call(
        functools.partial(_combine_kernel, f=f, final=final, n_alias=len(aliases), tile0=tile0,
                          n_lat_tiles=n_lat_tiles),
        grid=(n_tiles,),
        in_specs=in_specs,
        out_specs=out_specs,
        out_shape=out_shape,
        input_output_aliases=aliases,
        compiler_params=_cparams("arbitrary"),
        name=f"combine{layer}",
    )(*args)
    if not final:
        return outs[0]
    return (outs[0] if has_lat else y_lat), (outs[-1] if has_ctx else None)


SC_CHUNK = 128


def _sc_workers():
    info = plsc.get_sparse_core_info()
    return info.num_cores, info.num_subcores


def _sc_dispatch(rows, dest, cap):
    n, w = rows.shape
    nc, ns = _sc_workers()
    per_w = n // (nc * ns)
    assert per_w * nc * ns == n and per_w % SC_CHUNK == 0
    mesh = plsc.VectorSubcoreMesh(core_axis_name="c", subcore_axis_name="s")

    @functools.partial(
        pl.kernel, mesh=mesh, out_type=jax.ShapeDtypeStruct((cap, w), rows.dtype),
        scratch_types=[pltpu.VMEM((TOP_K, SC_CHUNK), jnp.int32),
                       pltpu.VMEM((SC_CHUNK, w), rows.dtype)],
        name="sc_dispatch")
    def run(rows_hbm, dest_hbm, out_hbm, idx_v, rows_v):
        wid = lax.axis_index("s") * nc + lax.axis_index("c")

        @pl.loop(0, per_w // SC_CHUNK)
        def _(ci):
            base = pl.multiple_of(wid * per_w + ci * SC_CHUNK, SC_CHUNK)
            pltpu.sync_copy(dest_hbm.at[wid * (per_w // SC_CHUNK) + ci], idx_v)
            pltpu.sync_copy(rows_hbm.at[pl.ds(base, SC_CHUNK)], rows_v)
            for k in range(TOP_K):
                pltpu.sync_copy(rows_v, out_hbm.at[idx_v.at[k]])

    return run(rows, dest)


SC_TOK = 8
SC_LANES = 16


def _sc_combine(table, dest, gates):
    n_chunks, k_top, _ = dest.shape
    n = n_chunks * SC_CHUNK
    w = table.shape[1]
    nc, ns = _sc_workers()
    per_w = n // (nc * ns)
    assert per_w * nc * ns == n and per_w % SC_CHUNK == 0 and w % (16 * SC_LANES) == 0
    gb = gates.reshape(k_top, n_chunks, SC_CHUNK).transpose(1, 0, 2)
    mesh = plsc.VectorSubcoreMesh(core_axis_name="c", subcore_axis_name="s")

    n_blocks = SC_CHUNK // SC_TOK

    @functools.partial(
        pl.kernel, mesh=mesh, out_type=jax.ShapeDtypeStruct((n, 2 * w), F32),
        scratch_types=[pltpu.VMEM((k_top, SC_CHUNK), jnp.int32),
                       pltpu.VMEM((k_top, SC_CHUNK), F32),
                       pltpu.VMEM((2, k_top, SC_TOK, w), table.dtype),
                       pltpu.VMEM((2, SC_TOK, 2 * w), F32),
                       pltpu.SemaphoreType.DMA((2,)), pltpu.SemaphoreType.DMA((2,))],
        compiler_params=pltpu.CompilerParams(needs_layout_passes=False),
        name="sc_combine")
    def run(table_hbm, dest_hbm, gb_hbm, out_hbm, idx_v, g_v, rows_v, out_v, sem_in, sem_out):
        wid = lax.axis_index("s") * nc + lax.axis_index("c")

        def gathers(blk, slot):
            off = pl.multiple_of(blk * SC_TOK, SC_TOK)
            return [pltpu.make_async_copy(table_hbm.at[idx_v.at[k, pl.ds(off, SC_TOK)]],
                                          rows_v.at[slot, k], sem_in.at[slot]) for k in range(k_top)]

        def write_back(chunk, blk, slot):
            row0 = pl.multiple_of(chunk * SC_CHUNK + blk * SC_TOK, SC_TOK)
            return pltpu.make_async_copy(out_v.at[slot], out_hbm.at[pl.ds(row0, SC_TOK)],
                                         sem_out.at[slot])

        def weighted_sum(blk, slot):
            @pl.loop(0, SC_TOK)
            def _(j):
                for base in range(0, w, 16 * SC_LANES):
                    acc = None
                    for k in range(k_top):
                        tok = blk * SC_TOK + j
                        g16 = g_v[k, pl.ds(pl.multiple_of(tok // SC_LANES * SC_LANES, SC_LANES),
                                           SC_LANES)]
                        g = g16.at[jnp.full((SC_LANES,), tok % SC_LANES, jnp.int32)].get(
                            mode="promise_in_bounds")
                        terms = []
                        for v in range(16):
                            word = rows_v[slot, k, j, pl.ds(base + v * SC_LANES, SC_LANES)]
                            lo = lax.bitcast_convert_type(lax.shift_left(word, 16), F32)
                            hi = lax.bitcast_convert_type(word & jnp.int32(-65536), F32)
                            terms.append((g * lo, g * hi))
                        acc = terms if acc is None else [
                            (a + x, b + y) for (a, b), (x, y) in zip(acc, terms)]
                    for v, (a, b) in enumerate(acc):
                        out_v[slot, j, pl.ds(base + v * SC_LANES, SC_LANES)] = a
                        out_v[slot, j, pl.ds(w + base + v * SC_LANES, SC_LANES)] = b

        @pl.loop(0, per_w // SC_CHUNK)
        def _(ci):
            chunk = wid * (per_w // SC_CHUNK) + ci
            pltpu.sync_copy(dest_hbm.at[chunk], idx_v)
            pltpu.sync_copy(gb_hbm.at[chunk], g_v)
            for cp in gathers(0, 0):
                cp.start()

            @pl.loop(0, n_blocks, step=2)
            def _(b0):
                for slot in range(2):
                    blk = b0 + slot
                    for cp in gathers(blk, slot):
                        cp.wait()

                    @pl.when(blk + 1 < n_blocks)
                    def _():
                        for cp in gathers(blk + 1, 1 - slot):
                            cp.start()

                    @pl.when(blk >= 2)
                    def _():
                        write_back(chunk, blk - 2, slot).wait()

                    weighted_sum(blk, slot)
                    write_back(chunk, blk, slot).start()

            for slot in range(2):
                write_back(chunk, n_blocks - 2 + slot, slot).wait()


    return run(table, dest, gb)


def _split_bf16(w):
    hi = w.astype(BF16)
    return hi, (w - hi.astype(F32)).astype(BF16)


def kernel(x_prompt, x_sample, cache_k, cache_v, c, c_ctx, ada_w, ada_b, norm1_g, norm2_g, w_qkv, w_o, na_rpb, diff_lambda, diff_subln_g, router_w, router_b, exp_w_gate, exp_w_up, exp_w_down, shared_w_gate, shared_w_up, shared_w_down, final_g):
    batch, seq, d = x_prompt.shape
    n_lat_batch, lat_seq, _ = x_sample.shape
    depth = w_qkv.shape[0]
    n_exp = router_w.shape[-1]
    n_lat = n_lat_batch * lat_seq
    n_ctx = batch * seq
    n = n_lat + n_ctx
    assert d % LANES == 0 and na_rpb.shape[1] * HEAD_DIM == d
    assert diff_lambda.shape[-1] == HEAD_DIM and lat_seq % GRID_W == 0
    assert n_lat % seq == 0
    tm = _pick_tile(math.gcd(lat_seq, n_ctx), 512)
    tmb = 1024
    n_all = n // tm
    ranges = [(r * n_all // MOE_RANGES, (r + 1) * n_all // MOE_RANGES - r * n_all // MOE_RANGES)
              for r in range(MOE_RANGES)]
    scale = HEAD_DIM ** -0.5 * LOG2E

    xs = [x_sample.reshape(n_lat, d), x_prompt.reshape(n_ctx, d)]
    mod_rows = -(-(n_lat_batch + 1) // 8) * 8
    cond = jnp.zeros((mod_rows, d), F32).at[:n_lat_batch].set(c).at[n_lat_batch].set(c_ctx)
    mod = _modulation(cond, ada_w, ada_b).reshape(depth, mod_rows, N_MOD, d)

    w_qkv_b = w_qkv.astype(BF16)
    w_o_b = w_o.astype(BF16)
    rope_tab = _rope_tables(lat_seq, tm)
    caches = None
    for i in range(depth):
        is_diff = i % 2 == 1
        j = i // 2
        q, k, v, *caches = _qkv(xs, mod, norm1_g.reshape(depth, 1, d), w_qkv_b,
                                rope_tab if is_diff else None, caches, layer=i, n_lat=n_lat,
                                batch=batch, seq=seq, lat_seq=lat_seq, n_lat_batch=n_lat_batch,
                                tm=tm, scale=scale)
        if is_diff:
            vt = caches.pop()
        if not is_diff:
            o = _na_attention(q, k, v, cache_k, cache_v, na_rpb[j], layer=i,
                              n_lat_batch=n_lat_batch, lat_seq=lat_seq)
            o = _ctx_dense(q, k, v, o, n_lat=n_lat, seq=seq)
        else:
            lam_init = 0.8 - 0.6 * math.exp(-0.3 * i)
            lp = diff_lambda[j].astype(F32)
            lam = (jnp.exp(jnp.sum(lp[0] * lp[1])) - jnp.exp(jnp.sum(lp[2] * lp[3]))
                   + lam_init).reshape(1)
            subg = diff_subln_g[j].reshape(1, LANES).astype(F32)
            o = _lat_diff(lam, q, k, vt, cache_k, cache_v, subg, layer=i,
                          n_lat_batch=n_lat_batch, lat_seq=lat_seq, out_scale=1.0 - lam_init)
            o = _ctx_diff(lam, q, k, v, subg, o, n_lat=n_lat, seq=seq, out_scale=1.0 - lam_init)

        rwh, rwl = _split_bf16(router_w[i].T)
        swgu = jnp.concatenate([shared_w_gate[i], shared_w_up[i]], axis=-1).astype(BF16)
        final = i == depth - 1
        routed = []
        for tile0, nt in ranges:
            xmid, h2, sel, gates, rank, counts = _post_attn(
                o, xs, mod, w_o_b[i], norm2_g[i].reshape(1, d), rwh, rwl,
                router_b[i].reshape(n_exp, 1).astype(F32), layer=i, tile0=tile0, n_tiles=nt,
                lat_seq=lat_seq, n_lat_batch=n_lat_batch, tm=tm)
            n_tiles = -(-(nt * tm * TOP_K) // tmb) + n_exp
            dest, tile_expert, tile_rows, n_used = _plan(counts, sel, rank, tmb=tmb, n_tiles=n_tiles)
            xb = _sc_dispatch(h2, dest, n_tiles * tmb)
            routed.append((xmid, h2, gates, dest, tile_expert, tile_rows, n_used, xb))
        ybs = [_experts(te, tr, nu, xb, exp_w_gate, exp_w_up, exp_w_down, layer=i, tmb=tmb)
               for (_, _, _, _, te, tr, nu, xb) in routed]
        sums = [_sc_combine(yb, r[3], r[2]) for yb, r in zip(ybs, routed)]
        xs, y_lat, y_ctx = [], None, None
        for (tile0, nt), (xmid, h2, *_), rsum in zip(ranges, routed, sums):
            out = _combine(xmid, h2, rsum, mod, swgu, shared_w_down[i].astype(BF16),
                           final_g.reshape(1, d), y_lat, layer=i, tile0=tile0, n_lat=n_lat,
                           n_ctx=n_ctx, lat_seq=lat_seq, n_lat_batch=n_lat_batch, tm=tm, final=final)
            if final:
                y_lat, y_ctx = out[0], (out[1] if out[1] is not None else y_ctx)
            else:
                xs.append(out)

    y_sample = y_lat.reshape(n_lat_batch, lat_seq, d)
    y_prompt = y_ctx.reshape(batch, seq, d)
    return (y_prompt, y_sample, caches[0], caches[1])
```

```python
import functools
import math

import jax
import jax.numpy as jnp
from jax import lax
from jax.experimental import pallas as pl
from jax.experimental.pallas import tpu as pltpu
from jax.experimental.pallas import tpu_sc as plsc

GRID_W = 64
NA_ROWS = 8
NA_COLS = 16
TOP_K = 8
ROUTED_SCALE = 2.5
ROPE_BASE = 10000.0
EPS = 1e-6
N_MOD = 6
HEAD_DIM = 64
LANES = 128
LOG2E = math.log2(math.e)
MASKED = -1e30
NA_QROWS = 2
NA_WIN = NA_ROWS + NA_QROWS - 1
EXPERT_ROW_BUFFERS = 4
POST_SPLIT = 1
MOE_RANGES = 2
QUERY_GROUPS = 4
VMEM_LIMIT = 56 * 1024 * 1024

F32 = jnp.float32
BF16 = jnp.bfloat16


def _cparams(*sem):
    return pltpu.CompilerParams(dimension_semantics=sem, vmem_limit_bytes=VMEM_LIMIT)


def _dot(a, b):
    return jnp.dot(a, b, preferred_element_type=F32)


def _dot_nt(a, b):
    return lax.dot_general(a, b, (((1,), (1,)), ((), ())), preferred_element_type=F32)


def _sigmoid(x):
    return 1.0 / (1.0 + jnp.exp(-x))


def _rms(x, g):
    return x * lax.rsqrt(jnp.mean(x * x, axis=-1, keepdims=True) + EPS) * g


def _pack_rows(xb):
    half = xb.shape[1] // 2
    u = lax.bitcast_convert_type(xb.astype(F32), jnp.uint32)
    packed = (u[:, :half] >> 16) | (u[:, half:] & jnp.uint32(0xFFFF0000))
    return lax.bitcast_convert_type(packed, jnp.int32)


def _unpack_rows(p):
    u = lax.bitcast_convert_type(p, jnp.uint32)
    return (lax.bitcast_convert_type(u << 16, F32),
            lax.bitcast_convert_type(u & jnp.uint32(0xFFFF0000), F32))


def _pick_tile(n, pref):
    t = pref
    while n % t:
        t //= 2
    return t


def _mod_kernel(c_ref, w_ref, b_ref, o_ref):
    c = c_ref[...]
    o_ref[...] = _dot(c * _sigmoid(c), w_ref[...]) + b_ref[...]


def _modulation(cond, ada_w, ada_b):
    depth, d, n6 = ada_w.shape
    rows = cond.shape[0]
    tn = _pick_tile(n6, 512)
    return pl.pallas_call(
        _mod_kernel,
        grid=(depth, n6 // tn),
        in_specs=[
            pl.BlockSpec((rows, d), lambda l, j: (0, 0)),
            pl.BlockSpec((None, d, tn), lambda l, j: (l, 0, j)),
            pl.BlockSpec((None, 1, tn), lambda l, j: (l, 0, j)),
        ],
        out_specs=pl.BlockSpec((None, rows, tn), lambda l, j: (l, 0, j)),
        out_shape=jax.ShapeDtypeStruct((depth, rows, n6), F32),
        compiler_params=_cparams("arbitrary", "arbitrary"),
        name="modulation",
    )(cond, ada_w, ada_b.reshape(depth, 1, n6))


def _read_tokens(x_refs, g, ends):
    x = x_refs[-1][...]
    for ref, end in zip(reversed(x_refs[:-1]), reversed(ends[:-1])):
        x = jnp.where(g < end, ref[...], x)
    return x


def _token_parts(xs, tm, tile0, n_tiles):
    arrays, specs, ends = [], [], []
    start = 0
    for x in xs:
        size = x.shape[0] // tm
        if start < tile0 + n_tiles and start + size > tile0:
            arrays.append(x)
            ends.append(start + size)
            specs.append(pl.BlockSpec(
                (tm, x.shape[1]), lambda i, s=start, z=size: (jnp.clip(i + tile0 - s, 0, z - 1), 0)))
        start += size
    return arrays, specs, ends


def _qkv_kernel(*refs, d, scale, rope, n_lat_tiles, ends, n_alias):
    n_x = len(ends)
    x_refs = refs[:n_x]
    mod_ref, g_ref, w_ref = refs[n_x:n_x + 3]
    rest = refs[n_x + 3:]
    if rope:
        rope_ref, wvt_ref, rest = rest[0], rest[1], rest[2:]
    rest = rest[n_alias:]
    q_ref, k_ref, v_ref, kf_ref, vf_ref = rest[:5]
    i = pl.program_id(0)
    x = _read_tokens(x_refs, i, ends)
    h = _rms(x, g_ref[...]) * (1.0 + mod_ref[1:2, :]) + mod_ref[0:1, :]
    hb = h.astype(BF16)

    def rot(t):
        if not rope:
            return t
        cols = []
        for j in range(d // LANES):
            tb = t[:, j * LANES:(j + 1) * LANES]
            cols.append(tb * rope_ref[0]
                        + pltpu.roll(tb, LANES - HEAD_DIM // 2, axis=1) * rope_ref[1]
                        + pltpu.roll(tb, HEAD_DIM // 2, axis=1) * rope_ref[2])
        return jnp.concatenate(cols, axis=1)

    q = _dot(hb, w_ref[:, 0:d])
    q_ref[...] = (rot(q) * scale).astype(BF16)
    k = _dot(hb, w_ref[:, d:2 * d])
    k_ref[...] = rot(k).astype(BF16)
    if rope:
        @pl.when(i < n_lat_tiles)
        def _():
            rest[5][...] = _dot_nt(wvt_ref[...], hb).astype(BF16)

    def context_values():
        v = _dot(hb, w_ref[:, 2 * d:3 * d])
        v_ref[...] = v.astype(BF16)
        return v

    if not rope:
        v = context_values()

    @pl.when(i >= n_lat_tiles)
    def _():
        kf_ref[...] = k.reshape(kf_ref.shape)
        vf_ref[...] = (context_values() if rope else v).reshape(vf_ref.shape)


def _qkv(xs, mod, g, w, rope_tab, caches, *, layer, n_lat, batch, seq, lat_seq, n_lat_batch, tm,
         scale):
    d = xs[0].shape[1]
    depth = w.shape[0]
    n = n_lat + batch * seq
    n_lat_tiles = n_lat // tm
    rope = rope_tab is not None
    assert tm % seq == 0
    x_arrays, x_specs, ends = _token_parts(xs, tm, 0, n // tm)

    def mod_idx(i):
        return (layer, jnp.minimum(i * tm // lat_seq, n_lat_batch), 0, 0)

    in_specs = x_specs + [
        pl.BlockSpec((None, None, N_MOD, d), mod_idx),
        pl.BlockSpec((None, 1, d), lambda i: (layer, 0, 0)),
        pl.BlockSpec((None, d, 3 * d), lambda i: (layer, 0, 0)),
    ]
    args = x_arrays + [mod, g, w]
    if rope:
        seq_tiles = lat_seq // tm
        in_specs.append(pl.BlockSpec(
            (3, tm, LANES),
            lambda i: (0, jnp.where(i < n_lat_tiles, i % seq_tiles, seq_tiles), 0)))
        in_specs.append(pl.BlockSpec((d, d), lambda i: (0, 0)))
        args += [rope_tab, w[layer, :, 2 * d:].T]
    aliases = {}
    if caches is not None:
        aliases = {len(args): 3, len(args) + 1: 4}
        in_specs += [pl.BlockSpec(memory_space=pl.ANY)] * 2
        args += list(caches)
    tok = pl.BlockSpec((tm, d), lambda i: (i, 0))
    cache_spec = pl.BlockSpec((tm // seq, None, seq, d),
                              lambda i: (jnp.maximum(i - n_lat_tiles, 0), layer, 0, 0))
    out_specs = [tok, tok, tok, cache_spec, cache_spec]
    out_shape = ([jax.ShapeDtypeStruct((n, d), BF16)] * 3
                 + [jax.ShapeDtypeStruct((batch, depth, seq, d), F32)] * 2)
    if rope:
        out_specs.append(pl.BlockSpec((None, d, tm),
                                      lambda i: (jnp.minimum(i, n_lat_tiles - 1), 0, 0)))
        out_shape.append(jax.ShapeDtypeStruct((n_lat_tiles, d, tm), BF16))
    return pl.pallas_call(
        functools.partial(_qkv_kernel, d=d, scale=scale, rope=rope, n_lat_tiles=n_lat_tiles,
                          ends=tuple(ends), n_alias=len(aliases)),
        grid=(n // tm,),
        in_specs=in_specs,
        out_specs=out_specs,
        out_shape=out_shape,
        input_output_aliases=aliases,
        compiler_params=_cparams("arbitrary"),
        name=f"qkv{layer}",
    )(*args)


def _rope_tables(lat_seq, tm):
    n_freq = HEAD_DIM // 4
    inv = ROPE_BASE ** (-jnp.arange(n_freq, dtype=F32) / n_freq)
    pos = jnp.arange(lat_seq)
    row = (pos // GRID_W).astype(F32)
    col = (pos % GRID_W).astype(F32)
    ang = jnp.concatenate([row[:, None] * inv, col[:, None] * inv], axis=-1)
    reps = LANES // (HEAD_DIM // 2)
    cos = jnp.tile(jnp.cos(ang), (1, reps))
    sin = jnp.tile(jnp.sin(ang), (1, reps))
    first_half = (jnp.arange(LANES) % HEAD_DIM) < HEAD_DIM // 2
    s_next = jnp.where(first_half, -sin, 0.0)
    s_prev = jnp.where(first_half, 0.0, sin)
    ident = jnp.stack([jnp.ones((tm, LANES), F32), jnp.zeros((tm, LANES), F32),
                       jnp.zeros((tm, LANES), F32)])
    return jnp.concatenate([jnp.stack([cos, s_next, s_prev]), ident], axis=1)


def _lane_lo():
    return lax.broadcasted_iota(jnp.int32, (1, LANES), 1) < HEAD_DIM


def _ctx_dense_kernel(q_ref, k_ref, v_ref, o_all_ref, o_ref, *, d):
    del o_all_ref
    lo = _lane_lo()
    blks = [slice(j * LANES, (j + 1) * LANES) for j in range(d // LANES)]
    scores = []
    for blk in blks:
        q2 = q_ref[:, blk]
        scores += [_dot_nt(jnp.where(lo, q2, 0), k_ref[:, blk]),
                   _dot_nt(jnp.where(lo, 0, q2), k_ref[:, blk])]
    probs = []
    for s in scores:
        p = jnp.exp2(s - jnp.max(s, axis=-1, keepdims=True))
        probs.append((p.astype(BF16), jnp.sum(p, axis=-1, keepdims=True)))
    outs = [_dot(p, v_ref[:, blks[h // 2]]) / l for h, (p, l) in enumerate(probs)]
    o_ref[...] = jnp.concatenate(
        [jnp.where(lo, outs[h], outs[h + 1]).astype(BF16) for h in range(0, len(outs), 2)], axis=1)


def _ctx_dense(q, k, v, o_all, *, n_lat, seq):
    n, d = q.shape
    off = n_lat // seq
    spec = pl.BlockSpec((seq, d), lambda b: (off + b, 0))
    return pl.pallas_call(
        functools.partial(_ctx_dense_kernel, d=d),
        grid=((n - n_lat) // seq,),
        in_specs=[spec, spec, spec, pl.BlockSpec(memory_space=pl.ANY)],
        out_specs=spec,
        out_shape=jax.ShapeDtypeStruct((n, d), BF16),
        input_output_aliases={3: 0},
        compiler_params=_cparams("arbitrary"),
        name="ctx_dense_attn",
    )(q, k, v, o_all)


def _subln(o, g_ref, out_scale):
    return _rms(o, g_ref[...]) * out_scale


def _ctx_diff_kernel(lam_ref, q_ref, k_ref, v_ref, g_ref, o_all_ref, o_ref, *, d, out_scale):
    del o_all_ref
    lo = _lane_lo()
    lam = lam_ref[0]
    blks = [slice(j * LANES, (j + 1) * LANES) for j in range(d // LANES)]
    scores = []
    for blk in blks:
        q2 = q_ref[:, blk]
        scores.append((_dot_nt(jnp.where(lo, q2, 0), k_ref[:, blk]),
                       _dot_nt(jnp.where(lo, 0, q2), k_ref[:, blk])))
    probs = []
    for pair in scores:
        ps = []
        for s in pair:
            p = jnp.exp2(s - jnp.max(s, axis=-1, keepdims=True))
            ps.append(p / jnp.sum(p, axis=-1, keepdims=True))
        probs.append((ps[0] - lam * ps[1]).astype(BF16))
    outs = [_dot(p, v_ref[:, blk]) for p, blk in zip(probs, blks)]
    o_ref[...] = jnp.concatenate([_subln(o, g_ref, out_scale).astype(BF16) for o in outs], axis=1)


def _ctx_diff(lam, q, k, v, subg, o_all, *, n_lat, seq, out_scale):
    n, d = q.shape
    off = n_lat // seq
    spec = pl.BlockSpec((seq, d), lambda b: (off + b, 0))
    return pl.pallas_call(
        functools.partial(_ctx_diff_kernel, d=d, out_scale=out_scale),
        grid=((n - n_lat) // seq,),
        in_specs=[pl.BlockSpec(memory_space=pltpu.SMEM), spec, spec, spec,
                  pl.BlockSpec((1, LANES), lambda b: (0, 0)), pl.BlockSpec(memory_space=pl.ANY)],
        out_specs=spec,
        out_shape=jax.ShapeDtypeStruct((n, d), BF16),
        input_output_aliases={5: 0},
        compiler_params=_cparams("arbitrary"),
        name="ctx_diff_attn",
    )(lam, q, k, v, subg, o_all)


def _na_windows(rows):
    starts, classes, geoms = [], [], []
    for r0 in range(0, rows, NA_QROWS):
        first = [min(max(r0 + i - NA_ROWS // 2, 0), rows - NA_ROWS) for i in range(NA_QROWS)]
        ws = min(first[0], rows - NA_WIN)
        geom = tuple(
            tuple((ws + u) - (r0 + i) + NA_ROWS - 1 if first[i] <= ws + u < first[i] + NA_ROWS else -1
                  for u in range(NA_WIN)) for i in range(NA_QROWS))
        if geom not in geoms:
            geoms.append(geom)
        starts.append(ws)
        classes.append(geoms.index(geom))
    return starts, classes, geoms


def _na_kernel(ws_ref, cls_ref, q_ref, k_ref, v_ref, ck_ref, cv_ref, bias_ref, o_ref, *, d):
    del cls_ref
    lo = _lane_lo()
    win = NA_WIN * GRID_W
    start = pl.multiple_of(ws_ref[pl.program_id(1)] * GRID_W, GRID_W)
    n_heads = 2 * (d // LANES)

    def blk(h):
        return slice(h // 2 * LANES, (h // 2 + 1) * LANES)

    scores = []
    for h in range(n_heads):
        q2 = q_ref[:, blk(h)]
        qm = jnp.where(lo, q2, 0) if h % 2 == 0 else jnp.where(lo, 0, q2)
        scores.append((_dot_nt(qm, k_ref[pl.ds(start, win), blk(h)]) + bias_ref[h],
                       _dot_nt(qm, ck_ref[:, blk(h)].astype(BF16))))
    probs = []
    for s_loc, s_ctx in scores:
        m = jnp.maximum(jnp.max(s_loc, axis=-1, keepdims=True),
                        jnp.max(s_ctx, axis=-1, keepdims=True))
        p_loc = jnp.exp2(s_loc - m)
        p_ctx = jnp.exp2(s_ctx - m)
        l = jnp.sum(p_loc, axis=-1, keepdims=True) + jnp.sum(p_ctx, axis=-1, keepdims=True)
        probs.append((p_loc.astype(BF16), p_ctx.astype(BF16), l))
    outs = [(_dot(p_loc, v_ref[pl.ds(start, win), blk(h)])
             + _dot(p_ctx, cv_ref[:, blk(h)].astype(BF16))) / l
            for h, (p_loc, p_ctx, l) in enumerate(probs)]
    o_ref[...] = jnp.concatenate(
        [jnp.where(lo, outs[h], outs[h + 1]).astype(BF16) for h in range(0, n_heads, 2)], axis=1)


def _na_bias_table(rpb, geoms):
    h = rpb.shape[0]
    c = jnp.arange(GRID_W)
    kc = jnp.arange(GRID_W)
    c0 = jnp.clip(c - NA_COLS // 2, 0, GRID_W - NA_COLS)
    off_c = kc[None, :] - c[:, None] + (NA_COLS - 1)
    inside = (kc[None, :] >= c0[:, None]) & (kc[None, :] < c0[:, None] + NA_COLS)
    oh_c = jax.nn.one_hot(off_c, 2 * NA_COLS - 1, dtype=F32)
    blocks = jnp.einsum('hrs,cqs->rhcq', rpb.astype(F32), oh_c, precision=lax.Precision.HIGHEST)
    blocks = jnp.where(inside, blocks * LOG2E, MASKED)
    masked = jnp.full((h, GRID_W, GRID_W), MASKED, F32)
    return jnp.stack([
        jnp.concatenate([
            jnp.concatenate([blocks[r] if r >= 0 else masked for r in row], axis=-1)
            for row in geom], axis=-2)
        for geom in geoms])


def _na_attention(q, k, v, ck, cv, rpb, *, layer, n_lat_batch, lat_seq):
    n, d = q.shape
    rows = lat_seq // GRID_W
    past = ck.shape[2]
    n_heads = rpb.shape[0]
    assert rows % NA_QROWS == 0 and rows >= NA_WIN
    starts, classes, geoms = _na_windows(rows)
    bias = _na_bias_table(rpb, geoms)
    groups = rows // NA_QROWS
    tq = NA_QROWS * GRID_W
    kv_spec = pl.BlockSpec((lat_seq, d), lambda b, r, ws, cls: (b, 0))
    c_spec = pl.BlockSpec((None, None, past, d), lambda b, r, ws, cls: (b, layer, 0, 0))
    q_spec = pl.BlockSpec((tq, d), lambda b, r, ws, cls: (b * groups + r, 0))
    return pl.pallas_call(
        functools.partial(_na_kernel, d=d),
        grid_spec=pltpu.PrefetchScalarGridSpec(
            num_scalar_prefetch=2,
            grid=(n_lat_batch, groups),
            in_specs=[q_spec, kv_spec, kv_spec, c_spec, c_spec,
                      pl.BlockSpec((None, n_heads, tq, NA_WIN * GRID_W),
                                   lambda b, r, ws, cls: (cls[r], 0, 0, 0))],
            out_specs=q_spec),
        out_shape=jax.ShapeDtypeStruct((n, d), BF16),
        compiler_params=_cparams("arbitrary", "arbitrary"),
        name="na_attn",
    )(jnp.asarray(starts, jnp.int32), jnp.asarray(classes, jnp.int32), q, k, v, ck, cv, bias)


def _lat_diff_kernel(lam_ref, q_ref, k_ref, vt_ref, ck_ref, cv_ref, g_ref, o_ref, *, out_scale):
    lo = _lane_lo()
    lam = lam_ref[0]
    q2 = q_ref[...]
    tq = q2.shape[0]
    tg = tq // QUERY_GROUPS
    qs = []
    for g in range(QUERY_GROUPS):
        qg = q2[g * tg:(g + 1) * tg, :]
        qs += [jnp.where(lo, qg, 0), jnp.where(lo, 0, qg)]

    def softmax_step(state, s):
        m, l, acc = state
        m_new = jnp.maximum(m, jnp.max(s, axis=0, keepdims=True))
        alpha = jnp.exp2(m - m_new)
        p = jnp.exp2(s - m_new)
        return m_new, alpha * l + jnp.sum(p, axis=0, keepdims=True), alpha * acc, p.astype(BF16)

    def scores(kt):
        return [_dot_nt(kt, qm) for qm in qs]

    def absorb(states, ss, vt):
        out = []
        for g in range(0, len(qs), 2):
            stepped = [softmax_step(states[g + j], ss[g + j]) for j in range(2)]
            out += [(m, l, acc + _dot(vt, p)) for m, l, acc, p in stepped]
        return out

    tile = vt_ref.shape[2]
    n_tiles = vt_ref.shape[0]

    def keys(t):
        return k_ref[t * tile:(t + 1) * tile, :] if t < n_tiles else ck_ref[...].astype(BF16)

    def values_t(t):
        return vt_ref[t] if t < n_tiles else cv_ref[...].T.astype(BF16)

    init = (jnp.full((1, tg), -jnp.inf, F32), jnp.zeros((1, tg), F32), jnp.zeros((LANES, tg), F32))
    states = [init] * len(qs)
    ss = scores(keys(0))
    for t in range(n_tiles + 1):
        ss_next = scores(keys(t + 1)) if t < n_tiles else None
        states = absorb(states, ss, values_t(t))
        ss = ss_next
    o = jnp.concatenate([states[g][2] / states[g][1] - lam * (states[g + 1][2] / states[g + 1][1])
                         for g in range(0, len(qs), 2)], axis=1).T
    o_ref[...] = _subln(o, g_ref, out_scale).astype(BF16)


def _lat_diff(lam, q, k, vt, ck, cv, subg, *, layer, n_lat_batch, lat_seq, out_scale):
    n, d = q.shape
    past = ck.shape[2]
    tk = vt.shape[2]
    tq = _pick_tile(lat_seq, QUERY_GROUPS * 256)
    qt = lat_seq // tq
    kt = lat_seq // tk
    c_spec = pl.BlockSpec((None, None, past, LANES), lambda b, h, i: (b, layer, 0, h))
    q_spec = pl.BlockSpec((tq, LANES), lambda b, h, i: (b * qt + i, h))
    return pl.pallas_call(
        functools.partial(_lat_diff_kernel, out_scale=out_scale),
        grid=(n_lat_batch, d // LANES, qt),
        in_specs=[pl.BlockSpec(memory_space=pltpu.SMEM), q_spec,
                  pl.BlockSpec((lat_seq, LANES), lambda b, h, i: (b, h)),
                  pl.BlockSpec((kt, LANES, tk), lambda b, h, i: (b, h, 0)),
                  c_spec, c_spec, pl.BlockSpec((1, LANES), lambda b, h, i: (0, 0))],
        out_specs=q_spec,
        out_shape=jax.ShapeDtypeStruct((n, d), BF16),
        compiler_params=_cparams("arbitrary", "arbitrary", "arbitrary"),
        name="lat_diff_attn",
    )(lam, q, k, vt, ck, cv, subg)


def _post_attn_kernel(o_ref, *refs, ends, tile0):
    n_x = len(ends)
    x_refs = refs[:n_x]
    (mod_ref, wo_ref, g_ref, rwh_ref, rwl_ref, rb_ref, tri_ref,
     xmid_ref, h_ref, sel_ref, gate_ref, rank_ref, cnt_ref, carry_ref) = refs[n_x:]
    i = pl.program_id(0)

    @pl.when(i == 0)
    def _():
        carry_ref[...] = jnp.zeros_like(carry_ref)

    tm = o_ref.shape[0]
    hm = tm // POST_SPLIT
    groups = [slice(r * hm, (r + 1) * hm) for r in range(POST_SPLIT)]
    x_in = _read_tokens(x_refs, i + tile0, ends)
    x1s = [x_in[rows] + mod_ref[2:3, :] * _dot(o_ref[rows, :], wo_ref[...]) for rows in groups]
    logits = []
    for rows, x1 in zip(groups, x1s):
        xmid_ref[rows, :] = x1
        h = _rms(x1, g_ref[...]) * (1.0 + mod_ref[4:5, :]) + mod_ref[3:4, :]
        hb = h.astype(BF16)
        h_ref[rows, :] = _pack_rows(hb)
        h_lo = (h - hb.astype(F32)).astype(BF16)
        logits.append(_dot_nt(rwh_ref[...], hb)
                      + (_dot_nt(rwl_ref[...], hb) + _dot_nt(rwh_ref[...], h_lo)))
    n_exp = rwh_ref.shape[0]
    row = lax.broadcasted_iota(jnp.int32, (n_exp, hm), 0).astype(F32)
    picked = []
    for lg in logits:
        scores = _sigmoid(lg)
        biased = scores + rb_ref[...]
        total = jnp.zeros((n_exp, hm), F32)
        sels, gates = [], []
        for _ in range(TOP_K):
            m = jnp.max(biased, axis=0, keepdims=True)
            idx = jnp.min(jnp.where(biased == m, row, float(n_exp)), axis=0, keepdims=True)
            hit = row == idx
            gates.append(jnp.sum(jnp.where(hit, scores, 0.0), axis=0, keepdims=True))
            biased = jnp.where(hit, -jnp.inf, biased)
            total = total + jnp.where(hit, 1.0, 0.0)
            sels.append(idx)
        picked.append((sels, gates, total))
    carry = carry_ref[:, 0:1]
    for rows, (sels, gates, total) in zip(groups, picked):
        denom = gates[0]
        for g in gates[1:]:
            denom = denom + g
        before = _dot(total.astype(BF16), tri_ref[...]) + carry
        for k in range(TOP_K):
            sel_ref[k:k + 1, rows] = sels[k].astype(jnp.int32)
            gate_ref[k:k + 1, rows] = gates[k] / denom * ROUTED_SCALE
            rank_ref[k:k + 1, rows] = jnp.sum(jnp.where(row == sels[k], before, 0.0), axis=0,
                                              keepdims=True).astype(jnp.int32)
        carry = carry + jnp.sum(total, axis=1, keepdims=True)
    carry_ref[...] = jnp.broadcast_to(carry, carry_ref.shape)
    cnt_ref[...] = carry_ref[...].astype(jnp.int32)


def _post_attn(o, xs, mod, wo, g, rwh, rwl, rb, *, layer, tile0, n_tiles, lat_seq, n_lat_batch, tm):
    d = o.shape[1]
    n = n_tiles * tm
    n_exp = rwh.shape[0]
    hm = tm // POST_SPLIT
    tri = (jnp.arange(hm)[:, None] < jnp.arange(hm)[None, :]).astype(BF16)
    x_arrays, x_specs, ends = _token_parts(xs, tm, tile0, n_tiles)

    def mod_idx(i):
        return (layer, jnp.minimum((i + tile0) * tm // lat_seq, n_lat_batch), 0, 0)

    tok = pl.BlockSpec((tm, d), lambda i: (i, 0))
    tok_p = pl.BlockSpec((tm, d // 2), lambda i: (i, 0))
    kt = pl.BlockSpec((TOP_K, tm), lambda i: (0, i))
    const2 = lambda i: (0, 0)
    return pl.pallas_call(
        functools.partial(_post_attn_kernel, ends=tuple(ends), tile0=tile0),
        grid=(n_tiles,),
        in_specs=[pl.BlockSpec((tm, d), lambda i: (i + tile0, 0))] + x_specs + [
                  pl.BlockSpec((None, None, N_MOD, d), mod_idx),
                  pl.BlockSpec((d, d), const2), pl.BlockSpec((1, d), const2),
                  pl.BlockSpec((n_exp, d), const2), pl.BlockSpec((n_exp, d), const2),
                  pl.BlockSpec((n_exp, 1), const2), pl.BlockSpec((hm, hm), const2)],
        out_specs=[tok, tok_p, kt, kt, kt, pl.BlockSpec((n_exp, LANES), const2)],
        out_shape=[jax.ShapeDtypeStruct((n, d), F32), jax.ShapeDtypeStruct((n, d // 2), jnp.int32),
                   jax.ShapeDtypeStruct((TOP_K, n), jnp.int32),
                   jax.ShapeDtypeStruct((TOP_K, n), F32),
                   jax.ShapeDtypeStruct((TOP_K, n), jnp.int32),
                   jax.ShapeDtypeStruct((n_exp, LANES), jnp.int32)],
        scratch_shapes=[pltpu.VMEM((n_exp, LANES), F32)],
        compiler_params=_cparams("arbitrary"),
        name=f"post_attn{layer}",
    )(o, *x_arrays, mod, wo, g, rwh, rwl, rb, tri)


def _plan_kernel(cnt_ref, sel_ref, rank_ref, dest_ref, te_ref, tv_ref, nu_ref, *, tmb):
    n_exp = cnt_ref.shape[0]
    sel = sel_ref[...]
    tile_start = lax.broadcasted_iota(jnp.int32, te_ref.shape, 1) * tmb
    dest = rank_ref[...]
    te = jnp.zeros(te_ref.shape, jnp.int32)
    tv = jnp.zeros(te_ref.shape, jnp.int32)
    start = jnp.zeros((1, 1), jnp.int32)
    for e in range(n_exp):
        dest = dest + jnp.where(sel == e, start, 0)
        cnt = cnt_ref[e:e + 1, 0:1]
        stop = start + ((cnt + (tmb - 1)) & -tmb)
        mine = (tile_start >= start) & (tile_start < stop)
        tv = tv + jnp.where(mine, jnp.clip(start + cnt - tile_start, 0, tmb), 0)
        start = stop
        te = te + jnp.where(start <= tile_start, 1, 0)
    te_ref[...] = jnp.minimum(te, n_exp - 1)
    tv_ref[...] = tv
    nu_ref[...] = jnp.broadcast_to(start, nu_ref.shape)
    for c in range(dest_ref.shape[0]):
        dest_ref[c] = dest[:, c * SC_CHUNK:(c + 1) * SC_CHUNK]


def _plan(counts, sel, rank, *, tmb, n_tiles):
    k_top, n = sel.shape
    assert tmb & (tmb - 1) == 0
    tn = _pick_tile(n, 16 * SC_CHUNK)
    te_w = -(-n_tiles // LANES) * LANES
    tok = pl.BlockSpec((k_top, tn), lambda i: (0, i))
    const2 = lambda i: (0, 0)
    dest, te, tv, nu = pl.pallas_call(
        functools.partial(_plan_kernel, tmb=tmb),
        grid=(n // tn,),
        in_specs=[pl.BlockSpec(counts.shape, const2), tok, tok],
        out_specs=[pl.BlockSpec((tn // SC_CHUNK, k_top, SC_CHUNK), lambda i: (i, 0, 0)),
                   pl.BlockSpec((1, te_w), const2), pl.BlockSpec((1, te_w), const2),
                   pl.BlockSpec((1, LANES), const2)],
        out_shape=[jax.ShapeDtypeStruct((n // SC_CHUNK, k_top, SC_CHUNK), jnp.int32),
                   jax.ShapeDtypeStruct((1, te_w), jnp.int32),
                   jax.ShapeDtypeStruct((1, te_w), jnp.int32),
                   jax.ShapeDtypeStruct((1, LANES), jnp.int32)],
        compiler_params=_cparams("arbitrary"),
        name="plan",
    )(counts, sel, rank)
    return dest, te[0, :n_tiles], tv[0, :n_tiles], nu[0, :1] // tmb


def _expert_kernel(te_ref, tv_ref, nu_ref, x_hbm, wg_ref, wu_ref, wd_ref, y_ref, wg_b, wu_b, wd_b,
                   xbuf, sem):
    i = pl.program_id(0)
    n_buf, tmb, _ = xbuf.shape
    half = tmb // 2

    def x_copy(j, slot):
        return pltpu.make_async_copy(x_hbm.at[pl.ds(pl.multiple_of(j * tmb, tmb), tmb)],
                                     xbuf.at[slot], sem.at[slot])

    @pl.when(i == 0)
    def _():
        for j in range(n_buf - 1):
            @pl.when(j < nu_ref[0])
            def _():
                x_copy(j, j).start()

    ahead = i + (n_buf - 1)

    @pl.when(ahead < nu_ref[0])
    def _():
        x_copy(ahead, lax.rem(ahead, n_buf)).start()

    slot = lax.rem(i, n_buf)

    def rows(r):
        return jnp.concatenate(_unpack_rows(xbuf[slot, r * half:(r + 1) * half, :]),
                               axis=1).astype(BF16)

    def swiglu(xs):
        gu = [(_dot(x, wg_b[...]), _dot(x, wu_b[...])) for x in xs]
        acts = [(g * _sigmoid(g) * u).astype(BF16) for g, u in gu]
        return [_pack_rows(_dot(a, wd_b[...]).astype(BF16)) for a in acts]

    @pl.when(i < nu_ref[0])
    def _():
        x_copy(i, slot).wait()

        @pl.when((i == 0) | (te_ref[i] != te_ref[jnp.maximum(i - 1, 0)]))
        def _():
            wg_b[...] = wg_ref[...].astype(BF16)
            wu_b[...] = wu_ref[...].astype(BF16)
            wd_b[...] = wd_ref[...].astype(BF16)

        @pl.when(tv_ref[i] > half)
        def _():
            ya, yb = swiglu([rows(0), rows(1)])
            y_ref[0:half, :] = ya
            y_ref[half:2 * half, :] = yb

        @pl.when(tv_ref[i] <= half)
        def _():
            y_ref[0:half, :] = swiglu([rows(0)])[0]


def _experts(tile_expert, tile_rows, n_used, xb, wg, wu, wd, *, layer, tmb):
    cap, dp = xb.shape
    f, d = wd.shape[2:]

    def row_idx(i, te, tv, nu):
        return (jnp.minimum(i, nu[0] - 1), 0)

    def w_idx(i, te, tv, nu):
        return (layer, te[i], 0, 0)

    return pl.pallas_call(
        _expert_kernel,
        grid_spec=pltpu.PrefetchScalarGridSpec(
            num_scalar_prefetch=3,
            grid=(cap // tmb,),
            in_specs=[pl.BlockSpec(memory_space=pl.ANY),
                      pl.BlockSpec((None, None, d, f), w_idx),
                      pl.BlockSpec((None, None, d, f), w_idx),
                      pl.BlockSpec((None, None, f, d), w_idx)],
            out_specs=pl.BlockSpec((tmb, dp), row_idx),
            scratch_shapes=[pltpu.VMEM((d, f), BF16), pltpu.VMEM((d, f), BF16),
                            pltpu.VMEM((f, d), BF16),
                            pltpu.VMEM((EXPERT_ROW_BUFFERS, tmb, dp), jnp.int32),
                            pltpu.SemaphoreType.DMA((EXPERT_ROW_BUFFERS,))]),
        out_shape=jax.ShapeDtypeStruct((cap, dp), jnp.int32),
        compiler_params=_cparams("arbitrary"),
        name="experts",
    )(tile_expert, tile_rows, n_used, xb, wg, wu, wd)


def _combine_kernel(x_ref, h_ref, routed_ref, mod_ref, wgu_ref, wd_ref, fg_ref, *refs, f, final,
                    n_alias, tile0, n_lat_tiles):
    o_refs = refs[n_alias:]
    h = jnp.concatenate(_unpack_rows(h_ref[...]), axis=1).astype(BF16)
    gu = _dot(h, wgu_ref[...])
    gate, up = gu[:, :f], gu[:, f:]
    shared = _dot((gate * _sigmoid(gate) * up).astype(BF16), wd_ref[...])
    x2 = x_ref[...] + mod_ref[5:6, :] * (shared + routed_ref[...])
    if not final:
        o_refs[0][...] = x2
    elif len(o_refs) == 1:
        o_refs[0][...] = _rms(x2, fg_ref[...])
    else:
        y = _rms(x2, fg_ref[...])
        tile = pl.program_id(0) + tile0

        @pl.when(tile < n_lat_tiles)
        def _():
            o_refs[0][...] = y

        @pl.when(tile >= n_lat_tiles)
        def _():
            o_refs[1][...] = y


def _combine(x, h, routed, mod, wgu, wd, final_g, y_lat, *, layer, tile0, n_lat, n_ctx, lat_seq,
             n_lat_batch, tm, final):
    n, d = x.shape
    f = wd.shape[0]
    n_tiles = n // tm
    n_lat_tiles = n_lat // tm

    def mod_idx(i):
        return (layer, jnp.minimum((i + tile0) * tm // lat_seq, n_lat_batch), 0, 0)

    tok = pl.BlockSpec((tm, d), lambda i: (i, 0))
    tok_p = pl.BlockSpec((tm, d // 2), lambda i: (i, 0))
    const2 = lambda i: (0, 0)
    in_specs = [tok, tok_p, tok,
                pl.BlockSpec((None, None, N_MOD, d), mod_idx),
                pl.BlockSpec((d, 2 * f), const2), pl.BlockSpec((f, d), const2),
                pl.BlockSpec((1, d), const2)]
    args = [x, h, routed, mod, wgu, wd, final_g]
    aliases = {}
    has_lat = has_ctx = False
    if not final:
        out_specs, out_shape = [tok], [jax.ShapeDtypeStruct((n, d), F32)]
    else:
        has_lat = tile0 < n_lat_tiles
        has_ctx = tile0 + n_tiles > n_lat_tiles
        out_specs, out_shape = [], []
        if has_lat:
            out_specs.append(pl.BlockSpec(
                (tm, d), lambda i: (jnp.minimum(i + tile0, n_lat_tiles - 1), 0)))
            out_shape.append(jax.ShapeDtypeStruct((n_lat, d), F32))
            if y_lat is not None:
                aliases = {len(args): 0}
                in_specs.append(pl.BlockSpec(memory_space=pl.ANY))
                args.append(y_lat)
        if has_ctx:
            out_specs.append(pl.BlockSpec(
                (tm, d), lambda i: (jnp.maximum(i + tile0 - n_lat_tiles, 0), 0)))
            out_shape.append(jax.ShapeDtypeStruct((n_ctx, d), F32))
    outs = pl.pallas_call(
        functools.partial(_combine_kernel, f=f, final=final, n_alias=len(aliases), tile0=tile0,
                          n_lat_tiles=n_lat_tiles),
        grid=(n_tiles,),
        in_specs=in_specs,
        out_specs=out_specs,
        out_shape=out_shape,
        input_output_aliases=aliases,
        compiler_params=_cparams("arbitrary"),
        name=f"combine{layer}",
    )(*args)
    if not final:
        return outs[0]
    return (outs[0] if has_lat else y_lat), (outs[-1] if has_ctx else None)


SC_CHUNK = 128


def _sc_workers():
    info = plsc.get_sparse_core_info()
    return info.num_cores, info.num_subcores


def _sc_dispatch(rows, dest, cap):
    n, w = rows.shape
    nc, ns = _sc_workers()
    per_w = n // (nc * ns)
    assert per_w * nc * ns == n and per_w % SC_CHUNK == 0
    mesh = plsc.VectorSubcoreMesh(core_axis_name="c", subcore_axis_name="s")

    @functools.partial(
        pl.kernel, mesh=mesh, out_type=jax.ShapeDtypeStruct((cap, w), rows.dtype),
        scratch_types=[pltpu.VMEM((TOP_K, SC_CHUNK), jnp.int32),
                       pltpu.VMEM((SC_CHUNK, w), rows.dtype)],
        name="sc_dispatch")
    def run(rows_hbm, dest_hbm, out_hbm, idx_v, rows_v):
        wid = lax.axis_index("s") * nc + lax.axis_index("c")

        @pl.loop(0, per_w // SC_CHUNK)
        def _(ci):
            base = pl.multiple_of(wid * per_w + ci * SC_CHUNK, SC_CHUNK)
            pltpu.sync_copy(dest_hbm.at[wid * (per_w // SC_CHUNK) + ci], idx_v)
            pltpu.sync_copy(rows_hbm.at[pl.ds(base, SC_CHUNK)], rows_v)
            for k in range(TOP_K):
                pltpu.sync_copy(rows_v, out_hbm.at[idx_v.at[k]])

    return run(rows, dest)


SC_TOK = 8
SC_LANES = 16


def _sc_combine(table, dest, gates):
    n_chunks, k_top, _ = dest.shape
    n = n_chunks * SC_CHUNK
    w = table.shape[1]
    nc, ns = _sc_workers()
    per_w = n // (nc * ns)
    assert per_w * nc * ns == n and per_w % SC_CHUNK == 0 and w % (16 * SC_LANES) == 0
    gb = gates.reshape(k_top, n_chunks, SC_CHUNK).transpose(1, 0, 2)
    mesh = plsc.VectorSubcoreMesh(core_axis_name="c", subcore_axis_name="s")

    n_blocks = SC_CHUNK // SC_TOK

    @functools.partial(
        pl.kernel, mesh=mesh, out_type=jax.ShapeDtypeStruct((n, 2 * w), F32),
        scratch_types=[pltpu.VMEM((k_top, SC_CHUNK), jnp.int32),
                       pltpu.VMEM((k_top, SC_CHUNK), F32),
                       pltpu.VMEM((2, k_top, SC_TOK, w), table.dtype),
                       pltpu.VMEM((2, SC_TOK, 2 * w), F32),
                       pltpu.SemaphoreType.DMA((2,)), pltpu.SemaphoreType.DMA((2,))],
        compiler_params=pltpu.CompilerParams(needs_layout_passes=False),
        name="sc_combine")
    def run(table_hbm, dest_hbm, gb_hbm, out_hbm, idx_v, g_v, rows_v, out_v, sem_in, sem_out):
        wid = lax.axis_index("s") * nc + lax.axis_index("c")

        def gathers(blk, slot):
            off = pl.multiple_of(blk * SC_TOK, SC_TOK)
            return [pltpu.make_async_copy(table_hbm.at[idx_v.at[k, pl.ds(off, SC_TOK)]],
                                          rows_v.at[slot, k], sem_in.at[slot]) for k in range(k_top)]

        def write_back(chunk, blk, slot):
            row0 = pl.multiple_of(chunk * SC_CHUNK + blk * SC_TOK, SC_TOK)
            return pltpu.make_async_copy(out_v.at[slot], out_hbm.at[pl.ds(row0, SC_TOK)],
                                         sem_out.at[slot])

        def weighted_sum(blk, slot):
            @pl.loop(0, SC_TOK)
            def _(j):
                for base in range(0, w, 16 * SC_LANES):
                    acc = None
                    for k in range(k_top):
                        tok = blk * SC_TOK + j
                        g16 = g_v[k, pl.ds(pl.multiple_of(tok // SC_LANES * SC_LANES, SC_LANES),
                                           SC_LANES)]
                        g = g16.at[jnp.full((SC_LANES,), tok % SC_LANES, jnp.int32)].get(
                            mode="promise_in_bounds")
                        terms = []
                        for v in range(16):
                            word = rows_v[slot, k, j, pl.ds(base + v * SC_LANES, SC_LANES)]
                            lo = lax.bitcast_convert_type(lax.shift_left(word, 16), F32)
                            hi = lax.bitcast_convert_type(word & jnp.int32(-65536), F32)
                            terms.append((g * lo, g * hi))
                        acc = terms if acc is None else [
                            (a + x, b + y) for (a, b), (x, y) in zip(acc, terms)]
                    for v, (a, b) in enumerate(acc):
                        out_v[slot, j, pl.ds(base + v * SC_LANES, SC_LANES)] = a
                        out_v[slot, j, pl.ds(w + base + v * SC_LANES, SC_LANES)] = b

        @pl.loop(0, per_w // SC_CHUNK)
        def _(ci):
            chunk = wid * (per_w // SC_CHUNK) + ci
            pltpu.sync_copy(dest_hbm.at[chunk], idx_v)
            pltpu.sync_copy(gb_hbm.at[chunk], g_v)
            for cp in gathers(0, 0):
                cp.start()

            @pl.loop(0, n_blocks, step=2)
            def _(b0):
                for slot in range(2):
                    blk = b0 + slot
                    for cp in gathers(blk, slot):
                        cp.wait()

                    @pl.when(blk + 1 < n_blocks)
                    def _():
                        for cp in gathers(blk + 1, 1 - slot):
                            cp.start()

                    @pl.when(blk >= 2)
                    def _():
                        write_back(chunk, blk - 2, slot).wait()

                    weighted_sum(blk, slot)
                    write_back(chunk, blk, slot).start()

            for slot in range(2):
                write_back(chunk, n_blocks - 2 + slot, slot).wait()


    return run(table, dest, gb)


def _split_bf16(w):
    hi = w.astype(BF16)
    return hi, (w - hi.astype(F32)).astype(BF16)


def kernel(x_prompt, x_sample, cache_k, cache_v, c, c_ctx, ada_w, ada_b, norm1_g, norm2_g, w_qkv, w_o, na_rpb, diff_lambda, diff_subln_g, router_w, router_b, exp_w_gate, exp_w_up, exp_w_down, shared_w_gate, shared_w_up, shared_w_down, final_g):
    batch, seq, d = x_prompt.shape
    n_lat_batch, lat_seq, _ = x_sample.shape
    depth = w_qkv.shape[0]
    n_exp = router_w.shape[-1]
    n_lat = n_lat_batch * lat_seq
    n_ctx = batch * seq
    n = n_lat + n_ctx
    assert d % LANES == 0 and na_rpb.shape[1] * HEAD_DIM == d
    assert diff_lambda.shape[-1] == HEAD_DIM and lat_seq % GRID_W == 0
    assert n_lat % seq == 0
    tm = _pick_tile(math.gcd(lat_seq, n_ctx), 512)
    tmb = 1024
    n_all = n // tm
    ranges = [(r * n_all // MOE_RANGES, (r + 1) * n_all // MOE_RANGES - r * n_all // MOE_RANGES)
              for r in range(MOE_RANGES)]
    scale = HEAD_DIM ** -0.5 * LOG2E

    xs = [x_sample.reshape(n_lat, d), x_prompt.reshape(n_ctx, d)]
    mod_rows = -(-(n_lat_batch + 1) // 8) * 8
    cond = jnp.zeros((mod_rows, d), F32).at[:n_lat_batch].set(c).at[n_lat_batch].set(c_ctx)
    mod = _modulation(cond, ada_w, ada_b).reshape(depth, mod_rows, N_MOD, d)

    w_qkv_b = w_qkv.astype(BF16)
    w_o_b = w_o.astype(BF16)
    rope_tab = _rope_tables(lat_seq, tm)
    caches = None
    for i in range(depth):
        is_diff = i % 2 == 1
        j = i // 2
        q, k, v, *caches = _qkv(xs, mod, norm1_g.reshape(depth, 1, d), w_qkv_b,
                                rope_tab if is_diff else None, caches, layer=i, n_lat=n_lat,
                                batch=batch, seq=seq, lat_seq=lat_seq, n_lat_batch=n_lat_batch,
                                tm=tm, scale=scale)
        if is_diff:
            vt = caches.pop()
        if not is_diff:
            o = _na_attention(q, k, v, cache_k, cache_v, na_rpb[j], layer=i,
                              n_lat_batch=n_lat_batch, lat_seq=lat_seq)
            o = _ctx_dense(q, k, v, o, n_lat=n_lat, seq=seq)
        else:
            lam_init = 0.8 - 0.6 * math.exp(-0.3 * i)
            lp = diff_lambda[j].astype(F32)
            lam = (jnp.exp(jnp.sum(lp[0] * lp[1])) - jnp.exp(jnp.sum(lp[2] * lp[3]))
                   + lam_init).reshape(1)
            subg = diff_subln_g[j].reshape(1, LANES).astype(F32)
            o = _lat_diff(lam, q, k, vt, cache_k, cache_v, subg, layer=i,
                          n_lat_batch=n_lat_batch, lat_seq=lat_seq, out_scale=1.0 - lam_init)
            o = _ctx_diff(lam, q, k, v, subg, o, n_lat=n_lat, seq=seq, out_scale=1.0 - lam_init)

        rwh, rwl = _split_bf16(router_w[i].T)
        swgu = jnp.concatenate([shared_w_gate[i], shared_w_up[i]], axis=-1).astype(BF16)
        final = i == depth - 1
        routed = []
        for tile0, nt in ranges:
            xmid, h2, sel, gates, rank, counts = _post_attn(
                o, xs, mod, w_o_b[i], norm2_g[i].reshape(1, d), rwh, rwl,
                router_b[i].reshape(n_exp, 1).astype(F32), layer=i, tile0=tile0, n_tiles=nt,
                lat_seq=lat_seq, n_lat_batch=n_lat_batch, tm=tm)
            n_tiles = -(-(nt * tm * TOP_K) // tmb) + n_exp
            dest, tile_expert, tile_rows, n_used = _plan(counts, sel, rank, tmb=tmb, n_tiles=n_tiles)
            xb = _sc_dispatch(h2, dest, n_tiles * tmb)
            routed.append((xmid, h2, gates, dest, tile_expert, tile_rows, n_used, xb))
        ybs = [_experts(te, tr, nu, xb, exp_w_gate, exp_w_up, exp_w_down, layer=i, tmb=tmb)
               for (_, _, _, _, te, tr, nu, xb) in routed]
        sums = [_sc_combine(yb, r[3], r[2]) for yb, r in zip(ybs, routed)]
        xs, y_lat, y_ctx = [], None, None
        for (tile0, nt), (xmid, h2, *_), rsum in zip(ranges, routed, sums):
            out = _combine(xmid, h2, rsum, mod, swgu, shared_w_down[i].astype(BF16),
                           final_g.reshape(1, d), y_lat, layer=i, tile0=tile0, n_lat=n_lat,
                           n_ctx=n_ctx, lat_seq=lat_seq, n_lat_batch=n_lat_batch, tm=tm, final=final)
            if final:
                y_lat, y_ctx = out[0], (out[1] if out[1] is not None else y_ctx)
            else:
                xs.append(out)

    y_sample = y_lat.reshape(n_lat_batch, lat_seq, d)
    y_prompt = y_ctx.reshape(batch, seq, d)
    return (y_prompt, y_sample, caches[0], caches[1])
```
